```python
import math
import jax, jax.numpy as jnp
from jax import lax
import numpy as np

D_MODEL = 4096
BATCH = 2
SEQ = 8192
DEPTH = 1

D_MIX = D_MODEL
SB_HEADS = 16
SB_HEAD_DIM = 128
SB_WIDTH = SB_HEADS * SB_HEAD_DIM
ML_HEADS = 4
ML_HEAD_DIM = 512
ML_WIDTH = ML_HEADS * ML_HEAD_DIM
CONV_WIDTH = 4
ML_CHUNK = 64
Q_BLOCK = 128
PROJ_OUT = 3 * SB_WIDTH + 4 * ML_WIDTH + 2 * ML_HEADS
N_GROUPS = 4
EXPERTS_PER_GROUP = 8
N_EXPERTS = N_GROUPS * EXPERTS_PER_GROUP
TOP_K = 2
D_FF_EXPERT = 1024
MOE_BLOCK = 256
NORM_EPS = 1e-6
N_LAYER_MOD = 6
N_FINAL_MOD = 2

kernel_name = "hybrid_stickbreak_mlstm_hmoe_block"


def rms_norm(x, w):
    xf = x.astype(jnp.float32)
    y = xf * lax.rsqrt(jnp.mean(xf * xf, axis=-1, keepdims=True) + NORM_EPS)
    return (y * w.astype(jnp.float32)).astype(x.dtype)


def modulate(h, shift, scale):
    return h * (1.0 + scale[:, None, :]) + shift[:, None, :]


def stick_breaking_attention(q, k, v):
    b, s, h, d = q.shape
    n_blk = s // Q_BLOCK
    scale = 1.0 / math.sqrt(d)
    qb = q.reshape(b, n_blk, Q_BLOCK, h, d).transpose(1, 0, 3, 2, 4)
    kf = k.astype(jnp.float32)
    vf = v.astype(jnp.float32)
    key_pos = jnp.arange(s)

    def block(args):
        q_blk, blk_idx = args
        z = jnp.einsum('bhqd,bkhd->bhqk', q_blk.astype(jnp.float32), kf) * scale
        q_pos = blk_idx * Q_BLOCK + jnp.arange(Q_BLOCK)
        causal = key_pos[None, :] < q_pos[:, None]
        log_keep = jnp.where(causal, jax.nn.log_sigmoid(-z), 0.0)
        between = lax.cumsum(log_keep, axis=3, reverse=True) - log_keep
        w = jnp.where(causal, jnp.exp(jax.nn.log_sigmoid(z) + between), 0.0)
        return jnp.einsum('bhqk,bkhd->bqhd', w, vf)

    out = lax.map(block, (qb, jnp.arange(n_blk)))
    return out.transpose(1, 0, 2, 3, 4).reshape(b, s, h, d).astype(q.dtype)


def causal_depthwise_conv(x, w):
    kw, ch = w.shape
    return lax.conv_general_dilated(
        x, w.reshape(kw, 1, ch).astype(x.dtype), window_strides=(1,),
        padding=[(kw - 1, 0)], dimension_numbers=('NWC', 'WIO', 'NWC'),
        feature_group_count=ch)


def mlstm_chunkwise(q, k, v, i_pre, f_pre):
    b, s, h, d = q.shape
    L = ML_CHUNK
    nc = s // L

    def chunks4(t):
        return t.astype(jnp.float32).reshape(b, nc, L, h, d).transpose(1, 0, 3, 2, 4)

    def chunks3(t):
        return t.astype(jnp.float32).reshape(b, nc, L, h).transpose(1, 0, 3, 2)

    qc = chunks4(q)
    kc = chunks4(k) * (1.0 / math.sqrt(d))
    vc = chunks4(v)
    log_i = chunks3(i_pre)
    log_f = jax.nn.log_sigmoid(chunks3(f_pre))
    causal = jnp.tril(jnp.ones((L, L), dtype=bool))

    def step(carry, xs):
        c_st, n_st, m_st = carry
        qk_, kk_, vk_, li, lf = xs
        b_cum = jnp.cumsum(lf, axis=-1)
        a = b_cum + m_st[..., None]
        d_mat = jnp.where(causal, b_cum[..., :, None] - b_cum[..., None, :] + li[..., None, :], -jnp.inf)
        m_t = jnp.maximum(a, jnp.max(d_mat, axis=-1))
        w_intra = jnp.exp(d_mat - m_t[..., None])
        w_inter = jnp.exp(a - m_t)
        sc = jnp.einsum('bhtd,bhsd->bhts', qk_, kk_) * w_intra
        num = (w_inter[..., None] * jnp.einsum('bhtd,bhde->bhte', qk_, c_st)
               + jnp.einsum('bhts,bhse->bhte', sc, vk_))
        den = w_inter * jnp.einsum('bhtd,bhd->bht', qk_, n_st) + jnp.sum(sc, axis=-1)
        h_out = num / jnp.maximum(jnp.abs(den), jnp.exp(-m_t))[..., None]
        b_last = b_cum[..., -1]
        g = b_last[..., None] - b_cum + li
        m_new = jnp.maximum(b_last + m_st, jnp.max(g, axis=-1))
        decay = jnp.exp(b_last + m_st - m_new)
        kw_ = kk_ * jnp.exp(g - m_new[..., None])[..., None]
        c_new = decay[..., None, None] * c_st + jnp.einsum('bhsd,bhse->bhde', kw_, vk_)
        n_new = decay[..., None] * n_st + jnp.sum(kw_, axis=2)
        return (c_new, n_new, m_new), h_out

    init = (jnp.zeros((b, h, d, d), jnp.float32), jnp.zeros((b, h, d), jnp.float32),
            jnp.zeros((b, h), jnp.float32))
    _, hs = lax.scan(step, init, (qc, kc, vc, log_i, log_f))
    return hs.transpose(1, 0, 3, 2, 4).reshape(b, s, h, d)


def hybrid_mixer(u, w_in, conv_w, ml_gate_bias, ml_norm_w, w_out):
    b, s, _ = u.shape
    proj = u @ w_in
    o1 = 3 * SB_WIDTH
    o2 = o1 + 2 * ML_WIDTH
    o3 = o2 + ML_WIDTH
    o4 = o3 + ML_WIDTH
    sb_qkv, ml_qk, ml_v, ml_o, ml_if = jnp.split(proj, [o1, o2, o3, o4], axis=-1)
    sb_q, sb_k, sb_v = [t.reshape(b, s, SB_HEADS, SB_HEAD_DIM) for t in jnp.split(sb_qkv, 3, axis=-1)]
    y_sb = stick_breaking_attention(sb_q, sb_k, sb_v).reshape(b, s, SB_WIDTH)
    ml_qk = jax.nn.silu(causal_depthwise_conv(ml_qk, conv_w))
    ml_q, ml_k = [t.reshape(b, s, ML_HEADS, ML_HEAD_DIM) for t in jnp.split(ml_qk, 2, axis=-1)]
    ml_v = ml_v.reshape(b, s, ML_HEADS, ML_HEAD_DIM)
    ml_if = ml_if + ml_gate_bias
    i_pre, f_pre = jnp.split(ml_if, 2, axis=-1)
    hm = mlstm_chunkwise(ml_q, ml_k, ml_v, i_pre, f_pre)
    hm = hm * lax.rsqrt(jnp.mean(hm * hm, axis=-1, keepdims=True) + NORM_EPS)
    hm = (hm.reshape(b, s, ML_WIDTH) * ml_norm_w.astype(jnp.float32)).astype(u.dtype)
    y_ml = jax.nn.sigmoid(ml_o) * hm
    return jnp.concatenate([y_sb, y_ml], axis=-1) @ w_out


def hierarchical_moe(h, w_router_group, b_router_group, w_router_expert, b_router_expert,
                     w_exp_gate, w_exp_up, w_exp_down):
    n, d = h.shape
    hf = h.astype(jnp.float32)
    g_probs = jax.nn.softmax(hf @ w_router_group.astype(jnp.float32) + b_router_group.astype(jnp.float32), axis=-1)
    g_p, g_sel = lax.top_k(g_probs, 1)
    e_logits = (hf @ w_router_expert.astype(jnp.float32) + b_router_expert.astype(jnp.float32))
    e_logits = e_logits.reshape(n, N_GROUPS, EXPERTS_PER_GROUP)
    e_in_group = jnp.take_along_axis(e_logits, g_sel[:, :, None], axis=1)[:, 0]
    e_probs = jax.nn.softmax(e_in_group, axis=-1)
    top_p, top_local = lax.top_k(e_probs, TOP_K)
    top_w = top_p / jnp.sum(top_p, axis=-1, keepdims=True) * g_p
    top_e = g_sel * EXPERTS_PER_GROUP + top_local

    n_assign = n * TOP_K
    flat_e = top_e.reshape(-1).astype(jnp.int32)
    flat_w = top_w.reshape(-1)
    flat_tok = jnp.repeat(jnp.arange(n, dtype=jnp.int32), TOP_K)
    order = jnp.argsort(flat_e)
    sorted_e = flat_e[order]
    counts = jnp.zeros((N_EXPERTS,), jnp.int32).at[flat_e].add(1)
    padded = (counts + MOE_BLOCK - 1) // MOE_BLOCK * MOE_BLOCK
    starts = jnp.cumsum(counts) - counts
    pends = jnp.cumsum(padded)
    pstarts = pends - padded
    dest = pstarts[sorted_e] + (jnp.arange(n_assign, dtype=jnp.int32) - starts[sorted_e])
    cap = (-(-n_assign // MOE_BLOCK) + N_EXPERTS) * MOE_BLOCK
    n_blocks = cap // MOE_BLOCK
    slot_tok = jnp.full((cap,), n, jnp.int32).at[dest].set(flat_tok[order])
    slot_w = jnp.zeros((cap,), h.dtype).at[dest].set(flat_w[order].astype(h.dtype))
    block_e = jnp.minimum(jnp.searchsorted(pends, jnp.arange(n_blocks, dtype=jnp.int32) * MOE_BLOCK, side='right'),
                          N_EXPERTS - 1).astype(jnp.int32)

    h_pad = jnp.concatenate([h, jnp.zeros((1, d), h.dtype)], axis=0)
    xb = h_pad[slot_tok].reshape(n_blocks, MOE_BLOCK, d)

    def expert_block(args):
        xblk, e = args
        return (jax.nn.silu(xblk @ w_exp_gate[e]) * (xblk @ w_exp_up[e])) @ w_exp_down[e]

    yb = lax.map(expert_block, (xb, block_e)).reshape(cap, d)
    out = jnp.zeros((n + 1, d), h.dtype).at[slot_tok].add(yb * slot_w[:, None])
    return out[:n]


def setup_inputs(seed: int = 0) -> dict:
    key = jax.random.key(seed)
    ks = jax.random.split(key, 24)
    f32 = jnp.float32
    nrm = lambda k, shape, s: jax.random.normal(k, shape, f32) * s
    i_bias = -1.0 + 0.1 * jax.random.normal(ks[4], (DEPTH, ML_HEADS), f32)
    f_bias = jnp.linspace(3.0, 6.0, ML_HEADS, dtype=f32)[None, :] + 0.1 * jax.random.normal(ks[5], (DEPTH, ML_HEADS), f32)
    return {
        "x": nrm(ks[0], (BATCH, SEQ, D_MODEL), 1.0),
        "c": nrm(ks[1], (BATCH, D_MODEL), 1.0),
        "norm1_w": 1.0 + nrm(ks[2], (DEPTH, D_MODEL), 0.02),
        "w_in": nrm(ks[3], (DEPTH, D_MODEL, PROJ_OUT), D_MODEL ** -0.5),
        "conv_w": nrm(ks[6], (DEPTH, CONV_WIDTH, 2 * ML_WIDTH), CONV_WIDTH ** -0.5),
        "ml_gate_bias": jnp.concatenate([i_bias, f_bias], axis=-1),
        "ml_norm_w": 1.0 + nrm(ks[7], (DEPTH, ML_WIDTH), 0.02),
        "w_out": nrm(ks[8], (DEPTH, D_MIX, D_MODEL), D_MIX ** -0.5),
        "norm2_w": 1.0 + nrm(ks[9], (DEPTH, D_MODEL), 0.02),
        "w_router_group": nrm(ks[10], (DEPTH, D_MODEL, N_GROUPS), D_MODEL ** -0.5),
        "b_router_group": nrm(ks[11], (DEPTH, N_GROUPS), 0.01),
        "w_router_expert": nrm(ks[12], (DEPTH, D_MODEL, N_EXPERTS), D_MODEL ** -0.5),
        "b_router_expert": nrm(ks[13], (DEPTH, N_EXPERTS), 0.01),
        "w_exp_gate": nrm(ks[14], (DEPTH, N_EXPERTS, D_MODEL, D_FF_EXPERT), D_MODEL ** -0.5),
        "w_exp_up": nrm(ks[15], (DEPTH, N_EXPERTS, D_MODEL, D_FF_EXPERT), D_MODEL ** -0.5),
        "w_exp_down": nrm(ks[16], (DEPTH, N_EXPERTS, D_FF_EXPERT, D_MODEL), D_FF_EXPERT ** -0.5),
        "w_ada": nrm(ks[17], (DEPTH, D_MODEL, N_LAYER_MOD * D_MODEL), 0.5 * D_MODEL ** -0.5),
        "b_ada": nrm(ks[18], (DEPTH, N_LAYER_MOD * D_MODEL), 0.01),
        "final_norm_w": 1.0 + nrm(ks[19], (D_MODEL,), 0.02),
        "w_ada_final": nrm(ks[20], (D_MODEL, N_FINAL_MOD * D_MODEL), 0.5 * D_MODEL ** -0.5),
        "b_ada_final": nrm(ks[21], (N_FINAL_MOD * D_MODEL,), 0.01),
    }


def reference(x, c, norm1_w, w_in, conv_w, ml_gate_bias, ml_norm_w, w_out, norm2_w,
              w_router_group, b_router_group, w_router_expert, b_router_expert,
              w_exp_gate, w_exp_up, w_exp_down, w_ada, b_ada, final_norm_w,
              w_ada_final, b_ada_final):
    b, s, d = x.shape
    c_act = jax.nn.silu(c)
    h = x
    for layer in range(DEPTH):
        mod = c_act @ w_ada[layer] + b_ada[layer]
        sh1, sc1, g1, sh2, sc2, g2 = jnp.split(mod, N_LAYER_MOD, axis=-1)
        u = modulate(rms_norm(h, norm1_w[layer]), sh1, sc1)
        mix = hybrid_mixer(u, w_in[layer], conv_w[layer], ml_gate_bias[layer], ml_norm_w[layer], w_out[layer])
        h = h + g1[:, None, :] * mix
        u = modulate(rms_norm(h, norm2_w[layer]), sh2, sc2)
        ffn = hierarchical_moe(u.reshape(b * s, d), w_router_group[layer], b_router_group[layer],
                               w_router_expert[layer], b_router_expert[layer],
                               w_exp_gate[layer], w_exp_up[layer], w_exp_down[layer]).reshape(b, s, d)
        h = h + g2[:, None, :] * ffn
    fmod = c_act @ w_ada_final + b_ada_final
    sh_f, sc_f = jnp.split(fmod, N_FINAL_MOD, axis=-1)
    return modulate(rms_norm(h, final_norm_w), sh_f, sc_f)
```

```python
import functools
import math

import jax
import jax.numpy as jnp
from jax import lax
from jax.experimental import pallas as pl
from jax.experimental.pallas import tpu as pltpu

SB_HEADS = 16
SB_HEAD_DIM = 128
ML_HEADS = 4
ML_HEAD_DIM = 512
CONV_WIDTH = 4
N_GROUPS = 4
EXPERTS_PER_GROUP = 8
N_EXPERTS = N_GROUPS * EXPERTS_PER_GROUP
TOP_K = 2
NORM_EPS = 1e-6

LANES = 128
SUBLANES = 8
VMEM_LIMIT = 56 * 1024 * 1024

ATTN_BLOCK = 256
ML_CHUNK = 256
MOE_BLOCK = 256
UNDERFLOW_LOG = -90.0

F32 = jnp.float32
BF16 = jnp.bfloat16


def _cparams(*sem):
    return pltpu.CompilerParams(dimension_semantics=sem, vmem_limit_bytes=VMEM_LIMIT)


def _split_bf16(x):
    hi = x.astype(BF16)
    lo = (x - hi.astype(F32)).astype(BF16)
    return hi, lo


def _log_sigmoid(x):
    return jnp.minimum(x, 0.0) - jnp.log1p(jnp.exp(-jnp.abs(x)))


def _ada_kernel(c_ref, w_ref, b_ref, o_ref):
    nb, tn = o_ref.shape
    rows = []
    for b in range(nb):
        cb = c_ref[b]
        cb = cb * jax.nn.sigmoid(cb)
        pieces = [jnp.sum(w_ref[:, j * LANES:(j + 1) * LANES] * cb, axis=0, keepdims=True)
                  for j in range(tn // LANES)]
        rows.append(jnp.concatenate(pieces, axis=1))
    o_ref[...] = jnp.concatenate(rows, axis=0) + b_ref[...]


def _ada(c_rep, w, bias, tn=512):
    nb, k, _ = c_rep.shape
    n_out = w.shape[1]
    return pl.pallas_call(
        _ada_kernel,
        grid=(n_out // tn,),
        in_specs=[pl.BlockSpec((nb, k, LANES), lambda j: (0, 0, 0)),
                  pl.BlockSpec((k, tn), lambda j: (0, j)),
                  pl.BlockSpec((1, tn), lambda j: (0, j))],
        out_specs=pl.BlockSpec((nb, tn), lambda j: (0, j)),
        out_shape=jax.ShapeDtypeStruct((nb, n_out), F32),
        compiler_params=_cparams("arbitrary"),
        name="ada",
    )(c_rep, w, bias.reshape(1, n_out))


def _rms_mod(x, w, shift, scale):
    var = jnp.mean(x * x, axis=-1, keepdims=True)
    y = x * lax.rsqrt(var + NORM_EPS) * w
    return y * (1.0 + scale) + shift


def _norm_mod_kernel(x_ref, w_ref, sh_ref, sc_ref, o_ref):
    o_ref[...] = _rms_mod(x_ref[...], w_ref[...], sh_ref[...], sc_ref[...]).astype(o_ref.dtype)


def _norm_mod(x, w, shift, scale, out_dtype, tm=256):
    b, s, d = x.shape
    vec = pl.BlockSpec((None, 1, d), lambda bi, i: (bi, 0, 0))
    return pl.pallas_call(
        _norm_mod_kernel,
        grid=(b, s // tm),
        in_specs=[pl.BlockSpec((None, tm, d), lambda bi, i: (bi, i, 0)),
                  pl.BlockSpec((1, d), lambda bi, i: (0, 0)), vec, vec],
        out_specs=pl.BlockSpec((None, tm, d), lambda bi, i: (bi, i, 0)),
        out_shape=jax.ShapeDtypeStruct((b, s, d), out_dtype),
        compiler_params=_cparams("arbitrary", "arbitrary"),
        name="norm_mod",
    )(x, w.reshape(1, d), shift.reshape(b, 1, d), scale.reshape(b, 1, d))


def _matmul_kernel(a_ref, w_ref, o_ref):
    o_ref[...] = jnp.dot(a_ref[...], w_ref[...], preferred_element_type=F32).astype(o_ref.dtype)


def _matmul_bias_kernel(a_ref, w_ref, b_ref, o_ref):
    o_ref[...] = (jnp.dot(a_ref[...], w_ref[...], preferred_element_type=F32) + b_ref[...]).astype(o_ref.dtype)


def _matmul(a, w, col0, ncols, out_dtype, tm, tn, bias=None):
    m, k = a.shape
    j0 = col0 // tn
    in_specs = [pl.BlockSpec((tm, k), lambda i, j: (i, 0)),
                pl.BlockSpec((k, tn), lambda i, j: (0, j + j0))]
    args = [a, w]
    kern = _matmul_kernel
    if bias is not None:
        in_specs.append(pl.BlockSpec((1, tn), lambda i, j: (0, j + j0)))
        args.append(bias)
        kern = _matmul_bias_kernel
    return pl.pallas_call(
        kern,
        grid=(m // tm, ncols // tn),
        in_specs=in_specs,
        out_specs=pl.BlockSpec((tm, tn), lambda i, j: (i, j)),
        out_shape=jax.ShapeDtypeStruct((m, ncols), out_dtype),
        compiler_params=_cparams("arbitrary", "arbitrary"),
        name="proj",
    )(*args)


def _sb_attn_kernel(q_ref, k_ref, v_ref, tri_ref, o_ref, acc_ref, r_ref, *, blk, scale):
    qi = pl.program_id(2)
    q = q_ref[...]
    tri = tri_ref[...]

    def visit(j, diag):
        start = pl.multiple_of(j * blk, blk)
        kb = k_ref[pl.ds(start, blk), :]
        vb = v_ref[pl.ds(start, blk), :]
        z = lax.dot_general(q, kb, (((1,), (1,)), ((), ())), preferred_element_type=F32) * scale
        log_keep = -(jnp.maximum(z, 0.0) + jnp.log(1.0 + jnp.exp(-jnp.abs(z))))
        log_beta = log_keep + z
        if diag:
            row = lax.broadcasted_iota(jnp.int32, (blk, blk), 0)
            col = lax.broadcasted_iota(jnp.int32, (blk, blk), 1)
            causal = col < row
            log_keep = jnp.where(causal, log_keep, 0.0)
        hi, lo = _split_bf16(log_keep)
        between = (jnp.dot(hi, tri, preferred_element_type=F32)
                   + jnp.dot(lo, tri, preferred_element_type=F32))
        r = r_ref[...]
        r_wide = jnp.concatenate([r] * (blk // LANES), axis=1)
        w = jnp.exp(log_beta + between + r_wide)
        if diag:
            w = jnp.where(causal, w, 0.0)
        acc_ref[...] += jnp.dot(w.astype(BF16), vb, preferred_element_type=F32)
        r_new = r + jnp.sum(log_keep, axis=1, keepdims=True)
        r_ref[...] = r_new
        return jnp.max(r_new)

    acc_ref[...] = jnp.zeros_like(acc_ref)
    r_ref[...] = jnp.zeros_like(r_ref)
    r_max = visit(qi, True)

    def cond(carry):
        j, r_max = carry
        return jnp.logical_and(j >= 0, r_max > UNDERFLOW_LOG)

    def body(carry):
        j, _ = carry
        return j - 1, visit(j, False)

    lax.while_loop(cond, body, (qi - 1, r_max))
    o_ref[...] = acc_ref[...].astype(o_ref.dtype)


def _sb_attention(qkv, blk=ATTN_BLOCK):
    b, s, _ = qkv.shape
    h, d = SB_HEADS, SB_HEAD_DIM
    idx = lax.broadcasted_iota(jnp.int32, (blk, blk), 0) > lax.broadcasted_iota(jnp.int32, (blk, blk), 1)
    tri = idx.astype(BF16)
    kern = functools.partial(_sb_attn_kernel, blk=blk, scale=1.0 / math.sqrt(d))
    return pl.pallas_call(
        kern,
        grid=(b, h, s // blk),
        in_specs=[pl.BlockSpec((None, blk, d), lambda bi, hi, qi: (bi, qi, hi)),
                  pl.BlockSpec((None, s, d), lambda bi, hi, qi: (bi, 0, h + hi)),
                  pl.BlockSpec((None, s, d), lambda bi, hi, qi: (bi, 0, 2 * h + hi)),
                  pl.BlockSpec((blk, blk), lambda bi, hi, qi: (0, 0))],
        out_specs=pl.BlockSpec((None, blk, d), lambda bi, hi, qi: (bi, qi, hi)),
        out_shape=jax.ShapeDtypeStruct((b, s, h * d), BF16),
        scratch_shapes=[pltpu.VMEM((blk, d), F32), pltpu.VMEM((blk, LANES), F32)],
        compiler_params=_cparams("arbitrary", "arbitrary", "arbitrary"),
        name="sb_attn",
    )(qkv, qkv, qkv, tri)


def _mlstm_kernel(qp_ref, kp_ref, v_ref, og_ref, gcol_ref, grow_ref, cwq_ref, cwk_ref, nw_ref,
                  tril_ref, triu_ref, y_ref, c_st, n_st, m_st, qbuf, kbuf, *, chunk, dim):
    head = pl.program_id(1)
    ci = pl.program_id(2)
    halo = SUBLANES

    @pl.when(ci == 0)
    def _():
        c_st[...] = jnp.zeros_like(c_st)
        n_st[...] = jnp.zeros_like(n_st)
        m_st[...] = jnp.zeros_like(m_st)
        qbuf[0:halo, :] = jnp.zeros((halo, dim), F32)
        kbuf[0:halo, :] = jnp.zeros((halo, dim), F32)

    qbuf[halo:halo + chunk, :] = qp_ref[...]
    kbuf[halo:halo + chunk, :] = kp_ref[...]

    def conv_silu(buf, cw_ref):
        acc = buf[halo:halo + chunk, :] * cw_ref[CONV_WIDTH - 1:CONV_WIDTH, :]
        for j in range(CONV_WIDTH - 1):
            off = halo - (CONV_WIDTH - 1) + j
            acc = acc + buf[off:off + chunk, :] * cw_ref[j:j + 1, :]
        return acc * jax.nn.sigmoid(acc)

    q = conv_silu(qbuf, cwq_ref)
    k = conv_silu(kbuf, cwk_ref) * (1.0 / math.sqrt(dim))
    qbuf[0:halo, :] = qbuf[chunk:chunk + halo, :]
    kbuf[0:halo, :] = kbuf[chunk:chunk + halo, :]

    gcol = gcol_ref[...]
    lane = lax.broadcasted_iota(jnp.int32, gcol.shape, 1)
    li_col = jnp.sum(jnp.where(lane == head, gcol, 0.0), axis=1, keepdims=True)
    lf_col = _log_sigmoid(jnp.sum(jnp.where(lane == ML_HEADS + head, gcol, 0.0), axis=1, keepdims=True))
    grow = grow_ref[...]
    sub = lax.broadcasted_iota(jnp.int32, grow.shape, 0)
    li_row = jnp.sum(jnp.where(sub == head, grow, 0.0), axis=0, keepdims=True)
    lf_row = _log_sigmoid(jnp.sum(jnp.where(sub == ML_HEADS + head, grow, 0.0), axis=0, keepdims=True))

    hi, lo = _split_bf16(jnp.broadcast_to(lf_col, (chunk, LANES)))
    b_col = (jnp.dot(tril_ref[...], hi, preferred_element_type=F32)
             + jnp.dot(tril_ref[...], lo, preferred_element_type=F32))[:, 0:1]
    hi, lo = _split_bf16(jnp.broadcast_to(lf_row, (2 * SUBLANES, chunk)))
    b_row = (jnp.dot(hi, triu_ref[...], preferred_element_type=F32)
             + jnp.dot(lo, triu_ref[...], preferred_element_type=F32))[0:1, :]

    m_prev = m_st[...]
    a_col = b_col + m_prev
    row_t = lax.broadcasted_iota(jnp.int32, (chunk, chunk), 0)
    col_s = lax.broadcasted_iota(jnp.int32, (chunk, chunk), 1)
    d_mat = jnp.where(col_s <= row_t, b_col - b_row + li_row, -jnp.inf)
    m_col = jnp.maximum(a_col, jnp.max(d_mat, axis=1, keepdims=True))
    w_intra = jnp.exp(d_mat - m_col)
    w_inter = jnp.exp(a_col - m_col)

    qb = q.astype(BF16)
    kb = k.astype(BF16)
    vb = v_ref[...]
    sc = lax.dot_general(qb, kb, (((1,), (1,)), ((), ())), preferred_element_type=F32) * w_intra
    num = (w_inter * jnp.dot(qb, c_st[...].astype(BF16), preferred_element_type=F32)
           + jnp.dot(sc.astype(BF16), vb, preferred_element_type=F32))
    den = (w_inter * jnp.sum(q * n_st[...], axis=1, keepdims=True)
           + jnp.sum(sc, axis=1, keepdims=True))
    h_out = num / jnp.maximum(jnp.abs(den), jnp.exp(-m_col))

    b_last = b_col[chunk - 1:chunk, :]
    g_col = b_last - b_col + li_col
    m_new = jnp.maximum(b_last + m_prev, jnp.max(g_col, axis=0, keepdims=True))
    decay = jnp.exp(b_last + m_prev - m_new)
    kw = k * jnp.exp(g_col - m_new)
    c_st[...] = decay * c_st[...] + lax.dot_general(kw.astype(BF16), vb, (((0,), (0,)), ((), ())),
                                                    preferred_element_type=F32)
    n_st[...] = decay * n_st[...] + jnp.sum(kw, axis=0, keepdims=True)
    m_st[...] = m_new

    hn = h_out * lax.rsqrt(jnp.mean(h_out * h_out, axis=1, keepdims=True) + NORM_EPS) * nw_ref[...]
    y_ref[...] = (jax.nn.sigmoid(og_ref[...]) * hn).astype(y_ref.dtype)


def _mlstm(ml_qk, ml_v, ml_o, gates, conv_w, ml_norm_w, chunk=ML_CHUNK):
    b, s, _ = ml_v.shape
    h, d = ML_HEADS, ML_HEAD_DIM
    gates_t = jnp.swapaxes(gates, 1, 2)
    r = lax.broadcasted_iota(jnp.int32, (chunk, chunk), 0)
    c = lax.broadcasted_iota(jnp.int32, (chunk, chunk), 1)
    tril = (c <= r).astype(BF16)
    triu = (r <= c).astype(BF16)
    head_blk = lambda off: pl.BlockSpec((None, chunk, d), lambda bi, hi, ci: (bi, ci, hi + off))
    kern = functools.partial(_mlstm_kernel, chunk=chunk, dim=d)
    return pl.pallas_call(
        kern,
        grid=(b, h, s // chunk),
        in_specs=[head_blk(0), head_blk(h), head_blk(0), head_blk(0),
                  pl.BlockSpec((None, chunk, 2 * h), lambda bi, hi, ci: (bi, ci, 0)),
                  pl.BlockSpec((None, 2 * h, chunk), lambda bi, hi, ci: (bi, 0, ci)),
                  pl.BlockSpec((CONV_WIDTH, d), lambda bi, hi, ci: (0, hi)),
                  pl.BlockSpec((CONV_WIDTH, d), lambda bi, hi, ci: (0, hi + h)),
                  pl.BlockSpec((1, d), lambda bi, hi, ci: (0, hi)),
                  pl.BlockSpec((chunk, chunk), lambda bi, hi, ci: (0, 0)),
                  pl.BlockSpec((chunk, chunk), lambda bi, hi, ci: (0, 0))],
        out_specs=head_blk(0),
        out_shape=jax.ShapeDtypeStruct((b, s, h * d), BF16),
        scratch_shapes=[pltpu.VMEM((d, d), F32), pltpu.VMEM((1, d), F32), pltpu.VMEM((1, 1), F32),
                        pltpu.VMEM((chunk + SUBLANES, d), F32), pltpu.VMEM((chunk + SUBLANES, d), F32)],
        compiler_params=_cparams("arbitrary", "arbitrary", "arbitrary"),
        name="mlstm",
    )(ml_qk, ml_qk, ml_v, ml_o, gates, gates_t, conv_w, conv_w, ml_norm_w.reshape(1, h * d), tril, triu)


def _out_proj_kernel(ya_ref, yb_ref, wa_ref, wb_ref, x_ref, g_ref, o_ref):
    mix = (jnp.dot(ya_ref[...], wa_ref[...], preferred_element_type=F32)
           + jnp.dot(yb_ref[...], wb_ref[...], preferred_element_type=F32))
    o_ref[...] = x_ref[...] + g_ref[...] * mix


def _out_proj(y_sb, y_ml, w_out, x, gate, tm, tn=512):
    b, s, d = x.shape
    ka, kb = y_sb.shape[-1], y_ml.shape[-1]
    assert ka == kb
    n = b * s
    rows_per_seq = s // tm
    out = pl.pallas_call(
        _out_proj_kernel,
        grid=(n // tm, d // tn),
        in_specs=[pl.BlockSpec((tm, ka), lambda i, j: (i, 0)),
                  pl.BlockSpec((tm, kb), lambda i, j: (i, 0)),
                  pl.BlockSpec((ka, tn), lambda i, j: (0, j)),
                  pl.BlockSpec((kb, tn), lambda i, j: (1, j)),
                  pl.BlockSpec((tm, tn), lambda i, j: (i, j)),
                  pl.BlockSpec((None, 1, tn), lambda i, j: (i // rows_per_seq, 0, j))],
        out_specs=pl.BlockSpec((tm, tn), lambda i, j: (i, j)),
        out_shape=jax.ShapeDtypeStruct((n, d), F32),
        compiler_params=_cparams("arbitrary", "arbitrary"),
        name="out_proj",
    )(y_sb.reshape(n, ka), y_ml.reshape(n, kb), w_out, w_out, x.reshape(n, d), gate.reshape(b, 1, d))
    return out.reshape(b, s, d)


def _router_kernel(h_ref, nw_ref, sh_ref, sc_ref, wr_ref, br_ref, tri_ref, u_ref, ri_ref, rw_ref, cnt_ref, run_ref):
    first = jnp.logical_and(pl.program_id(0) == 0, pl.program_id(1) == 0)

    @pl.when(first)
    def _():
        run_ref[...] = jnp.zeros_like(run_ref)

    u = _rms_mod(h_ref[...], nw_ref[...], sh_ref[...], sc_ref[...])
    u_ref[...] = u
    tm = u.shape[0]
    uh, ul = _split_bf16(u)
    wh, wl = _split_bf16(wr_ref[...])
    logits = (jnp.dot(uh, wh, preferred_element_type=F32) + jnp.dot(ul, wh, preferred_element_type=F32)
              + jnp.dot(uh, wl, preferred_element_type=F32) + br_ref[...])

    lane = lax.broadcasted_iota(jnp.int32, (tm, LANES), 1)
    lane_f = lane.astype(F32)
    neg = -jnp.inf
    first_lane = lambda hit: jnp.min(jnp.where(hit, lane_f, float(LANES)), axis=1, keepdims=True)

    gl = jnp.where(lane < N_GROUPS, logits, neg)
    g_max = jnp.max(gl, axis=1, keepdims=True)
    g_p = 1.0 / jnp.sum(jnp.exp(gl - g_max), axis=1, keepdims=True)
    g_sel = first_lane(gl == g_max)

    lo_lane = N_GROUPS + EXPERTS_PER_GROUP * g_sel
    el = jnp.where(jnp.logical_and(lane_f >= lo_lane, lane_f < lo_lane + EXPERTS_PER_GROUP), logits, neg)
    e1_max = jnp.max(el, axis=1, keepdims=True)
    l1 = first_lane(el == e1_max)
    el2 = jnp.where(lane_f == l1, neg, el)
    e2_max = jnp.max(el2, axis=1, keepdims=True)
    l2 = first_lane(el2 == e2_max)
    z_sum = jnp.sum(jnp.exp(el - e1_max), axis=1, keepdims=True)
    p1 = 1.0 / z_sum
    p2 = jnp.exp(e2_max - e1_max) / z_sum
    w1 = p1 / (p1 + p2) * g_p
    w2 = p2 / (p1 + p2) * g_p
    e1 = l1 - N_GROUPS
    e2 = l2 - N_GROUPS

    hit1 = lane_f == e1
    hit2 = lane_f == e2
    onehot = jnp.where(jnp.logical_or(hit1, hit2), 1.0, 0.0)
    before = jnp.dot(tri_ref[...], onehot.astype(BF16), preferred_element_type=F32) + run_ref[...]
    rank1 = jnp.sum(jnp.where(hit1, before, 0.0), axis=1, keepdims=True)
    rank2 = jnp.sum(jnp.where(hit2, before, 0.0), axis=1, keepdims=True)
    run = run_ref[...] + jnp.sum(onehot, axis=0, keepdims=True)
    run_ref[...] = run

    ri_ref[...] = jnp.where(lane == 0, e1, jnp.where(lane == 1, e2, jnp.where(lane == 2, rank1,
                            jnp.where(lane == 3, rank2, 0.0)))).astype(jnp.int32)
    rw_ref[...] = jnp.where(lane == 0, w1, jnp.where(lane == 1, w2, 0.0))
    cnt_ref[...] = jnp.broadcast_to(run, cnt_ref.shape).astype(jnp.int32)


def _router(h, norm_w, shift, scale, w_router, b_router, tm=256):
    b, s, d = h.shape
    n = b * s
    tri = (lax.broadcasted_iota(jnp.int32, (tm, tm), 1) < lax.broadcasted_iota(jnp.int32, (tm, tm), 0)).astype(BF16)
    vec = pl.BlockSpec((None, 1, d), lambda bi, i: (bi, 0, 0))
    rows = lambda width: pl.BlockSpec((tm, width), lambda bi, i: (bi * (s // tm) + i, 0))
    return pl.pallas_call(
        _router_kernel,
        grid=(b, s // tm),
        in_specs=[pl.BlockSpec((None, tm, d), lambda bi, i: (bi, i, 0)),
                  pl.BlockSpec((1, d), lambda bi, i: (0, 0)), vec, vec,
                  pl.BlockSpec((d, LANES), lambda bi, i: (0, 0)),
                  pl.BlockSpec((1, LANES), lambda bi, i: (0, 0)),
                  pl.BlockSpec((tm, tm), lambda bi, i: (0, 0))],
        out_specs=[rows(d), rows(LANES), rows(LANES),
                   pl.BlockSpec((SUBLANES, LANES), lambda bi, i: (0, 0))],
        out_shape=[jax.ShapeDtypeStruct((n, d), F32), jax.ShapeDtypeStruct((n, LANES), jnp.int32),
                   jax.ShapeDtypeStruct((n, LANES), F32), jax.ShapeDtypeStruct((SUBLANES, LANES), jnp.int32)],
        scratch_shapes=[pltpu.VMEM((1, LANES), F32)],
        compiler_params=_cparams("arbitrary", "arbitrary"),
        name="router",
    )(h, norm_w.reshape(1, d), shift.reshape(b, 1, d), scale.reshape(b, 1, d), w_router, b_router, tri)


def _row_copy(src_hbm, src_row, dst_hbm, dst_row, sem):
    return pltpu.make_async_copy(src_hbm.at[pl.ds(src_row, 1)], dst_hbm.at[pl.ds(dst_row, 1)], sem)


def _dispatch_kernel(dest_ref, u_hbm, xb_in_hbm, xb_hbm, sem, *, tokens):
    del xb_in_hbm
    base = pl.program_id(0) * tokens

    def issue(t, carry):
        for k in range(TOP_K):
            _row_copy(u_hbm, base + t, xb_hbm, dest_ref[0, TOP_K * t + k], sem).start()
        return carry

    lax.fori_loop(0, tokens, issue, 0)

    def drain(t, carry):
        for _ in range(TOP_K):
            _row_copy(u_hbm, 0, xb_hbm, 0, sem).wait()
        return carry

    lax.fori_loop(0, tokens, drain, 0)


def _dispatch(u, dest, cap, tokens=512):
    n, d = u.shape
    steps = n // tokens
    kern = functools.partial(_dispatch_kernel, tokens=tokens)
    return pl.pallas_call(
        kern,
        grid=(steps,),
        in_specs=[pl.BlockSpec((None, 1, TOP_K * tokens), lambda i: (i, 0, 0), memory_space=pltpu.SMEM),
                  pl.BlockSpec(memory_space=pl.ANY),
                  pl.BlockSpec(memory_space=pl.ANY)],
        out_specs=pl.BlockSpec(memory_space=pl.ANY),
        out_shape=jax.ShapeDtypeStruct((cap, d), u.dtype),
        scratch_shapes=[pltpu.SemaphoreType.DMA(())],
        input_output_aliases={2: 0},
        compiler_params=_cparams("arbitrary"),
        name="dispatch",
    )(dest.reshape(steps, 1, TOP_K * tokens), u, jnp.zeros((cap, d), u.dtype))


def _e1_kernel(blk_ref, exp_ref, ft_ref, oblk_ref, oft_ref, ok_ref, x_ref, wg_ref, wu_ref, o_ref):
    del blk_ref, exp_ref, ft_ref, oblk_ref, oft_ref
    ok = ok_ref[pl.program_id(0)] == 1

    @pl.when(ok)
    def _():
        x = x_ref[...]
        g = jnp.dot(x, wg_ref[...], preferred_element_type=F32)
        up = jnp.dot(x, wu_ref[...], preferred_element_type=F32)
        o_ref[...] = (g * jax.nn.sigmoid(g)) * up

    @pl.when(jnp.logical_not(ok))
    def _():
        o_ref[...] = jnp.zeros_like(o_ref)


def _e2_kernel(blk_ref, exp_ref, oblk_ref, ok_ref, h_ref, wd_ref, o_ref):
    del blk_ref, exp_ref, oblk_ref
    ok = ok_ref[pl.program_id(0)] == 1

    @pl.when(ok)
    def _():
        o_ref[...] = jnp.dot(h_ref[...], wd_ref[...], preferred_element_type=F32)

    @pl.when(jnp.logical_not(ok))
    def _():
        o_ref[...] = jnp.zeros_like(o_ref)


def _experts(xb, blocks_per_expert, w_gate, w_up, w_down, bm=MOE_BLOCK, tf=512):
    cap, d = xb.shape
    n_exp, _, dff = w_gate.shape
    n_blocks = cap // bm
    n_ft = dff // tf
    nb = blocks_per_expert.astype(jnp.int32)
    bstart = jnp.cumsum(nb) - nb
    used = jnp.sum(nb)

    steps1 = n_blocks * n_ft
    step = jnp.arange(steps1, dtype=jnp.int32)
    ok1 = step < used * n_ft
    send = jnp.cumsum(nb * n_ft)
    s_idx = jnp.minimum(step, used * n_ft - 1)
    e_of = jnp.minimum(jnp.searchsorted(send, s_idx, side='right'), n_exp - 1).astype(jnp.int32)
    local = s_idx - (send[e_of] - nb[e_of] * n_ft)
    nbe = jnp.maximum(nb[e_of], 1)
    ft_of = (local // nbe).astype(jnp.int32)
    blk_of = (bstart[e_of] + local % nbe).astype(jnp.int32)
    spare = step - used * n_ft
    oblk_of = jnp.where(ok1, blk_of, used + spare // n_ft).astype(jnp.int32)
    oft_of = jnp.where(ok1, ft_of, spare % n_ft).astype(jnp.int32)

    hmid = pl.pallas_call(
        _e1_kernel,
        grid_spec=pltpu.PrefetchScalarGridSpec(
            num_scalar_prefetch=6,
            grid=(steps1,),
            in_specs=[pl.BlockSpec((bm, d), lambda s, blk, ex, ft, oblk, oft, ok: (blk[s], 0)),
                      pl.BlockSpec((None, d, tf), lambda s, blk, ex, ft, oblk, oft, ok: (ex[s], 0, ft[s])),
                      pl.BlockSpec((None, d, tf), lambda s, blk, ex, ft, oblk, oft, ok: (ex[s], 0, ft[s]))],
            out_specs=pl.BlockSpec((bm, tf), lambda s, blk, ex, ft, oblk, oft, ok: (oblk[s], oft[s]))),
        out_shape=jax.ShapeDtypeStruct((cap, dff), F32),
        compiler_params=_cparams("arbitrary"),
        name="e1",
    )(blk_of, e_of, ft_of, oblk_of, oft_of, ok1.astype(jnp.int32), xb, w_gate, w_up)

    blk = jnp.arange(n_blocks, dtype=jnp.int32)
    b_idx = jnp.minimum(blk, used - 1)
    bend = jnp.cumsum(nb)
    e2_of = jnp.minimum(jnp.searchsorted(bend, b_idx, side='right'), n_exp - 1).astype(jnp.int32)
    ok2 = (blk < used).astype(jnp.int32)
    yb = pl.pallas_call(
        _e2_kernel,
        grid_spec=pltpu.PrefetchScalarGridSpec(
            num_scalar_prefetch=4,
            grid=(n_blocks,),
            in_specs=[pl.BlockSpec((bm, dff), lambda s, hb, ex, ob, ok: (hb[s], 0)),
                      pl.BlockSpec((None, dff, d), lambda s, hb, ex, ob, ok: (ex[s], 0, 0))],
            out_specs=pl.BlockSpec((bm, d), lambda s, hb, ex, ob, ok: (ob[s], 0))),
        out_shape=jax.ShapeDtypeStruct((cap, d), F32),
        compiler_params=_cparams("arbitrary"),
        name="e2",
    )(b_idx, e2_of, blk, ok2, hmid, w_down)
    return yb


def _combine_kernel(dest_ref, yb_hbm, h_ref, rw_ref, g_ref, nw_ref, sh_ref, sc_ref, o_ref, rows, sem, *, tokens):
    def issue(t, carry):
        for k in range(TOP_K):
            pltpu.make_async_copy(yb_hbm.at[pl.ds(dest_ref[0, TOP_K * t + k], 1)],
                                  rows.at[k, pl.ds(t, 1)], sem).start()
        return carry

    lax.fori_loop(0, tokens, issue, 0)

    def drain(t, carry):
        for k in range(TOP_K):
            pltpu.make_async_copy(yb_hbm.at[pl.ds(0, 1)], rows.at[k, pl.ds(0, 1)], sem).wait()
        return carry

    lax.fori_loop(0, tokens, drain, 0)

    rw = rw_ref[...]
    ffn = rows[0] * rw[:, 0:1]
    for k in range(1, TOP_K):
        ffn = ffn + rows[k] * rw[:, k:k + 1]
    h2 = h_ref[...] + g_ref[...] * ffn
    o_ref[...] = _rms_mod(h2, nw_ref[...], sh_ref[...], sc_ref[...])


def _combine(yb, dest, h, route_w, gate, norm_w, shift, scale, tokens=256):
    b, s, d = h.shape
    n = b * s
    per_seq = s // tokens
    kern = functools.partial(_combine_kernel, tokens=tokens)
    vec = pl.BlockSpec((None, 1, d), lambda i: (i // per_seq, 0, 0))
    out = pl.pallas_call(
        kern,
        grid=(n // tokens,),
        in_specs=[pl.BlockSpec((None, 1, TOP_K * tokens), lambda i: (i, 0, 0), memory_space=pltpu.SMEM),
                  pl.BlockSpec(memory_space=pl.ANY),
                  pl.BlockSpec((tokens, d), lambda i: (i, 0)),
                  pl.BlockSpec((tokens, LANES), lambda i: (i, 0)),
                  vec,
                  pl.BlockSpec((1, d), lambda i: (0, 0)), vec, vec],
        out_specs=pl.BlockSpec((tokens, d), lambda i: (i, 0)),
        out_shape=jax.ShapeDtypeStruct((n, d), F32),
        scratch_shapes=[pltpu.VMEM((TOP_K, tokens, d), F32), pltpu.SemaphoreType.DMA(())],
        compiler_params=_cparams("arbitrary"),
        name="combine",
    )(dest.reshape(n // tokens, 1, TOP_K * tokens), yb, h.reshape(n, d), route_w,
      gate.reshape(b, 1, d), norm_w.reshape(1, d), shift.reshape(b, 1, d), scale.reshape(b, 1, d))
    return out.reshape(b, s, d)


def _layer(h, mod, norm1_w, w_in, conv_w, ml_gate_bias, ml_norm_w, w_out, norm2_w,
           w_router_group, b_router_group, w_router_expert, b_router_expert,
           w_exp_gate, w_exp_up, w_exp_down, final):
    b, s, d = h.shape
    n = b * s
    sh1, sc1, g1, sh2, sc2, g2 = jnp.split(mod, 6, axis=-1)
    sb_w = SB_HEADS * SB_HEAD_DIM
    ml_w = ML_HEADS * ML_HEAD_DIM

    u = _norm_mod(h, norm1_w, sh1, sc1, BF16).reshape(n, d)
    w_in16 = w_in.astype(BF16)
    tm = 1024 if n % 1024 == 0 else 512
    sb_qkv = _matmul(u, w_in16, 0, 3 * sb_w, BF16, tm, 512)
    ml_qk = _matmul(u, w_in16, 3 * sb_w, 2 * ml_w, F32, tm, 512)
    ml_v = _matmul(u, w_in16, 3 * sb_w + 2 * ml_w, ml_w, BF16, tm, 512)
    ml_o = _matmul(u, w_in16, 3 * sb_w + 3 * ml_w, ml_w, F32, tm, 512)
    n_gate = 2 * ML_HEADS
    w_gate = jnp.pad(w_in16[:, 3 * sb_w + 4 * ml_w:], ((0, 0), (0, LANES - n_gate)))
    b_gate = jnp.pad(ml_gate_bias, (0, LANES - n_gate)).reshape(1, LANES)
    gates = _matmul(u, w_gate, 0, LANES, F32, tm, LANES, bias=b_gate)[:, :n_gate]

    y_sb = _sb_attention(sb_qkv.reshape(b, s, 3 * sb_w))
    y_ml = _mlstm(ml_qk.reshape(b, s, 2 * ml_w), ml_v.reshape(b, s, ml_w), ml_o.reshape(b, s, ml_w),
                  gates.reshape(b, s, n_gate), conv_w, ml_norm_w)
    h = _out_proj(y_sb, y_ml, w_out.astype(BF16), h, g1, tm=min(1024, s))

    w_router = jnp.pad(jnp.concatenate([w_router_group, w_router_expert], axis=1),
                       ((0, 0), (0, LANES - N_GROUPS - N_EXPERTS)))
    b_router = jnp.pad(jnp.concatenate([b_router_group, b_router_expert]),
                       (0, LANES - N_GROUPS - N_EXPERTS)).reshape(1, LANES)
    u2, route_i, route_w, counts = _router(h, norm2_w, sh2, sc2, w_router, b_router)
    counts = counts[0, :N_EXPERTS]
    blocks_per_expert = (counts + MOE_BLOCK - 1) // MOE_BLOCK
    padded = blocks_per_expert * MOE_BLOCK
    pstarts = jnp.cumsum(padded) - padded
    dest = pstarts[route_i[:, 0:TOP_K]] + route_i[:, TOP_K:2 * TOP_K]
    cap = (-(-(n * TOP_K) // MOE_BLOCK) + N_EXPERTS) * MOE_BLOCK
    xb = _dispatch(u2, dest, cap)
    yb = _experts(xb, blocks_per_expert, w_exp_gate, w_exp_up, w_exp_down)
    norm_w, shift, scale = final
    return _combine(yb, dest, h, route_w, g2, norm_w, shift, scale)


def kernel(x, c, norm1_w, w_in, conv_w, ml_gate_bias, ml_norm_w, w_out, norm2_w, w_router_group, b_router_group, w_router_expert, b_router_expert, w_exp_gate, w_exp_up, w_exp_down, w_ada, b_ada, final_norm_w, w_ada_final, b_ada_final):
    b, s, d = x.shape
    depth = w_in.shape[0]
    assert depth == 1, "the final norm is fused into the (single) layer's combine step"
    c_rep = jnp.broadcast_to(c[:, :, None], (b, d, LANES))
    fmod = _ada(c_rep, w_ada_final, b_ada_final)
    sh_f, sc_f = jnp.split(fmod, 2, axis=-1)
    mod = _ada(c_rep, w_ada[0], b_ada[0])
    return _layer(x, mod, norm1_w[0], w_in[0], conv_w[0], ml_gate_bias[0], ml_norm_w[0], w_out[0], norm2_w[0],
                  w_router_group[0], b_router_group[0], w_router_expert[0], b_router_expert[0],
                  w_exp_gate[0], w_exp_up[0], w_exp_down[0], (final_norm_w, sh_f, sc_f))
```

```python
import functools
import math

import jax
import jax.numpy as jnp
from jax import lax
from jax.experimental import pallas as pl
from jax.experimental.pallas import tpu as pltpu

SB_HEADS = 16
SB_HEAD_DIM = 128
ML_HEADS = 4
ML_HEAD_DIM = 512
CONV_WIDTH = 4
N_GROUPS = 4
EXPERTS_PER_GROUP = 8
N_EXPERTS = N_GROUPS * EXPERTS_PER_GROUP
TOP_K = 2
NORM_EPS = 1e-6

LANES = 128
SUBLANES = 8
VMEM_LIMIT = 56 * 1024 * 1024

ATTN_BLOCK = 256
ML_CHUNK = 256
MOE_BLOCK = 256
UNDERFLOW_LOG = -90.0

F32 = jnp.float32
BF16 = jnp.bfloat16


def _cparams(*sem):
    return pltpu.CompilerParams(dimension_semantics=sem, vmem_limit_bytes=VMEM_LIMIT)


def _split_bf16(x):
    hi = x.astype(BF16)
    lo = (x - hi.astype(F32)).astype(BF16)
    return hi, lo


def _log_sigmoid(x):
    return jnp.minimum(x, 0.0) - jnp.log1p(jnp.exp(-jnp.abs(x)))


def _ada_kernel(c_ref, w_ref, b_ref, o_ref):
    nb, tn = o_ref.shape
    rows = []
    for b in range(nb):
        cb = c_ref[b]
        cb = cb * jax.nn.sigmoid(cb)
        pieces = [jnp.sum(w_ref[:, j * LANES:(j + 1) * LANES] * cb, axis=0, keepdims=True)
                  for j in range(tn // LANES)]
        rows.append(jnp.concatenate(pieces, axis=1))
    o_ref[...] = jnp.concatenate(rows, axis=0) + b_ref[...]


def _ada(c_rep, w, bias, tn=512):
    nb, k, _ = c_rep.shape
    n_out = w.shape[1]
    return pl.pallas_call(
        _ada_kernel,
        grid=(n_out // tn,),
        in_specs=[pl.BlockSpec((nb, k, LANES), lambda j: (0, 0, 0)),
                  pl.BlockSpec((k, tn), lambda j: (0, j)),
                  pl.BlockSpec((1, tn), lambda j: (0, j))],
        out_specs=pl.BlockSpec((nb, tn), lambda j: (0, j)),
        out_shape=jax.ShapeDtypeStruct((nb, n_out), F32),
        compiler_params=_cparams("arbitrary"),
        name="ada",
    )(c_rep, w, bias.reshape(1, n_out))


def _rms_mod(x, w, shift, scale):
    var = jnp.mean(x * x, axis=-1, keepdims=True)
    y = x * lax.rsqrt(var + NORM_EPS) * w
    return y * (1.0 + scale) + shift


def _norm_mod_kernel(x_ref, w_ref, sh_ref, sc_ref, o_ref):
    o_ref[...] = _rms_mod(x_ref[...], w_ref[...], sh_ref[...], sc_ref[...]).astype(o_ref.dtype)


def _norm_mod(x, w, shift, scale, out_dtype, tm=256):
    b, s, d = x.shape
    vec = pl.BlockSpec((None, 1, d), lambda bi, i: (bi, 0, 0))
    return pl.pallas_call(
        _norm_mod_kernel,
        grid=(b, s // tm),
        in_specs=[pl.BlockSpec((None, tm, d), lambda bi, i: (bi, i, 0)),
                  pl.BlockSpec((1, d), lambda bi, i: (0, 0)), vec, vec],
        out_specs=pl.BlockSpec((None, tm, d), lambda bi, i: (bi, i, 0)),
        out_shape=jax.ShapeDtypeStruct((b, s, d), out_dtype),
        compiler_params=_cparams("arbitrary", "arbitrary"),
        name="norm_mod",
    )(x, w.reshape(1, d), shift.reshape(b, 1, d), scale.reshape(b, 1, d))


def _matmul_kernel(a_ref, w_ref, o_ref):
    o_ref[...] = jnp.dot(a_ref[...], w_ref[...], preferred_element_type=F32).astype(o_ref.dtype)


def _matmul_bias_kernel(a_ref, w_ref, b_ref, o_ref):
    o_ref[...] = (jnp.dot(a_ref[...], w_ref[...], preferred_element_type=F32) + b_ref[...]).astype(o_ref.dtype)


def _matmul(a, w, col0, ncols, out_dtype, tm, tn, bias=None):
    m, k = a.shape
    j0 = col0 // tn
    in_specs = [pl.BlockSpec((tm, k), lambda i, j: (i, 0)),
                pl.BlockSpec((k, tn), lambda i, j: (0, j + j0))]
    args = [a, w]
    kern = _matmul_kernel
    if bias is not None:
        in_specs.append(pl.BlockSpec((1, tn), lambda i, j: (0, j + j0)))
        args.append(bias)
        kern = _matmul_bias_kernel
    return pl.pallas_call(
        kern,
        grid=(m // tm, ncols // tn),
        in_specs=in_specs,
        out_specs=pl.BlockSpec((tm, tn), lambda i, j: (i, j)),
        out_shape=jax.ShapeDtypeStruct((m, ncols), out_dtype),
        compiler_params=_cparams("arbitrary", "arbitrary"),
        name="proj",
    )(*args)


def _sb_attn_kernel(q_ref, k_ref, v_ref, tri_ref, o_ref, acc_ref, r_ref, *, blk, scale):
    qi = pl.program_id(2)
    q = q_ref[...]
    tri = tri_ref[...]

    def visit(j, diag):
        start = pl.multiple_of(j * blk, blk)
        kb = k_ref[pl.ds(start, blk), :]
        vb = v_ref[pl.ds(start, blk), :]
        z = lax.dot_general(q, kb, (((1,), (1,)), ((), ())), preferred_element_type=F32) * scale
        log_keep = -(jnp.maximum(z, 0.0) + jnp.log(1.0 + jnp.exp(-jnp.abs(z))))
        log_beta = log_keep + z
        if diag:
            row = lax.broadcasted_iota(jnp.int32, (blk, blk), 0)
            col = lax.broadcasted_iota(jnp.int32, (blk, blk), 1)
            causal = col < row
            log_keep = jnp.where(causal, log_keep, 0.0)
        hi, lo = _split_bf16(log_keep)
        between = (jnp.dot(hi, tri, preferred_element_type=F32)
                   + jnp.dot(lo, tri, preferred_element_type=F32))
        r = r_ref[...]
        r_wide = jnp.concatenate([r] * (blk // LANES), axis=1)
        w = jnp.exp(log_beta + between + r_wide)
        if diag:
            w = jnp.where(causal, w, 0.0)
        acc_ref[...] += jnp.dot(w.astype(BF16), vb, preferred_element_type=F32)
        r_new = r + jnp.sum(log_keep, axis=1, keepdims=True)
        r_ref[...] = r_new
        return jnp.max(r_new)

    acc_ref[...] = jnp.zeros_like(acc_ref)
    r_ref[...] = jnp.zeros_like(r_ref)
    r_max = visit(qi, True)

    def cond(carry):
        j, r_max = carry
        return jnp.logical_and(j >= 0, r_max > UNDERFLOW_LOG)

    def body(carry):
        j, _ = carry
        return j - 1, visit(j, False)

    lax.while_loop(cond, body, (qi - 1, r_max))
    o_ref[...] = acc_ref[...].astype(o_ref.dtype)


def _sb_attention(qkv, blk=ATTN_BLOCK):
    b, s, _ = qkv.shape
    h, d = SB_HEADS, SB_HEAD_DIM
    idx = lax.broadcasted_iota(jnp.int32, (blk, blk), 0) > lax.broadcasted_iota(jnp.int32, (blk, blk), 1)
    tri = idx.astype(BF16)
    kern = functools.partial(_sb_attn_kernel, blk=blk, scale=1.0 / math.sqrt(d))
    return pl.pallas_call(
        kern,
        grid=(b, h, s // blk),
        in_specs=[pl.BlockSpec((None, blk, d), lambda bi, hi, qi: (bi, qi, hi)),
                  pl.BlockSpec((None, s, d), lambda bi, hi, qi: (bi, 0, h + hi)),
                  pl.BlockSpec((None, s, d), lambda bi, hi, qi: (bi, 0, 2 * h + hi)),
                  pl.BlockSpec((blk, blk), lambda bi, hi, qi: (0, 0))],
        out_specs=pl.BlockSpec((None, blk, d), lambda bi, hi, qi: (bi, qi, hi)),
        out_shape=jax.ShapeDtypeStruct((b, s, h * d), BF16),
        scratch_shapes=[pltpu.VMEM((blk, d), F32), pltpu.VMEM((blk, LANES), F32)],
        compiler_params=_cparams("arbitrary", "arbitrary", "arbitrary"),
        name="sb_attn",
    )(qkv, qkv, qkv, tri)


def _mlstm_kernel(qp_ref, kp_ref, v_ref, og_ref, gcol_ref, grow_ref, cwq_ref, cwk_ref, nw_ref,
                  tril_ref, triu_ref, y_ref, c_st, n_st, m_st, qbuf, kbuf, *, chunk, dim):
    head = pl.program_id(1)
    ci = pl.program_id(2)
    halo = SUBLANES

    @pl.when(ci == 0)
    def _():
        c_st[...] = jnp.zeros_like(c_st)
        n_st[...] = jnp.zeros_like(n_st)
        m_st[...] = jnp.zeros_like(m_st)
        qbuf[0:halo, :] = jnp.zeros((halo, dim), F32)
        kbuf[0:halo, :] = jnp.zeros((halo, dim), F32)

    qbuf[halo:halo + chunk, :] = qp_ref[...]
    kbuf[halo:halo + chunk, :] = kp_ref[...]

    def conv_silu(buf, cw_ref):
        acc = buf[halo:halo + chunk, :] * cw_ref[CONV_WIDTH - 1:CONV_WIDTH, :]
        for j in range(CONV_WIDTH - 1):
            off = halo - (CONV_WIDTH - 1) + j
            acc = acc + buf[off:off + chunk, :] * cw_ref[j:j + 1, :]
        return acc * jax.nn.sigmoid(acc)

    q = conv_silu(qbuf, cwq_ref)
    k = conv_silu(kbuf, cwk_ref) * (1.0 / math.sqrt(dim))
    qbuf[0:halo, :] = qbuf[chunk:chunk + halo, :]
    kbuf[0:halo, :] = kbuf[chunk:chunk + halo, :]

    gcol = gcol_ref[...]
    lane = lax.broadcasted_iota(jnp.int32, gcol.shape, 1)
    li_col = jnp.sum(jnp.where(lane == head, gcol, 0.0), axis=1, keepdims=True)
    lf_col = _log_sigmoid(jnp.sum(jnp.where(lane == ML_HEADS + head, gcol, 0.0), axis=1, keepdims=True))
    grow = grow_ref[...]
    sub = lax.broadcasted_iota(jnp.int32, grow.shape, 0)
    li_row = jnp.sum(jnp.where(sub == head, grow, 0.0), axis=0, keepdims=True)
    lf_row = _log_sigmoid(jnp.sum(jnp.where(sub == ML_HEADS + head, grow, 0.0), axis=0, keepdims=True))

    hi, lo = _split_bf16(jnp.broadcast_to(lf_col, (chunk, LANES)))
    b_col = (jnp.dot(tril_ref[...], hi, preferred_element_type=F32)
             + jnp.dot(tril_ref[...], lo, preferred_element_type=F32))[:, 0:1]
    hi, lo = _split_bf16(jnp.broadcast_to(lf_row, (2 * SUBLANES, chunk)))
    b_row = (jnp.dot(hi, triu_ref[...], preferred_element_type=F32)
             + jnp.dot(lo, triu_ref[...], preferred_element_type=F32))[0:1, :]

    m_prev = m_st[...]
    a_col = b_col + m_prev
    row_t = lax.broadcasted_iota(jnp.int32, (chunk, chunk), 0)
    col_s = lax.broadcasted_iota(jnp.int32, (chunk, chunk), 1)
    d_mat = jnp.where(col_s <= row_t, b_col - b_row + li_row, -jnp.inf)
    m_col = jnp.maximum(a_col, jnp.max(d_mat, axis=1, keepdims=True))
    w_intra = jnp.exp(d_mat - m_col)
    w_inter = jnp.exp(a_col - m_col)

    qb = q.astype(BF16)
    kb = k.astype(BF16)
    vb = v_ref[...]
    sc = lax.dot_general(qb, kb, (((1,), (1,)), ((), ())), preferred_element_type=F32) * w_intra
    num = (w_inter * jnp.dot(qb, c_st[...].astype(BF16), preferred_element_type=F32)
           + jnp.dot(sc.astype(BF16), vb, preferred_element_type=F32))
    den = (w_inter * jnp.sum(q * n_st[...], axis=1, keepdims=True)
           + jnp.sum(sc, axis=1, keepdims=True))
    h_out = num / jnp.maximum(jnp.abs(den), jnp.exp(-m_col))

    b_last = b_col[chunk - 1:chunk, :]
    g_col = b_last - b_col + li_col
    m_new = jnp.maximum(b_last + m_prev, jnp.max(g_col, axis=0, keepdims=True))
    decay = jnp.exp(b_last + m_prev - m_new)
    kw = k * jnp.exp(g_col - m_new)
    c_st[...] = decay * c_st[...] + lax.dot_general(kw.astype(BF16), vb, (((0,), (0,)), ((), ())),
                                                    preferred_element_type=F32)
    n_st[...] = decay * n_st[...] + jnp.sum(kw, axis=0, keepdims=True)
    m_st[...] = m_new

    hn = h_out * lax.rsqrt(jnp.mean(h_out * h_out, axis=1, keepdims=True) + NORM_EPS) * nw_ref[...]
    y_ref[...] = (jax.nn.sigmoid(og_ref[...]) * hn).astype(y_ref.dtype)


def _mlstm(ml_qk, ml_v, ml_o, gates, conv_w, ml_norm_w, chunk=ML_CHUNK):
    b, s, _ = ml_v.shape
    h, d = ML_HEADS, ML_HEAD_DIM
    gates_t = jnp.swapaxes(gates, 1, 2)
    r = lax.broadcasted_iota(jnp.int32, (chunk, chunk), 0)
    c = lax.broadcasted_iota(jnp.int32, (chunk, chunk), 1)
    tril = (c <= r).astype(BF16)
    triu = (r <= c).astype(BF16)
    head_blk = lambda off: pl.BlockSpec((None, chunk, d), lambda bi, hi, ci: (bi, ci, hi + off))
    kern = functools.partial(_mlstm_kernel, chunk=chunk, dim=d)
    return pl.pallas_call(
        kern,
        grid=(b, h, s // chunk),
        in_specs=[head_blk(0), head_blk(h), head_blk(0), head_blk(0),
                  pl.BlockSpec((None, chunk, 2 * h), lambda bi, hi, ci: (bi, ci, 0)),
                  pl.BlockSpec((None, 2 * h, chunk), lambda bi, hi, ci: (bi, 0, ci)),
                  pl.BlockSpec((CONV_WIDTH, d), lambda bi, hi, ci: (0, hi)),
                  pl.BlockSpec((CONV_WIDTH, d), lambda bi, hi, ci: (0, hi + h)),
                  pl.BlockSpec((1, d), lambda bi, hi, ci: (0, hi)),
                  pl.BlockSpec((chunk, chunk), lambda bi, hi, ci: (0, 0)),
                  pl.BlockSpec((chunk, chunk), lambda bi, hi, ci: (0, 0))],
        out_specs=head_blk(0),
        out_shape=jax.ShapeDtypeStruct((b, s, h * d), BF16),
        scratch_shapes=[pltpu.VMEM((d, d), F32), pltpu.VMEM((1, d), F32), pltpu.VMEM((1, 1), F32),
                        pltpu.VMEM((chunk + SUBLANES, d), F32), pltpu.VMEM((chunk + SUBLANES, d), F32)],
        compiler_params=_cparams("arbitrary", "arbitrary", "arbitrary"),
        name="mlstm",
    )(ml_qk, ml_qk, ml_v, ml_o, gates, gates_t, conv_w, conv_w, ml_norm_w.reshape(1, h * d), tril, triu)


def _out_proj_kernel(ya_ref, yb_ref, wa_ref, wb_ref, x_ref, g_ref, o_ref):
    mix = (jnp.dot(ya_ref[...], wa_ref[...], preferred_element_type=F32)
           + jnp.dot(yb_ref[...], wb_ref[...], preferred_element_type=F32))
    o_ref[...] = x_ref[...] + g_ref[...] * mix


def _out_proj(y_sb, y_ml, w_out, x, gate, tm, tn=512):
    b, s, d = x.shape
    ka, kb = y_sb.shape[-1], y_ml.shape[-1]
    assert ka == kb
    n = b * s
    rows_per_seq = s // tm
    out = pl.pallas_call(
        _out_proj_kernel,
        grid=(n // tm, d // tn),
        in_specs=[pl.BlockSpec((tm, ka), lambda i, j: (i, 0)),
                  pl.BlockSpec((tm, kb), lambda i, j: (i, 0)),
                  pl.BlockSpec((ka, tn), lambda i, j: (0, j)),
                  pl.BlockSpec((kb, tn), lambda i, j: (1, j)),
                  pl.BlockSpec((tm, tn), lambda i, j: (i, j)),
                  pl.BlockSpec((None, 1, tn), lambda i, j: (i // rows_per_seq, 0, j))],
        out_specs=pl.BlockSpec((tm, tn), lambda i, j: (i, j)),
        out_shape=jax.ShapeDtypeStruct((n, d), F32),
        compiler_params=_cparams("arbitrary", "arbitrary"),
        name="out_proj",
    )(y_sb.reshape(n, ka), y_ml.reshape(n, kb), w_out, w_out, x.reshape(n, d), gate.reshape(b, 1, d))
    return out.reshape(b, s, d)


def _router_kernel(h_ref, nw_ref, sh_ref, sc_ref, wr_ref, br_ref, tri_ref, u_ref, ri_ref, rw_ref, cnt_ref, run_ref):
    first = jnp.logical_and(pl.program_id(0) == 0, pl.program_id(1) == 0)

    @pl.when(first)
    def _():
        run_ref[...] = jnp.zeros_like(run_ref)

    u = _rms_mod(h_ref[...], nw_ref[...], sh_ref[...], sc_ref[...])
    u_ref[...] = u
    tm = u.shape[0]
    uh, ul = _split_bf16(u)
    wh, wl = _split_bf16(wr_ref[...])
    logits = (jnp.dot(uh, wh, preferred_element_type=F32) + jnp.dot(ul, wh, preferred_element_type=F32)
              + jnp.dot(uh, wl, preferred_element_type=F32) + br_ref[...])

    lane = lax.broadcasted_iota(jnp.int32, (tm, LANES), 1)
    lane_f = lane.astype(F32)
    neg = -jnp.inf
    first_lane = lambda hit: jnp.min(jnp.where(hit, lane_f, float(LANES)), axis=1, keepdims=True)

    gl = jnp.where(lane < N_GROUPS, logits, neg)
    g_max = jnp.max(gl, axis=1, keepdims=True)
    g_p = 1.0 / jnp.sum(jnp.exp(gl - g_max), axis=1, keepdims=True)
    g_sel = first_lane(gl == g_max)

    lo_lane = N_GROUPS + EXPERTS_PER_GROUP * g_sel
    el = jnp.where(jnp.logical_and(lane_f >= lo_lane, lane_f < lo_lane + EXPERTS_PER_GROUP), logits, neg)
    e1_max = jnp.max(el, axis=1, keepdims=True)
    l1 = first_lane(el == e1_max)
    el2 = jnp.where(lane_f == l1, neg, el)
    e2_max = jnp.max(el2, axis=1, keepdims=True)
    l2 = first_lane(el2 == e2_max)
    z_sum = jnp.sum(jnp.exp(el - e1_max), axis=1, keepdims=True)
    p1 = 1.0 / z_sum
    p2 = jnp.exp(e2_max - e1_max) / z_sum
    w1 = p1 / (p1 + p2) * g_p
    w2 = p2 / (p1 + p2) * g_p
    e1 = l1 - N_GROUPS
    e2 = l2 - N_GROUPS

    hit1 = lane_f == e1
    hit2 = lane_f == e2
    onehot = jnp.where(jnp.logical_or(hit1, hit2), 1.0, 0.0)
    before = jnp.dot(tri_ref[...], onehot.astype(BF16), preferred_element_type=F32) + run_ref[...]
    rank1 = jnp.sum(jnp.where(hit1, before, 0.0), axis=1, keepdims=True)
    rank2 = jnp.sum(jnp.where(hit2, before, 0.0), axis=1, keepdims=True)
    run = run_ref[...] + jnp.sum(onehot, axis=0, keepdims=True)
    run_ref[...] = run

    ri_ref[...] = jnp.where(lane == 0, e1, jnp.where(lane == 1, e2, jnp.where(lane == 2, rank1,
                            jnp.where(lane == 3, rank2, 0.0)))).astype(jnp.int32)
    rw_ref[...] = jnp.where(lane == 0, w1, jnp.where(lane == 1, w2, 0.0))
    cnt_ref[...] = jnp.broadcast_to(run, cnt_ref.shape).astype(jnp.int32)


def _router(h, norm_w, shift, scale, w_router, b_router, tm=256):
    b, s, d = h.shape
    n = b * s
    tri = (lax.broadcasted_iota(jnp.int32, (tm, tm), 1) < lax.broadcasted_iota(jnp.int32, (tm, tm), 0)).astype(BF16)
    vec = pl.BlockSpec((None, 1, d), lambda bi, i: (bi, 0, 0))
    rows = lambda width: pl.BlockSpec((tm, width), lambda bi, i: (bi * (s // tm) + i, 0))
    return pl.pallas_call(
        _router_kernel,
        grid=(b, s // tm),
        in_specs=[pl.BlockSpec((None, tm, d), lambda bi, i: (bi, i, 0)),
                  pl.BlockSpec((1, d), lambda bi, i: (0, 0)), vec, vec,
                  pl.BlockSpec((d, LANES), lambda bi, i: (0, 0)),
                  pl.BlockSpec((1, LANES), lambda bi, i: (0, 0)),
                  pl.BlockSpec((tm, tm), lambda bi, i: (0, 0))],
        out_specs=[rows(d), rows(LANES), rows(LANES),
                   pl.BlockSpec((SUBLANES, LANES), lambda bi, i: (0, 0))],
        out_shape=[jax.ShapeDtypeStruct((n, d), F32), jax.ShapeDtypeStruct((n, LANES), jnp.int32),
                   jax.ShapeDtypeStruct((n, LANES), F32), jax.ShapeDtypeStruct((SUBLANES, LANES), jnp.int32)],
        scratch_shapes=[pltpu.VMEM((1, LANES), F32)],
        compiler_params=_cparams("arbitrary", "arbitrary"),
        name="router",
    )(h, norm_w.reshape(1, d), shift.reshape(b, 1, d), scale.reshape(b, 1, d), w_router, b_router, tri)


ZERO_ROWS = 128


def _dispatch_kernel(tail_ref, dest_ref, u_ref, xb_hbm, zbuf, zsem, sem, *, tokens, block):
    n_exp = tail_ref.shape[0]

    def clear(e, half):
        row = pl.multiple_of(tail_ref[e] + half * ZERO_ROWS, ZERO_ROWS)
        return pltpu.make_async_copy(zbuf, xb_hbm.at[pl.ds(row, ZERO_ROWS)], zsem)

    @pl.when(pl.program_id(0) == 0)
    def _():
        zbuf[...] = jnp.zeros_like(zbuf)
        for start in (True, False):
            def each(e, carry):
                @pl.when(tail_ref[e] >= 0)
                def _():
                    for half in range(block // ZERO_ROWS):
                        clear(e, half).start() if start else clear(e, half).wait()
                return carry
            lax.fori_loop(0, n_exp, each, 0)

    def row_copy(t, k):
        return pltpu.make_async_copy(u_ref.at[pl.ds(t, 1)], xb_hbm.at[pl.ds(dest_ref[0, TOP_K * t + k], 1)], sem)

    def issue(t, carry):
        for k in range(TOP_K):
            row_copy(t, k).start()
        return carry

    lax.fori_loop(0, tokens, issue, 0)

    def drain(t, carry):
        for k in range(TOP_K):
            row_copy(t, k).wait()
        return carry

    lax.fori_loop(0, tokens, drain, 0)


def _dispatch(u, dest, tail_block_row, cap, tokens=256, block=MOE_BLOCK):
    n, d = u.shape
    steps = n // tokens
    kern = functools.partial(_dispatch_kernel, tokens=tokens, block=block)
    return pl.pallas_call(
        kern,
        grid_spec=pltpu.PrefetchScalarGridSpec(
            num_scalar_prefetch=1,
            grid=(steps,),
            in_specs=[pl.BlockSpec((None, 1, TOP_K * tokens), lambda i, tail: (i, 0, 0), memory_space=pltpu.SMEM),
                      pl.BlockSpec((tokens, d), lambda i, tail: (i, 0))],
            out_specs=pl.BlockSpec(memory_space=pl.ANY),
            scratch_shapes=[pltpu.VMEM((ZERO_ROWS, d), u.dtype), pltpu.SemaphoreType.DMA(()),
                            pltpu.SemaphoreType.DMA(())]),
        out_shape=jax.ShapeDtypeStruct((cap, d), u.dtype),
        compiler_params=_cparams("arbitrary"),
        name="dispatch",
    )(tail_block_row, dest.reshape(steps, 1, TOP_K * tokens), u)


def _e1_kernel(blk_ref, exp_ref, ft_ref, oblk_ref, oft_ref, ok_ref, x_ref, wg_ref, wu_ref, o_ref):
    del blk_ref, exp_ref, ft_ref, oblk_ref, oft_ref
    ok = ok_ref[pl.program_id(0)] == 1

    @pl.when(ok)
    def _():
        x = x_ref[...]
        g = jnp.dot(x, wg_ref[...], preferred_element_type=F32)
        up = jnp.dot(x, wu_ref[...], preferred_element_type=F32)
        o_ref[...] = (g * jax.nn.sigmoid(g)) * up

    @pl.when(jnp.logical_not(ok))
    def _():
        o_ref[...] = jnp.zeros_like(o_ref)


def _e2_kernel(blk_ref, exp_ref, oblk_ref, ok_ref, h_ref, wd_ref, o_ref):
    del blk_ref, exp_ref, oblk_ref
    ok = ok_ref[pl.program_id(0)] == 1

    @pl.when(ok)
    def _():
        o_ref[...] = jnp.dot(h_ref[...], wd_ref[...], preferred_element_type=F32)

    @pl.when(jnp.logical_not(ok))
    def _():
        o_ref[...] = jnp.zeros_like(o_ref)


def _experts(xb, blocks_per_expert, w_gate, w_up, w_down, bm=MOE_BLOCK, tf=512):
    cap, d = xb.shape
    n_exp, _, dff = w_gate.shape
    n_blocks = cap // bm
    n_ft = dff // tf
    nb = blocks_per_expert.astype(jnp.int32)
    bstart = jnp.cumsum(nb) - nb
    used = jnp.sum(nb)

    steps1 = n_blocks * n_ft
    step = jnp.arange(steps1, dtype=jnp.int32)
    ok1 = step < used * n_ft
    send = jnp.cumsum(nb * n_ft)
    s_idx = jnp.minimum(step, used * n_ft - 1)
    e_of = jnp.minimum(jnp.sum(s_idx[:, None] >= send[None, :], axis=1), n_exp - 1).astype(jnp.int32)
    local = s_idx - (send[e_of] - nb[e_of] * n_ft)
    nbe = jnp.maximum(nb[e_of], 1)
    ft_of = (local // nbe).astype(jnp.int32)
    blk_of = (bstart[e_of] + local % nbe).astype(jnp.int32)
    spare = step - used * n_ft
    oblk_of = jnp.where(ok1, blk_of, used + spare // n_ft).astype(jnp.int32)
    oft_of = jnp.where(ok1, ft_of, spare % n_ft).astype(jnp.int32)

    hmid = pl.pallas_call(
        _e1_kernel,
        grid_spec=pltpu.PrefetchScalarGridSpec(
            num_scalar_prefetch=6,
            grid=(steps1,),
            in_specs=[pl.BlockSpec((bm, d), lambda s, blk, ex, ft, oblk, oft, ok: (blk[s], 0)),
                      pl.BlockSpec((None, d, tf), lambda s, blk, ex, ft, oblk, oft, ok: (ex[s], 0, ft[s])),
                      pl.BlockSpec((None, d, tf), lambda s, blk, ex, ft, oblk, oft, ok: (ex[s], 0, ft[s]))],
            out_specs=pl.BlockSpec((bm, tf), lambda s, blk, ex, ft, oblk, oft, ok: (oblk[s], oft[s]))),
        out_shape=jax.ShapeDtypeStruct((cap, dff), F32),
        compiler_params=_cparams("arbitrary"),
        name="e1",
    )(blk_of, e_of, ft_of, oblk_of, oft_of, ok1.astype(jnp.int32), xb, w_gate, w_up)

    blk = jnp.arange(n_blocks, dtype=jnp.int32)
    b_idx = jnp.minimum(blk, used - 1)
    bend = jnp.cumsum(nb)
    e2_of = jnp.minimum(jnp.sum(b_idx[:, None] >= bend[None, :], axis=1), n_exp - 1).astype(jnp.int32)
    ok2 = (blk < used).astype(jnp.int32)
    yb = pl.pallas_call(
        _e2_kernel,
        grid_spec=pltpu.PrefetchScalarGridSpec(
            num_scalar_prefetch=4,
            grid=(n_blocks,),
            in_specs=[pl.BlockSpec((bm, dff), lambda s, hb, ex, ob, ok: (hb[s], 0)),
                      pl.BlockSpec((None, dff, d), lambda s, hb, ex, ob, ok: (ex[s], 0, 0))],
            out_specs=pl.BlockSpec((bm, d), lambda s, hb, ex, ob, ok: (ob[s], 0))),
        out_shape=jax.ShapeDtypeStruct((cap, d), F32),
        compiler_params=_cparams("arbitrary"),
        name="e2",
    )(b_idx, e2_of, blk, ok2, hmid, w_down)
    return yb


def _combine_kernel(dest_ref, yb_hbm, h_ref, rw_ref, g_ref, nw_ref, sh_ref, sc_ref, o_ref, rows, sem, *, tokens):
    def issue(t, carry):
        for k in range(TOP_K):
            pltpu.make_async_copy(yb_hbm.at[pl.ds(dest_ref[0, TOP_K * t + k], 1)],
                                  rows.at[k, pl.ds(t, 1)], sem).start()
        return carry

    lax.fori_loop(0, tokens, issue, 0)

    def drain(t, carry):
        for k in range(TOP_K):
            pltpu.make_async_copy(yb_hbm.at[pl.ds(0, 1)], rows.at[k, pl.ds(0, 1)], sem).wait()
        return carry

    lax.fori_loop(0, tokens, drain, 0)

    rw = rw_ref[...]
    ffn = rows[0] * rw[:, 0:1]
    for k in range(1, TOP_K):
        ffn = ffn + rows[k] * rw[:, k:k + 1]
    h2 = h_ref[...] + g_ref[...] * ffn
    o_ref[...] = _rms_mod(h2, nw_ref[...], sh_ref[...], sc_ref[...])


def _combine(yb, dest, h, route_w, gate, norm_w, shift, scale, tokens=256):
    b, s, d = h.shape
    n = b * s
    per_seq = s // tokens
    kern = functools.partial(_combine_kernel, tokens=tokens)
    vec = pl.BlockSpec((None, 1, d), lambda i: (i // per_seq, 0, 0))
    out = pl.pallas_call(
        kern,
        grid=(n // tokens,),
        in_specs=[pl.BlockSpec((None, 1, TOP_K * tokens), lambda i: (i, 0, 0), memory_space=pltpu.SMEM),
                  pl.BlockSpec(memory_space=pl.ANY),
                  pl.BlockSpec((tokens, d), lambda i: (i, 0)),
                  pl.BlockSpec((tokens, LANES), lambda i: (i, 0)),
                  vec,
                  pl.BlockSpec((1, d), lambda i: (0, 0)), vec, vec],
        out_specs=pl.BlockSpec((tokens, d), lambda i: (i, 0)),
        out_shape=jax.ShapeDtypeStruct((n, d), F32),
        scratch_shapes=[pltpu.VMEM((TOP_K, tokens, d), F32), pltpu.SemaphoreType.DMA(())],
        compiler_params=_cparams("arbitrary"),
        name="combine",
    )(dest.reshape(n // tokens, 1, TOP_K * tokens), yb, h.reshape(n, d), route_w,
      gate.reshape(b, 1, d), norm_w.reshape(1, d), shift.reshape(b, 1, d), scale.reshape(b, 1, d))
    return out.reshape(b, s, d)


def _layer(h, mod, norm1_w, w_in, conv_w, ml_gate_bias, ml_norm_w, w_out, norm2_w,
           w_router_group, b_router_group, w_router_expert, b_router_expert,
           w_exp_gate, w_exp_up, w_exp_down, final):
    b, s, d = h.shape
    n = b * s
    sh1, sc1, g1, sh2, sc2, g2 = jnp.split(mod, 6, axis=-1)
    sb_w = SB_HEADS * SB_HEAD_DIM
    ml_w = ML_HEADS * ML_HEAD_DIM

    u = _norm_mod(h, norm1_w, sh1, sc1, BF16).reshape(n, d)
    w_in16 = w_in.astype(BF16)
    tm = 1024 if n % 1024 == 0 else 512
    sb_qkv = _matmul(u, w_in16, 0, 3 * sb_w, BF16, tm, 512)
    ml_qk = _matmul(u, w_in16, 3 * sb_w, 2 * ml_w, F32, tm, 512)
    ml_v = _matmul(u, w_in16, 3 * sb_w + 2 * ml_w, ml_w, BF16, tm, 512)
    ml_o = _matmul(u, w_in16, 3 * sb_w + 3 * ml_w, ml_w, F32, tm, 512)
    n_gate = 2 * ML_HEADS
    w_gate = jnp.pad(w_in16[:, 3 * sb_w + 4 * ml_w:], ((0, 0), (0, LANES - n_gate)))
    b_gate = jnp.pad(ml_gate_bias, (0, LANES - n_gate)).reshape(1, LANES)
    gates = _matmul(u, w_gate, 0, LANES, F32, tm, LANES, bias=b_gate)[:, :n_gate]

    y_sb = _sb_attention(sb_qkv.reshape(b, s, 3 * sb_w))
    y_ml = _mlstm(ml_qk.reshape(b, s, 2 * ml_w), ml_v.reshape(b, s, ml_w), ml_o.reshape(b, s, ml_w),
                  gates.reshape(b, s, n_gate), conv_w, ml_norm_w)
    h = _out_proj(y_sb, y_ml, w_out.astype(BF16), h, g1, tm=min(1024, s))

    w_router = jnp.pad(jnp.concatenate([w_router_group, w_router_expert], axis=1),
                       ((0, 0), (0, LANES - N_GROUPS - N_EXPERTS)))
    b_router = jnp.pad(jnp.concatenate([b_router_group, b_router_expert]),
                       (0, LANES - N_GROUPS - N_EXPERTS)).reshape(1, LANES)
    u2, route_i, route_w, counts = _router(h, norm2_w, sh2, sc2, w_router, b_router)
    counts = counts[0, :N_EXPERTS]
    blocks_per_expert = (counts + MOE_BLOCK - 1) // MOE_BLOCK
    padded = blocks_per_expert * MOE_BLOCK
    pstarts = jnp.cumsum(padded) - padded
    dest = pstarts[route_i[:, 0:TOP_K]] + route_i[:, TOP_K:2 * TOP_K]
    cap = (-(-(n * TOP_K) // MOE_BLOCK) + N_EXPERTS) * MOE_BLOCK
    tail_block_row = jnp.where(counts % MOE_BLOCK != 0, pstarts + padded - MOE_BLOCK, -1)
    spare_row = jnp.sum(padded) + MOE_BLOCK * jnp.arange(N_EXPERTS, dtype=jnp.int32)
    spare_row = jnp.where(spare_row < cap, spare_row, -1)
    clear_rows = jnp.concatenate([tail_block_row, spare_row]).astype(jnp.int32)
    xb = _dispatch(u2, dest, clear_rows, cap)
    yb = _experts(xb, blocks_per_expert, w_exp_gate, w_exp_up, w_exp_down)
    norm_w, shift, scale = final
    return _combine(yb, dest, h, route_w, g2, norm_w, shift, scale)


def kernel(x, c, norm1_w, w_in, conv_w, ml_gate_bias, ml_norm_w, w_out, norm2_w, w_router_group, b_router_group, w_router_expert, b_router_expert, w_exp_gate, w_exp_up, w_exp_down, w_ada, b_ada, final_norm_w, w_ada_final, b_ada_final):
    b, s, d = x.shape
    depth = w_in.shape[0]
    assert depth == 1, "the final norm is fused into the (single) layer's combine step"
    c_rep = jnp.broadcast_to(c[:, :, None], (b, d, LANES))
    fmod = _ada(c_rep, w_ada_final, b_ada_final)
    sh_f, sc_f = jnp.split(fmod, 2, axis=-1)
    mod = _ada(c_rep, w_ada[0], b_ada[0])
    return _layer(x, mod, norm1_w[0], w_in[0], conv_w[0], ml_gate_bias[0], ml_norm_w[0], w_out[0], norm2_w[0],
                  w_router_group[0], b_router_group[0], w_router_expert[0], b_router_expert[0],
                  w_exp_gate[0], w_exp_up[0], w_exp_down[0], (final_norm_w, sh_f, sc_f))
```

```python
import functools
import math

import jax
import jax.numpy as jnp
from jax import lax
from jax.experimental import pallas as pl
from jax.experimental.pallas import tpu as pltpu

SB_HEADS = 16
SB_HEAD_DIM = 128
ML_HEADS = 4
ML_HEAD_DIM = 512
CONV_WIDTH = 4
N_GROUPS = 4
EXPERTS_PER_GROUP = 8
N_EXPERTS = N_GROUPS * EXPERTS_PER_GROUP
TOP_K = 2
NORM_EPS = 1e-6

LANES = 128
SUBLANES = 8
VMEM_LIMIT = 56 * 1024 * 1024

ATTN_SUB = 128
ATTN_SUBS_PER_STEP = 8
ML_CHUNK = 256
MOE_BLOCK = 256
UNDERFLOW_LOG = -90.0

F32 = jnp.float32
BF16 = jnp.bfloat16


def _cparams(*sem):
    return pltpu.CompilerParams(dimension_semantics=sem, vmem_limit_bytes=VMEM_LIMIT)


def _split_bf16(x):
    hi = x.astype(BF16)
    lo = (x - hi.astype(F32)).astype(BF16)
    return hi, lo


_HIGH_HALF = -65536


def _pack_bf16_pairs(x):
    k = x.shape[1] // 2
    bits = lambda v: lax.bitcast_convert_type(v.astype(F32), jnp.int32)
    return lax.shift_right_logical(bits(x[:, :k]), 16) | (bits(x[:, k:]) & _HIGH_HALF)


def _unpack_bf16_pairs(w):
    lo = lax.bitcast_convert_type(lax.shift_left(w, 16), F32)
    hi = lax.bitcast_convert_type(w & _HIGH_HALF, F32)
    return lo, hi


def _log_sigmoid(x):
    return jnp.minimum(x, 0.0) - jnp.log1p(jnp.exp(-jnp.abs(x)))


def _ada_kernel(c_ref, w_ref, b_ref, o_ref):
    nb, tn = o_ref.shape
    rows = []
    for b in range(nb):
        cb = c_ref[b]
        cb = cb * jax.nn.sigmoid(cb)
        pieces = [jnp.sum(w_ref[:, j * LANES:(j + 1) * LANES] * cb, axis=0, keepdims=True)
                  for j in range(tn // LANES)]
        rows.append(jnp.concatenate(pieces, axis=1))
    o_ref[...] = jnp.concatenate(rows, axis=0) + b_ref[...]


def _ada(c_rep, w, bias, tn=512):
    nb, k, _ = c_rep.shape
    n_out = w.shape[1]
    return pl.pallas_call(
        _ada_kernel,
        grid=(n_out // tn,),
        in_specs=[pl.BlockSpec((nb, k, LANES), lambda j: (0, 0, 0)),
                  pl.BlockSpec((k, tn), lambda j: (0, j)),
                  pl.BlockSpec((1, tn), lambda j: (0, j))],
        out_specs=pl.BlockSpec((nb, tn), lambda j: (0, j)),
        out_shape=jax.ShapeDtypeStruct((nb, n_out), F32),
        compiler_params=_cparams("arbitrary"),
        name="ada",
    )(c_rep, w, bias.reshape(1, n_out))


def _rms_mod(x, w, shift, scale):
    var = jnp.mean(x * x, axis=-1, keepdims=True)
    y = x * lax.rsqrt(var + NORM_EPS) * w
    return y * (1.0 + scale) + shift


def _norm_mod_kernel(x_ref, w_ref, sh_ref, sc_ref, o_ref):
    o_ref[...] = _rms_mod(x_ref[...], w_ref[...], sh_ref[...], sc_ref[...]).astype(o_ref.dtype)


def _norm_mod(x, w, shift, scale, out_dtype, tm=256):
    b, s, d = x.shape
    vec = pl.BlockSpec((None, 1, d), lambda bi, i: (bi, 0, 0))
    return pl.pallas_call(
        _norm_mod_kernel,
        grid=(b, s // tm),
        in_specs=[pl.BlockSpec((None, tm, d), lambda bi, i: (bi, i, 0)),
                  pl.BlockSpec((1, d), lambda bi, i: (0, 0)), vec, vec],
        out_specs=pl.BlockSpec((None, tm, d), lambda bi, i: (bi, i, 0)),
        out_shape=jax.ShapeDtypeStruct((b, s, d), out_dtype),
        compiler_params=_cparams("arbitrary", "arbitrary"),
        name="norm_mod",
    )(x, w.reshape(1, d), shift.reshape(b, 1, d), scale.reshape(b, 1, d))


def _matmul_kernel(a_ref, w_ref, *rest, has_bias):
    b_ref, o_ref, w16 = rest if has_bias else (None,) + rest

    @pl.when(pl.program_id(1) == 0)
    def _():
        w16[...] = w_ref[...].astype(BF16)

    acc = jnp.dot(a_ref[...], w16[...], preferred_element_type=F32)
    if has_bias:
        acc = acc + b_ref[...]
    o_ref[...] = acc.astype(o_ref.dtype)


def _matmul(a, w, col0, ncols, out_dtype, tm, tn, bias=None):
    m, k = a.shape
    j0 = col0 // tn
    in_specs = [pl.BlockSpec((tm, k), lambda j, i: (i, 0)),
                pl.BlockSpec((k, tn), lambda j, i: (0, j + j0))]
    args = [a, w]
    if bias is not None:
        in_specs.append(pl.BlockSpec((1, tn), lambda j, i: (0, j + j0)))
        args.append(bias)
    return pl.pallas_call(
        functools.partial(_matmul_kernel, has_bias=bias is not None),
        grid=(ncols // tn, m // tm),
        in_specs=in_specs,
        out_specs=pl.BlockSpec((tm, tn), lambda j, i: (i, j)),
        out_shape=jax.ShapeDtypeStruct((m, ncols), out_dtype),
        scratch_shapes=[pltpu.VMEM((k, tn), BF16)],
        compiler_params=_cparams("arbitrary", "arbitrary"),
        name="proj",
    )(*args)


def _sb_attn_kernel(q_ref, k_ref, v_ref, tri_ref, o_ref, acc_ref, r_ref, *, sub, n_sub, scale):
    first = pl.program_id(2) * n_sub
    tri = tri_ref[...]
    row = lax.broadcasted_iota(jnp.int32, (sub, sub), 0)
    col = lax.broadcasted_iota(jnp.int32, (sub, sub), 1)
    causal = col < row

    def visit(q, blk, r, mask):
        start = pl.multiple_of(jnp.maximum(blk, 0) * sub, sub)
        kb = k_ref[pl.ds(start, sub), :]
        vb = v_ref[pl.ds(start, sub), :]
        z = lax.dot_general(q, kb, (((1,), (1,)), ((), ())), preferred_element_type=F32) * scale
        log_keep = -(jnp.maximum(z, 0.0) + jnp.log(1.0 + jnp.exp(-jnp.abs(z))))
        log_beta = log_keep + z
        if mask is not None:
            log_keep = jnp.where(mask, log_keep, 0.0)
        hi, lo = _split_bf16(log_keep)
        between = (jnp.dot(hi, tri, preferred_element_type=F32)
                   + jnp.dot(lo, tri, preferred_element_type=F32))
        w = jnp.exp(log_beta + between + r)
        if mask is not None:
            w = jnp.where(mask, w, 0.0)
        pv = jnp.dot(w.astype(BF16), vb, preferred_element_type=F32)
        return pv, r + jnp.sum(log_keep, axis=1, keepdims=True)

    r_top = None
    for s in range(n_sub):
        q = q_ref[s * sub:(s + 1) * sub, :]
        blk = first + s
        pv0, r = visit(q, blk, jnp.zeros((sub, LANES), F32), causal)
        pv1, r = visit(q, blk - 1, r, (blk >= 1) if s == 0 else None)
        acc_ref[s] = pv0 + pv1
        r_ref[s] = r
        r_top = r if r_top is None else jnp.maximum(r_top, r)

    @pl.when(jnp.max(r_top) > UNDERFLOW_LOG)
    def _():
        for s in range(n_sub):
            q = q_ref[s * sub:(s + 1) * sub, :]

            def cond(carry):
                j, r_max = carry
                return jnp.logical_and(j >= 0, r_max > UNDERFLOW_LOG)

            def body(carry, s=s, q=q):
                j, _ = carry
                pv, r = visit(q, j, r_ref[s], None)
                acc_ref[s] += pv
                r_ref[s] = r
                return j - 1, jnp.max(r)

            lax.while_loop(cond, body, (first + s - 2, jnp.max(r_ref[s])))

    for s in range(n_sub):
        o_ref[s * sub:(s + 1) * sub, :] = acc_ref[s].astype(o_ref.dtype)


def _sb_attention(qkv, sub=ATTN_SUB, n_sub=ATTN_SUBS_PER_STEP):
    b, s, _ = qkv.shape
    h, d = SB_HEADS, SB_HEAD_DIM
    assert sub == LANES, "the carried row sums are kept lane-replicated at the key sub-block width"
    n_sub = min(n_sub, s // sub)
    tq = sub * n_sub
    idx = lax.broadcasted_iota(jnp.int32, (sub, sub), 0) > lax.broadcasted_iota(jnp.int32, (sub, sub), 1)
    tri = idx.astype(BF16)
    kern = functools.partial(_sb_attn_kernel, sub=sub, n_sub=n_sub, scale=1.0 / math.sqrt(d))
    return pl.pallas_call(
        kern,
        grid=(b, h, s // tq),
        in_specs=[pl.BlockSpec((None, tq, d), lambda bi, hi, qi: (bi, qi, hi)),
                  pl.BlockSpec((None, s, d), lambda bi, hi, qi: (bi, 0, h + hi)),
                  pl.BlockSpec((None, s, d), lambda bi, hi, qi: (bi, 0, 2 * h + hi)),
                  pl.BlockSpec((sub, sub), lambda bi, hi, qi: (0, 0))],
        out_specs=pl.BlockSpec((None, tq, d), lambda bi, hi, qi: (bi, qi, hi)),
        out_shape=jax.ShapeDtypeStruct((b, s, h * d), BF16),
        scratch_shapes=[pltpu.VMEM((n_sub, sub, d), F32), pltpu.VMEM((n_sub, sub, LANES), F32)],
        compiler_params=_cparams("arbitrary", "arbitrary", "arbitrary"),
        name="sb_attn",
    )(qkv, qkv, qkv, tri)


def _mlstm_kernel(qp_ref, kp_ref, v_ref, og_ref, gcol_ref, grow_ref, cwq_ref, cwk_ref, nw_ref,
                  tril_ref, triu_ref, y_ref, c_st, n_st, m_st, qbuf, kbuf, *, chunk, dim):
    head = pl.program_id(1)
    ci = pl.program_id(2)
    halo = SUBLANES

    @pl.when(ci == 0)
    def _():
        c_st[...] = jnp.zeros_like(c_st)
        n_st[...] = jnp.zeros_like(n_st)
        m_st[...] = jnp.zeros_like(m_st)
        qbuf[0:halo, :] = jnp.zeros((halo, dim), F32)
        kbuf[0:halo, :] = jnp.zeros((halo, dim), F32)

    qbuf[halo:halo + chunk, :] = qp_ref[...]
    kbuf[halo:halo + chunk, :] = kp_ref[...]

    def conv_silu(buf, cw_ref):
        acc = buf[halo:halo + chunk, :] * cw_ref[CONV_WIDTH - 1:CONV_WIDTH, :]
        for j in range(CONV_WIDTH - 1):
            off = halo - (CONV_WIDTH - 1) + j
            acc = acc + buf[off:off + chunk, :] * cw_ref[j:j + 1, :]
        return acc * jax.nn.sigmoid(acc)

    q = conv_silu(qbuf, cwq_ref)
    k = conv_silu(kbuf, cwk_ref) * (1.0 / math.sqrt(dim))
    qbuf[0:halo, :] = qbuf[chunk:chunk + halo, :]
    kbuf[0:halo, :] = kbuf[chunk:chunk + halo, :]

    gcol = gcol_ref[...]
    lane = lax.broadcasted_iota(jnp.int32, gcol.shape, 1)
    li_col = jnp.sum(jnp.where(lane == head, gcol, 0.0), axis=1, keepdims=True)
    lf_col = _log_sigmoid(jnp.sum(jnp.where(lane == ML_HEADS + head, gcol, 0.0), axis=1, keepdims=True))
    grow = grow_ref[...]
    sub = lax.broadcasted_iota(jnp.int32, grow.shape, 0)
    li_row = jnp.sum(jnp.where(sub == head, grow, 0.0), axis=0, keepdims=True)
    lf_row = _log_sigmoid(jnp.sum(jnp.where(sub == ML_HEADS + head, grow, 0.0), axis=0, keepdims=True))

    hi, lo = _split_bf16(jnp.broadcast_to(lf_col, (chunk, LANES)))
    b_col = (jnp.dot(tril_ref[...], hi, preferred_element_type=F32)
             + jnp.dot(tril_ref[...], lo, preferred_element_type=F32))[:, 0:1]
    hi, lo = _split_bf16(jnp.broadcast_to(lf_row, (2 * SUBLANES, chunk)))
    b_row = (jnp.dot(hi, triu_ref[...], preferred_element_type=F32)
             + jnp.dot(lo, triu_ref[...], preferred_element_type=F32))[0:1, :]

    m_prev = m_st[...]
    a_col = b_col + m_prev
    row_t = lax.broadcasted_iota(jnp.int32, (chunk, chunk), 0)
    col_s = lax.broadcasted_iota(jnp.int32, (chunk, chunk), 1)
    d_mat = jnp.where(col_s <= row_t, b_col - b_row + li_row, -jnp.inf)
    m_col = jnp.maximum(a_col, jnp.max(d_mat, axis=1, keepdims=True))
    w_intra = jnp.exp(d_mat - m_col)
    w_inter = jnp.exp(a_col - m_col)

    qb = q.astype(BF16)
    kb = k.astype(BF16)
    vb = v_ref[...]
    sc = lax.dot_general(qb, kb, (((1,), (1,)), ((), ())), preferred_element_type=F32) * w_intra
    num = (w_inter * jnp.dot(qb, c_st[...].astype(BF16), preferred_element_type=F32)
           + jnp.dot(sc.astype(BF16), vb, preferred_element_type=F32))
    den = (w_inter * jnp.sum(q * n_st[...], axis=1, keepdims=True)
           + jnp.sum(sc, axis=1, keepdims=True))
    h_out = num / jnp.maximum(jnp.abs(den), jnp.exp(-m_col))

    b_last = b_col[chunk - 1:chunk, :]
    g_col = b_last - b_col + li_col
    m_new = jnp.maximum(b_last + m_prev, jnp.max(g_col, axis=0, keepdims=True))
    decay = jnp.exp(b_last + m_prev - m_new)
    kw = k * jnp.exp(g_col - m_new)
    c_st[...] = decay * c_st[...] + lax.dot_general(kw.astype(BF16), vb, (((0,), (0,)), ((), ())),
                                                    preferred_element_type=F32)
    n_st[...] = decay * n_st[...] + jnp.sum(kw, axis=0, keepdims=True)
    m_st[...] = m_new

    hn = h_out * lax.rsqrt(jnp.mean(h_out * h_out, axis=1, keepdims=True) + NORM_EPS) * nw_ref[...]
    y_ref[...] = (jax.nn.sigmoid(og_ref[...]) * hn).astype(y_ref.dtype)


def _mlstm(ml_qk, ml_v, ml_o, gates, conv_w, ml_norm_w, chunk=ML_CHUNK):
    b, s, _ = ml_v.shape
    h, d = ML_HEADS, ML_HEAD_DIM
    gates_t = jnp.swapaxes(gates, 1, 2)
    r = lax.broadcasted_iota(jnp.int32, (chunk, chunk), 0)
    c = lax.broadcasted_iota(jnp.int32, (chunk, chunk), 1)
    tril = (c <= r).astype(BF16)
    triu = (r <= c).astype(BF16)
    head_blk = lambda off: pl.BlockSpec((None, chunk, d), lambda bi, hi, ci: (bi, ci, hi + off))
    kern = functools.partial(_mlstm_kernel, chunk=chunk, dim=d)
    return pl.pallas_call(
        kern,
        grid=(b, h, s // chunk),
        in_specs=[head_blk(0), head_blk(h), head_blk(0), head_blk(0),
                  pl.BlockSpec((None, chunk, 2 * h), lambda bi, hi, ci: (bi, ci, 0)),
                  pl.BlockSpec((None, 2 * h, chunk), lambda bi, hi, ci: (bi, 0, ci)),
                  pl.BlockSpec((CONV_WIDTH, d), lambda bi, hi, ci: (0, hi)),
                  pl.BlockSpec((CONV_WIDTH, d), lambda bi, hi, ci: (0, hi + h)),
                  pl.BlockSpec((1, d), lambda bi, hi, ci: (0, hi)),
                  pl.BlockSpec((chunk, chunk), lambda bi, hi, ci: (0, 0)),
                  pl.BlockSpec((chunk, chunk), lambda bi, hi, ci: (0, 0))],
        out_specs=head_blk(0),
        out_shape=jax.ShapeDtypeStruct((b, s, h * d), BF16),
        scratch_shapes=[pltpu.VMEM((d, d), F32), pltpu.VMEM((1, d), F32), pltpu.VMEM((1, 1), F32),
                        pltpu.VMEM((chunk + SUBLANES, d), F32), pltpu.VMEM((chunk + SUBLANES, d), F32)],
        compiler_params=_cparams("arbitrary", "arbitrary", "arbitrary"),
        name="mlstm",
    )(ml_qk, ml_qk, ml_v, ml_o, gates, gates_t, conv_w, conv_w, ml_norm_w.reshape(1, h * d), tril, triu)


def _out_proj_kernel(ya_ref, yb_ref, wa_ref, wb_ref, x_ref, g_ref, o_ref):
    mix = (jnp.dot(ya_ref[...], wa_ref[...], preferred_element_type=F32)
           + jnp.dot(yb_ref[...], wb_ref[...], preferred_element_type=F32))
    o_ref[...] = x_ref[...] + g_ref[...] * mix


def _out_proj(y_sb, y_ml, w_out, x, gate, tm, tn=512):
    b, s, d = x.shape
    ka, kb = y_sb.shape[-1], y_ml.shape[-1]
    assert ka == kb
    n = b * s
    rows_per_seq = s // tm
    out = pl.pallas_call(
        _out_proj_kernel,
        grid=(n // tm, d // tn),
        in_specs=[pl.BlockSpec((tm, ka), lambda i, j: (i, 0)),
                  pl.BlockSpec((tm, kb), lambda i, j: (i, 0)),
                  pl.BlockSpec((ka, tn), lambda i, j: (0, j)),
                  pl.BlockSpec((kb, tn), lambda i, j: (1, j)),
                  pl.BlockSpec((tm, tn), lambda i, j: (i, j)),
                  pl.BlockSpec((None, 1, tn), lambda i, j: (i // rows_per_seq, 0, j))],
        out_specs=pl.BlockSpec((tm, tn), lambda i, j: (i, j)),
        out_shape=jax.ShapeDtypeStruct((n, d), F32),
        compiler_params=_cparams("arbitrary", "arbitrary"),
        name="out_proj",
    )(y_sb.reshape(n, ka), y_ml.reshape(n, kb), w_out, w_out, x.reshape(n, d), gate.reshape(b, 1, d))
    return out.reshape(b, s, d)


def _router_kernel(h_ref, nw_ref, sh_ref, sc_ref, wr_ref, br_ref, tri_ref, u_ref, ri_ref, rw_ref, cnt_ref, run_ref):
    first = jnp.logical_and(pl.program_id(0) == 0, pl.program_id(1) == 0)

    @pl.when(first)
    def _():
        run_ref[...] = jnp.zeros_like(run_ref)

    u = _rms_mod(h_ref[...], nw_ref[...], sh_ref[...], sc_ref[...])
    tm = u.shape[0]
    uh, ul = _split_bf16(u)
    u_ref[...] = _pack_bf16_pairs(uh)
    wh, wl = _split_bf16(wr_ref[...])
    logits = (jnp.dot(uh, wh, preferred_element_type=F32) + jnp.dot(ul, wh, preferred_element_type=F32)
              + jnp.dot(uh, wl, preferred_element_type=F32) + br_ref[...])

    lane = lax.broadcasted_iota(jnp.int32, (tm, LANES), 1)
    lane_f = lane.astype(F32)
    neg = -jnp.inf
    first_lane = lambda hit: jnp.min(jnp.where(hit, lane_f, float(LANES)), axis=1, keepdims=True)

    gl = jnp.where(lane < N_GROUPS, logits, neg)
    g_max = jnp.max(gl, axis=1, keepdims=True)
    g_p = 1.0 / jnp.sum(jnp.exp(gl - g_max), axis=1, keepdims=True)
    g_sel = first_lane(gl == g_max)

    lo_lane = N_GROUPS + EXPERTS_PER_GROUP * g_sel
    el = jnp.where(jnp.logical_and(lane_f >= lo_lane, lane_f < lo_lane + EXPERTS_PER_GROUP), logits, neg)
    e1_max = jnp.max(el, axis=1, keepdims=True)
    l1 = first_lane(el == e1_max)
    el2 = jnp.where(lane_f == l1, neg, el)
    e2_max = jnp.max(el2, axis=1, keepdims=True)
    l2 = first_lane(el2 == e2_max)
    z_sum = jnp.sum(jnp.exp(el - e1_max), axis=1, keepdims=True)
    p1 = 1.0 / z_sum
    p2 = jnp.exp(e2_max - e1_max) / z_sum
    w1 = p1 / (p1 + p2) * g_p
    w2 = p2 / (p1 + p2) * g_p
    e1 = l1 - N_GROUPS
    e2 = l2 - N_GROUPS

    hit1 = lane_f == e1
    hit2 = lane_f == e2
    onehot = jnp.where(jnp.logical_or(hit1, hit2), 1.0, 0.0)
    before = jnp.dot(tri_ref[...], onehot.astype(BF16), preferred_element_type=F32) + run_ref[...]
    rank1 = jnp.sum(jnp.where(hit1, before, 0.0), axis=1, keepdims=True)
    rank2 = jnp.sum(jnp.where(hit2, before, 0.0), axis=1, keepdims=True)
    run = run_ref[...] + jnp.sum(onehot, axis=0, keepdims=True)
    run_ref[...] = run

    ri_ref[...] = jnp.where(lane == 0, e1, jnp.where(lane == 1, e2, jnp.where(lane == 2, rank1,
                            jnp.where(lane == 3, rank2, 0.0)))).astype(jnp.int32)
    rw_ref[...] = jnp.where(lane == 0, w1, jnp.where(lane == 1, w2, 0.0))
    cnt_ref[...] = jnp.broadcast_to(run, cnt_ref.shape).astype(jnp.int32)


def _router(h, norm_w, shift, scale, w_router, b_router, tm=256):
    b, s, d = h.shape
    n = b * s
    tri = (lax.broadcasted_iota(jnp.int32, (tm, tm), 1) < lax.broadcasted_iota(jnp.int32, (tm, tm), 0)).astype(BF16)
    vec = pl.BlockSpec((None, 1, d), lambda bi, i: (bi, 0, 0))
    rows = lambda width: pl.BlockSpec((tm, width), lambda bi, i: (bi * (s // tm) + i, 0))
    return pl.pallas_call(
        _router_kernel,
        grid=(b, s // tm),
        in_specs=[pl.BlockSpec((None, tm, d), lambda bi, i: (bi, i, 0)),
                  pl.BlockSpec((1, d), lambda bi, i: (0, 0)), vec, vec,
                  pl.BlockSpec((d, LANES), lambda bi, i: (0, 0)),
                  pl.BlockSpec((1, LANES), lambda bi, i: (0, 0)),
                  pl.BlockSpec((tm, tm), lambda bi, i: (0, 0))],
        out_specs=[rows(d // 2), rows(LANES), rows(LANES),
                   pl.BlockSpec((SUBLANES, LANES), lambda bi, i: (0, 0))],
        out_shape=[jax.ShapeDtypeStruct((n, d // 2), jnp.int32), jax.ShapeDtypeStruct((n, LANES), jnp.int32),
                   jax.ShapeDtypeStruct((n, LANES), F32), jax.ShapeDtypeStruct((SUBLANES, LANES), jnp.int32)],
        scratch_shapes=[pltpu.VMEM((1, LANES), F32)],
        compiler_params=_cparams("arbitrary", "arbitrary"),
        name="router",
    )(h, norm_w.reshape(1, d), shift.reshape(b, 1, d), scale.reshape(b, 1, d), w_router, b_router, tri)


ZERO_ROWS = 128


def _dispatch_kernel(tail_ref, dest_ref, u_ref, xb_hbm, zbuf, zsem, sem, *, tokens, block):
    n_exp = tail_ref.shape[0]

    def clear(e, half):
        row = pl.multiple_of(tail_ref[e] + half * ZERO_ROWS, ZERO_ROWS)
        return pltpu.make_async_copy(zbuf, xb_hbm.at[pl.ds(row, ZERO_ROWS)], zsem)

    @pl.when(pl.program_id(0) == 0)
    def _():
        zbuf[...] = jnp.zeros_like(zbuf)
        for start in (True, False):
            def each(e, carry):
                @pl.when(tail_ref[e] >= 0)
                def _():
                    for half in range(block // ZERO_ROWS):
                        clear(e, half).start() if start else clear(e, half).wait()
                return carry
            lax.fori_loop(0, n_exp, each, 0)

    def row_copy(t, k):
        return pltpu.make_async_copy(u_ref.at[pl.ds(t, 1)], xb_hbm.at[pl.ds(dest_ref[0, TOP_K * t + k], 1)], sem)

    def issue(t, carry):
        for k in range(TOP_K):
            row_copy(t, k).start()
        return carry

    lax.fori_loop(0, tokens, issue, 0)

    def drain(t, carry):
        for k in range(TOP_K):
            row_copy(t, k).wait()
        return carry

    lax.fori_loop(0, tokens, drain, 0)


def _dispatch(u, dest, tail_block_row, cap, tokens=256, block=MOE_BLOCK):
    n, d = u.shape
    steps = n // tokens
    kern = functools.partial(_dispatch_kernel, tokens=tokens, block=block)
    return pl.pallas_call(
        kern,
        grid_spec=pltpu.PrefetchScalarGridSpec(
            num_scalar_prefetch=1,
            grid=(steps,),
            in_specs=[pl.BlockSpec((None, 1, TOP_K * tokens), lambda i, tail: (i, 0, 0), memory_space=pltpu.SMEM),
                      pl.BlockSpec((tokens, d), lambda i, tail: (i, 0))],
            out_specs=pl.BlockSpec(memory_space=pl.ANY),
            scratch_shapes=[pltpu.VMEM((ZERO_ROWS, d), u.dtype), pltpu.SemaphoreType.DMA(()),
                            pltpu.SemaphoreType.DMA(())]),
        out_shape=jax.ShapeDtypeStruct((cap, d), u.dtype),
        compiler_params=_cparams("arbitrary"),
        name="dispatch",
    )(tail_block_row, dest.reshape(steps, 1, TOP_K * tokens), u)


def _e1_kernel(blk_ref, exp_ref, ft_ref, oblk_ref, oft_ref, ok_ref, x_ref, wg_ref, wu_ref, o_ref):
    del blk_ref, exp_ref, ft_ref, oblk_ref, oft_ref
    ok = ok_ref[pl.program_id(0)] == 1

    @pl.when(ok)
    def _():
        x_lo, x_hi = _unpack_bf16_pairs(x_ref[...])
        half = x_lo.shape[1]
        mm = lambda w_ref: (jnp.dot(x_lo, w_ref[:half, :], preferred_element_type=F32)
                            + jnp.dot(x_hi, w_ref[half:, :], preferred_element_type=F32))
        g = mm(wg_ref)
        up = mm(wu_ref)
        o_ref[...] = ((g * jax.nn.sigmoid(g)) * up).astype(o_ref.dtype)

    @pl.when(jnp.logical_not(ok))
    def _():
        o_ref[...] = jnp.zeros_like(o_ref)


def _e2_kernel(blk_ref, exp_ref, oblk_ref, ok_ref, h_ref, wd_ref, o_ref):
    del blk_ref, exp_ref, oblk_ref
    ok = ok_ref[pl.program_id(0)] == 1

    @pl.when(ok)
    def _():
        o_ref[...] = jnp.dot(h_ref[...].astype(F32), wd_ref[...], preferred_element_type=F32)

    @pl.when(jnp.logical_not(ok))
    def _():
        o_ref[...] = jnp.zeros_like(o_ref)


def _experts(xb, blocks_per_expert, w_gate, w_up, w_down, bm=MOE_BLOCK, tf=512):
    cap, dp = xb.shape
    n_exp, d, dff = w_gate.shape
    n_blocks = cap // bm
    n_ft = dff // tf
    nb = blocks_per_expert.astype(jnp.int32)
    bstart = jnp.cumsum(nb) - nb
    used = jnp.sum(nb)

    steps1 = n_blocks * n_ft
    step = jnp.arange(steps1, dtype=jnp.int32)
    ok1 = step < used * n_ft
    send = jnp.cumsum(nb * n_ft)
    s_idx = jnp.minimum(step, used * n_ft - 1)
    e_of = jnp.minimum(jnp.sum(s_idx[:, None] >= send[None, :], axis=1), n_exp - 1).astype(jnp.int32)
    local = s_idx - (send[e_of] - nb[e_of] * n_ft)
    nbe = jnp.maximum(nb[e_of], 1)
    ft_of = (local // nbe).astype(jnp.int32)
    blk_of = (bstart[e_of] + local % nbe).astype(jnp.int32)
    spare = step - used * n_ft
    oblk_of = jnp.where(ok1, blk_of, used + spare // n_ft).astype(jnp.int32)
    oft_of = jnp.where(ok1, ft_of, spare % n_ft).astype(jnp.int32)

    hmid = pl.pallas_call(
        _e1_kernel,
        grid_spec=pltpu.PrefetchScalarGridSpec(
            num_scalar_prefetch=6,
            grid=(steps1,),
            in_specs=[pl.BlockSpec((bm, dp), lambda s, blk, ex, ft, oblk, oft, ok: (blk[s], 0)),
                      pl.BlockSpec((None, d, tf), lambda s, blk, ex, ft, oblk, oft, ok: (ex[s], 0, ft[s])),
                      pl.BlockSpec((None, d, tf), lambda s, blk, ex, ft, oblk, oft, ok: (ex[s], 0, ft[s]))],
            out_specs=pl.BlockSpec((bm, tf), lambda s, blk, ex, ft, oblk, oft, ok: (oblk[s], oft[s]))),
        out_shape=jax.ShapeDtypeStruct((cap, dff), BF16),
        compiler_params=_cparams("arbitrary"),
        name="e1",
    )(blk_of, e_of, ft_of, oblk_of, oft_of, ok1.astype(jnp.int32), xb, w_gate, w_up)

    blk = jnp.arange(n_blocks, dtype=jnp.int32)
    b_idx = jnp.minimum(blk, used - 1)
    bend = jnp.cumsum(nb)
    e2_of = jnp.minimum(jnp.sum(b_idx[:, None] >= bend[None, :], axis=1), n_exp - 1).astype(jnp.int32)
    ok2 = (blk < used).astype(jnp.int32)
    yb = pl.pallas_call(
        _e2_kernel,
        grid_spec=pltpu.PrefetchScalarGridSpec(
            num_scalar_prefetch=4,
            grid=(n_blocks,),
            in_specs=[pl.BlockSpec((bm, dff), lambda s, hb, ex, ob, ok: (hb[s], 0)),
                      pl.BlockSpec((None, dff, d), lambda s, hb, ex, ob, ok: (ex[s], 0, 0))],
            out_specs=pl.BlockSpec((bm, d), lambda s, hb, ex, ob, ok: (ob[s], 0))),
        out_shape=jax.ShapeDtypeStruct((cap, d), F32),
        compiler_params=_cparams("arbitrary"),
        name="e2",
    )(b_idx, e2_of, blk, ok2, hmid, w_down)
    return yb


def _combine_kernel(dest_ref, yb_hbm, h_ref, rw_ref, g_ref, nw_ref, sh_ref, sc_ref, o_ref, rows, sem, *, tokens):
    def issue(t, carry):
        for k in range(TOP_K):
            pltpu.make_async_copy(yb_hbm.at[pl.ds(dest_ref[0, TOP_K * t + k], 1)],
                                  rows.at[k, pl.ds(t, 1)], sem).start()
        return carry

    lax.fori_loop(0, tokens, issue, 0)

    def drain(t, carry):
        for k in range(TOP_K):
            pltpu.make_async_copy(yb_hbm.at[pl.ds(0, 1)], rows.at[k, pl.ds(0, 1)], sem).wait()
        return carry

    lax.fori_loop(0, tokens, drain, 0)

    rw = rw_ref[...]
    ffn = rows[0] * rw[:, 0:1]
    for k in range(1, TOP_K):
        ffn = ffn + rows[k] * rw[:, k:k + 1]
    h2 = h_ref[...] + g_ref[...] * ffn
    o_ref[...] = _rms_mod(h2, nw_ref[...], sh_ref[...], sc_ref[...])


def _combine(yb, dest, h, route_w, gate, norm_w, shift, scale, tokens=256):
    b, s, d = h.shape
    n = b * s
    per_seq = s // tokens
    kern = functools.partial(_combine_kernel, tokens=tokens)
    vec = pl.BlockSpec((None, 1, d), lambda i: (i // per_seq, 0, 0))
    out = pl.pallas_call(
        kern,
        grid=(n // tokens,),
        in_specs=[pl.BlockSpec((None, 1, TOP_K * tokens), lambda i: (i, 0, 0), memory_space=pltpu.SMEM),
                  pl.BlockSpec(memory_space=pl.ANY),
                  pl.BlockSpec((tokens, d), lambda i: (i, 0)),
                  pl.BlockSpec((tokens, LANES), lambda i: (i, 0)),
                  vec,
                  pl.BlockSpec((1, d), lambda i: (0, 0)), vec, vec],
        out_specs=pl.BlockSpec((tokens, d), lambda i: (i, 0)),
        out_shape=jax.ShapeDtypeStruct((n, d), F32),
        scratch_shapes=[pltpu.VMEM((TOP_K, tokens, d), F32), pltpu.SemaphoreType.DMA(())],
        compiler_params=_cparams("arbitrary"),
        name="combine",
    )(dest.reshape(n // tokens, 1, TOP_K * tokens), yb, h.reshape(n, d), route_w,
      gate.reshape(b, 1, d), norm_w.reshape(1, d), shift.reshape(b, 1, d), scale.reshape(b, 1, d))
    return out.reshape(b, s, d)


def _layer(h, mod, norm1_w, w_in, conv_w, ml_gate_bias, ml_norm_w, w_out, norm2_w,
           w_router_group, b_router_group, w_router_expert, b_router_expert,
           w_exp_gate, w_exp_up, w_exp_down, final):
    b, s, d = h.shape
    n = b * s
    sh1, sc1, g1, sh2, sc2, g2 = jnp.split(mod, 6, axis=-1)
    sb_w = SB_HEADS * SB_HEAD_DIM
    ml_w = ML_HEADS * ML_HEAD_DIM

    u = _norm_mod(h, norm1_w, sh1, sc1, BF16).reshape(n, d)
    tm = 1024 if n % 1024 == 0 else 512
    sb_qkv = _matmul(u, w_in, 0, 3 * sb_w, BF16, tm, 512)
    ml_qk = _matmul(u, w_in, 3 * sb_w, 2 * ml_w, F32, tm, 512)
    ml_v = _matmul(u, w_in, 3 * sb_w + 2 * ml_w, ml_w, BF16, tm, 512)
    ml_o = _matmul(u, w_in, 3 * sb_w + 3 * ml_w, ml_w, F32, tm, 512)
    n_gate = 2 * ML_HEADS
    w_gate = jnp.pad(w_in[:, 3 * sb_w + 4 * ml_w:], ((0, 0), (0, LANES - n_gate)))
    b_gate = jnp.pad(ml_gate_bias, (0, LANES - n_gate)).reshape(1, LANES)
    gates = _matmul(u, w_gate, 0, LANES, F32, tm, LANES, bias=b_gate)[:, :n_gate]

    y_sb = _sb_attention(sb_qkv.reshape(b, s, 3 * sb_w))
    y_ml = _mlstm(ml_qk.reshape(b, s, 2 * ml_w), ml_v.reshape(b, s, ml_w), ml_o.reshape(b, s, ml_w),
                  gates.reshape(b, s, n_gate), conv_w, ml_norm_w)
    h = _out_proj(y_sb, y_ml, w_out.astype(BF16), h, g1, tm=min(1024, s))

    w_router = jnp.pad(jnp.concatenate([w_router_group, w_router_expert], axis=1),
                       ((0, 0), (0, LANES - N_GROUPS - N_EXPERTS)))
    b_router = jnp.pad(jnp.concatenate([b_router_group, b_router_expert]),
                       (0, LANES - N_GROUPS - N_EXPERTS)).reshape(1, LANES)
    u2, route_i, route_w, counts = _router(h, norm2_w, sh2, sc2, w_router, b_router)
    counts = counts[0, :N_EXPERTS]
    blocks_per_expert = (counts + MOE_BLOCK - 1) // MOE_BLOCK
    padded = blocks_per_expert * MOE_BLOCK
    pstarts = jnp.cumsum(padded) - padded
    dest = pstarts[route_i[:, 0:TOP_K]] + route_i[:, TOP_K:2 * TOP_K]
    cap = (-(-(n * TOP_K) // MOE_BLOCK) + N_EXPERTS) * MOE_BLOCK
    tail_block_row = jnp.where(counts % MOE_BLOCK != 0, pstarts + padded - MOE_BLOCK, -1)
    spare_row = jnp.sum(padded) + MOE_BLOCK * jnp.arange(N_EXPERTS, dtype=jnp.int32)
    spare_row = jnp.where(spare_row < cap, spare_row, -1)
    clear_rows = jnp.concatenate([tail_block_row, spare_row]).astype(jnp.int32)
    xb = _dispatch(u2, dest, clear_rows, cap)
    yb = _experts(xb, blocks_per_expert, w_exp_gate, w_exp_up, w_exp_down)
    norm_w, shift, scale = final
    return _combine(yb, dest, h, route_w, g2, norm_w, shift, scale)


def kernel(x, c, norm1_w, w_in, conv_w, ml_gate_bias, ml_norm_w, w_out, norm2_w, w_router_group, b_router_group, w_router_expert, b_router_expert, w_exp_gate, w_exp_up, w_exp_down, w_ada, b_ada, final_norm_w, w_ada_final, b_ada_final):
    b, s, d = x.shape
    depth = w_in.shape[0]
    assert depth == 1, "the final norm is fused into the (single) layer's combine step"
    c_rep = jnp.broadcast_to(c[:, :, None], (b, d, LANES))
    fmod = _ada(c_rep, w_ada_final, b_ada_final)
    sh_f, sc_f = jnp.split(fmod, 2, axis=-1)
    mod = _ada(c_rep, w_ada[0], b_ada[0])
    return _layer(x, mod, norm1_w[0], w_in[0], conv_w[0], ml_gate_bias[0], ml_norm_w[0], w_out[0], norm2_w[0],
                  w_router_group[0], b_router_group[0], w_router_expert[0], b_router_expert[0],
                  w_exp_gate[0], w_exp_up[0], w_exp_down[0], (final_norm_w, sh_f, sc_f))
```

```python
import functools
import math

import jax
import jax.numpy as jnp
from jax import lax
from jax.experimental import pallas as pl
from jax.experimental.pallas import tpu as pltpu

SB_HEADS = 16
SB_HEAD_DIM = 128
ML_HEADS = 4
ML_HEAD_DIM = 512
CONV_WIDTH = 4
N_GROUPS = 4
EXPERTS_PER_GROUP = 8
N_EXPERTS = N_GROUPS * EXPERTS_PER_GROUP
TOP_K = 2
NORM_EPS = 1e-6

LANES = 128
SUBLANES = 8
VMEM_LIMIT = 56 * 1024 * 1024

ATTN_SUB = 128
ATTN_SUBS_PER_STEP = 8
ATTN_ALWAYS = 3
ML_CHUNK = 256
MOE_BLOCK = 256
UNDERFLOW_LOG = -90.0

F32 = jnp.float32
BF16 = jnp.bfloat16


def _cparams(*sem):
    return pltpu.CompilerParams(dimension_semantics=sem, vmem_limit_bytes=VMEM_LIMIT)


def _split_bf16(x):
    hi = x.astype(BF16)
    lo = (x - hi.astype(F32)).astype(BF16)
    return hi, lo


_HIGH_HALF = -65536


def _pack_bf16_pairs(x):
    k = x.shape[1] // 2
    bits = lambda v: lax.bitcast_convert_type(v.astype(F32), jnp.int32)
    return lax.shift_right_logical(bits(x[:, :k]), 16) | (bits(x[:, k:]) & _HIGH_HALF)


def _unpack_bf16_pairs(w):
    lo = lax.bitcast_convert_type(lax.shift_left(w, 16), F32)
    hi = lax.bitcast_convert_type(w & _HIGH_HALF, F32)
    return lo, hi


def _log_sigmoid(x):
    return jnp.minimum(x, 0.0) - jnp.log1p(jnp.exp(-jnp.abs(x)))


def _ada_kernel(c_ref, w_ref, b_ref, o_ref):
    nb, tn = o_ref.shape
    rows = []
    for b in range(nb):
        cb = c_ref[b]
        cb = cb * jax.nn.sigmoid(cb)
        pieces = [jnp.sum(w_ref[:, j * LANES:(j + 1) * LANES] * cb, axis=0, keepdims=True)
                  for j in range(tn // LANES)]
        rows.append(jnp.concatenate(pieces, axis=1))
    o_ref[...] = jnp.concatenate(rows, axis=0) + b_ref[...]


def _ada(c_rep, w, bias, tn=512):
    nb, k, _ = c_rep.shape
    n_out = w.shape[1]
    return pl.pallas_call(
        _ada_kernel,
        grid=(n_out // tn,),
        in_specs=[pl.BlockSpec((nb, k, LANES), lambda j: (0, 0, 0)),
                  pl.BlockSpec((k, tn), lambda j: (0, j)),
                  pl.BlockSpec((1, tn), lambda j: (0, j))],
        out_specs=pl.BlockSpec((nb, tn), lambda j: (0, j)),
        out_shape=jax.ShapeDtypeStruct((nb, n_out), F32),
        compiler_params=_cparams("arbitrary"),
        name="ada",
    )(c_rep, w, bias.reshape(1, n_out))


def _rms_mod(x, w, shift, scale):
    var = jnp.mean(x * x, axis=-1, keepdims=True)
    y = x * lax.rsqrt(var + NORM_EPS) * w
    return y * (1.0 + scale) + shift


def _norm_mod_kernel(x_ref, w_ref, sh_ref, sc_ref, o_ref):
    o_ref[...] = _rms_mod(x_ref[...], w_ref[...], sh_ref[...], sc_ref[...]).astype(o_ref.dtype)


def _norm_mod(x, w, shift, scale, out_dtype, tm=256):
    b, s, d = x.shape
    vec = pl.BlockSpec((None, 1, d), lambda bi, i: (bi, 0, 0))
    return pl.pallas_call(
        _norm_mod_kernel,
        grid=(b, s // tm),
        in_specs=[pl.BlockSpec((None, tm, d), lambda bi, i: (bi, i, 0)),
                  pl.BlockSpec((1, d), lambda bi, i: (0, 0)), vec, vec],
        out_specs=pl.BlockSpec((None, tm, d), lambda bi, i: (bi, i, 0)),
        out_shape=jax.ShapeDtypeStruct((b, s, d), out_dtype),
        compiler_params=_cparams("arbitrary", "arbitrary"),
        name="norm_mod",
    )(x, w.reshape(1, d), shift.reshape(b, 1, d), scale.reshape(b, 1, d))


def _matmul_kernel(a_ref, w_ref, *rest, has_bias):
    b_ref, o_ref, w16 = rest if has_bias else (None,) + rest

    @pl.when(pl.program_id(1) == 0)
    def _():
        w16[...] = w_ref[...].T.astype(BF16)

    acc = jnp.dot(a_ref[...], w16[...], preferred_element_type=F32)
    if has_bias:
        acc = acc + b_ref[...]
    o_ref[...] = acc.astype(o_ref.dtype)


def _matmul(a, w_t, col0, ncols, out_dtype, tm, tn, bias=None):
    m, k = a.shape
    j0 = col0 // tn
    in_specs = [pl.BlockSpec((tm, k), lambda j, i: (i, 0)),
                pl.BlockSpec((tn, k), lambda j, i: (j + j0, 0))]
    args = [a, w_t]
    if bias is not None:
        in_specs.append(pl.BlockSpec((1, tn), lambda j, i: (0, j + j0)))
        args.append(bias)
    return pl.pallas_call(
        functools.partial(_matmul_kernel, has_bias=bias is not None),
        grid=(ncols // tn, m // tm),
        in_specs=in_specs,
        out_specs=pl.BlockSpec((tm, tn), lambda j, i: (i, j)),
        out_shape=jax.ShapeDtypeStruct((m, ncols), out_dtype),
        scratch_shapes=[pltpu.VMEM((k, tn), BF16)],
        compiler_params=_cparams("arbitrary", "arbitrary"),
        name="proj",
    )(*args)


def _sb_attn_kernel(q_ref, k_ref, v_ref, tri_ref, o_ref, acc_ref, r_ref, *, sub, n_sub, scale):
    first = pl.program_id(2) * n_sub
    tri = tri_ref[...]
    row = lax.broadcasted_iota(jnp.int32, (sub, sub), 0)
    col = lax.broadcasted_iota(jnp.int32, (sub, sub), 1)
    causal = col < row

    def visit(q, blk, r, mask):
        start = pl.multiple_of(jnp.maximum(blk, 0) * sub, sub)
        kb = k_ref[pl.ds(start, sub), :]
        vb = v_ref[pl.ds(start, sub), :]
        z = lax.dot_general(q, kb, (((1,), (1,)), ((), ())), preferred_element_type=F32) * scale
        log_keep = -(jnp.maximum(z, 0.0) + jnp.log(1.0 + jnp.exp(-jnp.abs(z))))
        log_beta = log_keep + z
        if mask is not None:
            log_keep = jnp.where(mask, log_keep, 0.0)
        hi, lo = _split_bf16(log_keep)
        between = (jnp.dot(hi, tri, preferred_element_type=F32)
                   + jnp.dot(lo, tri, preferred_element_type=F32))
        w = jnp.exp(log_beta + between + r)
        if mask is not None:
            w = jnp.where(mask, w, 0.0)
        pv = jnp.dot(w.astype(BF16), vb, preferred_element_type=F32)
        return pv, r + jnp.sum(log_keep, axis=1, keepdims=True)

    n_win = ATTN_ALWAYS
    roww = lax.broadcasted_iota(jnp.int32, (sub, n_win * sub), 0)
    colw = lax.broadcasted_iota(jnp.int32, (sub, n_win * sub), 1)

    def win_start(s):
        blk = first + s
        clamp = s < n_win - 1
        first_blk = jnp.maximum(blk - (n_win - 1), 0) if clamp else blk - (n_win - 1)
        return pl.multiple_of(first_blk * sub, sub), clamp

    def masked(s, x):
        start, clamp = win_start(s)
        if clamp:
            return jnp.where(start + colw < (first + s) * sub + roww, x, 0.0)
        last = (n_win - 1) * sub
        return jnp.concatenate([x[:, :last], jnp.where(causal, x[:, last:], 0.0)], axis=1)

    scores = []
    for s in range(n_sub):
        kw = k_ref[pl.ds(win_start(s)[0], n_win * sub), :]
        q = q_ref[s * sub:(s + 1) * sub, :]
        scores.append(lax.dot_general(q, kw, (((1,), (1,)), ((), ())), preferred_element_type=F32) * scale)

    staged = []
    for s in range(n_sub):
        z = scores[s]
        log_keep = -(jnp.maximum(z, 0.0) + jnp.log(1.0 + jnp.exp(-jnp.abs(z))))
        log_beta = log_keep + z
        log_keep = masked(s, log_keep)
        chunks = [log_keep[:, c * sub:(c + 1) * sub] for c in range(n_win)]
        hi, lo = _split_bf16(jnp.concatenate(chunks, axis=0))
        inside = (jnp.dot(hi, tri, preferred_element_type=F32)
                  + jnp.dot(lo, tri, preferred_element_type=F32))
        staged.append((log_beta, chunks, inside))

    r_top = None
    for s in range(n_sub):
        log_beta, chunks, inside = staged[s]
        later = 0.0
        between = [None] * n_win
        for c in reversed(range(n_win)):
            between[c] = inside[c * sub:(c + 1) * sub, :] + later
            later = later + jnp.sum(chunks[c], axis=1, keepdims=True)
        w = masked(s, jnp.exp(log_beta + jnp.concatenate(between, axis=1)))
        vw = v_ref[pl.ds(win_start(s)[0], n_win * sub), :]
        acc_ref[s] = jnp.dot(w.astype(BF16), vw, preferred_element_type=F32)
        r = jnp.broadcast_to(later, (sub, LANES))
        r_ref[s] = r
        r_top = r if r_top is None else jnp.maximum(r_top, r)

    @pl.when(jnp.max(r_top) > UNDERFLOW_LOG)
    def _():
        for s in range(n_sub):
            q = q_ref[s * sub:(s + 1) * sub, :]

            def cond(carry):
                j, r_max = carry
                return jnp.logical_and(j >= 0, r_max > UNDERFLOW_LOG)

            def body(carry, s=s, q=q):
                j, _ = carry
                pv, r = visit(q, j, r_ref[s], None)
                acc_ref[s] += pv
                r_ref[s] = r
                return j - 1, jnp.max(r)

            lax.while_loop(cond, body, (first + s - ATTN_ALWAYS, jnp.max(r_ref[s])))

    for s in range(n_sub):
        o_ref[s * sub:(s + 1) * sub, :] = acc_ref[s].astype(o_ref.dtype)


def _sb_attention(qkv, sub=ATTN_SUB, n_sub=ATTN_SUBS_PER_STEP):
    b, s, _ = qkv.shape
    h, d = SB_HEADS, SB_HEAD_DIM
    assert sub == LANES, "the carried row sums are kept lane-replicated at the key sub-block width"
    assert s >= ATTN_ALWAYS * sub
    n_sub = min(n_sub, s // sub)
    tq = sub * n_sub
    idx = lax.broadcasted_iota(jnp.int32, (sub, sub), 0) > lax.broadcasted_iota(jnp.int32, (sub, sub), 1)
    tri = idx.astype(BF16)
    kern = functools.partial(_sb_attn_kernel, sub=sub, n_sub=n_sub, scale=1.0 / math.sqrt(d))
    return pl.pallas_call(
        kern,
        grid=(b, h, s // tq),
        in_specs=[pl.BlockSpec((None, tq, d), lambda bi, hi, qi: (bi, qi, hi)),
                  pl.BlockSpec((None, s, d), lambda bi, hi, qi: (bi, 0, h + hi)),
                  pl.BlockSpec((None, s, d), lambda bi, hi, qi: (bi, 0, 2 * h + hi)),
                  pl.BlockSpec((sub, sub), lambda bi, hi, qi: (0, 0))],
        out_specs=pl.BlockSpec((None, tq, d), lambda bi, hi, qi: (bi, qi, hi)),
        out_shape=jax.ShapeDtypeStruct((b, s, h * d), BF16),
        scratch_shapes=[pltpu.VMEM((n_sub, sub, d), F32), pltpu.VMEM((n_sub, sub, LANES), F32)],
        compiler_params=_cparams("arbitrary", "arbitrary", "arbitrary"),
        name="sb_attn",
    )(qkv, qkv, qkv, tri)


def _round_robin(stagewise):
    out = [None] * len(stagewise)
    live = list(range(len(stagewise)))
    while live:
        for i in list(live):
            try:
                next(stagewise[i])
            except StopIteration as stop:
                out[i] = stop.value
                live.remove(i)
    return out


def _mlstm_head(head, q, k, vb, og, gcol, grow, nw, tril_ref, triu_ref, c_st, n_st, m_st, *, chunk):
    li_col = gcol[:, head:head + 1]
    lf_col = _log_sigmoid(gcol[:, ML_HEADS + head:ML_HEADS + head + 1])
    li_row = grow[head:head + 1, :]
    lf_row = _log_sigmoid(grow[ML_HEADS + head:ML_HEADS + head + 1, :])

    hi, lo = _split_bf16(jnp.broadcast_to(lf_col, (chunk, LANES)))
    b_col = (jnp.dot(tril_ref[...], hi, preferred_element_type=F32)
             + jnp.dot(tril_ref[...], lo, preferred_element_type=F32))[:, 0:1]
    hi, lo = _split_bf16(jnp.broadcast_to(lf_row, (2 * SUBLANES, chunk)))
    b_row = (jnp.dot(hi, triu_ref[...], preferred_element_type=F32)
             + jnp.dot(lo, triu_ref[...], preferred_element_type=F32))[0:1, :]
    qb = q.astype(BF16)
    kb = k.astype(BF16)
    qk = lax.dot_general(qb, kb, (((1,), (1,)), ((), ())), preferred_element_type=F32)
    q_mem = jnp.dot(qb, c_st[...].astype(BF16), preferred_element_type=F32)
    yield

    m_prev = m_st[...]
    a_col = b_col + m_prev
    row_t = lax.broadcasted_iota(jnp.int32, (chunk, chunk), 0)
    col_s = lax.broadcasted_iota(jnp.int32, (chunk, chunk), 1)
    d_mat = jnp.where(col_s <= row_t, b_col - b_row + li_row, -jnp.inf)
    m_col = jnp.maximum(a_col, jnp.max(d_mat, axis=1, keepdims=True))
    w_intra = jnp.exp(d_mat - m_col)
    w_inter = jnp.exp(a_col - m_col)
    sc = qk * w_intra
    sc_v = jnp.dot(sc.astype(BF16), vb, preferred_element_type=F32)

    b_last = b_col[chunk - 1:chunk, :]
    g_col = b_last - b_col + li_col
    m_new = jnp.maximum(b_last + m_prev, jnp.max(g_col, axis=0, keepdims=True))
    decay = jnp.exp(b_last + m_prev - m_new)
    kw = k * jnp.exp(g_col - m_new)
    kw_v = lax.dot_general(kw.astype(BF16), vb, (((0,), (0,)), ((), ())), preferred_element_type=F32)
    yield

    num = w_inter * q_mem + sc_v
    den = (w_inter * jnp.sum(q * n_st[...], axis=1, keepdims=True)
           + jnp.sum(sc, axis=1, keepdims=True))
    h_out = num / jnp.maximum(jnp.abs(den), jnp.exp(-m_col))
    c_st[...] = decay * c_st[...] + kw_v
    n_st[...] = decay * n_st[...] + jnp.sum(kw, axis=0, keepdims=True)
    m_st[...] = m_new

    hn = h_out * lax.rsqrt(jnp.mean(h_out * h_out, axis=1, keepdims=True) + NORM_EPS) * nw
    return jax.nn.sigmoid(og) * hn


def _mlstm_kernel(qp_ref, kp_ref, v_ref, og_ref, gcol_ref, grow_ref, cwq_ref, cwk_ref, nw_ref,
                  tril_ref, triu_ref, y_ref, c_st, n_st, m_st, qbuf, kbuf, *, chunk, heads, dim):
    halo = SUBLANES

    @pl.when(pl.program_id(1) == 0)
    def _():
        c_st[...] = jnp.zeros_like(c_st)
        n_st[...] = jnp.zeros_like(n_st)
        m_st[...] = jnp.zeros_like(m_st)
        qbuf[0:halo, :] = jnp.zeros((halo, heads * dim), F32)
        kbuf[0:halo, :] = jnp.zeros((halo, heads * dim), F32)

    qbuf[halo:halo + chunk, :] = qp_ref[...]
    kbuf[halo:halo + chunk, :] = kp_ref[...]

    def conv_silu(buf, cw_ref):
        acc = buf[halo:halo + chunk, :] * cw_ref[CONV_WIDTH - 1:CONV_WIDTH, :]
        for j in range(CONV_WIDTH - 1):
            off = halo - (CONV_WIDTH - 1) + j
            acc = acc + buf[off:off + chunk, :] * cw_ref[j:j + 1, :]
        return acc * jax.nn.sigmoid(acc)

    q = conv_silu(qbuf, cwq_ref)
    k = conv_silu(kbuf, cwk_ref) * (1.0 / math.sqrt(dim))
    qbuf[0:halo, :] = qbuf[chunk:chunk + halo, :]
    kbuf[0:halo, :] = kbuf[chunk:chunk + halo, :]

    gcol = gcol_ref[...]
    grow = grow_ref[...]
    cols = [slice(head * dim, (head + 1) * dim) for head in range(heads)]
    ys = _round_robin([
        _mlstm_head(head, q[:, c], k[:, c], v_ref[:, c], og_ref[:, c], gcol, grow, nw_ref[:, c],
                    tril_ref, triu_ref, c_st.at[head], n_st.at[head], m_st.at[head], chunk=chunk)
        for head, c in enumerate(cols)])
    for y, c in zip(ys, cols):
        y_ref[:, c] = y.astype(y_ref.dtype)


def _mlstm(ml_qk, ml_v, ml_o, gates, conv_w, ml_norm_w, chunk=ML_CHUNK):
    b, s, _ = ml_v.shape
    h, d = ML_HEADS, ML_HEAD_DIM
    w = h * d
    gates_t = jnp.swapaxes(gates, 1, 2)
    r = lax.broadcasted_iota(jnp.int32, (chunk, chunk), 0)
    c = lax.broadcasted_iota(jnp.int32, (chunk, chunk), 1)
    tril = (c <= r).astype(BF16)
    triu = (r <= c).astype(BF16)
    wide = lambda off: pl.BlockSpec((None, chunk, w), lambda bi, ci: (bi, ci, off))
    const = lambda shape, off=0: pl.BlockSpec(shape, lambda bi, ci: (0, off))
    kern = functools.partial(_mlstm_kernel, chunk=chunk, heads=h, dim=d)
    return pl.pallas_call(
        kern,
        grid=(b, s // chunk),
        in_specs=[wide(0), wide(1), wide(0), wide(0),
                  pl.BlockSpec((None, chunk, 2 * h), lambda bi, ci: (bi, ci, 0)),
                  pl.BlockSpec((None, 2 * h, chunk), lambda bi, ci: (bi, 0, ci)),
                  const((CONV_WIDTH, w)), const((CONV_WIDTH, w), 1), const((1, w)),
                  const((chunk, chunk)), const((chunk, chunk))],
        out_specs=wide(0),
        out_shape=jax.ShapeDtypeStruct((b, s, w), BF16),
        scratch_shapes=[pltpu.VMEM((h, d, d), F32), pltpu.VMEM((h, 1, d), F32), pltpu.VMEM((h, 1, 1), F32),
                        pltpu.VMEM((chunk + SUBLANES, w), F32), pltpu.VMEM((chunk + SUBLANES, w), F32)],
        compiler_params=_cparams("arbitrary", "arbitrary"),
        name="mlstm",
    )(ml_qk, ml_qk, ml_v, ml_o, gates, gates_t, conv_w, conv_w, ml_norm_w.reshape(1, w), tril, triu)


def _out_proj_kernel(ya_ref, yb_ref, wa_ref, wb_ref, x_ref, g_ref, o_ref):
    mix = (jnp.dot(ya_ref[...], wa_ref[...], preferred_element_type=F32)
           + jnp.dot(yb_ref[...], wb_ref[...], preferred_element_type=F32))
    o_ref[...] = x_ref[...] + g_ref[...] * mix


def _out_proj(y_sb, y_ml, w_out, x, gate, tm, tn=512):
    b, s, d = x.shape
    ka, kb = y_sb.shape[-1], y_ml.shape[-1]
    assert ka == kb
    n = b * s
    rows_per_seq = s // tm
    out = pl.pallas_call(
        _out_proj_kernel,
        grid=(n // tm, d // tn),
        in_specs=[pl.BlockSpec((tm, ka), lambda i, j: (i, 0)),
                  pl.BlockSpec((tm, kb), lambda i, j: (i, 0)),
                  pl.BlockSpec((ka, tn), lambda i, j: (0, j)),
                  pl.BlockSpec((kb, tn), lambda i, j: (1, j)),
                  pl.BlockSpec((tm, tn), lambda i, j: (i, j)),
                  pl.BlockSpec((None, 1, tn), lambda i, j: (i // rows_per_seq, 0, j))],
        out_specs=pl.BlockSpec((tm, tn), lambda i, j: (i, j)),
        out_shape=jax.ShapeDtypeStruct((n, d), F32),
        compiler_params=_cparams("arbitrary", "arbitrary"),
        name="out_proj",
    )(y_sb.reshape(n, ka), y_ml.reshape(n, kb), w_out, w_out, x.reshape(n, d), gate.reshape(b, 1, d))
    return out.reshape(b, s, d)


def _router_kernel(h_ref, nw_ref, sh_ref, sc_ref, wr_ref, br_ref, tri_ref, u_ref, ri_ref, rw_ref, cnt_ref, run_ref):
    first = jnp.logical_and(pl.program_id(0) == 0, pl.program_id(1) == 0)

    @pl.when(first)
    def _():
        run_ref[...] = jnp.zeros_like(run_ref)

    u = _rms_mod(h_ref[...], nw_ref[...], sh_ref[...], sc_ref[...])
    tm = u.shape[0]
    uh, ul = _split_bf16(u)
    u_ref[...] = _pack_bf16_pairs(uh)
    wh, wl = _split_bf16(wr_ref[...])
    logits = (jnp.dot(uh, wh, preferred_element_type=F32) + jnp.dot(ul, wh, preferred_element_type=F32)
              + jnp.dot(uh, wl, preferred_element_type=F32) + br_ref[...])

    lane = lax.broadcasted_iota(jnp.int32, (tm, LANES), 1)
    lane_f = lane.astype(F32)
    neg = -jnp.inf
    first_lane = lambda hit: jnp.min(jnp.where(hit, lane_f, float(LANES)), axis=1, keepdims=True)

    gl = jnp.where(lane < N_GROUPS, logits, neg)
    g_max = jnp.max(gl, axis=1, keepdims=True)
    g_p = 1.0 / jnp.sum(jnp.exp(gl - g_max), axis=1, keepdims=True)
    g_sel = first_lane(gl == g_max)

    lo_lane = N_GROUPS + EXPERTS_PER_GROUP * g_sel
    el = jnp.where(jnp.logical_and(lane_f >= lo_lane, lane_f < lo_lane + EXPERTS_PER_GROUP), logits, neg)
    e1_max = jnp.max(el, axis=1, keepdims=True)
    l1 = first_lane(el == e1_max)
    el2 = jnp.where(lane_f == l1, neg, el)
    e2_max = jnp.max(el2, axis=1, keepdims=True)
    l2 = first_lane(el2 == e2_max)
    z_sum = jnp.sum(jnp.exp(el - e1_max), axis=1, keepdims=True)
    p1 = 1.0 / z_sum
    p2 = jnp.exp(e2_max - e1_max) / z_sum
    w1 = p1 / (p1 + p2) * g_p
    w2 = p2 / (p1 + p2) * g_p
    e1 = l1 - N_GROUPS
    e2 = l2 - N_GROUPS

    hit1 = lane_f == e1
    hit2 = lane_f == e2
    onehot = jnp.where(jnp.logical_or(hit1, hit2), 1.0, 0.0)
    before = jnp.dot(tri_ref[...], onehot.astype(BF16), preferred_element_type=F32) + run_ref[...]
    rank1 = jnp.sum(jnp.where(hit1, before, 0.0), axis=1, keepdims=True)
    rank2 = jnp.sum(jnp.where(hit2, before, 0.0), axis=1, keepdims=True)
    run = run_ref[...] + jnp.sum(onehot, axis=0, keepdims=True)
    run_ref[...] = run

    ri_ref[...] = jnp.where(lane == 0, e1, jnp.where(lane == 1, e2, jnp.where(lane == 2, rank1,
                            jnp.where(lane == 3, rank2, 0.0)))).astype(jnp.int32)
    rw_ref[...] = jnp.where(lane == 0, w1, jnp.where(lane == 1, w2, 0.0))
    cnt_ref[...] = jnp.broadcast_to(run, cnt_ref.shape).astype(jnp.int32)


def _router(h, norm_w, shift, scale, w_router, b_router, tm=256):
    b, s, d = h.shape
    n = b * s
    tri = (lax.broadcasted_iota(jnp.int32, (tm, tm), 1) < lax.broadcasted_iota(jnp.int32, (tm, tm), 0)).astype(BF16)
    vec = pl.BlockSpec((None, 1, d), lambda bi, i: (bi, 0, 0))
    rows = lambda width: pl.BlockSpec((tm, width), lambda bi, i: (bi * (s // tm) + i, 0))
    return pl.pallas_call(
        _router_kernel,
        grid=(b, s // tm),
        in_specs=[pl.BlockSpec((None, tm, d), lambda bi, i: (bi, i, 0)),
                  pl.BlockSpec((1, d), lambda bi, i: (0, 0)), vec, vec,
                  pl.BlockSpec((d, LANES), lambda bi, i: (0, 0)),
                  pl.BlockSpec((1, LANES), lambda bi, i: (0, 0)),
                  pl.BlockSpec((tm, tm), lambda bi, i: (0, 0))],
        out_specs=[rows(d // 2), rows(LANES), rows(LANES),
                   pl.BlockSpec((SUBLANES, LANES), lambda bi, i: (0, 0))],
        out_shape=[jax.ShapeDtypeStruct((n, d // 2), jnp.int32), jax.ShapeDtypeStruct((n, LANES), jnp.int32),
                   jax.ShapeDtypeStruct((n, LANES), F32), jax.ShapeDtypeStruct((SUBLANES, LANES), jnp.int32)],
        scratch_shapes=[pltpu.VMEM((1, LANES), F32)],
        compiler_params=_cparams("arbitrary", "arbitrary"),
        name="router",
    )(h, norm_w.reshape(1, d), shift.reshape(b, 1, d), scale.reshape(b, 1, d), w_router, b_router, tri)


ZERO_ROWS = 128


def _dispatch_kernel(tail_ref, dest_ref, u_ref, xb_hbm, zbuf, zsem, sem, *, tokens, block):
    n_exp = tail_ref.shape[0]

    def clear(e, half):
        row = pl.multiple_of(tail_ref[e] + half * ZERO_ROWS, ZERO_ROWS)
        return pltpu.make_async_copy(zbuf, xb_hbm.at[pl.ds(row, ZERO_ROWS)], zsem)

    @pl.when(pl.program_id(0) == 0)
    def _():
        zbuf[...] = jnp.zeros_like(zbuf)
        for start in (True, False):
            def each(e, carry):
                @pl.when(tail_ref[e] >= 0)
                def _():
                    for half in range(block // ZERO_ROWS):
                        clear(e, half).start() if start else clear(e, half).wait()
                return carry
            lax.fori_loop(0, n_exp, each, 0)

    def row_copy(t, k):
        return pltpu.make_async_copy(u_ref.at[pl.ds(t, 1)], xb_hbm.at[pl.ds(dest_ref[0, TOP_K * t + k], 1)], sem)

    def issue(t, carry):
        for k in range(TOP_K):
            row_copy(t, k).start()
        return carry

    lax.fori_loop(0, tokens, issue, 0)

    def drain(t, carry):
        for k in range(TOP_K):
            row_copy(t, k).wait()
        return carry

    lax.fori_loop(0, tokens, drain, 0)


def _dispatch(u, dest, tail_block_row, cap, tokens=256, block=MOE_BLOCK):
    n, d = u.shape
    steps = n // tokens
    kern = functools.partial(_dispatch_kernel, tokens=tokens, block=block)
    return pl.pallas_call(
        kern,
        grid_spec=pltpu.PrefetchScalarGridSpec(
            num_scalar_prefetch=1,
            grid=(steps,),
            in_specs=[pl.BlockSpec((None, 1, TOP_K * tokens), lambda i, tail: (i, 0, 0), memory_space=pltpu.SMEM),
                      pl.BlockSpec((tokens, d), lambda i, tail: (i, 0))],
            out_specs=pl.BlockSpec(memory_space=pl.ANY),
            scratch_shapes=[pltpu.VMEM((ZERO_ROWS, d), u.dtype), pltpu.SemaphoreType.DMA(()),
                            pltpu.SemaphoreType.DMA(())]),
        out_shape=jax.ShapeDtypeStruct((cap, d), u.dtype),
        compiler_params=_cparams("arbitrary"),
        name="dispatch",
    )(tail_block_row, dest.reshape(steps, 1, TOP_K * tokens), u)


def _e1_kernel(blk_ref, exp_ref, ft_ref, oblk_ref, oft_ref, ok_ref, x_ref, wg_ref, wu_ref, o_ref):
    del blk_ref, exp_ref, ft_ref, oblk_ref, oft_ref
    ok = ok_ref[pl.program_id(0)] == 1

    @pl.when(ok)
    def _():
        x_lo, x_hi = _unpack_bf16_pairs(x_ref[...])
        half = x_lo.shape[1]
        mm = lambda w_ref: (jnp.dot(x_lo, w_ref[:half, :], preferred_element_type=F32)
                            + jnp.dot(x_hi, w_ref[half:, :], preferred_element_type=F32))
        g = mm(wg_ref)
        up = mm(wu_ref)
        o_ref[...] = ((g * jax.nn.sigmoid(g)) * up).astype(o_ref.dtype)

    @pl.when(jnp.logical_not(ok))
    def _():
        o_ref[...] = jnp.zeros_like(o_ref)


def _e2_kernel(blk_ref, exp_ref, oblk_ref, ok_ref, h_ref, wd_ref, o_ref):
    del blk_ref, exp_ref, oblk_ref
    ok = ok_ref[pl.program_id(0)] == 1

    @pl.when(ok)
    def _():
        o_ref[...] = jnp.dot(h_ref[...].astype(F32), wd_ref[...], preferred_element_type=F32)

    @pl.when(jnp.logical_not(ok))
    def _():
        o_ref[...] = jnp.zeros_like(o_ref)


def _experts(xb, blocks_per_expert, w_gate, w_up, w_down, bm=MOE_BLOCK, tf=512):
    cap, dp = xb.shape
    n_exp, d, dff = w_gate.shape
    n_blocks = cap // bm
    n_ft = dff // tf
    nb = blocks_per_expert.astype(jnp.int32)
    bstart = jnp.cumsum(nb) - nb
    used = jnp.sum(nb)

    steps1 = n_blocks * n_ft
    step = jnp.arange(steps1, dtype=jnp.int32)
    ok1 = step < used * n_ft
    send = jnp.cumsum(nb * n_ft)
    s_idx = jnp.minimum(step, used * n_ft - 1)
    e_of = jnp.minimum(jnp.sum(s_idx[:, None] >= send[None, :], axis=1), n_exp - 1).astype(jnp.int32)
    local = s_idx - (send[e_of] - nb[e_of] * n_ft)
    nbe = jnp.maximum(nb[e_of], 1)
    ft_of = (local // nbe).astype(jnp.int32)
    blk_of = (bstart[e_of] + local % nbe).astype(jnp.int32)
    spare = step - used * n_ft
    oblk_of = jnp.where(ok1, blk_of, used + spare // n_ft).astype(jnp.int32)
    oft_of = jnp.where(ok1, ft_of, spare % n_ft).astype(jnp.int32)

    hmid = pl.pallas_call(
        _e1_kernel,
        grid_spec=pltpu.PrefetchScalarGridSpec(
            num_scalar_prefetch=6,
            grid=(steps1,),
            in_specs=[pl.BlockSpec((bm, dp), lambda s, blk, ex, ft, oblk, oft, ok: (blk[s], 0)),
                      pl.BlockSpec((None, d, tf), lambda s, blk, ex, ft, oblk, oft, ok: (ex[s], 0, ft[s])),
                      pl.BlockSpec((None, d, tf), lambda s, blk, ex, ft, oblk, oft, ok: (ex[s], 0, ft[s]))],
            out_specs=pl.BlockSpec((bm, tf), lambda s, blk, ex, ft, oblk, oft, ok: (oblk[s], oft[s]))),
        out_shape=jax.ShapeDtypeStruct((cap, dff), BF16),
        compiler_params=_cparams("arbitrary"),
        name="e1",
    )(blk_of, e_of, ft_of, oblk_of, oft_of, ok1.astype(jnp.int32), xb, w_gate, w_up)

    blk = jnp.arange(n_blocks, dtype=jnp.int32)
    b_idx = jnp.minimum(blk, used - 1)
    bend = jnp.cumsum(nb)
    e2_of = jnp.minimum(jnp.sum(b_idx[:, None] >= bend[None, :], axis=1), n_exp - 1).astype(jnp.int32)
    ok2 = (blk < used).astype(jnp.int32)
    yb = pl.pallas_call(
        _e2_kernel,
        grid_spec=pltpu.PrefetchScalarGridSpec(
            num_scalar_prefetch=4,
            grid=(n_blocks,),
            in_specs=[pl.BlockSpec((bm, dff), lambda s, hb, ex, ob, ok: (hb[s], 0)),
                      pl.BlockSpec((None, dff, d), lambda s, hb, ex, ob, ok: (ex[s], 0, 0))],
            out_specs=pl.BlockSpec((bm, d), lambda s, hb, ex, ob, ok: (ob[s], 0))),
        out_shape=jax.ShapeDtypeStruct((cap, d), F32),
        compiler_params=_cparams("arbitrary"),
        name="e2",
    )(b_idx, e2_of, blk, ok2, hmid, w_down)
    return yb


def _combine_kernel(dest_ref, yb_hbm, h_ref, rw_ref, g_ref, nw_ref, sh_ref, sc_ref, o_ref, rows, sem, *, tokens):
    def issue(t, carry):
        for k in range(TOP_K):
            pltpu.make_async_copy(yb_hbm.at[pl.ds(dest_ref[0, TOP_K * t + k], 1)],
                                  rows.at[k, pl.ds(t, 1)], sem).start()
        return carry

    lax.fori_loop(0, tokens, issue, 0)

    def drain(t, carry):
        for k in range(TOP_K):
            pltpu.make_async_copy(yb_hbm.at[pl.ds(0, 1)], rows.at[k, pl.ds(0, 1)], sem).wait()
        return carry

    lax.fori_loop(0, tokens, drain, 0)

    rw = rw_ref[...]
    ffn = rows[0] * rw[:, 0:1]
    for k in range(1, TOP_K):
        ffn = ffn + rows[k] * rw[:, k:k + 1]
    h2 = h_ref[...] + g_ref[...] * ffn
    o_ref[...] = _rms_mod(h2, nw_ref[...], sh_ref[...], sc_ref[...])


def _combine(yb, dest, h, route_w, gate, norm_w, shift, scale, tokens=256):
    b, s, d = h.shape
    n = b * s
    per_seq = s // tokens
    kern = functools.partial(_combine_kernel, tokens=tokens)
    vec = pl.BlockSpec((None, 1, d), lambda i: (i // per_seq, 0, 0))
    out = pl.pallas_call(
        kern,
        grid=(n // tokens,),
        in_specs=[pl.BlockSpec((None, 1, TOP_K * tokens), lambda i: (i, 0, 0), memory_space=pltpu.SMEM),
                  pl.BlockSpec(memory_space=pl.ANY),
                  pl.BlockSpec((tokens, d), lambda i: (i, 0)),
                  pl.BlockSpec((tokens, LANES), lambda i: (i, 0)),
                  vec,
                  pl.BlockSpec((1, d), lambda i: (0, 0)), vec, vec],
        out_specs=pl.BlockSpec((tokens, d), lambda i: (i, 0)),
        out_shape=jax.ShapeDtypeStruct((n, d), F32),
        scratch_shapes=[pltpu.VMEM((TOP_K, tokens, d), F32), pltpu.SemaphoreType.DMA(())],
        compiler_params=_cparams("arbitrary"),
        name="combine",
    )(dest.reshape(n // tokens, 1, TOP_K * tokens), yb, h.reshape(n, d), route_w,
      gate.reshape(b, 1, d), norm_w.reshape(1, d), shift.reshape(b, 1, d), scale.reshape(b, 1, d))
    return out.reshape(b, s, d)


def _layer(h, mod, norm1_w, w_in, conv_w, ml_gate_bias, ml_norm_w, w_out, norm2_w,
           w_router_group, b_router_group, w_router_expert, b_router_expert,
           w_exp_gate, w_exp_up, w_exp_down, final):
    b, s, d = h.shape
    n = b * s
    sh1, sc1, g1, sh2, sc2, g2 = jnp.split(mod, 6, axis=-1)
    sb_w = SB_HEADS * SB_HEAD_DIM
    ml_w = ML_HEADS * ML_HEAD_DIM

    u = _norm_mod(h, norm1_w, sh1, sc1, BF16).reshape(n, d)
    tm = 1024 if n % 1024 == 0 else 512
    w_in_t = w_in.T
    sb_qkv = _matmul(u, w_in_t, 0, 3 * sb_w, BF16, tm, 512)
    ml_qk = _matmul(u, w_in_t, 3 * sb_w, 2 * ml_w, F32, tm, 512)
    ml_v = _matmul(u, w_in_t, 3 * sb_w + 2 * ml_w, ml_w, BF16, tm, 512)
    ml_o = _matmul(u, w_in_t, 3 * sb_w + 3 * ml_w, ml_w, F32, tm, 512)
    n_gate = 2 * ML_HEADS
    w_gate = jnp.pad(w_in_t[3 * sb_w + 4 * ml_w:, :], ((0, LANES - n_gate), (0, 0)))
    b_gate = jnp.pad(ml_gate_bias, (0, LANES - n_gate)).reshape(1, LANES)
    gates = _matmul(u, w_gate, 0, LANES, F32, tm, LANES, bias=b_gate)[:, :n_gate]

    y_sb = _sb_attention(sb_qkv.reshape(b, s, 3 * sb_w))
    y_ml = _mlstm(ml_qk.reshape(b, s, 2 * ml_w), ml_v.reshape(b, s, ml_w), ml_o.reshape(b, s, ml_w),
                  gates.reshape(b, s, n_gate), conv_w, ml_norm_w)
    h = _out_proj(y_sb, y_ml, w_out.astype(BF16), h, g1, tm=min(1024, s))

    w_router = jnp.pad(jnp.concatenate([w_router_group, w_router_expert], axis=1),
                       ((0, 0), (0, LANES - N_GROUPS - N_EXPERTS)))
    b_router = jnp.pad(jnp.concatenate([b_router_group, b_router_expert]),
                       (0, LANES - N_GROUPS - N_EXPERTS)).reshape(1, LANES)
    u2, route_i, route_w, counts = _router(h, norm2_w, sh2, sc2, w_router, b_router)
    counts = counts[0, :N_EXPERTS]
    blocks_per_expert = (counts + MOE_BLOCK - 1) // MOE_BLOCK
    padded = blocks_per_expert * MOE_BLOCK
    pstarts = jnp.cumsum(padded) - padded
    dest = pstarts[route_i[:, 0:TOP_K]] + route_i[:, TOP_K:2 * TOP_K]
    cap = (-(-(n * TOP_K) // MOE_BLOCK) + N_EXPERTS) * MOE_BLOCK
    tail_block_row = jnp.where(counts % MOE_BLOCK != 0, pstarts + padded - MOE_BLOCK, -1)
    spare_row = jnp.sum(padded) + MOE_BLOCK * jnp.arange(N_EXPERTS, dtype=jnp.int32)
    spare_row = jnp.where(spare_row < cap, spare_row, -1)
    clear_rows = jnp.concatenate([tail_block_row, spare_row]).astype(jnp.int32)
    xb = _dispatch(u2, dest, clear_rows, cap)
    yb = _experts(xb, blocks_per_expert, w_exp_gate, w_exp_up, w_exp_down)
    norm_w, shift, scale = final
    return _combine(yb, dest, h, route_w, g2, norm_w, shift, scale)


def kernel(x, c, norm1_w, w_in, conv_w, ml_gate_bias, ml_norm_w, w_out, norm2_w, w_router_group, b_router_group, w_router_expert, b_router_expert, w_exp_gate, w_exp_up, w_exp_down, w_ada, b_ada, final_norm_w, w_ada_final, b_ada_final):
    b, s, d = x.shape
    depth = w_in.shape[0]
    assert depth == 1, "the final norm is fused into the (single) layer's combine step"
    c_rep = jnp.broadcast_to(c[:, :, None], (b, d, LANES))
    fmod = _ada(c_rep, w_ada_final, b_ada_final)
    sh_f, sc_f = jnp.split(fmod, 2, axis=-1)
    mod = _ada(c_rep, w_ada[0], b_ada[0])
    return _layer(x, mod, norm1_w[0], w_in[0], conv_w[0], ml_gate_bias[0], ml_norm_w[0], w_out[0], norm2_w[0],
                  w_router_group[0], b_router_group[0], w_router_expert[0], b_router_expert[0],
                  w_exp_gate[0], w_exp_up[0], w_exp_down[0], (final_norm_w, sh_f, sc_f))
```

```python
import functools
import math

import jax
import jax.numpy as jnp
from jax import lax
from jax.experimental import pallas as pl
from jax.experimental.pallas import tpu as pltpu

SB_HEADS = 16
SB_HEAD_DIM = 128
ML_HEADS = 4
ML_HEAD_DIM = 512
CONV_WIDTH = 4
N_GROUPS = 4
EXPERTS_PER_GROUP = 8
N_EXPERTS = N_GROUPS * EXPERTS_PER_GROUP
TOP_K = 2
NORM_EPS = 1e-6

LANES = 128
SUBLANES = 8
VMEM_LIMIT = 56 * 1024 * 1024

ATTN_SUB = 128
ATTN_SUBS_PER_STEP = 8
ATTN_ALWAYS = 3
ML_CHUNK = 256
MOE_BLOCK = 256
UNDERFLOW_LOG = -90.0

F32 = jnp.float32
BF16 = jnp.bfloat16


def _cparams(*sem):
    return pltpu.CompilerParams(dimension_semantics=sem, vmem_limit_bytes=VMEM_LIMIT)


def _split_bf16(x):
    hi = x.astype(BF16)
    lo = (x - hi.astype(F32)).astype(BF16)
    return hi, lo


_HIGH_HALF = -65536


def _pack_bf16_pairs(x):
    k = x.shape[1] // 2
    bits = lambda v: lax.bitcast_convert_type(v.astype(F32), jnp.int32)
    return lax.shift_right_logical(bits(x[:, :k]), 16) | (bits(x[:, k:]) & _HIGH_HALF)


def _unpack_bf16_pairs(w):
    lo = lax.bitcast_convert_type(lax.shift_left(w, 16), F32)
    hi = lax.bitcast_convert_type(w & _HIGH_HALF, F32)
    return lo, hi


def _log_sigmoid(x):
    return jnp.minimum(x, 0.0) - jnp.log1p(jnp.exp(-jnp.abs(x)))


def _ada_kernel(c_ref, w_ref, b_ref, o_ref):
    nb, tn = o_ref.shape
    rows = []
    for b in range(nb):
        cb = c_ref[b]
        cb = cb * jax.nn.sigmoid(cb)
        pieces = [jnp.sum(w_ref[:, j * LANES:(j + 1) * LANES] * cb, axis=0, keepdims=True)
                  for j in range(tn // LANES)]
        rows.append(jnp.concatenate(pieces, axis=1))
    o_ref[...] = jnp.concatenate(rows, axis=0) + b_ref[...]


def _ada(c_rep, w, bias, tn=512):
    nb, k, _ = c_rep.shape
    n_out = w.shape[1]
    return pl.pallas_call(
        _ada_kernel,
        grid=(n_out // tn,),
        in_specs=[pl.BlockSpec((nb, k, LANES), lambda j: (0, 0, 0)),
                  pl.BlockSpec((k, tn), lambda j: (0, j)),
                  pl.BlockSpec((1, tn), lambda j: (0, j))],
        out_specs=pl.BlockSpec((nb, tn), lambda j: (0, j)),
        out_shape=jax.ShapeDtypeStruct((nb, n_out), F32),
        compiler_params=_cparams("arbitrary"),
        name="ada",
    )(c_rep, w, bias.reshape(1, n_out))


def _rms_mod(x, w, shift, scale):
    var = jnp.mean(x * x, axis=-1, keepdims=True)
    y = x * lax.rsqrt(var + NORM_EPS) * w
    return y * (1.0 + scale) + shift


def _norm_mod_kernel(x_ref, w_ref, sh_ref, sc_ref, o_ref):
    o_ref[...] = _rms_mod(x_ref[...], w_ref[...], sh_ref[...], sc_ref[...]).astype(o_ref.dtype)


def _norm_mod(x, w, shift, scale, out_dtype, tm=256):
    b, s, d = x.shape
    vec = pl.BlockSpec((None, 1, d), lambda bi, i: (bi, 0, 0))
    return pl.pallas_call(
        _norm_mod_kernel,
        grid=(b, s // tm),
        in_specs=[pl.BlockSpec((None, tm, d), lambda bi, i: (bi, i, 0)),
                  pl.BlockSpec((1, d), lambda bi, i: (0, 0)), vec, vec],
        out_specs=pl.BlockSpec((None, tm, d), lambda bi, i: (bi, i, 0)),
        out_shape=jax.ShapeDtypeStruct((b, s, d), out_dtype),
        compiler_params=_cparams("arbitrary", "arbitrary"),
        name="norm_mod",
    )(x, w.reshape(1, d), shift.reshape(b, 1, d), scale.reshape(b, 1, d))


def _matmul_kernel(a_ref, w_ref, *rest, has_bias):
    b_ref, o_ref, w16 = rest if has_bias else (None,) + rest

    @pl.when(pl.program_id(1) == 0)
    def _():
        w16[...] = w_ref[...].T.astype(BF16)

    acc = jnp.dot(a_ref[...], w16[...], preferred_element_type=F32)
    if has_bias:
        acc = acc + b_ref[...]
    o_ref[...] = acc.astype(o_ref.dtype)


def _matmul(a, w_t, col0, ncols, out_dtype, tm, tn, bias=None):
    m, k = a.shape
    j0 = col0 // tn
    in_specs = [pl.BlockSpec((tm, k), lambda j, i: (i, 0)),
                pl.BlockSpec((tn, k), lambda j, i: (j + j0, 0))]
    args = [a, w_t]
    if bias is not None:
        in_specs.append(pl.BlockSpec((1, tn), lambda j, i: (0, j + j0)))
        args.append(bias)
    return pl.pallas_call(
        functools.partial(_matmul_kernel, has_bias=bias is not None),
        grid=(ncols // tn, m // tm),
        in_specs=in_specs,
        out_specs=pl.BlockSpec((tm, tn), lambda j, i: (i, j)),
        out_shape=jax.ShapeDtypeStruct((m, ncols), out_dtype),
        scratch_shapes=[pltpu.VMEM((k, tn), BF16)],
        compiler_params=_cparams("arbitrary", "arbitrary"),
        name="proj",
    )(*args)


def _sb_attn_kernel(q_ref, k_ref, v_ref, tri_ref, o_ref, acc_ref, r_ref, *, sub, n_sub, scale):
    first = pl.program_id(2) * n_sub
    tri = tri_ref[...]
    row = lax.broadcasted_iota(jnp.int32, (sub, sub), 0)
    col = lax.broadcasted_iota(jnp.int32, (sub, sub), 1)
    causal = col < row

    def visit(q, blk, r, mask):
        start = pl.multiple_of(jnp.maximum(blk, 0) * sub, sub)
        kb = k_ref[pl.ds(start, sub), :]
        vb = v_ref[pl.ds(start, sub), :]
        z = lax.dot_general(q, kb, (((1,), (1,)), ((), ())), preferred_element_type=F32) * scale
        log_keep = -(jnp.maximum(z, 0.0) + jnp.log(1.0 + jnp.exp(-jnp.abs(z))))
        log_beta = log_keep + z
        if mask is not None:
            log_keep = jnp.where(mask, log_keep, 0.0)
        hi, lo = _split_bf16(log_keep)
        between = (jnp.dot(hi, tri, preferred_element_type=F32)
                   + jnp.dot(lo, tri, preferred_element_type=F32))
        w = jnp.exp(log_beta + between + r)
        if mask is not None:
            w = jnp.where(mask, w, 0.0)
        pv = jnp.dot(w.astype(BF16), vb, preferred_element_type=F32)
        return pv, r + jnp.sum(log_keep, axis=1, keepdims=True)

    n_win = ATTN_ALWAYS
    roww = lax.broadcasted_iota(jnp.int32, (sub, n_win * sub), 0)
    colw = lax.broadcasted_iota(jnp.int32, (sub, n_win * sub), 1)

    def win_start(s):
        blk = first + s
        clamp = s < n_win - 1
        first_blk = jnp.maximum(blk - (n_win - 1), 0) if clamp else blk - (n_win - 1)
        return pl.multiple_of(first_blk * sub, sub), clamp

    def masked(s, x):
        start, clamp = win_start(s)
        if clamp:
            return jnp.where(start + colw < (first + s) * sub + roww, x, 0.0)
        last = (n_win - 1) * sub
        return jnp.concatenate([x[:, :last], jnp.where(causal, x[:, last:], 0.0)], axis=1)

    scores = []
    for s in range(n_sub):
        kw = k_ref[pl.ds(win_start(s)[0], n_win * sub), :]
        q = q_ref[s * sub:(s + 1) * sub, :]
        scores.append(lax.dot_general(q, kw, (((1,), (1,)), ((), ())), preferred_element_type=F32) * scale)

    staged = []
    for s in range(n_sub):
        z = scores[s]
        log_keep = -(jnp.maximum(z, 0.0) + jnp.log(1.0 + jnp.exp(-jnp.abs(z))))
        log_beta = log_keep + z
        log_keep = masked(s, log_keep)
        chunks = [log_keep[:, c * sub:(c + 1) * sub] for c in range(n_win)]
        hi, lo = _split_bf16(jnp.concatenate(chunks, axis=0))
        inside = (jnp.dot(hi, tri, preferred_element_type=F32)
                  + jnp.dot(lo, tri, preferred_element_type=F32))
        staged.append((log_beta, chunks, inside))

    r_top = None
    for s in range(n_sub):
        log_beta, chunks, inside = staged[s]
        later = 0.0
        between = [None] * n_win
        for c in reversed(range(n_win)):
            between[c] = inside[c * sub:(c + 1) * sub, :] + later
            later = later + jnp.sum(chunks[c], axis=1, keepdims=True)
        w = masked(s, jnp.exp(log_beta + jnp.concatenate(between, axis=1)))
        vw = v_ref[pl.ds(win_start(s)[0], n_win * sub), :]
        acc_ref[s] = jnp.dot(w.astype(BF16), vw, preferred_element_type=F32)
        r = jnp.broadcast_to(later, (sub, LANES))
        r_ref[s] = r
        r_top = r if r_top is None else jnp.maximum(r_top, r)

    @pl.when(jnp.max(r_top) > UNDERFLOW_LOG)
    def _():
        for s in range(n_sub):
            q = q_ref[s * sub:(s + 1) * sub, :]

            def cond(carry):
                j, r_max = carry
                return jnp.logical_and(j >= 0, r_max > UNDERFLOW_LOG)

            def body(carry, s=s, q=q):
                j, _ = carry
                pv, r = visit(q, j, r_ref[s], None)
                acc_ref[s] += pv
                r_ref[s] = r
                return j - 1, jnp.max(r)

            lax.while_loop(cond, body, (first + s - ATTN_ALWAYS, jnp.max(r_ref[s])))

    for s in range(n_sub):
        o_ref[s * sub:(s + 1) * sub, :] = acc_ref[s].astype(o_ref.dtype)


def _sb_attention(qkv, sub=ATTN_SUB, n_sub=ATTN_SUBS_PER_STEP):
    b, s, _ = qkv.shape
    h, d = SB_HEADS, SB_HEAD_DIM
    assert sub == LANES, "the carried row sums are kept lane-replicated at the key sub-block width"
    assert s >= ATTN_ALWAYS * sub
    n_sub = min(n_sub, s // sub)
    tq = sub * n_sub
    idx = lax.broadcasted_iota(jnp.int32, (sub, sub), 0) > lax.broadcasted_iota(jnp.int32, (sub, sub), 1)
    tri = idx.astype(BF16)
    kern = functools.partial(_sb_attn_kernel, sub=sub, n_sub=n_sub, scale=1.0 / math.sqrt(d))
    return pl.pallas_call(
        kern,
        grid=(b, h, s // tq),
        in_specs=[pl.BlockSpec((None, tq, d), lambda bi, hi, qi: (bi, qi, hi)),
                  pl.BlockSpec((None, s, d), lambda bi, hi, qi: (bi, 0, h + hi)),
                  pl.BlockSpec((None, s, d), lambda bi, hi, qi: (bi, 0, 2 * h + hi)),
                  pl.BlockSpec((sub, sub), lambda bi, hi, qi: (0, 0))],
        out_specs=pl.BlockSpec((None, tq, d), lambda bi, hi, qi: (bi, qi, hi)),
        out_shape=jax.ShapeDtypeStruct((b, s, h * d), BF16),
        scratch_shapes=[pltpu.VMEM((n_sub, sub, d), F32), pltpu.VMEM((n_sub, sub, LANES), F32)],
        compiler_params=_cparams("arbitrary", "arbitrary", "arbitrary"),
        name="sb_attn",
    )(qkv, qkv, qkv, tri)


def _round_robin(stagewise):
    out = [None] * len(stagewise)
    live = list(range(len(stagewise)))
    while live:
        for i in list(live):
            try:
                next(stagewise[i])
            except StopIteration as stop:
                out[i] = stop.value
                live.remove(i)
    return out


def _mlstm_head(head, q, k, vb, og, gcol, grow, nw, tril_ref, triu_ref, c_st, n_st, m_st, *, chunk):
    li_col = gcol[:, head:head + 1]
    lf_col = _log_sigmoid(gcol[:, ML_HEADS + head:ML_HEADS + head + 1])
    li_row = grow[head:head + 1, :]
    lf_row = _log_sigmoid(grow[ML_HEADS + head:ML_HEADS + head + 1, :])

    hi, lo = _split_bf16(jnp.broadcast_to(lf_col, (chunk, LANES)))
    b_col = (jnp.dot(tril_ref[...], hi, preferred_element_type=F32)
             + jnp.dot(tril_ref[...], lo, preferred_element_type=F32))[:, 0:1]
    hi, lo = _split_bf16(jnp.broadcast_to(lf_row, (2 * SUBLANES, chunk)))
    b_row = (jnp.dot(hi, triu_ref[...], preferred_element_type=F32)
             + jnp.dot(lo, triu_ref[...], preferred_element_type=F32))[0:1, :]
    qb = q.astype(BF16)
    kb = k.astype(BF16)
    qk = lax.dot_general(qb, kb, (((1,), (1,)), ((), ())), preferred_element_type=F32)
    q_mem = jnp.dot(qb, c_st[...].astype(BF16), preferred_element_type=F32)
    yield

    m_prev = m_st[...]
    a_col = b_col + m_prev
    row_t = lax.broadcasted_iota(jnp.int32, (chunk, chunk), 0)
    col_s = lax.broadcasted_iota(jnp.int32, (chunk, chunk), 1)
    d_mat = jnp.where(col_s <= row_t, b_col - b_row + li_row, -jnp.inf)
    m_col = jnp.maximum(a_col, jnp.max(d_mat, axis=1, keepdims=True))
    w_intra = jnp.exp(d_mat - m_col)
    w_inter = jnp.exp(a_col - m_col)
    sc = qk * w_intra
    sc_v = jnp.dot(sc.astype(BF16), vb, preferred_element_type=F32)

    b_last = b_col[chunk - 1:chunk, :]
    g_col = b_last - b_col + li_col
    m_new = jnp.maximum(b_last + m_prev, jnp.max(g_col, axis=0, keepdims=True))
    decay = jnp.exp(b_last + m_prev - m_new)
    kw = k * jnp.exp(g_col - m_new)
    kw_v = lax.dot_general(kw.astype(BF16), vb, (((0,), (0,)), ((), ())), preferred_element_type=F32)
    yield

    num = w_inter * q_mem + sc_v
    den = (w_inter * jnp.sum(q * n_st[...], axis=1, keepdims=True)
           + jnp.sum(sc, axis=1, keepdims=True))
    h_out = num / jnp.maximum(jnp.abs(den), jnp.exp(-m_col))
    c_st[...] = decay * c_st[...] + kw_v
    n_st[...] = decay * n_st[...] + jnp.sum(kw, axis=0, keepdims=True)
    m_st[...] = m_new

    hn = h_out * lax.rsqrt(jnp.mean(h_out * h_out, axis=1, keepdims=True) + NORM_EPS) * nw
    return jax.nn.sigmoid(og) * hn


def _mlstm_kernel(qp_ref, kp_ref, v_ref, og_ref, gcol_ref, grow_ref, cwq_ref, cwk_ref, nw_ref,
                  tril_ref, triu_ref, y_ref, c_st, n_st, m_st, qbuf, kbuf, *, chunk, heads, dim):
    halo = SUBLANES

    @pl.when(pl.program_id(1) == 0)
    def _():
        c_st[...] = jnp.zeros_like(c_st)
        n_st[...] = jnp.zeros_like(n_st)
        m_st[...] = jnp.zeros_like(m_st)
        qbuf[0:halo, :] = jnp.zeros((halo, heads * dim), F32)
        kbuf[0:halo, :] = jnp.zeros((halo, heads * dim), F32)

    qbuf[halo:halo + chunk, :] = qp_ref[...]
    kbuf[halo:halo + chunk, :] = kp_ref[...]

    def conv_silu(buf, cw_ref):
        acc = buf[halo:halo + chunk, :] * cw_ref[CONV_WIDTH - 1:CONV_WIDTH, :]
        for j in range(CONV_WIDTH - 1):
            off = halo - (CONV_WIDTH - 1) + j
            acc = acc + buf[off:off + chunk, :] * cw_ref[j:j + 1, :]
        return acc * jax.nn.sigmoid(acc)

    q = conv_silu(qbuf, cwq_ref)
    k = conv_silu(kbuf, cwk_ref) * (1.0 / math.sqrt(dim))
    qbuf[0:halo, :] = qbuf[chunk:chunk + halo, :]
    kbuf[0:halo, :] = kbuf[chunk:chunk + halo, :]

    gcol = gcol_ref[...]
    grow = grow_ref[...]
    cols = [slice(head * dim, (head + 1) * dim) for head in range(heads)]
    ys = _round_robin([
        _mlstm_head(head, q[:, c], k[:, c], v_ref[:, c], og_ref[:, c], gcol, grow, nw_ref[:, c],
                    tril_ref, triu_ref, c_st.at[head], n_st.at[head], m_st.at[head], chunk=chunk)
        for head, c in enumerate(cols)])
    for y, c in zip(ys, cols):
        y_ref[:, c] = y.astype(y_ref.dtype)


def _mlstm(ml_qk, ml_v, ml_o, gates, conv_w, ml_norm_w, chunk=ML_CHUNK):
    b, s, _ = ml_v.shape
    h, d = ML_HEADS, ML_HEAD_DIM
    w = h * d
    gates_t = jnp.swapaxes(gates, 1, 2)
    r = lax.broadcasted_iota(jnp.int32, (chunk, chunk), 0)
    c = lax.broadcasted_iota(jnp.int32, (chunk, chunk), 1)
    tril = (c <= r).astype(BF16)
    triu = (r <= c).astype(BF16)
    wide = lambda off: pl.BlockSpec((None, chunk, w), lambda bi, ci: (bi, ci, off))
    const = lambda shape, off=0: pl.BlockSpec(shape, lambda bi, ci: (0, off))
    kern = functools.partial(_mlstm_kernel, chunk=chunk, heads=h, dim=d)
    return pl.pallas_call(
        kern,
        grid=(b, s // chunk),
        in_specs=[wide(0), wide(1), wide(0), wide(0),
                  pl.BlockSpec((None, chunk, 2 * h), lambda bi, ci: (bi, ci, 0)),
                  pl.BlockSpec((None, 2 * h, chunk), lambda bi, ci: (bi, 0, ci)),
                  const((CONV_WIDTH, w)), const((CONV_WIDTH, w), 1), const((1, w)),
                  const((chunk, chunk)), const((chunk, chunk))],
        out_specs=wide(0),
        out_shape=jax.ShapeDtypeStruct((b, s, w), BF16),
        scratch_shapes=[pltpu.VMEM((h, d, d), F32), pltpu.VMEM((h, 1, d), F32), pltpu.VMEM((h, 1, 1), F32),
                        pltpu.VMEM((chunk + SUBLANES, w), F32), pltpu.VMEM((chunk + SUBLANES, w), F32)],
        compiler_params=_cparams("arbitrary", "arbitrary"),
        name="mlstm",
    )(ml_qk, ml_qk, ml_v, ml_o, gates, gates_t, conv_w, conv_w, ml_norm_w.reshape(1, w), tril, triu)


def _out_proj_kernel(ya_ref, yb_ref, wa_ref, wb_ref, x_ref, g_ref, o_ref):
    mix = (jnp.dot(ya_ref[...], wa_ref[...], preferred_element_type=F32)
           + jnp.dot(yb_ref[...], wb_ref[...], preferred_element_type=F32))
    o_ref[...] = x_ref[...] + g_ref[...] * mix


def _out_proj(y_sb, y_ml, w_out, x, gate, tm, tn=512):
    b, s, d = x.shape
    ka, kb = y_sb.shape[-1], y_ml.shape[-1]
    assert ka == kb
    n = b * s
    rows_per_seq = s // tm
    out = pl.pallas_call(
        _out_proj_kernel,
        grid=(n // tm, d // tn),
        in_specs=[pl.BlockSpec((tm, ka), lambda i, j: (i, 0)),
                  pl.BlockSpec((tm, kb), lambda i, j: (i, 0)),
                  pl.BlockSpec((ka, tn), lambda i, j: (0, j)),
                  pl.BlockSpec((kb, tn), lambda i, j: (1, j)),
                  pl.BlockSpec((tm, tn), lambda i, j: (i, j)),
                  pl.BlockSpec((None, 1, tn), lambda i, j: (i // rows_per_seq, 0, j))],
        out_specs=pl.BlockSpec((tm, tn), lambda i, j: (i, j)),
        out_shape=jax.ShapeDtypeStruct((n, d), F32),
        compiler_params=_cparams("arbitrary", "arbitrary"),
        name="out_proj",
    )(y_sb.reshape(n, ka), y_ml.reshape(n, kb), w_out, w_out, x.reshape(n, d), gate.reshape(b, 1, d))
    return out.reshape(b, s, d)


def _router_kernel(h_ref, nw_ref, sh_ref, sc_ref, wr_ref, br_ref, tri_ref, u_ref, ri_ref, rw_ref, cnt_ref, run_ref):
    first = jnp.logical_and(pl.program_id(0) == 0, pl.program_id(1) == 0)

    @pl.when(first)
    def _():
        run_ref[...] = jnp.zeros_like(run_ref)

    u = _rms_mod(h_ref[...], nw_ref[...], sh_ref[...], sc_ref[...])
    tm = u.shape[0]
    uh, ul = _split_bf16(u)
    u_ref[...] = _pack_bf16_pairs(uh)
    wh, wl = _split_bf16(wr_ref[...])
    logits = (jnp.dot(uh, wh, preferred_element_type=F32) + jnp.dot(ul, wh, preferred_element_type=F32)
              + jnp.dot(uh, wl, preferred_element_type=F32) + br_ref[...])

    lane = lax.broadcasted_iota(jnp.int32, (tm, LANES), 1)
    lane_f = lane.astype(F32)
    neg = -jnp.inf
    first_lane = lambda hit: jnp.min(jnp.where(hit, lane_f, float(LANES)), axis=1, keepdims=True)

    gl = jnp.where(lane < N_GROUPS, logits, neg)
    g_max = jnp.max(gl, axis=1, keepdims=True)
    g_p = 1.0 / jnp.sum(jnp.exp(gl - g_max), axis=1, keepdims=True)
    g_sel = first_lane(gl == g_max)

    lo_lane = N_GROUPS + EXPERTS_PER_GROUP * g_sel
    el = jnp.where(jnp.logical_and(lane_f >= lo_lane, lane_f < lo_lane + EXPERTS_PER_GROUP), logits, neg)
    e1_max = jnp.max(el, axis=1, keepdims=True)
    l1 = first_lane(el == e1_max)
    el2 = jnp.where(lane_f == l1, neg, el)
    e2_max = jnp.max(el2, axis=1, keepdims=True)
    l2 = first_lane(el2 == e2_max)
    z_sum = jnp.sum(jnp.exp(el - e1_max), axis=1, keepdims=True)
    p1 = 1.0 / z_sum
    p2 = jnp.exp(e2_max - e1_max) / z_sum
    w1 = p1 / (p1 + p2) * g_p
    w2 = p2 / (p1 + p2) * g_p
    e1 = l1 - N_GROUPS
    e2 = l2 - N_GROUPS

    hit1 = lane_f == e1
    hit2 = lane_f == e2
    onehot = jnp.where(jnp.logical_or(hit1, hit2), 1.0, 0.0)
    before = jnp.dot(tri_ref[...], onehot.astype(BF16), preferred_element_type=F32) + run_ref[...]
    rank1 = jnp.sum(jnp.where(hit1, before, 0.0), axis=1, keepdims=True)
    rank2 = jnp.sum(jnp.where(hit2, before, 0.0), axis=1, keepdims=True)
    run = run_ref[...] + jnp.sum(onehot, axis=0, keepdims=True)
    run_ref[...] = run

    ri_ref[...] = jnp.where(lane == 0, e1, jnp.where(lane == 1, e2, jnp.where(lane == 2, rank1,
                            jnp.where(lane == 3, rank2, 0.0)))).astype(jnp.int32)
    rw_ref[...] = jnp.where(lane == 0, w1, jnp.where(lane == 1, w2, 0.0))
    cnt_ref[...] = jnp.broadcast_to(run, cnt_ref.shape).astype(jnp.int32)


def _router(h, norm_w, shift, scale, w_router, b_router, tm=256):
    b, s, d = h.shape
    n = b * s
    tri = (lax.broadcasted_iota(jnp.int32, (tm, tm), 1) < lax.broadcasted_iota(jnp.int32, (tm, tm), 0)).astype(BF16)
    vec = pl.BlockSpec((None, 1, d), lambda bi, i: (bi, 0, 0))
    rows = lambda width: pl.BlockSpec((tm, width), lambda bi, i: (bi * (s // tm) + i, 0))
    return pl.pallas_call(
        _router_kernel,
        grid=(b, s // tm),
        in_specs=[pl.BlockSpec((None, tm, d), lambda bi, i: (bi, i, 0)),
                  pl.BlockSpec((1, d), lambda bi, i: (0, 0)), vec, vec,
                  pl.BlockSpec((d, LANES), lambda bi, i: (0, 0)),
                  pl.BlockSpec((1, LANES), lambda bi, i: (0, 0)),
                  pl.BlockSpec((tm, tm), lambda bi, i: (0, 0))],
        out_specs=[rows(d // 2), rows(LANES), rows(LANES),
                   pl.BlockSpec((SUBLANES, LANES), lambda bi, i: (0, 0))],
        out_shape=[jax.ShapeDtypeStruct((n, d // 2), jnp.int32), jax.ShapeDtypeStruct((n, LANES), jnp.int32),
                   jax.ShapeDtypeStruct((n, LANES), F32), jax.ShapeDtypeStruct((SUBLANES, LANES), jnp.int32)],
        scratch_shapes=[pltpu.VMEM((1, LANES), F32)],
        compiler_params=_cparams("arbitrary", "arbitrary"),
        name="router",
    )(h, norm_w.reshape(1, d), shift.reshape(b, 1, d), scale.reshape(b, 1, d), w_router, b_router, tri)


ZERO_ROWS = 128


def _dispatch_kernel(tail_ref, dest_ref, u_ref, xb_hbm, zbuf, zsem, sem, *, tokens, block):
    n_exp = tail_ref.shape[0]

    def clear(e, half):
        row = pl.multiple_of(tail_ref[e] + half * ZERO_ROWS, ZERO_ROWS)
        return pltpu.make_async_copy(zbuf, xb_hbm.at[pl.ds(row, ZERO_ROWS)], zsem)

    @pl.when(pl.program_id(0) == 0)
    def _():
        zbuf[...] = jnp.zeros_like(zbuf)
        for start in (True, False):
            def each(e, carry):
                @pl.when(tail_ref[e] >= 0)
                def _():
                    for half in range(block // ZERO_ROWS):
                        clear(e, half).start() if start else clear(e, half).wait()
                return carry
            lax.fori_loop(0, n_exp, each, 0)

    def row_copy(t, k):
        return pltpu.make_async_copy(u_ref.at[pl.ds(t, 1)], xb_hbm.at[pl.ds(dest_ref[0, TOP_K * t + k], 1)], sem)

    def issue(t, carry):
        for k in range(TOP_K):
            row_copy(t, k).start(priority=k % 2)
        return carry

    lax.fori_loop(0, tokens, issue, 0)

    def drain(t, carry):
        for k in range(TOP_K):
            row_copy(t, k).wait()
        return carry

    lax.fori_loop(0, tokens, drain, 0)


def _dispatch(u, dest, tail_block_row, cap, tokens=256, block=MOE_BLOCK):
    n, d = u.shape
    steps = n // tokens
    kern = functools.partial(_dispatch_kernel, tokens=tokens, block=block)
    return pl.pallas_call(
        kern,
        grid_spec=pltpu.PrefetchScalarGridSpec(
            num_scalar_prefetch=1,
            grid=(steps,),
            in_specs=[pl.BlockSpec((None, 1, TOP_K * tokens), lambda i, tail: (i, 0, 0), memory_space=pltpu.SMEM),
                      pl.BlockSpec((tokens, d), lambda i, tail: (i, 0))],
            out_specs=pl.BlockSpec(memory_space=pl.ANY),
            scratch_shapes=[pltpu.VMEM((ZERO_ROWS, d), u.dtype), pltpu.SemaphoreType.DMA(()),
                            pltpu.SemaphoreType.DMA(())]),
        out_shape=jax.ShapeDtypeStruct((cap, d), u.dtype),
        compiler_params=_cparams("arbitrary"),
        name="dispatch",
    )(tail_block_row, dest.reshape(steps, 1, TOP_K * tokens), u)


def _run_plan(keys, ok):
    steps = ok.shape[0]
    idx = jnp.arange(steps, dtype=jnp.int32)
    changed = idx == 0
    for key in keys:
        changed = jnp.logical_or(changed, key != jnp.roll(key, 1))
    first = jnp.logical_and(changed, ok)
    slot = (jnp.cumsum(first.astype(jnp.int32)) - 1) % 2
    starts_from = lax.cummin(jnp.where(first, idx, steps), axis=0, reverse=True)
    nxt = jnp.concatenate([starts_from[1:], jnp.full((1,), steps, jnp.int32)])
    more = nxt < steps
    as_i32 = lambda a: a.astype(jnp.int32)
    return as_i32(first), as_i32(slot), as_i32(more), as_i32(jnp.minimum(nxt, steps - 1))


def _run_weights(step, first_ref, slot_ref, more_ref, fetch_cur, fetch_next):
    slot = slot_ref[step]

    @pl.when(step == 0)
    def _():
        for copy in fetch_cur(slot):
            copy.start()

    @pl.when(first_ref[step] == 1)
    def _():
        @pl.when(more_ref[step] == 1)
        def _():
            for copy in fetch_next(1 - slot):
                copy.start()

        for copy in fetch_cur(slot):
            copy.wait()

    return slot


def _e1_kernel(blk_ref, ex_ref, ft_ref, oblk_ref, oft_ref, ok_ref, first_ref, slot_ref, more_ref, nex_ref, nft_ref,
               x_ref, wg_hbm, wu_hbm, o_ref, wg_buf, wu_buf, sems, *, tf):
    del blk_ref, oblk_ref, oft_ref
    step = pl.program_id(0)

    def fetch(e, f, slot):
        cols = pl.ds(pl.multiple_of(f * tf, tf), tf)
        return (pltpu.make_async_copy(wg_hbm.at[e, :, cols], wg_buf.at[slot], sems.at[slot, 0]),
                pltpu.make_async_copy(wu_hbm.at[e, :, cols], wu_buf.at[slot], sems.at[slot, 1]))

    slot = _run_weights(step, first_ref, slot_ref, more_ref,
                        lambda sl: fetch(ex_ref[step], ft_ref[step], sl),
                        lambda sl: fetch(nex_ref[step], nft_ref[step], sl))
    ok = ok_ref[step] == 1

    @pl.when(ok)
    def _():
        x_lo, x_hi = _unpack_bf16_pairs(x_ref[...])
        half = x_lo.shape[1]
        mm = lambda w_buf: (jnp.dot(x_lo, w_buf[slot, :half, :], preferred_element_type=F32)
                            + jnp.dot(x_hi, w_buf[slot, half:, :], preferred_element_type=F32))
        g = mm(wg_buf)
        up = mm(wu_buf)
        o_ref[...] = ((g * jax.nn.sigmoid(g)) * up).astype(o_ref.dtype)

    @pl.when(jnp.logical_not(ok))
    def _():
        o_ref[...] = jnp.zeros_like(o_ref)


def _e2_kernel(blk_ref, ex_ref, oblk_ref, ok_ref, first_ref, slot_ref, more_ref, nex_ref,
               h_ref, wd_hbm, o_ref, wd_buf, sems):
    del blk_ref, oblk_ref
    step = pl.program_id(0)
    fetch = lambda e, slot: (pltpu.make_async_copy(wd_hbm.at[e], wd_buf.at[slot], sems.at[slot]),)
    slot = _run_weights(step, first_ref, slot_ref, more_ref,
                        lambda sl: fetch(ex_ref[step], sl), lambda sl: fetch(nex_ref[step], sl))
    ok = ok_ref[step] == 1

    @pl.when(ok)
    def _():
        o_ref[...] = jnp.dot(h_ref[...].astype(F32), wd_buf[slot], preferred_element_type=F32)

    @pl.when(jnp.logical_not(ok))
    def _():
        o_ref[...] = jnp.zeros_like(o_ref)


def _experts(xb, blocks_per_expert, w_gate, w_up, w_down, bm=MOE_BLOCK, tf=512):
    cap, dp = xb.shape
    n_exp, d, dff = w_gate.shape
    n_blocks = cap // bm
    n_ft = dff // tf
    nb = blocks_per_expert.astype(jnp.int32)
    bstart = jnp.cumsum(nb) - nb
    used = jnp.sum(nb)

    steps1 = n_blocks * n_ft
    step = jnp.arange(steps1, dtype=jnp.int32)
    ok1 = step < used * n_ft
    send = jnp.cumsum(nb * n_ft)
    s_idx = jnp.minimum(step, used * n_ft - 1)
    e_of = jnp.minimum(jnp.sum(s_idx[:, None] >= send[None, :], axis=1), n_exp - 1).astype(jnp.int32)
    local = s_idx - (send[e_of] - nb[e_of] * n_ft)
    nbe = jnp.maximum(nb[e_of], 1)
    ft_of = (local // nbe).astype(jnp.int32)
    blk_of = (bstart[e_of] + local % nbe).astype(jnp.int32)
    spare = step - used * n_ft
    oblk_of = jnp.where(ok1, blk_of, used + spare // n_ft).astype(jnp.int32)
    oft_of = jnp.where(ok1, ft_of, spare % n_ft).astype(jnp.int32)

    first1, slot1, more1, nxt1 = _run_plan([e_of, ft_of], ok1)
    hmid = pl.pallas_call(
        functools.partial(_e1_kernel, tf=tf),
        grid_spec=pltpu.PrefetchScalarGridSpec(
            num_scalar_prefetch=11,
            grid=(steps1,),
            in_specs=[pl.BlockSpec((bm, dp), lambda s, blk, *_: (blk[s], 0)),
                      pl.BlockSpec(memory_space=pl.ANY),
                      pl.BlockSpec(memory_space=pl.ANY)],
            out_specs=pl.BlockSpec((bm, tf), lambda s, blk, ex, ft, oblk, oft, *_: (oblk[s], oft[s])),
            scratch_shapes=[pltpu.VMEM((2, d, tf), F32), pltpu.VMEM((2, d, tf), F32),
                            pltpu.SemaphoreType.DMA((2, 2))]),
        out_shape=jax.ShapeDtypeStruct((cap, dff), BF16),
        compiler_params=_cparams("arbitrary"),
        name="e1",
    )(blk_of, e_of, ft_of, oblk_of, oft_of, ok1.astype(jnp.int32), first1, slot1, more1, e_of[nxt1], ft_of[nxt1],
      xb, w_gate, w_up)

    blk = jnp.arange(n_blocks, dtype=jnp.int32)
    b_idx = jnp.minimum(blk, used - 1)
    bend = jnp.cumsum(nb)
    e2_of = jnp.minimum(jnp.sum(b_idx[:, None] >= bend[None, :], axis=1), n_exp - 1).astype(jnp.int32)
    ok2 = blk < used
    first2, slot2, more2, nxt2 = _run_plan([e2_of], ok2)
    yb = pl.pallas_call(
        _e2_kernel,
        grid_spec=pltpu.PrefetchScalarGridSpec(
            num_scalar_prefetch=8,
            grid=(n_blocks,),
            in_specs=[pl.BlockSpec((bm, dff), lambda s, hb, *_: (hb[s], 0)),
                      pl.BlockSpec(memory_space=pl.ANY)],
            out_specs=pl.BlockSpec((bm, d), lambda s, hb, ex, ob, *_: (ob[s], 0)),
            scratch_shapes=[pltpu.VMEM((2, dff, d), F32), pltpu.SemaphoreType.DMA((2,))]),
        out_shape=jax.ShapeDtypeStruct((cap, d), F32),
        compiler_params=_cparams("arbitrary"),
        name="e2",
    )(b_idx, e2_of, blk, ok2.astype(jnp.int32), first2, slot2, more2, e2_of[nxt2], hmid, w_down)
    return yb


def _combine_kernel(dest_ref, dest_next_ref, yb_hbm, h_ref, rw_ref, g_ref, nw_ref, sh_ref, sc_ref, o_ref,
                    rows, sems, *, tokens, steps):
    step = pl.program_id(0)
    slot = step % 2

    def row_copy(d_ref, buf, t, k):
        return pltpu.make_async_copy(yb_hbm.at[pl.ds(d_ref[0, TOP_K * t + k], 1)],
                                     rows.at[buf, k, pl.ds(t, 1)], sems.at[buf])

    def gather(d_ref, buf):
        def issue(t, carry):
            for k in range(TOP_K):
                row_copy(d_ref, buf, t, k).start(priority=k % 2)
            return carry

        lax.fori_loop(0, tokens, issue, 0)

    @pl.when(step == 0)
    def _():
        gather(dest_ref, 0)

    @pl.when(step + 1 < steps)
    def _():
        gather(dest_next_ref, 1 - slot)

    def drain(t, carry):
        for k in range(TOP_K):
            row_copy(dest_ref, slot, t, k).wait()
        return carry

    lax.fori_loop(0, tokens, drain, 0)

    rw = rw_ref[...]
    ffn = rows[slot, 0] * rw[:, 0:1]
    for k in range(1, TOP_K):
        ffn = ffn + rows[slot, k] * rw[:, k:k + 1]
    h2 = h_ref[...] + g_ref[...] * ffn
    o_ref[...] = _rms_mod(h2, nw_ref[...], sh_ref[...], sc_ref[...])


def _combine(yb, dest, h, route_w, gate, norm_w, shift, scale, tokens=256):
    b, s, d = h.shape
    n = b * s
    per_seq = s // tokens
    steps = n // tokens
    kern = functools.partial(_combine_kernel, tokens=tokens, steps=steps)
    vec = pl.BlockSpec((None, 1, d), lambda i: (i // per_seq, 0, 0))
    dest_blocks = dest.reshape(steps, 1, TOP_K * tokens)
    dest_spec = lambda ahead: pl.BlockSpec((None, 1, TOP_K * tokens),
                                           lambda i: (jnp.minimum(i + ahead, steps - 1), 0, 0),
                                           memory_space=pltpu.SMEM)
    out = pl.pallas_call(
        kern,
        grid=(steps,),
        in_specs=[dest_spec(0), dest_spec(1),
                  pl.BlockSpec(memory_space=pl.ANY),
                  pl.BlockSpec((tokens, d), lambda i: (i, 0)),
                  pl.BlockSpec((tokens, LANES), lambda i: (i, 0)),
                  vec,
                  pl.BlockSpec((1, d), lambda i: (0, 0)), vec, vec],
        out_specs=pl.BlockSpec((tokens, d), lambda i: (i, 0)),
        out_shape=jax.ShapeDtypeStruct((n, d), F32),
        scratch_shapes=[pltpu.VMEM((2, TOP_K, tokens, d), F32), pltpu.SemaphoreType.DMA((2,))],
        compiler_params=_cparams("arbitrary"),
        name="combine",
    )(dest_blocks, dest_blocks, yb, h.reshape(n, d), route_w,
      gate.reshape(b, 1, d), norm_w.reshape(1, d), shift.reshape(b, 1, d), scale.reshape(b, 1, d))
    return out.reshape(b, s, d)


def _layer(h, mod, norm1_w, w_in, conv_w, ml_gate_bias, ml_norm_w, w_out, norm2_w,
           w_router_group, b_router_group, w_router_expert, b_router_expert,
           w_exp_gate, w_exp_up, w_exp_down, final):
    b, s, d = h.shape
    n = b * s
    sh1, sc1, g1, sh2, sc2, g2 = jnp.split(mod, 6, axis=-1)
    sb_w = SB_HEADS * SB_HEAD_DIM
    ml_w = ML_HEADS * ML_HEAD_DIM

    u = _norm_mod(h, norm1_w, sh1, sc1, BF16).reshape(n, d)
    tm = 1024 if n % 1024 == 0 else 512
    w_in_t = w_in.T
    sb_qkv = _matmul(u, w_in_t, 0, 3 * sb_w, BF16, tm, 512)
    ml_qk = _matmul(u, w_in_t, 3 * sb_w, 2 * ml_w, F32, tm, 512)
    ml_v = _matmul(u, w_in_t, 3 * sb_w + 2 * ml_w, ml_w, BF16, tm, 512)
    ml_o = _matmul(u, w_in_t, 3 * sb_w + 3 * ml_w, ml_w, F32, tm, 512)
    n_gate = 2 * ML_HEADS
    w_gate = jnp.pad(w_in_t[3 * sb_w + 4 * ml_w:, :], ((0, LANES - n_gate), (0, 0)))
    b_gate = jnp.pad(ml_gate_bias, (0, LANES - n_gate)).reshape(1, LANES)
    gates = _matmul(u, w_gate, 0, LANES, F32, tm, LANES, bias=b_gate)[:, :n_gate]

    y_sb = _sb_attention(sb_qkv.reshape(b, s, 3 * sb_w))
    y_ml = _mlstm(ml_qk.reshape(b, s, 2 * ml_w), ml_v.reshape(b, s, ml_w), ml_o.reshape(b, s, ml_w),
                  gates.reshape(b, s, n_gate), conv_w, ml_norm_w)
    h = _out_proj(y_sb, y_ml, w_out.astype(BF16), h, g1, tm=min(1024, s))

    w_router = jnp.pad(jnp.concatenate([w_router_group, w_router_expert], axis=1),
                       ((0, 0), (0, LANES - N_GROUPS - N_EXPERTS)))
    b_router = jnp.pad(jnp.concatenate([b_router_group, b_router_expert]),
                       (0, LANES - N_GROUPS - N_EXPERTS)).reshape(1, LANES)
    u2, route_i, route_w, counts = _router(h, norm2_w, sh2, sc2, w_router, b_router)
    counts = counts[0, :N_EXPERTS]
    blocks_per_expert = (counts + MOE_BLOCK - 1) // MOE_BLOCK
    padded = blocks_per_expert * MOE_BLOCK
    pstarts = jnp.cumsum(padded) - padded
    dest = pstarts[route_i[:, 0:TOP_K]] + route_i[:, TOP_K:2 * TOP_K]
    cap = (-(-(n * TOP_K) // MOE_BLOCK) + N_EXPERTS) * MOE_BLOCK
    tail_block_row = jnp.where(counts % MOE_BLOCK != 0, pstarts + padded - MOE_BLOCK, -1)
    spare_row = jnp.sum(padded) + MOE_BLOCK * jnp.arange(N_EXPERTS, dtype=jnp.int32)
    spare_row = jnp.where(spare_row < cap, spare_row, -1)
    clear_rows = jnp.concatenate([tail_block_row, spare_row]).astype(jnp.int32)
    xb = _dispatch(u2, dest, clear_rows, cap)
    yb = _experts(xb, blocks_per_expert, w_exp_gate, w_exp_up, w_exp_down)
    norm_w, shift, scale = final
    return _combine(yb, dest, h, route_w, g2, norm_w, shift, scale)


def kernel(x, c, norm1_w, w_in, conv_w, ml_gate_bias, ml_norm_w, w_out, norm2_w, w_router_group, b_router_group, w_router_expert, b_router_expert, w_exp_gate, w_exp_up, w_exp_down, w_ada, b_ada, final_norm_w, w_ada_final, b_ada_final):
    b, s, d = x.shape
    depth = w_in.shape[0]
    assert depth == 1, "the final norm is fused into the (single) layer's combine step"
    c_rep = jnp.broadcast_to(c[:, :, None], (b, d, LANES))
    fmod = _ada(c_rep, w_ada_final, b_ada_final)
    sh_f, sc_f = jnp.split(fmod, 2, axis=-1)
    mod = _ada(c_rep, w_ada[0], b_ada[0])
    return _layer(x, mod, norm1_w[0], w_in[0], conv_w[0], ml_gate_bias[0], ml_norm_w[0], w_out[0], norm2_w[0],
                  w_router_group[0], b_router_group[0], w_router_expert[0], b_router_expert[0],
                  w_exp_gate[0], w_exp_up[0], w_exp_down[0], (final_norm_w, sh_f, sc_f))
```

```python
import functools
import math

import jax
import jax.numpy as jnp
from jax import lax
from jax.experimental import pallas as pl
from jax.experimental.pallas import tpu as pltpu

SB_HEADS = 16
SB_HEAD_DIM = 128
ML_HEADS = 4
ML_HEAD_DIM = 512
CONV_WIDTH = 4
N_GROUPS = 4
EXPERTS_PER_GROUP = 8
N_EXPERTS = N_GROUPS * EXPERTS_PER_GROUP
TOP_K = 2
NORM_EPS = 1e-6

LANES = 128
SUBLANES = 8
VMEM_LIMIT = 56 * 1024 * 1024

ATTN_SUB = 128
ATTN_SUBS_PER_STEP = 16
ATTN_ALWAYS = 3
ML_CHUNK = 256
MOE_BLOCK = 256
UNDERFLOW_LOG = -90.0

F32 = jnp.float32
BF16 = jnp.bfloat16


def _cparams(*sem):
    return pltpu.CompilerParams(dimension_semantics=sem, vmem_limit_bytes=VMEM_LIMIT)


def _split_bf16(x):
    hi = x.astype(BF16)
    lo = (x - hi.astype(F32)).astype(BF16)
    return hi, lo


_HIGH_HALF = -65536


def _pack_bf16_pairs(x):
    k = x.shape[1] // 2
    bits = lambda v: lax.bitcast_convert_type(v.astype(F32), jnp.int32)
    return lax.shift_right_logical(bits(x[:, :k]), 16) | (bits(x[:, k:]) & _HIGH_HALF)


def _unpack_bf16_pairs(w):
    lo = lax.bitcast_convert_type(lax.shift_left(w, 16), F32)
    hi = lax.bitcast_convert_type(w & _HIGH_HALF, F32)
    return lo, hi


def _log_sigmoid(x):
    return jnp.minimum(x, 0.0) - jnp.log1p(jnp.exp(-jnp.abs(x)))


def _ada_kernel(c_ref, w_ref, b_ref, o_ref):
    nb, tn = o_ref.shape
    rows = []
    for b in range(nb):
        cb = c_ref[b]
        cb = cb * jax.nn.sigmoid(cb)
        pieces = [jnp.sum(w_ref[:, j * LANES:(j + 1) * LANES] * cb, axis=0, keepdims=True)
                  for j in range(tn // LANES)]
        rows.append(jnp.concatenate(pieces, axis=1))
    o_ref[...] = jnp.concatenate(rows, axis=0) + b_ref[...]


def _ada(c_rep, w, bias, tn=512):
    nb, k, _ = c_rep.shape
    n_out = w.shape[1]
    return pl.pallas_call(
        _ada_kernel,
        grid=(n_out // tn,),
        in_specs=[pl.BlockSpec((nb, k, LANES), lambda j: (0, 0, 0)),
                  pl.BlockSpec((k, tn), lambda j: (0, j)),
                  pl.BlockSpec((1, tn), lambda j: (0, j))],
        out_specs=pl.BlockSpec((nb, tn), lambda j: (0, j)),
        out_shape=jax.ShapeDtypeStruct((nb, n_out), F32),
        compiler_params=_cparams("arbitrary"),
        name="ada",
    )(c_rep, w, bias.reshape(1, n_out))


def _rms_mod(x, w, shift, scale):
    var = jnp.mean(x * x, axis=-1, keepdims=True)
    y = x * lax.rsqrt(var + NORM_EPS) * w
    return y * (1.0 + scale) + shift


def _norm_mod_kernel(x_ref, w_ref, sh_ref, sc_ref, o_ref):
    o_ref[...] = _rms_mod(x_ref[...], w_ref[...], sh_ref[...], sc_ref[...]).astype(o_ref.dtype)


def _norm_mod(x, w, shift, scale, out_dtype, tm=256):
    b, s, d = x.shape
    vec = pl.BlockSpec((None, 1, d), lambda bi, i: (bi, 0, 0))
    return pl.pallas_call(
        _norm_mod_kernel,
        grid=(b, s // tm),
        in_specs=[pl.BlockSpec((None, tm, d), lambda bi, i: (bi, i, 0)),
                  pl.BlockSpec((1, d), lambda bi, i: (0, 0)), vec, vec],
        out_specs=pl.BlockSpec((None, tm, d), lambda bi, i: (bi, i, 0)),
        out_shape=jax.ShapeDtypeStruct((b, s, d), out_dtype),
        compiler_params=_cparams("arbitrary", "arbitrary"),
        name="norm_mod",
    )(x, w.reshape(1, d), shift.reshape(b, 1, d), scale.reshape(b, 1, d))


def _matmul_kernel(a_ref, w_ref, *rest, has_bias):
    b_ref, o_ref, w16 = rest if has_bias else (None,) + rest

    @pl.when(pl.program_id(1) == 0)
    def _():
        w16[...] = w_ref[...].T.astype(BF16)

    acc = jnp.dot(a_ref[...], w16[...], preferred_element_type=F32)
    if has_bias:
        acc = acc + b_ref[...]
    o_ref[...] = acc.astype(o_ref.dtype)


def _matmul(a, w_t, col0, ncols, out_dtype, tm, tn, bias=None):
    m, k = a.shape
    j0 = col0 // tn
    in_specs = [pl.BlockSpec((tm, k), lambda j, i: (i, 0)),
                pl.BlockSpec((tn, k), lambda j, i: (j + j0, 0))]
    args = [a, w_t]
    if bias is not None:
        in_specs.append(pl.BlockSpec((1, tn), lambda j, i: (0, j + j0)))
        args.append(bias)
    return pl.pallas_call(
        functools.partial(_matmul_kernel, has_bias=bias is not None),
        grid=(ncols // tn, m // tm),
        in_specs=in_specs,
        out_specs=pl.BlockSpec((tm, tn), lambda j, i: (i, j)),
        out_shape=jax.ShapeDtypeStruct((m, ncols), out_dtype),
        scratch_shapes=[pltpu.VMEM((k, tn), BF16)],
        compiler_params=_cparams("arbitrary", "arbitrary"),
        name="proj",
    )(*args)


def _sb_attn_kernel(q_ref, k_ref, v_ref, tri_ref, o_ref, acc_ref, r_ref, *, sub, n_sub, scale):
    first = pl.program_id(2) * n_sub
    tri = tri_ref[...]
    row = lax.broadcasted_iota(jnp.int32, (sub, sub), 0)
    col = lax.broadcasted_iota(jnp.int32, (sub, sub), 1)
    causal = col < row

    def visit(q, blk, r, mask):
        start = pl.multiple_of(jnp.maximum(blk, 0) * sub, sub)
        kb = k_ref[pl.ds(start, sub), :]
        vb = v_ref[pl.ds(start, sub), :]
        z = lax.dot_general(q, kb, (((1,), (1,)), ((), ())), preferred_element_type=F32) * scale
        log_keep = -(jnp.maximum(z, 0.0) + jnp.log(1.0 + jnp.exp(-jnp.abs(z))))
        log_beta = log_keep + z
        if mask is not None:
            log_keep = jnp.where(mask, log_keep, 0.0)
        hi, lo = _split_bf16(log_keep)
        between = (jnp.dot(hi, tri, preferred_element_type=F32)
                   + jnp.dot(lo, tri, preferred_element_type=F32))
        w = jnp.exp(log_beta + between + r)
        if mask is not None:
            w = jnp.where(mask, w, 0.0)
        pv = jnp.dot(w.astype(BF16), vb, preferred_element_type=F32)
        return pv, r + jnp.sum(log_keep, axis=1, keepdims=True)

    n_win = ATTN_ALWAYS
    roww = lax.broadcasted_iota(jnp.int32, (sub, n_win * sub), 0)
    colw = lax.broadcasted_iota(jnp.int32, (sub, n_win * sub), 1)

    def win_start(s):
        blk = first + s
        clamp = s < n_win - 1
        first_blk = jnp.maximum(blk - (n_win - 1), 0) if clamp else blk - (n_win - 1)
        return pl.multiple_of(first_blk * sub, sub), clamp

    def masked(s, x):
        start, clamp = win_start(s)
        if clamp:
            return jnp.where(start + colw < (first + s) * sub + roww, x, 0.0)
        last = (n_win - 1) * sub
        return jnp.concatenate([x[:, :last], jnp.where(causal, x[:, last:], 0.0)], axis=1)

    scores = []
    for s in range(n_sub):
        kw = k_ref[pl.ds(win_start(s)[0], n_win * sub), :]
        q = q_ref[s * sub:(s + 1) * sub, :]
        scores.append(lax.dot_general(q, kw, (((1,), (1,)), ((), ())), preferred_element_type=F32) * scale)

    staged = []
    for s in range(n_sub):
        z = scores[s]
        log_keep = -(jnp.maximum(z, 0.0) + jnp.log(1.0 + jnp.exp(-jnp.abs(z))))
        log_beta = log_keep + z
        log_keep = masked(s, log_keep)
        chunks = [log_keep[:, c * sub:(c + 1) * sub] for c in range(n_win)]
        hi, lo = _split_bf16(jnp.concatenate(chunks, axis=0))
        inside = (jnp.dot(hi, tri, preferred_element_type=F32)
                  + jnp.dot(lo, tri, preferred_element_type=F32))
        staged.append((log_beta, chunks, inside))

    r_top = None
    for s in range(n_sub):
        log_beta, chunks, inside = staged[s]
        later = 0.0
        between = [None] * n_win
        for c in reversed(range(n_win)):
            between[c] = inside[c * sub:(c + 1) * sub, :] + later
            later = later + jnp.sum(chunks[c], axis=1, keepdims=True)
        w = masked(s, jnp.exp(log_beta + jnp.concatenate(between, axis=1)))
        vw = v_ref[pl.ds(win_start(s)[0], n_win * sub), :]
        acc_ref[s] = jnp.dot(w.astype(BF16), vw, preferred_element_type=F32)
        r = jnp.broadcast_to(later, (sub, LANES))
        r_ref[s] = r
        r_top = r if r_top is None else jnp.maximum(r_top, r)

    @pl.when(jnp.max(r_top) > UNDERFLOW_LOG)
    def _():
        for s in range(n_sub):
            q = q_ref[s * sub:(s + 1) * sub, :]

            def cond(carry):
                j, r_max = carry
                return jnp.logical_and(j >= 0, r_max > UNDERFLOW_LOG)

            def body(carry, s=s, q=q):
                j, _ = carry
                pv, r = visit(q, j, r_ref[s], None)
                acc_ref[s] += pv
                r_ref[s] = r
                return j - 1, jnp.max(r)

            lax.while_loop(cond, body, (first + s - ATTN_ALWAYS, jnp.max(r_ref[s])))

    for s in range(n_sub):
        o_ref[s * sub:(s + 1) * sub, :] = acc_ref[s].astype(o_ref.dtype)


def _sb_attention(qkv, sub=ATTN_SUB, n_sub=ATTN_SUBS_PER_STEP):
    b, s, _ = qkv.shape
    h, d = SB_HEADS, SB_HEAD_DIM
    assert sub == LANES, "the carried row sums are kept lane-replicated at the key sub-block width"
    assert s >= ATTN_ALWAYS * sub
    n_sub = min(n_sub, s // sub)
    tq = sub * n_sub
    idx = lax.broadcasted_iota(jnp.int32, (sub, sub), 0) > lax.broadcasted_iota(jnp.int32, (sub, sub), 1)
    tri = idx.astype(BF16)
    kern = functools.partial(_sb_attn_kernel, sub=sub, n_sub=n_sub, scale=1.0 / math.sqrt(d))
    return pl.pallas_call(
        kern,
        grid=(b, h, s // tq),
        in_specs=[pl.BlockSpec((None, tq, d), lambda bi, hi, qi: (bi, qi, hi)),
                  pl.BlockSpec((None, s, d), lambda bi, hi, qi: (bi, 0, h + hi)),
                  pl.BlockSpec((None, s, d), lambda bi, hi, qi: (bi, 0, 2 * h + hi)),
                  pl.BlockSpec((sub, sub), lambda bi, hi, qi: (0, 0))],
        out_specs=pl.BlockSpec((None, tq, d), lambda bi, hi, qi: (bi, qi, hi)),
        out_shape=jax.ShapeDtypeStruct((b, s, h * d), BF16),
        scratch_shapes=[pltpu.VMEM((n_sub, sub, d), F32), pltpu.VMEM((n_sub, sub, LANES), F32)],
        compiler_params=_cparams("arbitrary", "arbitrary", "arbitrary"),
        name="sb_attn",
    )(qkv, qkv, qkv, tri)


def _round_robin(stagewise):
    out = [None] * len(stagewise)
    live = list(range(len(stagewise)))
    while live:
        for i in list(live):
            try:
                next(stagewise[i])
            except StopIteration as stop:
                out[i] = stop.value
                live.remove(i)
    return out


def _mlstm_head(head, q, k, vb, og, gcol, grow, nw, tril_ref, triu_ref, c_st, n_st, m_st, *, chunk):
    li_col = gcol[:, head:head + 1]
    lf_col = _log_sigmoid(gcol[:, ML_HEADS + head:ML_HEADS + head + 1])
    li_row = grow[head:head + 1, :]
    lf_row = _log_sigmoid(grow[ML_HEADS + head:ML_HEADS + head + 1, :])

    hi, lo = _split_bf16(jnp.broadcast_to(lf_col, (chunk, LANES)))
    b_col = (jnp.dot(tril_ref[...], hi, preferred_element_type=F32)
             + jnp.dot(tril_ref[...], lo, preferred_element_type=F32))[:, 0:1]
    hi, lo = _split_bf16(jnp.broadcast_to(lf_row, (2 * SUBLANES, chunk)))
    b_row = (jnp.dot(hi, triu_ref[...], preferred_element_type=F32)
             + jnp.dot(lo, triu_ref[...], preferred_element_type=F32))[0:1, :]
    qb = q.astype(BF16)
    kb = k.astype(BF16)
    qk = lax.dot_general(qb, kb, (((1,), (1,)), ((), ())), preferred_element_type=F32)
    q_mem = jnp.dot(qb, c_st[...].astype(BF16), preferred_element_type=F32)
    yield

    m_prev = m_st[...]
    a_col = b_col + m_prev
    row_t = lax.broadcasted_iota(jnp.int32, (chunk, chunk), 0)
    col_s = lax.broadcasted_iota(jnp.int32, (chunk, chunk), 1)
    d_mat = jnp.where(col_s <= row_t, b_col - b_row + li_row, -jnp.inf)
    m_col = jnp.maximum(a_col, jnp.max(d_mat, axis=1, keepdims=True))
    w_intra = jnp.exp(d_mat - m_col)
    w_inter = jnp.exp(a_col - m_col)
    sc = qk * w_intra
    sc_v = jnp.dot(sc.astype(BF16), vb, preferred_element_type=F32)

    b_last = b_col[chunk - 1:chunk, :]
    g_col = b_last - b_col + li_col
    m_new = jnp.maximum(b_last + m_prev, jnp.max(g_col, axis=0, keepdims=True))
    decay = jnp.exp(b_last + m_prev - m_new)
    kw = k * jnp.exp(g_col - m_new)
    kw_v = lax.dot_general(kw.astype(BF16), vb, (((0,), (0,)), ((), ())), preferred_element_type=F32)
    yield

    num = w_inter * q_mem + sc_v
    den = (w_inter * jnp.sum(q * n_st[...], axis=1, keepdims=True)
           + jnp.sum(sc, axis=1, keepdims=True))
    h_out = num / jnp.maximum(jnp.abs(den), jnp.exp(-m_col))
    c_st[...] = decay * c_st[...] + kw_v
    n_st[...] = decay * n_st[...] + jnp.sum(kw, axis=0, keepdims=True)
    m_st[...] = m_new

    hn = h_out * lax.rsqrt(jnp.mean(h_out * h_out, axis=1, keepdims=True) + NORM_EPS) * nw
    return jax.nn.sigmoid(og) * hn


def _mlstm_kernel(qp_ref, kp_ref, v_ref, og_ref, gcol_ref, grow_ref, cwq_ref, cwk_ref, nw_ref,
                  tril_ref, triu_ref, y_ref, c_st, n_st, m_st, qbuf, kbuf, *, chunk, heads, dim):
    halo = SUBLANES

    @pl.when(pl.program_id(1) == 0)
    def _():
        c_st[...] = jnp.zeros_like(c_st)
        n_st[...] = jnp.zeros_like(n_st)
        m_st[...] = jnp.zeros_like(m_st)
        qbuf[0:halo, :] = jnp.zeros((halo, heads * dim), F32)
        kbuf[0:halo, :] = jnp.zeros((halo, heads * dim), F32)

    qbuf[halo:halo + chunk, :] = qp_ref[...]
    kbuf[halo:halo + chunk, :] = kp_ref[...]

    def conv_silu(buf, cw_ref):
        acc = buf[halo:halo + chunk, :] * cw_ref[CONV_WIDTH - 1:CONV_WIDTH, :]
        for j in range(CONV_WIDTH - 1):
            off = halo - (CONV_WIDTH - 1) + j
            acc = acc + buf[off:off + chunk, :] * cw_ref[j:j + 1, :]
        return acc * jax.nn.sigmoid(acc)

    q = conv_silu(qbuf, cwq_ref)
    k = conv_silu(kbuf, cwk_ref) * (1.0 / math.sqrt(dim))
    qbuf[0:halo, :] = qbuf[chunk:chunk + halo, :]
    kbuf[0:halo, :] = kbuf[chunk:chunk + halo, :]

    gcol = gcol_ref[...]
    grow = grow_ref[...]
    cols = [slice(head * dim, (head + 1) * dim) for head in range(heads)]
    ys = _round_robin([
        _mlstm_head(head, q[:, c], k[:, c], v_ref[:, c], og_ref[:, c], gcol, grow, nw_ref[:, c],
                    tril_ref, triu_ref, c_st.at[head], n_st.at[head], m_st.at[head], chunk=chunk)
        for head, c in enumerate(cols)])
    for y, c in zip(ys, cols):
        y_ref[:, c] = y.astype(y_ref.dtype)


def _mlstm(ml_qk, ml_v, ml_o, gates, conv_w, ml_norm_w, chunk=ML_CHUNK):
    b, s, _ = ml_v.shape
    h, d = ML_HEADS, ML_HEAD_DIM
    w = h * d
    gates_t = jnp.swapaxes(gates, 1, 2)
    r = lax.broadcasted_iota(jnp.int32, (chunk, chunk), 0)
    c = lax.broadcasted_iota(jnp.int32, (chunk, chunk), 1)
    tril = (c <= r).astype(BF16)
    triu = (r <= c).astype(BF16)
    wide = lambda off: pl.BlockSpec((None, chunk, w), lambda bi, ci: (bi, ci, off))
    const = lambda shape, off=0: pl.BlockSpec(shape, lambda bi, ci: (0, off))
    kern = functools.partial(_mlstm_kernel, chunk=chunk, heads=h, dim=d)
    return pl.pallas_call(
        kern,
        grid=(b, s // chunk),
        in_specs=[wide(0), wide(1), wide(0), wide(0),
                  pl.BlockSpec((None, chunk, 2 * h), lambda bi, ci: (bi, ci, 0)),
                  pl.BlockSpec((None, 2 * h, chunk), lambda bi, ci: (bi, 0, ci)),
                  const((CONV_WIDTH, w)), const((CONV_WIDTH, w), 1), const((1, w)),
                  const((chunk, chunk)), const((chunk, chunk))],
        out_specs=wide(0),
        out_shape=jax.ShapeDtypeStruct((b, s, w), BF16),
        scratch_shapes=[pltpu.VMEM((h, d, d), F32), pltpu.VMEM((h, 1, d), F32), pltpu.VMEM((h, 1, 1), F32),
                        pltpu.VMEM((chunk + SUBLANES, w), F32), pltpu.VMEM((chunk + SUBLANES, w), F32)],
        compiler_params=_cparams("arbitrary", "arbitrary"),
        name="mlstm",
    )(ml_qk, ml_qk, ml_v, ml_o, gates, gates_t, conv_w, conv_w, ml_norm_w.reshape(1, w), tril, triu)


def _out_proj_kernel(ya_ref, yb_ref, wa_ref, wb_ref, x_ref, g_ref, o_ref):
    mix = (jnp.dot(ya_ref[...], wa_ref[...], preferred_element_type=F32)
           + jnp.dot(yb_ref[...], wb_ref[...], preferred_element_type=F32))
    o_ref[...] = x_ref[...] + g_ref[...] * mix


def _out_proj(y_sb, y_ml, w_out, x, gate, tm, tn=512):
    b, s, d = x.shape
    ka, kb = y_sb.shape[-1], y_ml.shape[-1]
    assert ka == kb
    n = b * s
    rows_per_seq = s // tm
    out = pl.pallas_call(
        _out_proj_kernel,
        grid=(n // tm, d // tn),
        in_specs=[pl.BlockSpec((tm, ka), lambda i, j: (i, 0)),
                  pl.BlockSpec((tm, kb), lambda i, j: (i, 0)),
                  pl.BlockSpec((ka, tn), lambda i, j: (0, j)),
                  pl.BlockSpec((kb, tn), lambda i, j: (1, j)),
                  pl.BlockSpec((tm, tn), lambda i, j: (i, j)),
                  pl.BlockSpec((None, 1, tn), lambda i, j: (i // rows_per_seq, 0, j))],
        out_specs=pl.BlockSpec((tm, tn), lambda i, j: (i, j)),
        out_shape=jax.ShapeDtypeStruct((n, d), F32),
        compiler_params=_cparams("arbitrary", "arbitrary"),
        name="out_proj",
    )(y_sb.reshape(n, ka), y_ml.reshape(n, kb), w_out, w_out, x.reshape(n, d), gate.reshape(b, 1, d))
    return out.reshape(b, s, d)


def _router_kernel(h_ref, nw_ref, sh_ref, sc_ref, wr_ref, br_ref, tri_ref, u_ref, ri_ref, rw_ref, cnt_ref, run_ref):
    first = jnp.logical_and(pl.program_id(0) == 0, pl.program_id(1) == 0)

    @pl.when(first)
    def _():
        run_ref[...] = jnp.zeros_like(run_ref)

    u = _rms_mod(h_ref[...], nw_ref[...], sh_ref[...], sc_ref[...])
    tm = u.shape[0]
    uh, ul = _split_bf16(u)
    u_ref[...] = _pack_bf16_pairs(uh)
    wh, wl = _split_bf16(wr_ref[...])
    logits = (jnp.dot(uh, wh, preferred_element_type=F32) + jnp.dot(ul, wh, preferred_element_type=F32)
              + jnp.dot(uh, wl, preferred_element_type=F32) + br_ref[...])

    lane = lax.broadcasted_iota(jnp.int32, (tm, LANES), 1)
    lane_f = lane.astype(F32)
    neg = -jnp.inf
    first_lane = lambda hit: jnp.min(jnp.where(hit, lane_f, float(LANES)), axis=1, keepdims=True)

    gl = jnp.where(lane < N_GROUPS, logits, neg)
    g_max = jnp.max(gl, axis=1, keepdims=True)
    g_p = 1.0 / jnp.sum(jnp.exp(gl - g_max), axis=1, keepdims=True)
    g_sel = first_lane(gl == g_max)

    lo_lane = N_GROUPS + EXPERTS_PER_GROUP * g_sel
    el = jnp.where(jnp.logical_and(lane_f >= lo_lane, lane_f < lo_lane + EXPERTS_PER_GROUP), logits, neg)
    e1_max = jnp.max(el, axis=1, keepdims=True)
    l1 = first_lane(el == e1_max)
    el2 = jnp.where(lane_f == l1, neg, el)
    e2_max = jnp.max(el2, axis=1, keepdims=True)
    l2 = first_lane(el2 == e2_max)
    z_sum = jnp.sum(jnp.exp(el - e1_max), axis=1, keepdims=True)
    p1 = 1.0 / z_sum
    p2 = jnp.exp(e2_max - e1_max) / z_sum
    w1 = p1 / (p1 + p2) * g_p
    w2 = p2 / (p1 + p2) * g_p
    e1 = l1 - N_GROUPS
    e2 = l2 - N_GROUPS

    hit1 = lane_f == e1
    hit2 = lane_f == e2
    onehot = jnp.where(jnp.logical_or(hit1, hit2), 1.0, 0.0)
    before = jnp.dot(tri_ref[...], onehot.astype(BF16), preferred_element_type=F32) + run_ref[...]
    rank1 = jnp.sum(jnp.where(hit1, before, 0.0), axis=1, keepdims=True)
    rank2 = jnp.sum(jnp.where(hit2, before, 0.0), axis=1, keepdims=True)
    run = run_ref[...] + jnp.sum(onehot, axis=0, keepdims=True)
    run_ref[...] = run

    ri_ref[...] = jnp.where(lane == 0, e1, jnp.where(lane == 1, e2, jnp.where(lane == 2, rank1,
                            jnp.where(lane == 3, rank2, 0.0)))).astype(jnp.int32)
    rw_ref[...] = jnp.where(lane == 0, w1, jnp.where(lane == 1, w2, 0.0))
    cnt_ref[...] = jnp.broadcast_to(run, cnt_ref.shape).astype(jnp.int32)


def _router(h, norm_w, shift, scale, w_router, b_router, tm=256):
    b, s, d = h.shape
    n = b * s
    tri = (lax.broadcasted_iota(jnp.int32, (tm, tm), 1) < lax.broadcasted_iota(jnp.int32, (tm, tm), 0)).astype(BF16)
    vec = pl.BlockSpec((None, 1, d), lambda bi, i: (bi, 0, 0))
    rows = lambda width: pl.BlockSpec((tm, width), lambda bi, i: (bi * (s // tm) + i, 0))
    return pl.pallas_call(
        _router_kernel,
        grid=(b, s // tm),
        in_specs=[pl.BlockSpec((None, tm, d), lambda bi, i: (bi, i, 0)),
                  pl.BlockSpec((1, d), lambda bi, i: (0, 0)), vec, vec,
                  pl.BlockSpec((d, LANES), lambda bi, i: (0, 0)),
                  pl.BlockSpec((1, LANES), lambda bi, i: (0, 0)),
                  pl.BlockSpec((tm, tm), lambda bi, i: (0, 0))],
        out_specs=[rows(d // 2), rows(LANES), rows(LANES),
                   pl.BlockSpec((SUBLANES, LANES), lambda bi, i: (0, 0))],
        out_shape=[jax.ShapeDtypeStruct((n, d // 2), jnp.int32), jax.ShapeDtypeStruct((n, LANES), jnp.int32),
                   jax.ShapeDtypeStruct((n, LANES), F32), jax.ShapeDtypeStruct((SUBLANES, LANES), jnp.int32)],
        scratch_shapes=[pltpu.VMEM((1, LANES), F32)],
        compiler_params=_cparams("arbitrary", "arbitrary"),
        name="router",
    )(h, norm_w.reshape(1, d), shift.reshape(b, 1, d), scale.reshape(b, 1, d), w_router, b_router, tri)


ROW_LOOP_UNROLL = 8
ZERO_ROWS = 128


def _dispatch_kernel(tail_ref, dest_ref, u_ref, xb_hbm, zbuf, zsem, sem, *, tokens, block):
    n_exp = tail_ref.shape[0]

    def clear(e, half):
        row = pl.multiple_of(tail_ref[e] + half * ZERO_ROWS, ZERO_ROWS)
        return pltpu.make_async_copy(zbuf, xb_hbm.at[pl.ds(row, ZERO_ROWS)], zsem)

    @pl.when(pl.program_id(0) == 0)
    def _():
        zbuf[...] = jnp.zeros_like(zbuf)
        for start in (True, False):
            def each(e, carry):
                @pl.when(tail_ref[e] >= 0)
                def _():
                    for half in range(block // ZERO_ROWS):
                        clear(e, half).start() if start else clear(e, half).wait()
                return carry
            lax.fori_loop(0, n_exp, each, 0)

    def row_copy(t, k):
        return pltpu.make_async_copy(u_ref.at[pl.ds(t, 1)], xb_hbm.at[pl.ds(dest_ref[0, TOP_K * t + k], 1)], sem)

    def issue(t, carry):
        for k in range(TOP_K):
            row_copy(t, k).start(priority=k % 2)
        return carry

    lax.fori_loop(0, tokens, issue, 0, unroll=ROW_LOOP_UNROLL)

    def drain(t, carry):
        for k in range(TOP_K):
            row_copy(t, k).wait()
        return carry

    lax.fori_loop(0, tokens, drain, 0, unroll=ROW_LOOP_UNROLL)


def _dispatch(u, dest, tail_block_row, cap, tokens=256, block=MOE_BLOCK):
    n, d = u.shape
    steps = n // tokens
    kern = functools.partial(_dispatch_kernel, tokens=tokens, block=block)
    return pl.pallas_call(
        kern,
        grid_spec=pltpu.PrefetchScalarGridSpec(
            num_scalar_prefetch=1,
            grid=(steps,),
            in_specs=[pl.BlockSpec((None, 1, TOP_K * tokens), lambda i, tail: (i, 0, 0), memory_space=pltpu.SMEM),
                      pl.BlockSpec((tokens, d), lambda i, tail: (i, 0))],
            out_specs=pl.BlockSpec(memory_space=pl.ANY),
            scratch_shapes=[pltpu.VMEM((ZERO_ROWS, d), u.dtype), pltpu.SemaphoreType.DMA(()),
                            pltpu.SemaphoreType.DMA(())]),
        out_shape=jax.ShapeDtypeStruct((cap, d), u.dtype),
        compiler_params=_cparams("arbitrary"),
        name="dispatch",
    )(tail_block_row, dest.reshape(steps, 1, TOP_K * tokens), u)


WEIGHT_DMA_PARTS = 4


def _run_plan(keys, ok):
    steps = ok.shape[0]
    idx = jnp.arange(steps, dtype=jnp.int32)
    changed = idx == 0
    for key in keys:
        changed = jnp.logical_or(changed, key != jnp.roll(key, 1))
    first = jnp.logical_and(changed, ok)
    slot = (jnp.cumsum(first.astype(jnp.int32)) - 1) % 2
    starts_from = lax.cummin(jnp.where(first, idx, steps), axis=0, reverse=True)
    nxt = jnp.concatenate([starts_from[1:], jnp.full((1,), steps, jnp.int32)])
    more = nxt < steps
    as_i32 = lambda a: a.astype(jnp.int32)
    return as_i32(first), as_i32(slot), as_i32(more), as_i32(jnp.minimum(nxt, steps - 1))


def _run_weights(step, first_ref, slot_ref, more_ref, fetch_cur, fetch_next):
    slot = slot_ref[step]

    @pl.when(step == 0)
    def _():
        for part, copy in enumerate(fetch_cur(slot)):
            copy.start(priority=part % 2)

    @pl.when(first_ref[step] == 1)
    def _():
        @pl.when(more_ref[step] == 1)
        def _():
            for part, copy in enumerate(fetch_next(1 - slot)):
                copy.start(priority=part % 2)

        for copy in fetch_cur(slot):
            copy.wait()

    return slot


def _e1_kernel(blk_ref, ex_ref, ft_ref, oblk_ref, oft_ref, ok_ref, first_ref, slot_ref, more_ref, nex_ref, nft_ref,
               x_ref, wg_hbm, wu_hbm, o_ref, wg_buf, wu_buf, sems, *, tf):
    del blk_ref, oblk_ref, oft_ref
    step = pl.program_id(0)

    def fetch(e, f, slot):
        cols = pl.ds(pl.multiple_of(f * tf, tf), tf)
        rows = wg_buf.shape[1] // WEIGHT_DMA_PARTS
        return [pltpu.make_async_copy(hbm.at[e, pl.ds(p * rows, rows), cols],
                                      buf.at[slot, pl.ds(p * rows, rows)], sems.at[slot, which])
                for p in range(WEIGHT_DMA_PARTS)
                for which, (hbm, buf) in enumerate(((wg_hbm, wg_buf), (wu_hbm, wu_buf)))]

    slot = _run_weights(step, first_ref, slot_ref, more_ref,
                        lambda sl: fetch(ex_ref[step], ft_ref[step], sl),
                        lambda sl: fetch(nex_ref[step], nft_ref[step], sl))
    ok = ok_ref[step] == 1

    @pl.when(ok)
    def _():
        x_lo, x_hi = _unpack_bf16_pairs(x_ref[...])
        half = x_lo.shape[1]
        mm = lambda w_buf: (jnp.dot(x_lo, w_buf[slot, :half, :], preferred_element_type=F32)
                            + jnp.dot(x_hi, w_buf[slot, half:, :], preferred_element_type=F32))
        g = mm(wg_buf)
        up = mm(wu_buf)
        o_ref[...] = ((g * jax.nn.sigmoid(g)) * up).astype(o_ref.dtype)

    @pl.when(jnp.logical_not(ok))
    def _():
        o_ref[...] = jnp.zeros_like(o_ref)


def _e2_kernel(blk_ref, ex_ref, oblk_ref, ok_ref, first_ref, slot_ref, more_ref, nex_ref,
               h_ref, wd_hbm, o_ref, wd_buf, sems):
    del blk_ref, oblk_ref
    step = pl.program_id(0)
    rows = wd_buf.shape[1] // WEIGHT_DMA_PARTS
    fetch = lambda e, slot: [pltpu.make_async_copy(wd_hbm.at[e, pl.ds(p * rows, rows)],
                                                   wd_buf.at[slot, pl.ds(p * rows, rows)], sems.at[slot])
                             for p in range(WEIGHT_DMA_PARTS)]
    slot = _run_weights(step, first_ref, slot_ref, more_ref,
                        lambda sl: fetch(ex_ref[step], sl), lambda sl: fetch(nex_ref[step], sl))
    ok = ok_ref[step] == 1

    @pl.when(ok)
    def _():
        o_ref[...] = jnp.dot(h_ref[...].astype(F32), wd_buf[slot], preferred_element_type=F32)

    @pl.when(jnp.logical_not(ok))
    def _():
        o_ref[...] = jnp.zeros_like(o_ref)


def _experts(xb, blocks_per_expert, w_gate, w_up, w_down, bm=MOE_BLOCK, tf=512):
    cap, dp = xb.shape
    n_exp, d, dff = w_gate.shape
    n_blocks = cap // bm
    n_ft = dff // tf
    nb = blocks_per_expert.astype(jnp.int32)
    bstart = jnp.cumsum(nb) - nb
    used = jnp.sum(nb)

    steps1 = n_blocks * n_ft
    step = jnp.arange(steps1, dtype=jnp.int32)
    ok1 = step < used * n_ft
    send = jnp.cumsum(nb * n_ft)
    s_idx = jnp.minimum(step, used * n_ft - 1)
    e_of = jnp.minimum(jnp.sum(s_idx[:, None] >= send[None, :], axis=1), n_exp - 1).astype(jnp.int32)
    local = s_idx - (send[e_of] - nb[e_of] * n_ft)
    nbe = jnp.maximum(nb[e_of], 1)
    ft_of = (local // nbe).astype(jnp.int32)
    blk_of = (bstart[e_of] + local % nbe).astype(jnp.int32)
    spare = step - used * n_ft
    oblk_of = jnp.where(ok1, blk_of, used + spare // n_ft).astype(jnp.int32)
    oft_of = jnp.where(ok1, ft_of, spare % n_ft).astype(jnp.int32)

    first1, slot1, more1, nxt1 = _run_plan([e_of, ft_of], ok1)
    hmid = pl.pallas_call(
        functools.partial(_e1_kernel, tf=tf),
        grid_spec=pltpu.PrefetchScalarGridSpec(
            num_scalar_prefetch=11,
            grid=(steps1,),
            in_specs=[pl.BlockSpec((bm, dp), lambda s, blk, *_: (blk[s], 0)),
                      pl.BlockSpec(memory_space=pl.ANY),
                      pl.BlockSpec(memory_space=pl.ANY)],
            out_specs=pl.BlockSpec((bm, tf), lambda s, blk, ex, ft, oblk, oft, *_: (oblk[s], oft[s])),
            scratch_shapes=[pltpu.VMEM((2, d, tf), F32), pltpu.VMEM((2, d, tf), F32),
                            pltpu.SemaphoreType.DMA((2, 2))]),
        out_shape=jax.ShapeDtypeStruct((cap, dff), BF16),
        compiler_params=_cparams("arbitrary"),
        name="e1",
    )(blk_of, e_of, ft_of, oblk_of, oft_of, ok1.astype(jnp.int32), first1, slot1, more1, e_of[nxt1], ft_of[nxt1],
      xb, w_gate, w_up)

    blk = jnp.arange(n_blocks, dtype=jnp.int32)
    b_idx = jnp.minimum(blk, used - 1)
    bend = jnp.cumsum(nb)
    e2_of = jnp.minimum(jnp.sum(b_idx[:, None] >= bend[None, :], axis=1), n_exp - 1).astype(jnp.int32)
    ok2 = blk < used
    first2, slot2, more2, nxt2 = _run_plan([e2_of], ok2)
    yb = pl.pallas_call(
        _e2_kernel,
        grid_spec=pltpu.PrefetchScalarGridSpec(
            num_scalar_prefetch=8,
            grid=(n_blocks,),
            in_specs=[pl.BlockSpec((bm, dff), lambda s, hb, *_: (hb[s], 0)),
                      pl.BlockSpec(memory_space=pl.ANY)],
            out_specs=pl.BlockSpec((bm, d), lambda s, hb, ex, ob, *_: (ob[s], 0)),
            scratch_shapes=[pltpu.VMEM((2, dff, d), F32), pltpu.SemaphoreType.DMA((2,))]),
        out_shape=jax.ShapeDtypeStruct((cap, d), F32),
        compiler_params=_cparams("arbitrary"),
        name="e2",
    )(b_idx, e2_of, blk, ok2.astype(jnp.int32), first2, slot2, more2, e2_of[nxt2], hmid, w_down)
    return yb


def _combine_kernel(dest_ref, dest_next_ref, yb_hbm, h_ref, rw_ref, g_ref, nw_ref, sh_ref, sc_ref, o_ref,
                    rows, sems, *, tokens, steps):
    step = pl.program_id(0)
    slot = step % 2

    def row_copy(d_ref, buf, t, k):
        return pltpu.make_async_copy(yb_hbm.at[pl.ds(d_ref[0, TOP_K * t + k], 1)],
                                     rows.at[buf, k, pl.ds(t, 1)], sems.at[buf])

    def gather(d_ref, buf):
        def issue(t, carry):
            for k in range(TOP_K):
                row_copy(d_ref, buf, t, k).start(priority=k % 2)
            return carry

        lax.fori_loop(0, tokens, issue, 0, unroll=ROW_LOOP_UNROLL)

    @pl.when(step == 0)
    def _():
        gather(dest_ref, 0)

    @pl.when(step + 1 < steps)
    def _():
        gather(dest_next_ref, 1 - slot)

    def drain(t, carry):
        for k in range(TOP_K):
            row_copy(dest_ref, slot, t, k).wait()
        return carry

    lax.fori_loop(0, tokens, drain, 0, unroll=ROW_LOOP_UNROLL)

    rw = rw_ref[...]
    ffn = rows[slot, 0] * rw[:, 0:1]
    for k in range(1, TOP_K):
        ffn = ffn + rows[slot, k] * rw[:, k:k + 1]
    h2 = h_ref[...] + g_ref[...] * ffn
    o_ref[...] = _rms_mod(h2, nw_ref[...], sh_ref[...], sc_ref[...])


def _combine(yb, dest, h, route_w, gate, norm_w, shift, scale, tokens=256):
    b, s, d = h.shape
    n = b * s
    per_seq = s // tokens
    steps = n // tokens
    kern = functools.partial(_combine_kernel, tokens=tokens, steps=steps)
    vec = pl.BlockSpec((None, 1, d), lambda i: (i // per_seq, 0, 0))
    dest_blocks = dest.reshape(steps, 1, TOP_K * tokens)
    dest_spec = lambda ahead: pl.BlockSpec((None, 1, TOP_K * tokens),
                                           lambda i: (jnp.minimum(i + ahead, steps - 1), 0, 0),
                                           memory_space=pltpu.SMEM)
    out = pl.pallas_call(
        kern,
        grid=(steps,),
        in_specs=[dest_spec(0), dest_spec(1),
                  pl.BlockSpec(memory_space=pl.ANY),
                  pl.BlockSpec((tokens, d), lambda i: (i, 0)),
                  pl.BlockSpec((tokens, LANES), lambda i: (i, 0)),
                  vec,
                  pl.BlockSpec((1, d), lambda i: (0, 0)), vec, vec],
        out_specs=pl.BlockSpec((tokens, d), lambda i: (i, 0)),
        out_shape=jax.ShapeDtypeStruct((n, d), F32),
        scratch_shapes=[pltpu.VMEM((2, TOP_K, tokens, d), F32), pltpu.SemaphoreType.DMA((2,))],
        compiler_params=_cparams("arbitrary"),
        name="combine",
    )(dest_blocks, dest_blocks, yb, h.reshape(n, d), route_w,
      gate.reshape(b, 1, d), norm_w.reshape(1, d), shift.reshape(b, 1, d), scale.reshape(b, 1, d))
    return out.reshape(b, s, d)


def _layer(h, mod, norm1_w, w_in, conv_w, ml_gate_bias, ml_norm_w, w_out, norm2_w,
           w_router_group, b_router_group, w_router_expert, b_router_expert,
           w_exp_gate, w_exp_up, w_exp_down, final):
    b, s, d = h.shape
    n = b * s
    sh1, sc1, g1, sh2, sc2, g2 = jnp.split(mod, 6, axis=-1)
    sb_w = SB_HEADS * SB_HEAD_DIM
    ml_w = ML_HEADS * ML_HEAD_DIM

    u = _norm_mod(h, norm1_w, sh1, sc1, BF16).reshape(n, d)
    tm = 1024 if n % 1024 == 0 else 512
    w_in_t = w_in.T
    sb_qkv = _matmul(u, w_in_t, 0, 3 * sb_w, BF16, tm, 512)
    ml_qk = _matmul(u, w_in_t, 3 * sb_w, 2 * ml_w, F32, tm, 512)
    ml_v = _matmul(u, w_in_t, 3 * sb_w + 2 * ml_w, ml_w, BF16, tm, 512)
    ml_o = _matmul(u, w_in_t, 3 * sb_w + 3 * ml_w, ml_w, F32, tm, 512)
    n_gate = 2 * ML_HEADS
    w_gate = jnp.pad(w_in_t[3 * sb_w + 4 * ml_w:, :], ((0, LANES - n_gate), (0, 0)))
    b_gate = jnp.pad(ml_gate_bias, (0, LANES - n_gate)).reshape(1, LANES)
    gates = _matmul(u, w_gate, 0, LANES, F32, tm, LANES, bias=b_gate)[:, :n_gate]

    y_sb = _sb_attention(sb_qkv.reshape(b, s, 3 * sb_w))
    y_ml = _mlstm(ml_qk.reshape(b, s, 2 * ml_w), ml_v.reshape(b, s, ml_w), ml_o.reshape(b, s, ml_w),
                  gates.reshape(b, s, n_gate), conv_w, ml_norm_w)
    h = _out_proj(y_sb, y_ml, w_out.astype(BF16), h, g1, tm=min(1024, s))

    w_router = jnp.pad(jnp.concatenate([w_router_group, w_router_expert], axis=1),
                       ((0, 0), (0, LANES - N_GROUPS - N_EXPERTS)))
    b_router = jnp.pad(jnp.concatenate([b_router_group, b_router_expert]),
                       (0, LANES - N_GROUPS - N_EXPERTS)).reshape(1, LANES)
    u2, route_i, route_w, counts = _router(h, norm2_w, sh2, sc2, w_router, b_router)
    counts = counts[0, :N_EXPERTS]
    blocks_per_expert = (counts + MOE_BLOCK - 1) // MOE_BLOCK
    padded = blocks_per_expert * MOE_BLOCK
    pstarts = jnp.cumsum(padded) - padded
    dest = pstarts[route_i[:, 0:TOP_K]] + route_i[:, TOP_K:2 * TOP_K]
    cap = (-(-(n * TOP_K) // MOE_BLOCK) + N_EXPERTS) * MOE_BLOCK
    tail_block_row = jnp.where(counts % MOE_BLOCK != 0, pstarts + padded - MOE_BLOCK, -1)
    spare_row = jnp.sum(padded) + MOE_BLOCK * jnp.arange(N_EXPERTS, dtype=jnp.int32)
    spare_row = jnp.where(spare_row < cap, spare_row, -1)
    clear_rows = jnp.concatenate([tail_block_row, spare_row]).astype(jnp.int32)
    xb = _dispatch(u2, dest, clear_rows, cap)
    yb = _experts(xb, blocks_per_expert, w_exp_gate, w_exp_up, w_exp_down)
    norm_w, shift, scale = final
    return _combine(yb, dest, h, route_w, g2, norm_w, shift, scale)


def kernel(x, c, norm1_w, w_in, conv_w, ml_gate_bias, ml_norm_w, w_out, norm2_w, w_router_group, b_router_group, w_router_expert, b_router_expert, w_exp_gate, w_exp_up, w_exp_down, w_ada, b_ada, final_norm_w, w_ada_final, b_ada_final):
    b, s, d = x.shape
    depth = w_in.shape[0]
    assert depth == 1, "the final norm is fused into the (single) layer's combine step"
    c_rep = jnp.broadcast_to(c[:, :, None], (b, d, LANES))
    fmod = _ada(c_rep, w_ada_final, b_ada_final)
    sh_f, sc_f = jnp.split(fmod, 2, axis=-1)
    mod = _ada(c_rep, w_ada[0], b_ada[0])
    return _layer(x, mod, norm1_w[0], w_in[0], conv_w[0], ml_gate_bias[0], ml_norm_w[0], w_out[0], norm2_w[0],
                  w_router_group[0], b_router_group[0], w_router_expert[0], b_router_expert[0],
                  w_exp_gate[0], w_exp_up[0], w_exp_down[0], (final_norm_w, sh_f, sc_f))
```

```python
import functools
import math

import jax
import jax.numpy as jnp
from jax import lax
from jax.experimental import pallas as pl
from jax.experimental.pallas import tpu as pltpu

SB_HEADS = 16
SB_HEAD_DIM = 128
ML_HEADS = 4
ML_HEAD_DIM = 512
CONV_WIDTH = 4
N_GROUPS = 4
EXPERTS_PER_GROUP = 8
N_EXPERTS = N_GROUPS * EXPERTS_PER_GROUP
TOP_K = 2
NORM_EPS = 1e-6

LANES = 128
SUBLANES = 8
VMEM_CAPACITY = 64 * 1024 * 1024
VMEM_RESERVE = 4 * 1024 * 1024
VMEM_LIMIT = 56 * 1024 * 1024

ATTN_SUB = 128
ATTN_SUBS_PER_STEP = 16
ATTN_ALWAYS = 3
ML_CHUNK = 256
MOE_BLOCK = 256
PROJ_TN = 1024
WEIGHT_DMA_PARTS = 4
UNDERFLOW_LOG = -90.0

F32 = jnp.float32
BF16 = jnp.bfloat16


def _cparams(*sem, vmem=VMEM_LIMIT):
    return pltpu.CompilerParams(dimension_semantics=sem, vmem_limit_bytes=vmem)


def _split_bf16(x):
    hi = x.astype(BF16)
    lo = (x - hi.astype(F32)).astype(BF16)
    return hi, lo


_HIGH_HALF = -65536


def _pack_bf16_pairs(x):
    k = x.shape[1] // 2
    bits = lambda v: lax.bitcast_convert_type(v.astype(F32), jnp.int32)
    return lax.shift_right_logical(bits(x[:, :k]), 16) | (bits(x[:, k:]) & _HIGH_HALF)


def _unpack_bf16_pairs(w):
    lo = lax.bitcast_convert_type(lax.shift_left(w, 16), F32)
    hi = lax.bitcast_convert_type(w & _HIGH_HALF, F32)
    return lo, hi


def _log_sigmoid(x):
    return jnp.minimum(x, 0.0) - jnp.log1p(jnp.exp(-jnp.abs(x)))


def _ada_kernel(c_ref, w_ref, b_ref, o_ref):
    nb, tn = o_ref.shape
    rows = []
    for b in range(nb):
        cb = c_ref[b]
        cb = cb * jax.nn.sigmoid(cb)
        pieces = [jnp.sum(w_ref[:, j * LANES:(j + 1) * LANES] * cb, axis=0, keepdims=True)
                  for j in range(tn // LANES)]
        rows.append(jnp.concatenate(pieces, axis=1))
    o_ref[...] = jnp.concatenate(rows, axis=0) + b_ref[...]


def _ada(c_rep, w, bias, tn=512):
    nb, k, _ = c_rep.shape
    n_out = w.shape[1]
    return pl.pallas_call(
        _ada_kernel,
        grid=(n_out // tn,),
        in_specs=[pl.BlockSpec((nb, k, LANES), lambda j: (0, 0, 0)),
                  pl.BlockSpec((k, tn), lambda j: (0, j)),
                  pl.BlockSpec((1, tn), lambda j: (0, j))],
        out_specs=pl.BlockSpec((nb, tn), lambda j: (0, j)),
        out_shape=jax.ShapeDtypeStruct((nb, n_out), F32),
        compiler_params=_cparams("arbitrary"),
        name="ada",
    )(c_rep, w, bias.reshape(1, n_out))


def _rms_mod(x, w, shift, scale):
    var = jnp.mean(x * x, axis=-1, keepdims=True)
    y = x * lax.rsqrt(var + NORM_EPS) * w
    return y * (1.0 + scale) + shift


def _norm_mod_kernel(x_ref, w_ref, sh_ref, sc_ref, o_ref):
    o_ref[...] = _rms_mod(x_ref[...], w_ref[...], sh_ref[...], sc_ref[...]).astype(o_ref.dtype)


def _norm_mod(x, w, shift, scale, out_dtype, tm=256):
    b, s, d = x.shape
    vec = pl.BlockSpec((None, 1, d), lambda bi, i: (bi, 0, 0))
    return pl.pallas_call(
        _norm_mod_kernel,
        grid=(b, s // tm),
        in_specs=[pl.BlockSpec((None, tm, d), lambda bi, i: (bi, i, 0)),
                  pl.BlockSpec((1, d), lambda bi, i: (0, 0)), vec, vec],
        out_specs=pl.BlockSpec((None, tm, d), lambda bi, i: (bi, i, 0)),
        out_shape=jax.ShapeDtypeStruct((b, s, d), out_dtype),
        compiler_params=_cparams("arbitrary", "arbitrary"),
        name="norm_mod",
    )(x, w.reshape(1, d), shift.reshape(b, 1, d), scale.reshape(b, 1, d))


def _matmul_kernel(a_ref, wt_hbm, *rest, has_bias, row0, n_col_tiles):
    b_ref, o_ref, stage, w16, sem = rest if has_bias else (None,) + rest
    j = pl.program_id(0)
    tn = stage.shape[0]
    rows = tn // WEIGHT_DMA_PARTS

    def fetch(col_tile):
        first = row0 + col_tile * tn
        return [pltpu.make_async_copy(wt_hbm.at[pl.ds(pl.multiple_of(first + p * rows, rows), rows), :],
                                      stage.at[pl.ds(p * rows, rows), :], sem)
                for p in range(WEIGHT_DMA_PARTS)]

    @pl.when(pl.program_id(1) == 0)
    def _():
        @pl.when(j == 0)
        def _():
            for part, copy in enumerate(fetch(0)):
                copy.start(priority=part % 2)

        for copy in fetch(j):
            copy.wait()
        chunk = min(tn, 2 * LANES)
        for c in range(0, tn, chunk):
            w16[:, c:c + chunk] = stage[c:c + chunk, :].T.astype(BF16)

        @pl.when(j + 1 < n_col_tiles)
        def _():
            for part, copy in enumerate(fetch(j + 1)):
                copy.start(priority=part % 2)

    acc = jnp.dot(a_ref[...], w16[...], preferred_element_type=F32)
    if has_bias:
        acc = acc + b_ref[...]
    o_ref[...] = acc.astype(o_ref.dtype)


def _matmul(a, w_t, col0, ncols, out_dtype, tm, tn, bias=None):
    m, k = a.shape
    f32_tile = tm * tn * 4
    vmem = min(tn * k * 4 + k * tn * 2 + 2 * tm * k * 2 + 2 * f32_tile + 3 * f32_tile, VMEM_CAPACITY - VMEM_RESERVE)
    in_specs = [pl.BlockSpec((tm, k), lambda j, i: (i, 0)),
                pl.BlockSpec(memory_space=pl.ANY)]
    args = [a, w_t]
    if bias is not None:
        in_specs.append(pl.BlockSpec((1, tn), lambda j, i: (0, j)))
        args.append(bias)
    return pl.pallas_call(
        functools.partial(_matmul_kernel, has_bias=bias is not None, row0=col0, n_col_tiles=ncols // tn),
        grid=(ncols // tn, m // tm),
        in_specs=in_specs,
        out_specs=pl.BlockSpec((tm, tn), lambda j, i: (i, j)),
        out_shape=jax.ShapeDtypeStruct((m, ncols), out_dtype),
        scratch_shapes=[pltpu.VMEM((tn, k), F32), pltpu.VMEM((k, tn), BF16), pltpu.SemaphoreType.DMA(())],
        compiler_params=_cparams("arbitrary", "arbitrary", vmem=vmem),
        name="proj",
    )(*args)


def _sb_attn_kernel(q_ref, k_ref, v_ref, tri_ref, o_ref, acc_ref, r_ref, *, sub, n_sub, scale):
    first = pl.program_id(2) * n_sub
    tri = tri_ref[...]
    row = lax.broadcasted_iota(jnp.int32, (sub, sub), 0)
    col = lax.broadcasted_iota(jnp.int32, (sub, sub), 1)
    causal = col < row

    def visit(q, blk, r, mask):
        start = pl.multiple_of(jnp.maximum(blk, 0) * sub, sub)
        kb = k_ref[pl.ds(start, sub), :]
        vb = v_ref[pl.ds(start, sub), :]
        z = lax.dot_general(q, kb, (((1,), (1,)), ((), ())), preferred_element_type=F32) * scale
        log_keep = -(jnp.maximum(z, 0.0) + jnp.log(1.0 + jnp.exp(-jnp.abs(z))))
        log_beta = log_keep + z
        if mask is not None:
            log_keep = jnp.where(mask, log_keep, 0.0)
        hi, lo = _split_bf16(log_keep)
        between = (jnp.dot(hi, tri, preferred_element_type=F32)
                   + jnp.dot(lo, tri, preferred_element_type=F32))
        w = jnp.exp(log_beta + between + r)
        if mask is not None:
            w = jnp.where(mask, w, 0.0)
        pv = jnp.dot(w.astype(BF16), vb, preferred_element_type=F32)
        return pv, r + jnp.sum(log_keep, axis=1, keepdims=True)

    n_win = ATTN_ALWAYS
    roww = lax.broadcasted_iota(jnp.int32, (sub, n_win * sub), 0)
    colw = lax.broadcasted_iota(jnp.int32, (sub, n_win * sub), 1)

    def win_start(s):
        blk = first + s
        clamp = s < n_win - 1
        first_blk = jnp.maximum(blk - (n_win - 1), 0) if clamp else blk - (n_win - 1)
        return pl.multiple_of(first_blk * sub, sub), clamp

    def masked(s, x):
        start, clamp = win_start(s)
        if clamp:
            return jnp.where(start + colw < (first + s) * sub + roww, x, 0.0)
        last = (n_win - 1) * sub
        return jnp.concatenate([x[:, :last], jnp.where(causal, x[:, last:], 0.0)], axis=1)

    scores = []
    for s in range(n_sub):
        kw = k_ref[pl.ds(win_start(s)[0], n_win * sub), :]
        q = q_ref[s * sub:(s + 1) * sub, :]
        scores.append(lax.dot_general(q, kw, (((1,), (1,)), ((), ())), preferred_element_type=F32) * scale)

    staged = []
    for s in range(n_sub):
        z = scores[s]
        log_keep = -(jnp.maximum(z, 0.0) + jnp.log(1.0 + jnp.exp(-jnp.abs(z))))
        log_beta = log_keep + z
        log_keep = masked(s, log_keep)
        chunks = [log_keep[:, c * sub:(c + 1) * sub] for c in range(n_win)]
        hi, lo = _split_bf16(jnp.concatenate(chunks, axis=0))
        inside = (jnp.dot(hi, tri, preferred_element_type=F32)
                  + jnp.dot(lo, tri, preferred_element_type=F32))
        staged.append((log_beta, chunks, inside))

    r_top = None
    for s in range(n_sub):
        log_beta, chunks, inside = staged[s]
        later = 0.0
        between = [None] * n_win
        for c in reversed(range(n_win)):
            between[c] = inside[c * sub:(c + 1) * sub, :] + later
            later = later + jnp.sum(chunks[c], axis=1, keepdims=True)
        w = masked(s, jnp.exp(log_beta + jnp.concatenate(between, axis=1)))
        vw = v_ref[pl.ds(win_start(s)[0], n_win * sub), :]
        acc_ref[s] = jnp.dot(w.astype(BF16), vw, preferred_element_type=F32)
        r = jnp.broadcast_to(later, (sub, LANES))
        r_ref[s] = r
        r_top = r if r_top is None else jnp.maximum(r_top, r)

    @pl.when(jnp.max(r_top) > UNDERFLOW_LOG)
    def _():
        for s in range(n_sub):
            q = q_ref[s * sub:(s + 1) * sub, :]

            def cond(carry):
                j, r_max = carry
                return jnp.logical_and(j >= 0, r_max > UNDERFLOW_LOG)

            def body(carry, s=s, q=q):
                j, _ = carry
                pv, r = visit(q, j, r_ref[s], None)
                acc_ref[s] += pv
                r_ref[s] = r
                return j - 1, jnp.max(r)

            lax.while_loop(cond, body, (first + s - ATTN_ALWAYS, jnp.max(r_ref[s])))

    for s in range(n_sub):
        o_ref[s * sub:(s + 1) * sub, :] = acc_ref[s].astype(o_ref.dtype)


def _sb_attention(qkv, sub=ATTN_SUB, n_sub=ATTN_SUBS_PER_STEP):
    b, s, _ = qkv.shape
    h, d = SB_HEADS, SB_HEAD_DIM
    assert sub == LANES, "the carried row sums are kept lane-replicated at the key sub-block width"
    assert s >= ATTN_ALWAYS * sub
    n_sub = min(n_sub, s // sub)
    tq = sub * n_sub
    idx = lax.broadcasted_iota(jnp.int32, (sub, sub), 0) > lax.broadcasted_iota(jnp.int32, (sub, sub), 1)
    tri = idx.astype(BF16)
    kern = functools.partial(_sb_attn_kernel, sub=sub, n_sub=n_sub, scale=1.0 / math.sqrt(d))
    return pl.pallas_call(
        kern,
        grid=(b, h, s // tq),
        in_specs=[pl.BlockSpec((None, tq, d), lambda bi, hi, qi: (bi, qi, hi)),
                  pl.BlockSpec((None, s, d), lambda bi, hi, qi: (bi, 0, h + hi)),
                  pl.BlockSpec((None, s, d), lambda bi, hi, qi: (bi, 0, 2 * h + hi)),
                  pl.BlockSpec((sub, sub), lambda bi, hi, qi: (0, 0))],
        out_specs=pl.BlockSpec((None, tq, d), lambda bi, hi, qi: (bi, qi, hi)),
        out_shape=jax.ShapeDtypeStruct((b, s, h * d), BF16),
        scratch_shapes=[pltpu.VMEM((n_sub, sub, d), F32), pltpu.VMEM((n_sub, sub, LANES), F32)],
        compiler_params=_cparams("arbitrary", "arbitrary", "arbitrary"),
        name="sb_attn",
    )(qkv, qkv, qkv, tri)


def _round_robin(stagewise):
    out = [None] * len(stagewise)
    live = list(range(len(stagewise)))
    while live:
        for i in list(live):
            try:
                next(stagewise[i])
            except StopIteration as stop:
                out[i] = stop.value
                live.remove(i)
    return out


def _mlstm_head(head, q, k, vb, og, gcol, grow, nw, tril_ref, triu_ref, c_st, n_st, m_st, *, chunk):
    li_col = gcol[:, head:head + 1]
    lf_col = _log_sigmoid(gcol[:, ML_HEADS + head:ML_HEADS + head + 1])
    li_row = grow[head:head + 1, :]
    lf_row = _log_sigmoid(grow[ML_HEADS + head:ML_HEADS + head + 1, :])

    hi, lo = _split_bf16(jnp.broadcast_to(lf_col, (chunk, LANES)))
    b_col = (jnp.dot(tril_ref[...], hi, preferred_element_type=F32)
             + jnp.dot(tril_ref[...], lo, preferred_element_type=F32))[:, 0:1]
    hi, lo = _split_bf16(jnp.broadcast_to(lf_row, (2 * SUBLANES, chunk)))
    b_row = (jnp.dot(hi, triu_ref[...], preferred_element_type=F32)
             + jnp.dot(lo, triu_ref[...], preferred_element_type=F32))[0:1, :]
    qb = q.astype(BF16)
    kb = k.astype(BF16)
    qk = lax.dot_general(qb, kb, (((1,), (1,)), ((), ())), preferred_element_type=F32)
    q_mem = jnp.dot(qb, c_st[...].astype(BF16), preferred_element_type=F32)
    yield

    m_prev = m_st[...]
    a_col = b_col + m_prev
    row_t = lax.broadcasted_iota(jnp.int32, (chunk, chunk), 0)
    col_s = lax.broadcasted_iota(jnp.int32, (chunk, chunk), 1)
    d_mat = jnp.where(col_s <= row_t, b_col - b_row + li_row, -jnp.inf)
    m_col = jnp.maximum(a_col, jnp.max(d_mat, axis=1, keepdims=True))
    w_intra = jnp.exp(d_mat - m_col)
    w_inter = jnp.exp(a_col - m_col)
    sc = qk * w_intra
    sc_v = jnp.dot(sc.astype(BF16), vb, preferred_element_type=F32)

    b_last = b_col[chunk - 1:chunk, :]
    g_col = b_last - b_col + li_col
    m_new = jnp.maximum(b_last + m_prev, jnp.max(g_col, axis=0, keepdims=True))
    decay = jnp.exp(b_last + m_prev - m_new)
    kw = k * jnp.exp(g_col - m_new)
    kw_v = lax.dot_general(kw.astype(BF16), vb, (((0,), (0,)), ((), ())), preferred_element_type=F32)
    yield

    num = w_inter * q_mem + sc_v
    den = (w_inter * jnp.sum(q * n_st[...], axis=1, keepdims=True)
           + jnp.sum(sc, axis=1, keepdims=True))
    h_out = num / jnp.maximum(jnp.abs(den), jnp.exp(-m_col))
    c_st[...] = decay * c_st[...] + kw_v
    n_st[...] = decay * n_st[...] + jnp.sum(kw, axis=0, keepdims=True)
    m_st[...] = m_new

    hn = h_out * lax.rsqrt(jnp.mean(h_out * h_out, axis=1, keepdims=True) + NORM_EPS) * nw
    return jax.nn.sigmoid(og) * hn


def _mlstm_kernel(qp_ref, kp_ref, v_ref, og_ref, gcol_ref, grow_ref, cwq_ref, cwk_ref, nw_ref,
                  tril_ref, triu_ref, y_ref, c_st, n_st, m_st, qbuf, kbuf, *, chunk, heads, dim):
    halo = SUBLANES

    @pl.when(pl.program_id(1) == 0)
    def _():
        c_st[...] = jnp.zeros_like(c_st)
        n_st[...] = jnp.zeros_like(n_st)
        m_st[...] = jnp.zeros_like(m_st)
        qbuf[0:halo, :] = jnp.zeros((halo, heads * dim), F32)
        kbuf[0:halo, :] = jnp.zeros((halo, heads * dim), F32)

    qbuf[halo:halo + chunk, :] = qp_ref[...]
    kbuf[halo:halo + chunk, :] = kp_ref[...]

    def conv_silu(buf, cw_ref):
        acc = buf[halo:halo + chunk, :] * cw_ref[CONV_WIDTH - 1:CONV_WIDTH, :]
        for j in range(CONV_WIDTH - 1):
            off = halo - (CONV_WIDTH - 1) + j
            acc = acc + buf[off:off + chunk, :] * cw_ref[j:j + 1, :]
        return acc * jax.nn.sigmoid(acc)

    q = conv_silu(qbuf, cwq_ref)
    k = conv_silu(kbuf, cwk_ref) * (1.0 / math.sqrt(dim))
    qbuf[0:halo, :] = qbuf[chunk:chunk + halo, :]
    kbuf[0:halo, :] = kbuf[chunk:chunk + halo, :]

    gcol = gcol_ref[...]
    grow = grow_ref[...]
    cols = [slice(head * dim, (head + 1) * dim) for head in range(heads)]
    ys = _round_robin([
        _mlstm_head(head, q[:, c], k[:, c], v_ref[:, c], og_ref[:, c], gcol, grow, nw_ref[:, c],
                    tril_ref, triu_ref, c_st.at[head], n_st.at[head], m_st.at[head], chunk=chunk)
        for head, c in enumerate(cols)])
    for y, c in zip(ys, cols):
        y_ref[:, c] = y.astype(y_ref.dtype)


def _mlstm(ml_qk, ml_v, ml_o, gates, conv_w, ml_norm_w, chunk=ML_CHUNK):
    b, s, _ = ml_v.shape
    h, d = ML_HEADS, ML_HEAD_DIM
    w = h * d
    gates_t = jnp.swapaxes(gates, 1, 2)
    r = lax.broadcasted_iota(jnp.int32, (chunk, chunk), 0)
    c = lax.broadcasted_iota(jnp.int32, (chunk, chunk), 1)
    tril = (c <= r).astype(BF16)
    triu = (r <= c).astype(BF16)
    wide = lambda off: pl.BlockSpec((None, chunk, w), lambda bi, ci: (bi, ci, off))
    const = lambda shape, off=0: pl.BlockSpec(shape, lambda bi, ci: (0, off))
    kern = functools.partial(_mlstm_kernel, chunk=chunk, heads=h, dim=d)
    return pl.pallas_call(
        kern,
        grid=(b, s // chunk),
        in_specs=[wide(0), wide(1), wide(0), wide(0),
                  pl.BlockSpec((None, chunk, 2 * h), lambda bi, ci: (bi, ci, 0)),
                  pl.BlockSpec((None, 2 * h, chunk), lambda bi, ci: (bi, 0, ci)),
                  const((CONV_WIDTH, w)), const((CONV_WIDTH, w), 1), const((1, w)),
                  const((chunk, chunk)), const((chunk, chunk))],
        out_specs=wide(0),
        out_shape=jax.ShapeDtypeStruct((b, s, w), BF16),
        scratch_shapes=[pltpu.VMEM((h, d, d), F32), pltpu.VMEM((h, 1, d), F32), pltpu.VMEM((h, 1, 1), F32),
                        pltpu.VMEM((chunk + SUBLANES, w), F32), pltpu.VMEM((chunk + SUBLANES, w), F32)],
        compiler_params=_cparams("arbitrary", "arbitrary"),
        name="mlstm",
    )(ml_qk, ml_qk, ml_v, ml_o, gates, gates_t, conv_w, conv_w, ml_norm_w.reshape(1, w), tril, triu)


def _out_proj_kernel(ya_ref, yb_ref, wa_ref, wb_ref, x_ref, g_ref, o_ref):
    mix = (jnp.dot(ya_ref[...], wa_ref[...], preferred_element_type=F32)
           + jnp.dot(yb_ref[...], wb_ref[...], preferred_element_type=F32))
    o_ref[...] = x_ref[...] + g_ref[...] * mix


def _out_proj(y_sb, y_ml, w_out, x, gate, tm, tn=512):
    b, s, d = x.shape
    ka, kb = y_sb.shape[-1], y_ml.shape[-1]
    assert ka == kb
    n = b * s
    rows_per_seq = s // tm
    out = pl.pallas_call(
        _out_proj_kernel,
        grid=(n // tm, d // tn),
        in_specs=[pl.BlockSpec((tm, ka), lambda i, j: (i, 0)),
                  pl.BlockSpec((tm, kb), lambda i, j: (i, 0)),
                  pl.BlockSpec((ka, tn), lambda i, j: (0, j)),
                  pl.BlockSpec((kb, tn), lambda i, j: (1, j)),
                  pl.BlockSpec((tm, tn), lambda i, j: (i, j)),
                  pl.BlockSpec((None, 1, tn), lambda i, j: (i // rows_per_seq, 0, j))],
        out_specs=pl.BlockSpec((tm, tn), lambda i, j: (i, j)),
        out_shape=jax.ShapeDtypeStruct((n, d), F32),
        compiler_params=_cparams("arbitrary", "arbitrary"),
        name="out_proj",
    )(y_sb.reshape(n, ka), y_ml.reshape(n, kb), w_out, w_out, x.reshape(n, d), gate.reshape(b, 1, d))
    return out.reshape(b, s, d)


def _router_kernel(h_ref, nw_ref, sh_ref, sc_ref, wr_ref, br_ref, tri_ref, u_ref, ri_ref, rw_ref, cnt_ref, run_ref):
    first = jnp.logical_and(pl.program_id(0) == 0, pl.program_id(1) == 0)

    @pl.when(first)
    def _():
        run_ref[...] = jnp.zeros_like(run_ref)

    u = _rms_mod(h_ref[...], nw_ref[...], sh_ref[...], sc_ref[...])
    tm = u.shape[0]
    uh, ul = _split_bf16(u)
    u_ref[...] = _pack_bf16_pairs(uh)
    wh, wl = _split_bf16(wr_ref[...])
    logits = (jnp.dot(uh, wh, preferred_element_type=F32) + jnp.dot(ul, wh, preferred_element_type=F32)
              + jnp.dot(uh, wl, preferred_element_type=F32) + br_ref[...])

    lane = lax.broadcasted_iota(jnp.int32, (tm, LANES), 1)
    lane_f = lane.astype(F32)
    neg = -jnp.inf
    first_lane = lambda hit: jnp.min(jnp.where(hit, lane_f, float(LANES)), axis=1, keepdims=True)

    gl = jnp.where(lane < N_GROUPS, logits, neg)
    g_max = jnp.max(gl, axis=1, keepdims=True)
    g_p = 1.0 / jnp.sum(jnp.exp(gl - g_max), axis=1, keepdims=True)
    g_sel = first_lane(gl == g_max)

    lo_lane = N_GROUPS + EXPERTS_PER_GROUP * g_sel
    el = jnp.where(jnp.logical_and(lane_f >= lo_lane, lane_f < lo_lane + EXPERTS_PER_GROUP), logits, neg)
    e1_max = jnp.max(el, axis=1, keepdims=True)
    l1 = first_lane(el == e1_max)
    el2 = jnp.where(lane_f == l1, neg, el)
    e2_max = jnp.max(el2, axis=1, keepdims=True)
    l2 = first_lane(el2 == e2_max)
    z_sum = jnp.sum(jnp.exp(el - e1_max), axis=1, keepdims=True)
    p1 = 1.0 / z_sum
    p2 = jnp.exp(e2_max - e1_max) / z_sum
    w1 = p1 / (p1 + p2) * g_p
    w2 = p2 / (p1 + p2) * g_p
    e1 = l1 - N_GROUPS
    e2 = l2 - N_GROUPS

    hit1 = lane_f == e1
    hit2 = lane_f == e2
    onehot = jnp.where(jnp.logical_or(hit1, hit2), 1.0, 0.0)
    before = jnp.dot(tri_ref[...], onehot.astype(BF16), preferred_element_type=F32) + run_ref[...]
    rank1 = jnp.sum(jnp.where(hit1, before, 0.0), axis=1, keepdims=True)
    rank2 = jnp.sum(jnp.where(hit2, before, 0.0), axis=1, keepdims=True)
    run = run_ref[...] + jnp.sum(onehot, axis=0, keepdims=True)
    run_ref[...] = run

    ri_ref[...] = jnp.where(lane == 0, e1, jnp.where(lane == 1, e2, jnp.where(lane == 2, rank1,
                            jnp.where(lane == 3, rank2, 0.0)))).astype(jnp.int32)
    rw_ref[...] = jnp.where(lane == 0, w1, jnp.where(lane == 1, w2, 0.0))
    cnt_ref[...] = jnp.broadcast_to(run, cnt_ref.shape).astype(jnp.int32)


def _router(h, norm_w, shift, scale, w_router, b_router, tm=256):
    b, s, d = h.shape
    n = b * s
    tri = (lax.broadcasted_iota(jnp.int32, (tm, tm), 1) < lax.broadcasted_iota(jnp.int32, (tm, tm), 0)).astype(BF16)
    vec = pl.BlockSpec((None, 1, d), lambda bi, i: (bi, 0, 0))
    rows = lambda width: pl.BlockSpec((tm, width), lambda bi, i: (bi * (s // tm) + i, 0))
    return pl.pallas_call(
        _router_kernel,
        grid=(b, s // tm),
        in_specs=[pl.BlockSpec((None, tm, d), lambda bi, i: (bi, i, 0)),
                  pl.BlockSpec((1, d), lambda bi, i: (0, 0)), vec, vec,
                  pl.BlockSpec((d, LANES), lambda bi, i: (0, 0)),
                  pl.BlockSpec((1, LANES), lambda bi, i: (0, 0)),
                  pl.BlockSpec((tm, tm), lambda bi, i: (0, 0))],
        out_specs=[rows(d // 2), rows(LANES), rows(LANES),
                   pl.BlockSpec((SUBLANES, LANES), lambda bi, i: (0, 0))],
        out_shape=[jax.ShapeDtypeStruct((n, d // 2), jnp.int32), jax.ShapeDtypeStruct((n, LANES), jnp.int32),
                   jax.ShapeDtypeStruct((n, LANES), F32), jax.ShapeDtypeStruct((SUBLANES, LANES), jnp.int32)],
        scratch_shapes=[pltpu.VMEM((1, LANES), F32)],
        compiler_params=_cparams("arbitrary", "arbitrary"),
        name="router",
    )(h, norm_w.reshape(1, d), shift.reshape(b, 1, d), scale.reshape(b, 1, d), w_router, b_router, tri)


ROW_LOOP_UNROLL = 8
ZERO_ROWS = 128


def _dispatch_kernel(tail_ref, dest_ref, u_ref, xb_hbm, zbuf, zsem, sem, *, tokens, block):
    n_exp = tail_ref.shape[0]

    def clear(e, half):
        row = pl.multiple_of(tail_ref[e] + half * ZERO_ROWS, ZERO_ROWS)
        return pltpu.make_async_copy(zbuf, xb_hbm.at[pl.ds(row, ZERO_ROWS)], zsem)

    @pl.when(pl.program_id(0) == 0)
    def _():
        zbuf[...] = jnp.zeros_like(zbuf)
        for start in (True, False):
            def each(e, carry):
                @pl.when(tail_ref[e] >= 0)
                def _():
                    for half in range(block // ZERO_ROWS):
                        clear(e, half).start() if start else clear(e, half).wait()
                return carry
            lax.fori_loop(0, n_exp, each, 0)

    def row_copy(t, k):
        return pltpu.make_async_copy(u_ref.at[pl.ds(t, 1)], xb_hbm.at[pl.ds(dest_ref[0, TOP_K * t + k], 1)], sem)

    def issue(t, carry):
        for k in range(TOP_K):
            row_copy(t, k).start(priority=k % 2)
        return carry

    lax.fori_loop(0, tokens, issue, 0, unroll=ROW_LOOP_UNROLL)

    def drain(t, carry):
        for k in range(TOP_K):
            row_copy(t, k).wait()
        return carry

    lax.fori_loop(0, tokens, drain, 0, unroll=ROW_LOOP_UNROLL)


def _dispatch(u, dest, tail_block_row, cap, tokens=256, block=MOE_BLOCK):
    n, d = u.shape
    steps = n // tokens
    kern = functools.partial(_dispatch_kernel, tokens=tokens, block=block)
    return pl.pallas_call(
        kern,
        grid_spec=pltpu.PrefetchScalarGridSpec(
            num_scalar_prefetch=1,
            grid=(steps,),
            in_specs=[pl.BlockSpec((None, 1, TOP_K * tokens), lambda i, tail: (i, 0, 0), memory_space=pltpu.SMEM),
                      pl.BlockSpec((tokens, d), lambda i, tail: (i, 0))],
            out_specs=pl.BlockSpec(memory_space=pl.ANY),
            scratch_shapes=[pltpu.VMEM((ZERO_ROWS, d), u.dtype), pltpu.SemaphoreType.DMA(()),
                            pltpu.SemaphoreType.DMA(())]),
        out_shape=jax.ShapeDtypeStruct((cap, d), u.dtype),
        compiler_params=_cparams("arbitrary"),
        name="dispatch",
    )(tail_block_row, dest.reshape(steps, 1, TOP_K * tokens), u)


def _run_plan(keys, ok):
    steps = ok.shape[0]
    idx = jnp.arange(steps, dtype=jnp.int32)
    changed = idx == 0
    for key in keys:
        changed = jnp.logical_or(changed, key != jnp.roll(key, 1))
    first = jnp.logical_and(changed, ok)
    slot = (jnp.cumsum(first.astype(jnp.int32)) - 1) % 2
    starts_from = lax.cummin(jnp.where(first, idx, steps), axis=0, reverse=True)
    nxt = jnp.concatenate([starts_from[1:], jnp.full((1,), steps, jnp.int32)])
    more = nxt < steps
    as_i32 = lambda a: a.astype(jnp.int32)
    return as_i32(first), as_i32(slot), as_i32(more), as_i32(jnp.minimum(nxt, steps - 1))


def _run_weights(step, first_ref, slot_ref, more_ref, fetch_cur, fetch_next):
    slot = slot_ref[step]

    @pl.when(step == 0)
    def _():
        for part, copy in enumerate(fetch_cur(slot)):
            copy.start(priority=part % 2)

    @pl.when(first_ref[step] == 1)
    def _():
        @pl.when(more_ref[step] == 1)
        def _():
            for part, copy in enumerate(fetch_next(1 - slot)):
                copy.start(priority=part % 2)

        for copy in fetch_cur(slot):
            copy.wait()

    return slot


def _e1_kernel(blk_ref, ex_ref, ft_ref, oblk_ref, oft_ref, ok_ref, first_ref, slot_ref, more_ref, nex_ref, nft_ref,
               x_ref, wg_hbm, wu_hbm, o_ref, wg_buf, wu_buf, sems, *, tf):
    del blk_ref, oblk_ref, oft_ref
    step = pl.program_id(0)

    def fetch(e, f, slot):
        cols = pl.ds(pl.multiple_of(f * tf, tf), tf)
        rows = wg_buf.shape[1] // WEIGHT_DMA_PARTS
        return [pltpu.make_async_copy(hbm.at[e, pl.ds(p * rows, rows), cols],
                                      buf.at[slot, pl.ds(p * rows, rows)], sems.at[slot, which])
                for p in range(WEIGHT_DMA_PARTS)
                for which, (hbm, buf) in enumerate(((wg_hbm, wg_buf), (wu_hbm, wu_buf)))]

    slot = _run_weights(step, first_ref, slot_ref, more_ref,
                        lambda sl: fetch(ex_ref[step], ft_ref[step], sl),
                        lambda sl: fetch(nex_ref[step], nft_ref[step], sl))
    ok = ok_ref[step] == 1

    @pl.when(ok)
    def _():
        x_lo, x_hi = _unpack_bf16_pairs(x_ref[...])
        half = x_lo.shape[1]
        mm = lambda w_buf: (jnp.dot(x_lo, w_buf[slot, :half, :], preferred_element_type=F32)
                            + jnp.dot(x_hi, w_buf[slot, half:, :], preferred_element_type=F32))
        g = mm(wg_buf)
        up = mm(wu_buf)
        o_ref[...] = ((g * jax.nn.sigmoid(g)) * up).astype(o_ref.dtype)

    @pl.when(jnp.logical_not(ok))
    def _():
        o_ref[...] = jnp.zeros_like(o_ref)


def _e2_kernel(blk_ref, ex_ref, oblk_ref, ok_ref, first_ref, slot_ref, more_ref, nex_ref,
               h_ref, wd_hbm, o_ref, wd_buf, sems):
    del blk_ref, oblk_ref
    step = pl.program_id(0)
    rows = wd_buf.shape[1] // WEIGHT_DMA_PARTS
    fetch = lambda e, slot: [pltpu.make_async_copy(wd_hbm.at[e, pl.ds(p * rows, rows)],
                                                   wd_buf.at[slot, pl.ds(p * rows, rows)], sems.at[slot])
                             for p in range(WEIGHT_DMA_PARTS)]
    slot = _run_weights(step, first_ref, slot_ref, more_ref,
                        lambda sl: fetch(ex_ref[step], sl), lambda sl: fetch(nex_ref[step], sl))
    ok = ok_ref[step] == 1

    @pl.when(ok)
    def _():
        o_ref[...] = jnp.dot(h_ref[...].astype(F32), wd_buf[slot], preferred_element_type=F32)

    @pl.when(jnp.logical_not(ok))
    def _():
        o_ref[...] = jnp.zeros_like(o_ref)


def _experts(xb, blocks_per_expert, w_gate, w_up, w_down, bm=MOE_BLOCK, tf=512):
    cap, dp = xb.shape
    n_exp, d, dff = w_gate.shape
    n_blocks = cap // bm
    n_ft = dff // tf
    nb = blocks_per_expert.astype(jnp.int32)
    bstart = jnp.cumsum(nb) - nb
    used = jnp.sum(nb)

    steps1 = n_blocks * n_ft
    step = jnp.arange(steps1, dtype=jnp.int32)
    ok1 = step < used * n_ft
    send = jnp.cumsum(nb * n_ft)
    s_idx = jnp.minimum(step, used * n_ft - 1)
    e_of = jnp.minimum(jnp.sum(s_idx[:, None] >= send[None, :], axis=1), n_exp - 1).astype(jnp.int32)
    local = s_idx - (send[e_of] - nb[e_of] * n_ft)
    nbe = jnp.maximum(nb[e_of], 1)
    ft_of = (local // nbe).astype(jnp.int32)
    blk_of = (bstart[e_of] + local % nbe).astype(jnp.int32)
    spare = step - used * n_ft
    oblk_of = jnp.where(ok1, blk_of, used + spare // n_ft).astype(jnp.int32)
    oft_of = jnp.where(ok1, ft_of, spare % n_ft).astype(jnp.int32)

    first1, slot1, more1, nxt1 = _run_plan([e_of, ft_of], ok1)
    hmid = pl.pallas_call(
        functools.partial(_e1_kernel, tf=tf),
        grid_spec=pltpu.PrefetchScalarGridSpec(
            num_scalar_prefetch=11,
            grid=(steps1,),
            in_specs=[pl.BlockSpec((bm, dp), lambda s, blk, *_: (blk[s], 0)),
                      pl.BlockSpec(memory_space=pl.ANY),
                      pl.BlockSpec(memory_space=pl.ANY)],
            out_specs=pl.BlockSpec((bm, tf), lambda s, blk, ex, ft, oblk, oft, *_: (oblk[s], oft[s])),
            scratch_shapes=[pltpu.VMEM((2, d, tf), F32), pltpu.VMEM((2, d, tf), F32),
                            pltpu.SemaphoreType.DMA((2, 2))]),
        out_shape=jax.ShapeDtypeStruct((cap, dff), BF16),
        compiler_params=_cparams("arbitrary"),
        name="e1",
    )(blk_of, e_of, ft_of, oblk_of, oft_of, ok1.astype(jnp.int32), first1, slot1, more1, e_of[nxt1], ft_of[nxt1],
      xb, w_gate, w_up)

    blk = jnp.arange(n_blocks, dtype=jnp.int32)
    b_idx = jnp.minimum(blk, used - 1)
    bend = jnp.cumsum(nb)
    e2_of = jnp.minimum(jnp.sum(b_idx[:, None] >= bend[None, :], axis=1), n_exp - 1).astype(jnp.int32)
    ok2 = blk < used
    first2, slot2, more2, nxt2 = _run_plan([e2_of], ok2)
    yb = pl.pallas_call(
        _e2_kernel,
        grid_spec=pltpu.PrefetchScalarGridSpec(
            num_scalar_prefetch=8,
            grid=(n_blocks,),
            in_specs=[pl.BlockSpec((bm, dff), lambda s, hb, *_: (hb[s], 0)),
                      pl.BlockSpec(memory_space=pl.ANY)],
            out_specs=pl.BlockSpec((bm, d), lambda s, hb, ex, ob, *_: (ob[s], 0)),
            scratch_shapes=[pltpu.VMEM((2, dff, d), F32), pltpu.SemaphoreType.DMA((2,))]),
        out_shape=jax.ShapeDtypeStruct((cap, d), F32),
        compiler_params=_cparams("arbitrary"),
        name="e2",
    )(b_idx, e2_of, blk, ok2.astype(jnp.int32), first2, slot2, more2, e2_of[nxt2], hmid, w_down)
    return yb


def _combine_kernel(dest_ref, dest_next_ref, yb_hbm, h_ref, rw_ref, g_ref, nw_ref, sh_ref, sc_ref, o_ref,
                    rows, sems, *, tokens, steps):
    step = pl.program_id(0)
    slot = step % 2

    def row_copy(d_ref, buf, t, k):
        return pltpu.make_async_copy(yb_hbm.at[pl.ds(d_ref[0, TOP_K * t + k], 1)],
                                     rows.at[buf, k, pl.ds(t, 1)], sems.at[buf])

    def gather(d_ref, buf):
        def issue(t, carry):
            for k in range(TOP_K):
                row_copy(d_ref, buf, t, k).start(priority=k % 2)
            return carry

        lax.fori_loop(0, tokens, issue, 0, unroll=ROW_LOOP_UNROLL)

    @pl.when(step == 0)
    def _():
        gather(dest_ref, 0)

    @pl.when(step + 1 < steps)
    def _():
        gather(dest_next_ref, 1 - slot)

    def drain(t, carry):
        for k in range(TOP_K):
            row_copy(dest_ref, slot, t, k).wait()
        return carry

    lax.fori_loop(0, tokens, drain, 0, unroll=ROW_LOOP_UNROLL)

    rw = rw_ref[...]
    ffn = rows[slot, 0] * rw[:, 0:1]
    for k in range(1, TOP_K):
        ffn = ffn + rows[slot, k] * rw[:, k:k + 1]
    h2 = h_ref[...] + g_ref[...] * ffn
    o_ref[...] = _rms_mod(h2, nw_ref[...], sh_ref[...], sc_ref[...])


def _combine(yb, dest, h, route_w, gate, norm_w, shift, scale, tokens=256):
    b, s, d = h.shape
    n = b * s
    per_seq = s // tokens
    steps = n // tokens
    kern = functools.partial(_combine_kernel, tokens=tokens, steps=steps)
    vec = pl.BlockSpec((None, 1, d), lambda i: (i // per_seq, 0, 0))
    dest_blocks = dest.reshape(steps, 1, TOP_K * tokens)
    dest_spec = lambda ahead: pl.BlockSpec((None, 1, TOP_K * tokens),
                                           lambda i: (jnp.minimum(i + ahead, steps - 1), 0, 0),
                                           memory_space=pltpu.SMEM)
    out = pl.pallas_call(
        kern,
        grid=(steps,),
        in_specs=[dest_spec(0), dest_spec(1),
                  pl.BlockSpec(memory_space=pl.ANY),
                  pl.BlockSpec((tokens, d), lambda i: (i, 0)),
                  pl.BlockSpec((tokens, LANES), lambda i: (i, 0)),
                  vec,
                  pl.BlockSpec((1, d), lambda i: (0, 0)), vec, vec],
        out_specs=pl.BlockSpec((tokens, d), lambda i: (i, 0)),
        out_shape=jax.ShapeDtypeStruct((n, d), F32),
        scratch_shapes=[pltpu.VMEM((2, TOP_K, tokens, d), F32), pltpu.SemaphoreType.DMA((2,))],
        compiler_params=_cparams("arbitrary"),
        name="combine",
    )(dest_blocks, dest_blocks, yb, h.reshape(n, d), route_w,
      gate.reshape(b, 1, d), norm_w.reshape(1, d), shift.reshape(b, 1, d), scale.reshape(b, 1, d))
    return out.reshape(b, s, d)


def _layer(h, mod, norm1_w, w_in, conv_w, ml_gate_bias, ml_norm_w, w_out, norm2_w,
           w_router_group, b_router_group, w_router_expert, b_router_expert,
           w_exp_gate, w_exp_up, w_exp_down, final):
    b, s, d = h.shape
    n = b * s
    sh1, sc1, g1, sh2, sc2, g2 = jnp.split(mod, 6, axis=-1)
    sb_w = SB_HEADS * SB_HEAD_DIM
    ml_w = ML_HEADS * ML_HEAD_DIM

    u = _norm_mod(h, norm1_w, sh1, sc1, BF16).reshape(n, d)
    tm = 1024 if n % 1024 == 0 else 512
    w_in_t = w_in.T
    sb_qkv = _matmul(u, w_in_t, 0, 3 * sb_w, BF16, tm, PROJ_TN)
    ml_qk = _matmul(u, w_in_t, 3 * sb_w, 2 * ml_w, F32, tm, PROJ_TN)
    ml_v = _matmul(u, w_in_t, 3 * sb_w + 2 * ml_w, ml_w, BF16, tm, PROJ_TN)
    ml_o = _matmul(u, w_in_t, 3 * sb_w + 3 * ml_w, ml_w, F32, tm, PROJ_TN)
    n_gate = 2 * ML_HEADS
    w_gate = jnp.pad(w_in_t[3 * sb_w + 4 * ml_w:, :], ((0, LANES - n_gate), (0, 0)))
    b_gate = jnp.pad(ml_gate_bias, (0, LANES - n_gate)).reshape(1, LANES)
    gates = _matmul(u, w_gate, 0, LANES, F32, tm, LANES, bias=b_gate)[:, :n_gate]

    y_sb = _sb_attention(sb_qkv.reshape(b, s, 3 * sb_w))
    y_ml = _mlstm(ml_qk.reshape(b, s, 2 * ml_w), ml_v.reshape(b, s, ml_w), ml_o.reshape(b, s, ml_w),
                  gates.reshape(b, s, n_gate), conv_w, ml_norm_w)
    h = _out_proj(y_sb, y_ml, w_out.astype(BF16), h, g1, tm=min(1024, s))

    w_router = jnp.pad(jnp.concatenate([w_router_group, w_router_expert], axis=1),
                       ((0, 0), (0, LANES - N_GROUPS - N_EXPERTS)))
    b_router = jnp.pad(jnp.concatenate([b_router_group, b_router_expert]),
                       (0, LANES - N_GROUPS - N_EXPERTS)).reshape(1, LANES)
    u2, route_i, route_w, counts = _router(h, norm2_w, sh2, sc2, w_router, b_router)
    counts = counts[0, :N_EXPERTS]
    blocks_per_expert = (counts + MOE_BLOCK - 1) // MOE_BLOCK
    padded = blocks_per_expert * MOE_BLOCK
    pstarts = jnp.cumsum(padded) - padded
    dest = pstarts[route_i[:, 0:TOP_K]] + route_i[:, TOP_K:2 * TOP_K]
    cap = (-(-(n * TOP_K) // MOE_BLOCK) + N_EXPERTS) * MOE_BLOCK
    tail_block_row = jnp.where(counts % MOE_BLOCK != 0, pstarts + padded - MOE_BLOCK, -1)
    spare_row = jnp.sum(padded) + MOE_BLOCK * jnp.arange(N_EXPERTS, dtype=jnp.int32)
    spare_row = jnp.where(spare_row < cap, spare_row, -1)
    clear_rows = jnp.concatenate([tail_block_row, spare_row]).astype(jnp.int32)
    xb = _dispatch(u2, dest, clear_rows, cap)
    yb = _experts(xb, blocks_per_expert, w_exp_gate, w_exp_up, w_exp_down)
    norm_w, shift, scale = final
    return _combine(yb, dest, h, route_w, g2, norm_w, shift, scale)


def kernel(x, c, norm1_w, w_in, conv_w, ml_gate_bias, ml_norm_w, w_out, norm2_w, w_router_group, b_router_group, w_router_expert, b_router_expert, w_exp_gate, w_exp_up, w_exp_down, w_ada, b_ada, final_norm_w, w_ada_final, b_ada_final):
    b, s, d = x.shape
    depth = w_in.shape[0]
    assert depth == 1, "the final norm is fused into the (single) layer's combine step"
    c_rep = jnp.broadcast_to(c[:, :, None], (b, d, LANES))
    fmod = _ada(c_rep, w_ada_final, b_ada_final)
    sh_f, sc_f = jnp.split(fmod, 2, axis=-1)
    mod = _ada(c_rep, w_ada[0], b_ada[0])
    return _layer(x, mod, norm1_w[0], w_in[0], conv_w[0], ml_gate_bias[0], ml_norm_w[0], w_out[0], norm2_w[0],
                  w_router_group[0], b_router_group[0], w_router_expert[0], b_router_expert[0],
                  w_exp_gate[0], w_exp_up[0], w_exp_down[0], (final_norm_w, sh_f, sc_f))
```

```python
import functools
import math

import jax
import jax.numpy as jnp
from jax import lax
from jax.experimental import pallas as pl
from jax.experimental.pallas import tpu as pltpu

SB_HEADS = 16
SB_HEAD_DIM = 128
ML_HEADS = 4
ML_HEAD_DIM = 512
CONV_WIDTH = 4
N_GROUPS = 4
EXPERTS_PER_GROUP = 8
N_EXPERTS = N_GROUPS * EXPERTS_PER_GROUP
TOP_K = 2
NORM_EPS = 1e-6

LANES = 128
SUBLANES = 8
VMEM_CAPACITY = 64 * 1024 * 1024
VMEM_RESERVE = 4 * 1024 * 1024
VMEM_LIMIT = 56 * 1024 * 1024

ATTN_SUB = 128
ATTN_SUBS_PER_STEP = 16
ATTN_ALWAYS = 3
ML_CHUNK = 256
MOE_BLOCK = 256
PROJ_TN = 1024
WEIGHT_DMA_PARTS = 4
UNDERFLOW_LOG = -90.0

F32 = jnp.float32
BF16 = jnp.bfloat16


def _cparams(*sem, vmem=VMEM_LIMIT):
    return pltpu.CompilerParams(dimension_semantics=sem, vmem_limit_bytes=vmem)


def _split_bf16(x):
    hi = x.astype(BF16)
    lo = (x - hi.astype(F32)).astype(BF16)
    return hi, lo


_HIGH_HALF = -65536


def _pack_bf16_pairs(x):
    k = x.shape[1] // 2
    bits = lambda v: lax.bitcast_convert_type(v.astype(F32), jnp.int32)
    return lax.shift_right_logical(bits(x[:, :k]), 16) | (bits(x[:, k:]) & _HIGH_HALF)


def _unpack_bf16_pairs(w):
    lo = lax.bitcast_convert_type(lax.shift_left(w, 16), F32)
    hi = lax.bitcast_convert_type(w & _HIGH_HALF, F32)
    return lo, hi


def _log_sigmoid(x):
    return jnp.minimum(x, 0.0) - jnp.log1p(jnp.exp(-jnp.abs(x)))


def _ada_kernel(c_ref, w_ref, b_ref, o_ref):
    nb, tn = o_ref.shape
    rows = []
    for b in range(nb):
        cb = c_ref[b]
        cb = cb * jax.nn.sigmoid(cb)
        pieces = [jnp.sum(w_ref[:, j * LANES:(j + 1) * LANES] * cb, axis=0, keepdims=True)
                  for j in range(tn // LANES)]
        rows.append(jnp.concatenate(pieces, axis=1))
    o_ref[...] = jnp.concatenate(rows, axis=0) + b_ref[...]


def _ada(c_rep, w, bias, tn=512):
    nb, k, _ = c_rep.shape
    n_out = w.shape[1]
    return pl.pallas_call(
        _ada_kernel,
        grid=(n_out // tn,),
        in_specs=[pl.BlockSpec((nb, k, LANES), lambda j: (0, 0, 0)),
                  pl.BlockSpec((k, tn), lambda j: (0, j)),
                  pl.BlockSpec((1, tn), lambda j: (0, j))],
        out_specs=pl.BlockSpec((nb, tn), lambda j: (0, j)),
        out_shape=jax.ShapeDtypeStruct((nb, n_out), F32),
        compiler_params=_cparams("arbitrary"),
        name="ada",
    )(c_rep, w, bias.reshape(1, n_out))


def _rms_mod(x, w, shift, scale):
    var = jnp.mean(x * x, axis=-1, keepdims=True)
    y = x * lax.rsqrt(var + NORM_EPS) * w
    return y * (1.0 + scale) + shift


def _norm_mod_kernel(x_ref, w_ref, sh_ref, sc_ref, o_ref):
    o_ref[...] = _rms_mod(x_ref[...], w_ref[...], sh_ref[...], sc_ref[...]).astype(o_ref.dtype)


def _norm_mod(x, w, shift, scale, out_dtype, tm=256):
    b, s, d = x.shape
    vec = pl.BlockSpec((None, 1, d), lambda bi, i: (bi, 0, 0))
    return pl.pallas_call(
        _norm_mod_kernel,
        grid=(b, s // tm),
        in_specs=[pl.BlockSpec((None, tm, d), lambda bi, i: (bi, i, 0)),
                  pl.BlockSpec((1, d), lambda bi, i: (0, 0)), vec, vec],
        out_specs=pl.BlockSpec((None, tm, d), lambda bi, i: (bi, i, 0)),
        out_shape=jax.ShapeDtypeStruct((b, s, d), out_dtype),
        compiler_params=_cparams("arbitrary", "arbitrary"),
        name="norm_mod",
    )(x, w.reshape(1, d), shift.reshape(b, 1, d), scale.reshape(b, 1, d))


def _matmul_kernel(a_ref, wt_hbm, *rest, has_bias, row0, n_col_tiles):
    b_ref, o_ref, stage, w16, sem = rest if has_bias else (None,) + rest
    j = pl.program_id(0)
    tn = stage.shape[0]
    rows = tn // WEIGHT_DMA_PARTS

    def fetch(col_tile):
        first = row0 + col_tile * tn
        return [pltpu.make_async_copy(wt_hbm.at[pl.ds(pl.multiple_of(first + p * rows, rows), rows), :],
                                      stage.at[pl.ds(p * rows, rows), :], sem)
                for p in range(WEIGHT_DMA_PARTS)]

    @pl.when(pl.program_id(1) == 0)
    def _():
        @pl.when(j == 0)
        def _():
            for part, copy in enumerate(fetch(0)):
                copy.start(priority=part % 2)

        for copy in fetch(j):
            copy.wait()
        chunk = min(tn, 2 * LANES)
        for c in range(0, tn, chunk):
            w16[:, c:c + chunk] = stage[c:c + chunk, :].T.astype(BF16)

        @pl.when(j + 1 < n_col_tiles)
        def _():
            for part, copy in enumerate(fetch(j + 1)):
                copy.start(priority=part % 2)

    acc = jnp.dot(a_ref[...], w16[...], preferred_element_type=F32)
    if has_bias:
        acc = acc + b_ref[...]
    o_ref[...] = acc.astype(o_ref.dtype)


def _matmul(a, w_t, col0, ncols, out_dtype, tm, tn, bias=None):
    m, k = a.shape
    f32_tile = tm * tn * 4
    vmem = min(tn * k * 4 + k * tn * 2 + 2 * tm * k * 2 + 2 * f32_tile + 3 * f32_tile, VMEM_CAPACITY - VMEM_RESERVE)
    in_specs = [pl.BlockSpec((tm, k), lambda j, i: (i, 0)),
                pl.BlockSpec(memory_space=pl.ANY)]
    args = [a, w_t]
    if bias is not None:
        in_specs.append(pl.BlockSpec((1, tn), lambda j, i: (0, j)))
        args.append(bias)
    return pl.pallas_call(
        functools.partial(_matmul_kernel, has_bias=bias is not None, row0=col0, n_col_tiles=ncols // tn),
        grid=(ncols // tn, m // tm),
        in_specs=in_specs,
        out_specs=pl.BlockSpec((tm, tn), lambda j, i: (i, j)),
        out_shape=jax.ShapeDtypeStruct((m, ncols), out_dtype),
        scratch_shapes=[pltpu.VMEM((tn, k), F32), pltpu.VMEM((k, tn), BF16), pltpu.SemaphoreType.DMA(())],
        compiler_params=_cparams("arbitrary", "arbitrary", vmem=vmem),
        name="proj",
    )(*args)


def _sb_attn_kernel(q_ref, k_ref, v_ref, tri_ref, o_ref, acc_ref, r_ref, *, sub, n_sub, scale):
    first = pl.program_id(2) * n_sub
    tri = tri_ref[...]
    row = lax.broadcasted_iota(jnp.int32, (sub, sub), 0)
    col = lax.broadcasted_iota(jnp.int32, (sub, sub), 1)
    causal = col < row

    def visit(q, blk, r, mask):
        start = pl.multiple_of(jnp.maximum(blk, 0) * sub, sub)
        kb = k_ref[pl.ds(start, sub), :]
        vb = v_ref[pl.ds(start, sub), :]
        z = lax.dot_general(q, kb, (((1,), (1,)), ((), ())), preferred_element_type=F32) * scale
        log_keep = -(jnp.maximum(z, 0.0) + jnp.log(1.0 + jnp.exp(-jnp.abs(z))))
        log_beta = log_keep + z
        if mask is not None:
            log_keep = jnp.where(mask, log_keep, 0.0)
        hi, lo = _split_bf16(log_keep)
        between = (jnp.dot(hi, tri, preferred_element_type=F32)
                   + jnp.dot(lo, tri, preferred_element_type=F32))
        w = jnp.exp(log_beta + between + r)
        if mask is not None:
            w = jnp.where(mask, w, 0.0)
        pv = jnp.dot(w.astype(BF16), vb, preferred_element_type=F32)
        return pv, r + jnp.sum(log_keep, axis=1, keepdims=True)

    n_win = ATTN_ALWAYS
    roww = lax.broadcasted_iota(jnp.int32, (sub, n_win * sub), 0)
    colw = lax.broadcasted_iota(jnp.int32, (sub, n_win * sub), 1)

    def win_start(s):
        blk = first + s
        clamp = s < n_win - 1
        first_blk = jnp.maximum(blk - (n_win - 1), 0) if clamp else blk - (n_win - 1)
        return pl.multiple_of(first_blk * sub, sub), clamp

    def masked(s, x):
        start, clamp = win_start(s)
        if clamp:
            return jnp.where(start + colw < (first + s) * sub + roww, x, 0.0)
        last = (n_win - 1) * sub
        return jnp.concatenate([x[:, :last], jnp.where(causal, x[:, last:], 0.0)], axis=1)

    scores = []
    for s in range(n_sub):
        kw = k_ref[pl.ds(win_start(s)[0], n_win * sub), :]
        q = q_ref[s * sub:(s + 1) * sub, :]
        scores.append(lax.dot_general(q, kw, (((1,), (1,)), ((), ())), preferred_element_type=F32) * scale)

    staged = []
    for s in range(n_sub):
        z = scores[s]
        log_keep = -(jnp.maximum(z, 0.0) + jnp.log(1.0 + jnp.exp(-jnp.abs(z))))
        log_beta = log_keep + z
        log_keep = masked(s, log_keep)
        chunks = [log_keep[:, c * sub:(c + 1) * sub] for c in range(n_win)]
        hi, lo = _split_bf16(jnp.concatenate(chunks, axis=0))
        inside = (jnp.dot(hi, tri, preferred_element_type=F32)
                  + jnp.dot(lo, tri, preferred_element_type=F32))
        staged.append((log_beta, chunks, inside))

    r_top = None
    for s in range(n_sub):
        log_beta, chunks, inside = staged[s]
        later = 0.0
        between = [None] * n_win
        for c in reversed(range(n_win)):
            between[c] = inside[c * sub:(c + 1) * sub, :] + later
            later = later + jnp.sum(chunks[c], axis=1, keepdims=True)
        w = masked(s, jnp.exp(log_beta + jnp.concatenate(between, axis=1)))
        vw = v_ref[pl.ds(win_start(s)[0], n_win * sub), :]
        acc_ref[s] = jnp.dot(w.astype(BF16), vw, preferred_element_type=F32)
        r = jnp.broadcast_to(later, (sub, LANES))
        r_ref[s] = r
        r_top = r if r_top is None else jnp.maximum(r_top, r)

    @pl.when(jnp.max(r_top) > UNDERFLOW_LOG)
    def _():
        for s in range(n_sub):
            q = q_ref[s * sub:(s + 1) * sub, :]

            def cond(carry):
                j, r_max = carry
                return jnp.logical_and(j >= 0, r_max > UNDERFLOW_LOG)

            def body(carry, s=s, q=q):
                j, _ = carry
                pv, r = visit(q, j, r_ref[s], None)
                acc_ref[s] += pv
                r_ref[s] = r
                return j - 1, jnp.max(r)

            lax.while_loop(cond, body, (first + s - ATTN_ALWAYS, jnp.max(r_ref[s])))

    for s in range(n_sub):
        o_ref[s * sub:(s + 1) * sub, :] = acc_ref[s].astype(o_ref.dtype)


def _sb_attention(qkv, sub=ATTN_SUB, n_sub=ATTN_SUBS_PER_STEP):
    b, s, _ = qkv.shape
    h, d = SB_HEADS, SB_HEAD_DIM
    assert sub == LANES, "the carried row sums are kept lane-replicated at the key sub-block width"
    assert s >= ATTN_ALWAYS * sub
    n_sub = min(n_sub, s // sub)
    tq = sub * n_sub
    idx = lax.broadcasted_iota(jnp.int32, (sub, sub), 0) > lax.broadcasted_iota(jnp.int32, (sub, sub), 1)
    tri = idx.astype(BF16)
    kern = functools.partial(_sb_attn_kernel, sub=sub, n_sub=n_sub, scale=1.0 / math.sqrt(d))
    return pl.pallas_call(
        kern,
        grid=(b, h, s // tq),
        in_specs=[pl.BlockSpec((None, tq, d), lambda bi, hi, qi: (bi, qi, hi)),
                  pl.BlockSpec((None, s, d), lambda bi, hi, qi: (bi, 0, h + hi)),
                  pl.BlockSpec((None, s, d), lambda bi, hi, qi: (bi, 0, 2 * h + hi)),
                  pl.BlockSpec((sub, sub), lambda bi, hi, qi: (0, 0))],
        out_specs=pl.BlockSpec((None, tq, d), lambda bi, hi, qi: (bi, qi, hi)),
        out_shape=jax.ShapeDtypeStruct((b, s, h * d), BF16),
        scratch_shapes=[pltpu.VMEM((n_sub, sub, d), F32), pltpu.VMEM((n_sub, sub, LANES), F32)],
        compiler_params=_cparams("arbitrary", "arbitrary", "arbitrary"),
        name="sb_attn",
    )(qkv, qkv, qkv, tri)


def _round_robin(stagewise):
    out = [None] * len(stagewise)
    live = list(range(len(stagewise)))
    while live:
        for i in list(live):
            try:
                next(stagewise[i])
            except StopIteration as stop:
                out[i] = stop.value
                live.remove(i)
    return out


def _mlstm_head(head, q, k, vb, og, gcol, grow, nw, tril_ref, triu_ref, c_st, n_st, m_st, *, chunk):
    li_col = gcol[:, head:head + 1]
    lf_col = _log_sigmoid(gcol[:, ML_HEADS + head:ML_HEADS + head + 1])
    li_row = grow[head:head + 1, :]
    lf_row = _log_sigmoid(grow[ML_HEADS + head:ML_HEADS + head + 1, :])

    hi, lo = _split_bf16(jnp.broadcast_to(lf_col, (chunk, LANES)))
    b_col = (jnp.dot(tril_ref[...], hi, preferred_element_type=F32)
             + jnp.dot(tril_ref[...], lo, preferred_element_type=F32))[:, 0:1]
    hi, lo = _split_bf16(jnp.broadcast_to(lf_row, (2 * SUBLANES, chunk)))
    b_row = (jnp.dot(hi, triu_ref[...], preferred_element_type=F32)
             + jnp.dot(lo, triu_ref[...], preferred_element_type=F32))[0:1, :]
    qb = q.astype(BF16)
    kb = k.astype(BF16)
    qk = lax.dot_general(qb, kb, (((1,), (1,)), ((), ())), preferred_element_type=F32)
    q_mem = jnp.dot(qb, c_st[...].astype(BF16), preferred_element_type=F32)
    yield

    m_prev = m_st[...]
    a_col = b_col + m_prev
    row_t = lax.broadcasted_iota(jnp.int32, (chunk, chunk), 0)
    col_s = lax.broadcasted_iota(jnp.int32, (chunk, chunk), 1)
    d_mat = jnp.where(col_s <= row_t, b_col - b_row + li_row, -jnp.inf)
    m_col = jnp.maximum(a_col, jnp.max(d_mat, axis=1, keepdims=True))
    w_intra = jnp.exp(d_mat - m_col)
    w_inter = jnp.exp(a_col - m_col)
    sc = qk * w_intra
    sc_v = jnp.dot(sc.astype(BF16), vb, preferred_element_type=F32)

    b_last = b_col[chunk - 1:chunk, :]
    g_col = b_last - b_col + li_col
    m_new = jnp.maximum(b_last + m_prev, jnp.max(g_col, axis=0, keepdims=True))
    decay = jnp.exp(b_last + m_prev - m_new)
    kw = k * jnp.exp(g_col - m_new)
    kw_v = lax.dot_general(kw.astype(BF16), vb, (((0,), (0,)), ((), ())), preferred_element_type=F32)
    yield

    num = w_inter * q_mem + sc_v
    den = (w_inter * jnp.sum(q * n_st[...], axis=1, keepdims=True)
           + jnp.sum(sc, axis=1, keepdims=True))
    h_out = num / jnp.maximum(jnp.abs(den), jnp.exp(-m_col))
    c_st[...] = decay * c_st[...] + kw_v
    n_st[...] = decay * n_st[...] + jnp.sum(kw, axis=0, keepdims=True)
    m_st[...] = m_new

    hn = h_out * lax.rsqrt(jnp.mean(h_out * h_out, axis=1, keepdims=True) + NORM_EPS) * nw
    return jax.nn.sigmoid(og) * hn


def _mlstm_kernel(qp_ref, kp_ref, v_ref, og_ref, gcol_ref, grow_ref, cwq_ref, cwk_ref, nw_ref,
                  tril_ref, triu_ref, y_ref, c_st, n_st, m_st, qbuf, kbuf, *, chunk, heads, dim):
    halo = SUBLANES

    @pl.when(pl.program_id(1) == 0)
    def _():
        c_st[...] = jnp.zeros_like(c_st)
        n_st[...] = jnp.zeros_like(n_st)
        m_st[...] = jnp.zeros_like(m_st)
        qbuf[0:halo, :] = jnp.zeros((halo, heads * dim), F32)
        kbuf[0:halo, :] = jnp.zeros((halo, heads * dim), F32)

    qbuf[halo:halo + chunk, :] = qp_ref[...]
    kbuf[halo:halo + chunk, :] = kp_ref[...]

    def conv_silu(buf, cw_ref):
        acc = buf[halo:halo + chunk, :] * cw_ref[CONV_WIDTH - 1:CONV_WIDTH, :]
        for j in range(CONV_WIDTH - 1):
            off = halo - (CONV_WIDTH - 1) + j
            acc = acc + buf[off:off + chunk, :] * cw_ref[j:j + 1, :]
        return acc * jax.nn.sigmoid(acc)

    q = conv_silu(qbuf, cwq_ref)
    k = conv_silu(kbuf, cwk_ref) * (1.0 / math.sqrt(dim))
    qbuf[0:halo, :] = qbuf[chunk:chunk + halo, :]
    kbuf[0:halo, :] = kbuf[chunk:chunk + halo, :]

    gcol = gcol_ref[...]
    grow = grow_ref[...]
    cols = [slice(head * dim, (head + 1) * dim) for head in range(heads)]
    ys = _round_robin([
        _mlstm_head(head, q[:, c], k[:, c], v_ref[:, c], og_ref[:, c], gcol, grow, nw_ref[:, c],
                    tril_ref, triu_ref, c_st.at[head], n_st.at[head], m_st.at[head], chunk=chunk)
        for head, c in enumerate(cols)])
    for y, c in zip(ys, cols):
        y_ref[:, c] = y.astype(y_ref.dtype)


def _mlstm(ml_qk, ml_v, ml_o, gates, conv_w, ml_norm_w, chunk=ML_CHUNK):
    b, s, _ = ml_v.shape
    h, d = ML_HEADS, ML_HEAD_DIM
    w = h * d
    gates_t = jnp.swapaxes(gates, 1, 2)
    r = lax.broadcasted_iota(jnp.int32, (chunk, chunk), 0)
    c = lax.broadcasted_iota(jnp.int32, (chunk, chunk), 1)
    tril = (c <= r).astype(BF16)
    triu = (r <= c).astype(BF16)
    wide = lambda off: pl.BlockSpec((None, chunk, w), lambda bi, ci: (bi, ci, off))
    const = lambda shape, off=0: pl.BlockSpec(shape, lambda bi, ci: (0, off))
    kern = functools.partial(_mlstm_kernel, chunk=chunk, heads=h, dim=d)
    return pl.pallas_call(
        kern,
        grid=(b, s // chunk),
        in_specs=[wide(0), wide(1), wide(0), wide(0),
                  pl.BlockSpec((None, chunk, 2 * h), lambda bi, ci: (bi, ci, 0)),
                  pl.BlockSpec((None, 2 * h, chunk), lambda bi, ci: (bi, 0, ci)),
                  const((CONV_WIDTH, w)), const((CONV_WIDTH, w), 1), const((1, w)),
                  const((chunk, chunk)), const((chunk, chunk))],
        out_specs=wide(0),
        out_shape=jax.ShapeDtypeStruct((b, s, w), BF16),
        scratch_shapes=[pltpu.VMEM((h, d, d), F32), pltpu.VMEM((h, 1, d), F32), pltpu.VMEM((h, 1, 1), F32),
                        pltpu.VMEM((chunk + SUBLANES, w), F32), pltpu.VMEM((chunk + SUBLANES, w), F32)],
        compiler_params=_cparams("arbitrary", "arbitrary"),
        name="mlstm",
    )(ml_qk, ml_qk, ml_v, ml_o, gates, gates_t, conv_w, conv_w, ml_norm_w.reshape(1, w), tril, triu)


def _out_proj_kernel(ya_ref, yb_ref, wa_ref, wb_ref, x_ref, g_ref, o_ref):
    mix = (jnp.dot(ya_ref[...], wa_ref[...], preferred_element_type=F32)
           + jnp.dot(yb_ref[...], wb_ref[...], preferred_element_type=F32))
    o_ref[...] = x_ref[...] + g_ref[...] * mix


def _out_proj(y_sb, y_ml, w_out, x, gate, tm, tn=512):
    b, s, d = x.shape
    ka, kb = y_sb.shape[-1], y_ml.shape[-1]
    assert ka == kb
    n = b * s
    rows_per_seq = s // tm
    out = pl.pallas_call(
        _out_proj_kernel,
        grid=(n // tm, d // tn),
        in_specs=[pl.BlockSpec((tm, ka), lambda i, j: (i, 0)),
                  pl.BlockSpec((tm, kb), lambda i, j: (i, 0)),
                  pl.BlockSpec((ka, tn), lambda i, j: (0, j)),
                  pl.BlockSpec((kb, tn), lambda i, j: (1, j)),
                  pl.BlockSpec((tm, tn), lambda i, j: (i, j)),
                  pl.BlockSpec((None, 1, tn), lambda i, j: (i // rows_per_seq, 0, j))],
        out_specs=pl.BlockSpec((tm, tn), lambda i, j: (i, j)),
        out_shape=jax.ShapeDtypeStruct((n, d), F32),
        compiler_params=_cparams("arbitrary", "arbitrary"),
        name="out_proj",
    )(y_sb.reshape(n, ka), y_ml.reshape(n, kb), w_out, w_out, x.reshape(n, d), gate.reshape(b, 1, d))
    return out.reshape(b, s, d)


def _router_kernel(h_ref, nw_ref, sh_ref, sc_ref, wr_ref, br_ref, tri_ref, u_ref, ri_ref, rw_ref, cnt_ref, run_ref):
    first = jnp.logical_and(pl.program_id(0) == 0, pl.program_id(1) == 0)

    @pl.when(first)
    def _():
        run_ref[...] = jnp.zeros_like(run_ref)

    u = _rms_mod(h_ref[...], nw_ref[...], sh_ref[...], sc_ref[...])
    tm = u.shape[0]
    uh, ul = _split_bf16(u)
    u_ref[...] = _pack_bf16_pairs(uh)
    wh, wl = _split_bf16(wr_ref[...])
    logits = (jnp.dot(uh, wh, preferred_element_type=F32) + jnp.dot(ul, wh, preferred_element_type=F32)
              + jnp.dot(uh, wl, preferred_element_type=F32) + br_ref[...])

    lane = lax.broadcasted_iota(jnp.int32, (tm, LANES), 1)
    lane_f = lane.astype(F32)
    neg = -jnp.inf
    first_lane = lambda hit: jnp.min(jnp.where(hit, lane_f, float(LANES)), axis=1, keepdims=True)

    gl = jnp.where(lane < N_GROUPS, logits, neg)
    g_max = jnp.max(gl, axis=1, keepdims=True)
    g_p = 1.0 / jnp.sum(jnp.exp(gl - g_max), axis=1, keepdims=True)
    g_sel = first_lane(gl == g_max)

    lo_lane = N_GROUPS + EXPERTS_PER_GROUP * g_sel
    el = jnp.where(jnp.logical_and(lane_f >= lo_lane, lane_f < lo_lane + EXPERTS_PER_GROUP), logits, neg)
    e1_max = jnp.max(el, axis=1, keepdims=True)
    l1 = first_lane(el == e1_max)
    el2 = jnp.where(lane_f == l1, neg, el)
    e2_max = jnp.max(el2, axis=1, keepdims=True)
    l2 = first_lane(el2 == e2_max)
    z_sum = jnp.sum(jnp.exp(el - e1_max), axis=1, keepdims=True)
    p1 = 1.0 / z_sum
    p2 = jnp.exp(e2_max - e1_max) / z_sum
    w1 = p1 / (p1 + p2) * g_p
    w2 = p2 / (p1 + p2) * g_p
    e1 = l1 - N_GROUPS
    e2 = l2 - N_GROUPS

    hit1 = lane_f == e1
    hit2 = lane_f == e2
    onehot = jnp.where(jnp.logical_or(hit1, hit2), 1.0, 0.0)
    before = jnp.dot(tri_ref[...], onehot.astype(BF16), preferred_element_type=F32) + run_ref[...]
    rank1 = jnp.sum(jnp.where(hit1, before, 0.0), axis=1, keepdims=True)
    rank2 = jnp.sum(jnp.where(hit2, before, 0.0), axis=1, keepdims=True)
    run = run_ref[...] + jnp.sum(onehot, axis=0, keepdims=True)
    run_ref[...] = run

    ri_ref[...] = jnp.where(lane == 0, e1, jnp.where(lane == 1, e2, jnp.where(lane == 2, rank1,
                            jnp.where(lane == 3, rank2, 0.0)))).astype(jnp.int32)
    rw_ref[...] = jnp.where(lane == 0, w1, jnp.where(lane == 1, w2, 0.0))
    cnt_ref[...] = jnp.broadcast_to(run, cnt_ref.shape).astype(jnp.int32)


def _router(h, norm_w, shift, scale, w_router, b_router, tm=256):
    b, s, d = h.shape
    n = b * s
    tri = (lax.broadcasted_iota(jnp.int32, (tm, tm), 1) < lax.broadcasted_iota(jnp.int32, (tm, tm), 0)).astype(BF16)
    vec = pl.BlockSpec((None, 1, d), lambda bi, i: (bi, 0, 0))
    rows = lambda width: pl.BlockSpec((tm, width), lambda bi, i: (bi * (s // tm) + i, 0))
    return pl.pallas_call(
        _router_kernel,
        grid=(b, s // tm),
        in_specs=[pl.BlockSpec((None, tm, d), lambda bi, i: (bi, i, 0)),
                  pl.BlockSpec((1, d), lambda bi, i: (0, 0)), vec, vec,
                  pl.BlockSpec((d, LANES), lambda bi, i: (0, 0)),
                  pl.BlockSpec((1, LANES), lambda bi, i: (0, 0)),
                  pl.BlockSpec((tm, tm), lambda bi, i: (0, 0))],
        out_specs=[rows(d // 2), rows(LANES), rows(LANES),
                   pl.BlockSpec((SUBLANES, LANES), lambda bi, i: (0, 0))],
        out_shape=[jax.ShapeDtypeStruct((n, d // 2), jnp.int32), jax.ShapeDtypeStruct((n, LANES), jnp.int32),
                   jax.ShapeDtypeStruct((n, LANES), F32), jax.ShapeDtypeStruct((SUBLANES, LANES), jnp.int32)],
        scratch_shapes=[pltpu.VMEM((1, LANES), F32)],
        compiler_params=_cparams("arbitrary", "arbitrary"),
        name="router",
    )(h, norm_w.reshape(1, d), shift.reshape(b, 1, d), scale.reshape(b, 1, d), w_router, b_router, tri)


ROW_LOOP_UNROLL = 8
ZERO_ROWS = 128


def _dispatch_kernel(tail_ref, dest_ref, u_ref, xb_hbm, zbuf, zsem, sem, *, tokens, block):
    n_exp = tail_ref.shape[0]

    def clear(e, half):
        row = pl.multiple_of(tail_ref[e] + half * ZERO_ROWS, ZERO_ROWS)
        return pltpu.make_async_copy(zbuf, xb_hbm.at[pl.ds(row, ZERO_ROWS)], zsem)

    @pl.when(pl.program_id(0) == 0)
    def _():
        zbuf[...] = jnp.zeros_like(zbuf)
        for start in (True, False):
            def each(e, carry):
                @pl.when(tail_ref[e] >= 0)
                def _():
                    for half in range(block // ZERO_ROWS):
                        clear(e, half).start() if start else clear(e, half).wait()
                return carry
            lax.fori_loop(0, n_exp, each, 0)

    def row_copy(t, k):
        return pltpu.make_async_copy(u_ref.at[pl.ds(t, 1)], xb_hbm.at[pl.ds(dest_ref[0, TOP_K * t + k], 1)], sem)

    def issue(t, carry):
        for k in range(TOP_K):
            row_copy(t, k).start(priority=k % 2)
        return carry

    lax.fori_loop(0, tokens, issue, 0, unroll=ROW_LOOP_UNROLL)

    def drain(t, carry):
        for k in range(TOP_K):
            row_copy(t, k).wait()
        return carry

    lax.fori_loop(0, tokens, drain, 0, unroll=ROW_LOOP_UNROLL)


def _dispatch(u, dest, tail_block_row, cap, tokens=256, block=MOE_BLOCK):
    n, d = u.shape
    steps = n // tokens
    kern = functools.partial(_dispatch_kernel, tokens=tokens, block=block)
    return pl.pallas_call(
        kern,
        grid_spec=pltpu.PrefetchScalarGridSpec(
            num_scalar_prefetch=1,
            grid=(steps,),
            in_specs=[pl.BlockSpec((None, 1, TOP_K * tokens), lambda i, tail: (i, 0, 0), memory_space=pltpu.SMEM),
                      pl.BlockSpec((tokens, d), lambda i, tail: (i, 0))],
            out_specs=pl.BlockSpec(memory_space=pl.ANY),
            scratch_shapes=[pltpu.VMEM((ZERO_ROWS, d), u.dtype), pltpu.SemaphoreType.DMA(()),
                            pltpu.SemaphoreType.DMA(())]),
        out_shape=jax.ShapeDtypeStruct((cap, d), u.dtype),
        compiler_params=_cparams("arbitrary"),
        name="dispatch",
    )(tail_block_row, dest.reshape(steps, 1, TOP_K * tokens), u)


def _lookup(table, idx):
    hit = idx[:, None] == jnp.arange(table.shape[0], dtype=idx.dtype)[None, :]
    return jnp.sum(jnp.where(hit, table[None, :], 0), axis=1).astype(table.dtype)


def _run_plan(keys, ok):
    steps = ok.shape[0]
    idx = jnp.arange(steps, dtype=jnp.int32)
    changed = idx == 0
    for key in keys:
        changed = jnp.logical_or(changed, key != jnp.roll(key, 1))
    first = jnp.logical_and(changed, ok)
    slot = (jnp.cumsum(first.astype(jnp.int32)) - 1) % 2
    starts_from = lax.cummin(jnp.where(first, idx, steps), axis=0, reverse=True)
    nxt = jnp.concatenate([starts_from[1:], jnp.full((1,), steps, jnp.int32)])
    more = nxt < steps
    as_i32 = lambda a: a.astype(jnp.int32)
    return as_i32(first), as_i32(slot), as_i32(more), as_i32(jnp.minimum(nxt, steps - 1))


def _run_weights(step, first_ref, slot_ref, more_ref, fetch_cur, fetch_next):
    slot = slot_ref[step]

    @pl.when(step == 0)
    def _():
        for part, copy in enumerate(fetch_cur(slot)):
            copy.start(priority=part % 2)

    @pl.when(first_ref[step] == 1)
    def _():
        @pl.when(more_ref[step] == 1)
        def _():
            for part, copy in enumerate(fetch_next(1 - slot)):
                copy.start(priority=part % 2)

        for copy in fetch_cur(slot):
            copy.wait()

    return slot


def _e1_kernel(blk_ref, ex_ref, ft_ref, oblk_ref, oft_ref, ok_ref, first_ref, slot_ref, more_ref, nex_ref, nft_ref,
               x_ref, wg_hbm, wu_hbm, o_ref, wg_buf, wu_buf, sems, *, tf):
    del blk_ref, oblk_ref, oft_ref
    step = pl.program_id(0)

    def fetch(e, f, slot):
        cols = pl.ds(pl.multiple_of(f * tf, tf), tf)
        rows = wg_buf.shape[1] // WEIGHT_DMA_PARTS
        return [pltpu.make_async_copy(hbm.at[e, pl.ds(p * rows, rows), cols],
                                      buf.at[slot, pl.ds(p * rows, rows)], sems.at[slot, which])
                for p in range(WEIGHT_DMA_PARTS)
                for which, (hbm, buf) in enumerate(((wg_hbm, wg_buf), (wu_hbm, wu_buf)))]

    slot = _run_weights(step, first_ref, slot_ref, more_ref,
                        lambda sl: fetch(ex_ref[step], ft_ref[step], sl),
                        lambda sl: fetch(nex_ref[step], nft_ref[step], sl))
    ok = ok_ref[step] == 1

    @pl.when(ok)
    def _():
        x_lo, x_hi = _unpack_bf16_pairs(x_ref[...])
        half = x_lo.shape[1]
        mm = lambda w_buf: (jnp.dot(x_lo, w_buf[slot, :half, :], preferred_element_type=F32)
                            + jnp.dot(x_hi, w_buf[slot, half:, :], preferred_element_type=F32))
        g = mm(wg_buf)
        up = mm(wu_buf)
        o_ref[...] = ((g * jax.nn.sigmoid(g)) * up).astype(o_ref.dtype)

    @pl.when(jnp.logical_not(ok))
    def _():
        o_ref[...] = jnp.zeros_like(o_ref)


def _e2_kernel(blk_ref, ex_ref, oblk_ref, ok_ref, first_ref, slot_ref, more_ref, nex_ref,
               h_ref, wd_hbm, o_ref, wd_buf, sems):
    del blk_ref, oblk_ref
    step = pl.program_id(0)
    rows = wd_buf.shape[1] // WEIGHT_DMA_PARTS
    fetch = lambda e, slot: [pltpu.make_async_copy(wd_hbm.at[e, pl.ds(p * rows, rows)],
                                                   wd_buf.at[slot, pl.ds(p * rows, rows)], sems.at[slot])
                             for p in range(WEIGHT_DMA_PARTS)]
    slot = _run_weights(step, first_ref, slot_ref, more_ref,
                        lambda sl: fetch(ex_ref[step], sl), lambda sl: fetch(nex_ref[step], sl))
    ok = ok_ref[step] == 1

    @pl.when(ok)
    def _():
        o_ref[...] = jnp.dot(h_ref[...].astype(F32), wd_buf[slot], preferred_element_type=F32)

    @pl.when(jnp.logical_not(ok))
    def _():
        o_ref[...] = jnp.zeros_like(o_ref)


def _experts(xb, blocks_per_expert, w_gate, w_up, w_down, bm=MOE_BLOCK, tf=512):
    cap, dp = xb.shape
    n_exp, d, dff = w_gate.shape
    n_blocks = cap // bm
    n_ft = dff // tf
    nb = blocks_per_expert.astype(jnp.int32)
    bstart = jnp.cumsum(nb) - nb
    used = jnp.sum(nb)

    steps1 = n_blocks * n_ft
    step = jnp.arange(steps1, dtype=jnp.int32)
    ok1 = step < used * n_ft
    send = jnp.cumsum(nb * n_ft)
    s_idx = jnp.minimum(step, used * n_ft - 1)
    e_of = jnp.minimum(jnp.sum(s_idx[:, None] >= send[None, :], axis=1), n_exp - 1).astype(jnp.int32)
    nb_of = _lookup(nb, e_of)
    local = s_idx - (_lookup(send, e_of) - nb_of * n_ft)
    nbe = jnp.maximum(nb_of, 1)
    ft_of = (local // nbe).astype(jnp.int32)
    blk_of = (_lookup(bstart, e_of) + local % nbe).astype(jnp.int32)
    spare = step - used * n_ft
    oblk_of = jnp.where(ok1, blk_of, used + spare // n_ft).astype(jnp.int32)
    oft_of = jnp.where(ok1, ft_of, spare % n_ft).astype(jnp.int32)

    first1, slot1, more1, nxt1 = _run_plan([e_of, ft_of], ok1)
    hmid = pl.pallas_call(
        functools.partial(_e1_kernel, tf=tf),
        grid_spec=pltpu.PrefetchScalarGridSpec(
            num_scalar_prefetch=11,
            grid=(steps1,),
            in_specs=[pl.BlockSpec((bm, dp), lambda s, blk, *_: (blk[s], 0)),
                      pl.BlockSpec(memory_space=pl.ANY),
                      pl.BlockSpec(memory_space=pl.ANY)],
            out_specs=pl.BlockSpec((bm, tf), lambda s, blk, ex, ft, oblk, oft, *_: (oblk[s], oft[s])),
            scratch_shapes=[pltpu.VMEM((2, d, tf), F32), pltpu.VMEM((2, d, tf), F32),
                            pltpu.SemaphoreType.DMA((2, 2))]),
        out_shape=jax.ShapeDtypeStruct((cap, dff), BF16),
        compiler_params=_cparams("arbitrary"),
        name="e1",
    )(blk_of, e_of, ft_of, oblk_of, oft_of, ok1.astype(jnp.int32), first1, slot1, more1, _lookup(e_of, nxt1), _lookup(ft_of, nxt1),
      xb, w_gate, w_up)

    blk = jnp.arange(n_blocks, dtype=jnp.int32)
    b_idx = jnp.minimum(blk, used - 1)
    bend = jnp.cumsum(nb)
    e2_of = jnp.minimum(jnp.sum(b_idx[:, None] >= bend[None, :], axis=1), n_exp - 1).astype(jnp.int32)
    ok2 = blk < used
    first2, slot2, more2, nxt2 = _run_plan([e2_of], ok2)
    yb = pl.pallas_call(
        _e2_kernel,
        grid_spec=pltpu.PrefetchScalarGridSpec(
            num_scalar_prefetch=8,
            grid=(n_blocks,),
            in_specs=[pl.BlockSpec((bm, dff), lambda s, hb, *_: (hb[s], 0)),
                      pl.BlockSpec(memory_space=pl.ANY)],
            out_specs=pl.BlockSpec((bm, d), lambda s, hb, ex, ob, *_: (ob[s], 0)),
            scratch_shapes=[pltpu.VMEM((2, dff, d), F32), pltpu.SemaphoreType.DMA((2,))]),
        out_shape=jax.ShapeDtypeStruct((cap, d), F32),
        compiler_params=_cparams("arbitrary"),
        name="e2",
    )(b_idx, e2_of, blk, ok2.astype(jnp.int32), first2, slot2, more2, _lookup(e2_of, nxt2), hmid, w_down)
    return yb


def _combine_kernel(dest_ref, dest_next_ref, yb_hbm, h_ref, rw_ref, g_ref, nw_ref, sh_ref, sc_ref, o_ref,
                    rows, sems, *, tokens, steps):
    step = pl.program_id(0)
    slot = step % 2

    def row_copy(d_ref, buf, t, k):
        return pltpu.make_async_copy(yb_hbm.at[pl.ds(d_ref[0, TOP_K * t + k], 1)],
                                     rows.at[buf, k, pl.ds(t, 1)], sems.at[buf])

    def gather(d_ref, buf):
        def issue(t, carry):
            for k in range(TOP_K):
                row_copy(d_ref, buf, t, k).start(priority=k % 2)
            return carry

        lax.fori_loop(0, tokens, issue, 0, unroll=ROW_LOOP_UNROLL)

    @pl.when(step == 0)
    def _():
        gather(dest_ref, 0)

    @pl.when(step + 1 < steps)
    def _():
        gather(dest_next_ref, 1 - slot)

    def drain(t, carry):
        for k in range(TOP_K):
            row_copy(dest_ref, slot, t, k).wait()
        return carry

    lax.fori_loop(0, tokens, drain, 0, unroll=ROW_LOOP_UNROLL)

    rw = rw_ref[...]
    ffn = rows[slot, 0] * rw[:, 0:1]
    for k in range(1, TOP_K):
        ffn = ffn + rows[slot, k] * rw[:, k:k + 1]
    h2 = h_ref[...] + g_ref[...] * ffn
    o_ref[...] = _rms_mod(h2, nw_ref[...], sh_ref[...], sc_ref[...])


def _combine(yb, dest, h, route_w, gate, norm_w, shift, scale, tokens=256):
    b, s, d = h.shape
    n = b * s
    per_seq = s // tokens
    steps = n // tokens
    kern = functools.partial(_combine_kernel, tokens=tokens, steps=steps)
    vec = pl.BlockSpec((None, 1, d), lambda i: (i // per_seq, 0, 0))
    dest_blocks = dest.reshape(steps, 1, TOP_K * tokens)
    dest_spec = lambda ahead: pl.BlockSpec((None, 1, TOP_K * tokens),
                                           lambda i: (jnp.minimum(i + ahead, steps - 1), 0, 0),
                                           memory_space=pltpu.SMEM)
    out = pl.pallas_call(
        kern,
        grid=(steps,),
        in_specs=[dest_spec(0), dest_spec(1),
                  pl.BlockSpec(memory_space=pl.ANY),
                  pl.BlockSpec((tokens, d), lambda i: (i, 0)),
                  pl.BlockSpec((tokens, LANES), lambda i: (i, 0)),
                  vec,
                  pl.BlockSpec((1, d), lambda i: (0, 0)), vec, vec],
        out_specs=pl.BlockSpec((tokens, d), lambda i: (i, 0)),
        out_shape=jax.ShapeDtypeStruct((n, d), F32),
        scratch_shapes=[pltpu.VMEM((2, TOP_K, tokens, d), F32), pltpu.SemaphoreType.DMA((2,))],
        compiler_params=_cparams("arbitrary"),
        name="combine",
    )(dest_blocks, dest_blocks, yb, h.reshape(n, d), route_w,
      gate.reshape(b, 1, d), norm_w.reshape(1, d), shift.reshape(b, 1, d), scale.reshape(b, 1, d))
    return out.reshape(b, s, d)


def _layer(h, mod, norm1_w, w_in, conv_w, ml_gate_bias, ml_norm_w, w_out, norm2_w,
           w_router_group, b_router_group, w_router_expert, b_router_expert,
           w_exp_gate, w_exp_up, w_exp_down, final):
    b, s, d = h.shape
    n = b * s
    sh1, sc1, g1, sh2, sc2, g2 = jnp.split(mod, 6, axis=-1)
    sb_w = SB_HEADS * SB_HEAD_DIM
    ml_w = ML_HEADS * ML_HEAD_DIM

    u = _norm_mod(h, norm1_w, sh1, sc1, BF16).reshape(n, d)
    tm = 1024 if n % 1024 == 0 else 512
    w_in_t = w_in.T
    sb_qkv = _matmul(u, w_in_t, 0, 3 * sb_w, BF16, tm, PROJ_TN)
    ml_qk = _matmul(u, w_in_t, 3 * sb_w, 2 * ml_w, F32, tm, PROJ_TN)
    ml_v = _matmul(u, w_in_t, 3 * sb_w + 2 * ml_w, ml_w, BF16, tm, PROJ_TN)
    ml_o = _matmul(u, w_in_t, 3 * sb_w + 3 * ml_w, ml_w, F32, tm, PROJ_TN)
    n_gate = 2 * ML_HEADS
    w_gate = jnp.pad(w_in_t[3 * sb_w + 4 * ml_w:, :], ((0, LANES - n_gate), (0, 0)))
    b_gate = jnp.pad(ml_gate_bias, (0, LANES - n_gate)).reshape(1, LANES)
    gates = _matmul(u, w_gate, 0, LANES, F32, tm, LANES, bias=b_gate)[:, :n_gate]

    y_sb = _sb_attention(sb_qkv.reshape(b, s, 3 * sb_w))
    y_ml = _mlstm(ml_qk.reshape(b, s, 2 * ml_w), ml_v.reshape(b, s, ml_w), ml_o.reshape(b, s, ml_w),
                  gates.reshape(b, s, n_gate), conv_w, ml_norm_w)
    h = _out_proj(y_sb, y_ml, w_out.astype(BF16), h, g1, tm=min(1024, s))

    w_router = jnp.pad(jnp.concatenate([w_router_group, w_router_expert], axis=1),
                       ((0, 0), (0, LANES - N_GROUPS - N_EXPERTS)))
    b_router = jnp.pad(jnp.concatenate([b_router_group, b_router_expert]),
                       (0, LANES - N_GROUPS - N_EXPERTS)).reshape(1, LANES)
    u2, route_i, route_w, counts = _router(h, norm2_w, sh2, sc2, w_router, b_router)
    counts = counts[0, :N_EXPERTS]
    blocks_per_expert = (counts + MOE_BLOCK - 1) // MOE_BLOCK
    padded = blocks_per_expert * MOE_BLOCK
    pstarts = jnp.cumsum(padded) - padded
    dest = (_lookup(pstarts, route_i[:, 0:TOP_K].reshape(-1)) + route_i[:, TOP_K:2 * TOP_K].reshape(-1))
    cap = (-(-(n * TOP_K) // MOE_BLOCK) + N_EXPERTS) * MOE_BLOCK
    tail_block_row = jnp.where(counts % MOE_BLOCK != 0, pstarts + padded - MOE_BLOCK, -1)
    spare_row = jnp.sum(padded) + MOE_BLOCK * jnp.arange(N_EXPERTS, dtype=jnp.int32)
    spare_row = jnp.where(spare_row < cap, spare_row, -1)
    clear_rows = jnp.concatenate([tail_block_row, spare_row]).astype(jnp.int32)
    xb = _dispatch(u2, dest, clear_rows, cap)
    yb = _experts(xb, blocks_per_expert, w_exp_gate, w_exp_up, w_exp_down)
    norm_w, shift, scale = final
    return _combine(yb, dest, h, route_w, g2, norm_w, shift, scale)


def kernel(x, c, norm1_w, w_in, conv_w, ml_gate_bias, ml_norm_w, w_out, norm2_w, w_router_group, b_router_group, w_router_expert, b_router_expert, w_exp_gate, w_exp_up, w_exp_down, w_ada, b_ada, final_norm_w, w_ada_final, b_ada_final):
    b, s, d = x.shape
    depth = w_in.shape[0]
    assert depth == 1, "the final norm is fused into the (single) layer's combine step"
    c_rep = jnp.broadcast_to(c[:, :, None], (b, d, LANES))
    fmod = _ada(c_rep, w_ada_final, b_ada_final)
    sh_f, sc_f = jnp.split(fmod, 2, axis=-1)
    mod = _ada(c_rep, w_ada[0], b_ada[0])
    return _layer(x, mod, norm1_w[0], w_in[0], conv_w[0], ml_gate_bias[0], ml_norm_w[0], w_out[0], norm2_w[0],
                  w_router_group[0], b_router_group[0], w_router_expert[0], b_router_expert[0],
                  w_exp_gate[0], w_exp_up[0], w_exp_down[0], (final_norm_w, sh_f, sc_f))
```

```python
import functools
import math

import jax
import jax.numpy as jnp
from jax import lax
from jax.experimental import pallas as pl
from jax.experimental.pallas import tpu as pltpu

SB_HEADS = 16
SB_HEAD_DIM = 128
ML_HEADS = 4
ML_HEAD_DIM = 512
CONV_WIDTH = 4
N_GROUPS = 4
EXPERTS_PER_GROUP = 8
N_EXPERTS = N_GROUPS * EXPERTS_PER_GROUP
TOP_K = 2
NORM_EPS = 1e-6

LANES = 128
SUBLANES = 8
VMEM_CAPACITY = 64 * 1024 * 1024
VMEM_RESERVE = 4 * 1024 * 1024
VMEM_LIMIT = 56 * 1024 * 1024

ATTN_SUB = 128
ATTN_SUBS_PER_STEP = 16
ATTN_ALWAYS = 3
ML_CHUNK = 256
MOE_BLOCK = 256
PROJ_TN = 1024
WEIGHT_DMA_PARTS = 4
UNDERFLOW_LOG = -90.0

F32 = jnp.float32
BF16 = jnp.bfloat16


def _cparams(*sem, vmem=VMEM_LIMIT):
    return pltpu.CompilerParams(dimension_semantics=sem, vmem_limit_bytes=vmem)


def _split_bf16(x):
    hi = x.astype(BF16)
    lo = (x - hi.astype(F32)).astype(BF16)
    return hi, lo


_HIGH_HALF = -65536


def _pack_bf16_pairs(x):
    k = x.shape[1] // 2
    bits = lambda v: lax.bitcast_convert_type(v.astype(F32), jnp.int32)
    return lax.shift_right_logical(bits(x[:, :k]), 16) | (bits(x[:, k:]) & _HIGH_HALF)


def _unpack_bf16_pairs(w):
    lo = lax.bitcast_convert_type(lax.shift_left(w, 16), F32)
    hi = lax.bitcast_convert_type(w & _HIGH_HALF, F32)
    return lo, hi


def _log_sigmoid(x):
    return jnp.minimum(x, 0.0) - jnp.log1p(jnp.exp(-jnp.abs(x)))


def _ada_kernel(c_ref, w_ref, b_ref, o_ref):
    nb, tn = o_ref.shape
    rows = []
    for b in range(nb):
        cb = c_ref[b]
        cb = cb * jax.nn.sigmoid(cb)
        pieces = [jnp.sum(w_ref[:, j * LANES:(j + 1) * LANES] * cb, axis=0, keepdims=True)
                  for j in range(tn // LANES)]
        rows.append(jnp.concatenate(pieces, axis=1))
    o_ref[...] = jnp.concatenate(rows, axis=0) + b_ref[...]


def _ada(c_rep, w, bias, tn=1024):
    nb, k, _ = c_rep.shape
    n_out = w.shape[1]
    return pl.pallas_call(
        _ada_kernel,
        grid=(n_out // tn,),
        in_specs=[pl.BlockSpec((nb, k, LANES), lambda j: (0, 0, 0)),
                  pl.BlockSpec((k, tn), lambda j: (0, j)),
                  pl.BlockSpec((1, tn), lambda j: (0, j))],
        out_specs=pl.BlockSpec((nb, tn), lambda j: (0, j)),
        out_shape=jax.ShapeDtypeStruct((nb, n_out), F32),
        compiler_params=_cparams("arbitrary"),
        name="ada",
    )(c_rep, w, bias.reshape(1, n_out))


def _rms_mod(x, w, shift, scale):
    var = jnp.mean(x * x, axis=-1, keepdims=True)
    y = x * lax.rsqrt(var + NORM_EPS) * w
    return y * (1.0 + scale) + shift


def _norm_mod_kernel(x_ref, w_ref, sh_ref, sc_ref, o_ref):
    o_ref[...] = _rms_mod(x_ref[...], w_ref[...], sh_ref[...], sc_ref[...]).astype(o_ref.dtype)


def _norm_mod(x, w, shift, scale, out_dtype, tm=256):
    b, s, d = x.shape
    vec = pl.BlockSpec((None, 1, d), lambda bi, i: (bi, 0, 0))
    return pl.pallas_call(
        _norm_mod_kernel,
        grid=(b, s // tm),
        in_specs=[pl.BlockSpec((None, tm, d), lambda bi, i: (bi, i, 0)),
                  pl.BlockSpec((1, d), lambda bi, i: (0, 0)), vec, vec],
        out_specs=pl.BlockSpec((None, tm, d), lambda bi, i: (bi, i, 0)),
        out_shape=jax.ShapeDtypeStruct((b, s, d), out_dtype),
        compiler_params=_cparams("arbitrary", "arbitrary"),
        name="norm_mod",
    )(x, w.reshape(1, d), shift.reshape(b, 1, d), scale.reshape(b, 1, d))


def _matmul_kernel(a_ref, wt_hbm, *rest, has_bias, row0, n_col_tiles):
    b_ref, o_ref, stage, w16, sem = rest if has_bias else (None,) + rest
    j = pl.program_id(0)
    tn = stage.shape[0]
    rows = tn // WEIGHT_DMA_PARTS

    def fetch(col_tile):
        first = row0 + col_tile * tn
        return [pltpu.make_async_copy(wt_hbm.at[pl.ds(pl.multiple_of(first + p * rows, rows), rows), :],
                                      stage.at[pl.ds(p * rows, rows), :], sem)
                for p in range(WEIGHT_DMA_PARTS)]

    @pl.when(pl.program_id(1) == 0)
    def _():
        @pl.when(j == 0)
        def _():
            for part, copy in enumerate(fetch(0)):
                copy.start(priority=part % 2)

        for copy in fetch(j):
            copy.wait()
        chunk = min(tn, 2 * LANES)
        for c in range(0, tn, chunk):
            w16[:, c:c + chunk] = stage[c:c + chunk, :].T.astype(BF16)

        @pl.when(j + 1 < n_col_tiles)
        def _():
            for part, copy in enumerate(fetch(j + 1)):
                copy.start(priority=part % 2)

    acc = jnp.dot(a_ref[...], w16[...], preferred_element_type=F32)
    if has_bias:
        acc = acc + b_ref[...]
    o_ref[...] = acc.astype(o_ref.dtype)


def _matmul(a, w_t, col0, ncols, out_dtype, tm, tn, bias=None):
    m, k = a.shape
    f32_tile = tm * tn * 4
    vmem = min(tn * k * 4 + k * tn * 2 + 2 * tm * k * 2 + 2 * f32_tile + 3 * f32_tile, VMEM_CAPACITY - VMEM_RESERVE)
    in_specs = [pl.BlockSpec((tm, k), lambda j, i: (i, 0)),
                pl.BlockSpec(memory_space=pl.ANY)]
    args = [a, w_t]
    if bias is not None:
        in_specs.append(pl.BlockSpec((1, tn), lambda j, i: (0, j)))
        args.append(bias)
    return pl.pallas_call(
        functools.partial(_matmul_kernel, has_bias=bias is not None, row0=col0, n_col_tiles=ncols // tn),
        grid=(ncols // tn, m // tm),
        in_specs=in_specs,
        out_specs=pl.BlockSpec((tm, tn), lambda j, i: (i, j)),
        out_shape=jax.ShapeDtypeStruct((m, ncols), out_dtype),
        scratch_shapes=[pltpu.VMEM((tn, k), F32), pltpu.VMEM((k, tn), BF16), pltpu.SemaphoreType.DMA(())],
        compiler_params=_cparams("arbitrary", "arbitrary", vmem=vmem),
        name="proj",
    )(*args)


def _sb_attn_kernel(q_ref, k_ref, v_ref, tri_ref, o_ref, acc_ref, r_ref, *, sub, n_sub, scale):
    first = pl.program_id(2) * n_sub
    tri = tri_ref[...]
    row = lax.broadcasted_iota(jnp.int32, (sub, sub), 0)
    col = lax.broadcasted_iota(jnp.int32, (sub, sub), 1)
    causal = col < row

    def visit(q, blk, r, mask):
        start = pl.multiple_of(jnp.maximum(blk, 0) * sub, sub)
        kb = k_ref[pl.ds(start, sub), :]
        vb = v_ref[pl.ds(start, sub), :]
        z = lax.dot_general(q, kb, (((1,), (1,)), ((), ())), preferred_element_type=F32) * scale
        log_keep = -(jnp.maximum(z, 0.0) + jnp.log(1.0 + jnp.exp(-jnp.abs(z))))
        log_beta = log_keep + z
        if mask is not None:
            log_keep = jnp.where(mask, log_keep, 0.0)
        hi, lo = _split_bf16(log_keep)
        between = (jnp.dot(hi, tri, preferred_element_type=F32)
                   + jnp.dot(lo, tri, preferred_element_type=F32))
        w = jnp.exp(log_beta + between + r)
        if mask is not None:
            w = jnp.where(mask, w, 0.0)
        pv = jnp.dot(w.astype(BF16), vb, preferred_element_type=F32)
        return pv, r + jnp.sum(log_keep, axis=1, keepdims=True)

    n_win = ATTN_ALWAYS
    roww = lax.broadcasted_iota(jnp.int32, (sub, n_win * sub), 0)
    colw = lax.broadcasted_iota(jnp.int32, (sub, n_win * sub), 1)

    def win_start(s):
        blk = first + s
        clamp = s < n_win - 1
        first_blk = jnp.maximum(blk - (n_win - 1), 0) if clamp else blk - (n_win - 1)
        return pl.multiple_of(first_blk * sub, sub), clamp

    def masked(s, x):
        start, clamp = win_start(s)
        if clamp:
            return jnp.where(start + colw < (first + s) * sub + roww, x, 0.0)
        last = (n_win - 1) * sub
        return jnp.concatenate([x[:, :last], jnp.where(causal, x[:, last:], 0.0)], axis=1)

    scores = []
    for s in range(n_sub):
        kw = k_ref[pl.ds(win_start(s)[0], n_win * sub), :]
        q = q_ref[s * sub:(s + 1) * sub, :]
        scores.append(lax.dot_general(q, kw, (((1,), (1,)), ((), ())), preferred_element_type=F32) * scale)

    staged = []
    for s in range(n_sub):
        z = scores[s]
        log_keep = -(jnp.maximum(z, 0.0) + jnp.log(1.0 + jnp.exp(-jnp.abs(z))))
        log_beta = log_keep + z
        log_keep = masked(s, log_keep)
        chunks = [log_keep[:, c * sub:(c + 1) * sub] for c in range(n_win)]
        hi, lo = _split_bf16(jnp.concatenate(chunks, axis=0))
        inside = (jnp.dot(hi, tri, preferred_element_type=F32)
                  + jnp.dot(lo, tri, preferred_element_type=F32))
        staged.append((log_beta, chunks, inside))

    r_top = None
    for s in range(n_sub):
        log_beta, chunks, inside = staged[s]
        later = 0.0
        between = [None] * n_win
        for c in reversed(range(n_win)):
            between[c] = inside[c * sub:(c + 1) * sub, :] + later
            later = later + jnp.sum(chunks[c], axis=1, keepdims=True)
        w = masked(s, jnp.exp(log_beta + jnp.concatenate(between, axis=1)))
        vw = v_ref[pl.ds(win_start(s)[0], n_win * sub), :]
        acc_ref[s] = jnp.dot(w.astype(BF16), vw, preferred_element_type=F32)
        r = jnp.broadcast_to(later, (sub, LANES))
        r_ref[s] = r
        r_top = r if r_top is None else jnp.maximum(r_top, r)

    @pl.when(jnp.max(r_top) > UNDERFLOW_LOG)
    def _():
        for s in range(n_sub):
            q = q_ref[s * sub:(s + 1) * sub, :]

            def cond(carry):
                j, r_max = carry
                return jnp.logical_and(j >= 0, r_max > UNDERFLOW_LOG)

            def body(carry, s=s, q=q):
                j, _ = carry
                pv, r = visit(q, j, r_ref[s], None)
                acc_ref[s] += pv
                r_ref[s] = r
                return j - 1, jnp.max(r)

            lax.while_loop(cond, body, (first + s - ATTN_ALWAYS, jnp.max(r_ref[s])))

    for s in range(n_sub):
        o_ref[s * sub:(s + 1) * sub, :] = acc_ref[s].astype(o_ref.dtype)


def _sb_attention(qkv, sub=ATTN_SUB, n_sub=ATTN_SUBS_PER_STEP):
    b, s, _ = qkv.shape
    h, d = SB_HEADS, SB_HEAD_DIM
    assert sub == LANES, "the carried row sums are kept lane-replicated at the key sub-block width"
    assert s >= ATTN_ALWAYS * sub
    n_sub = min(n_sub, s // sub)
    tq = sub * n_sub
    idx = lax.broadcasted_iota(jnp.int32, (sub, sub), 0) > lax.broadcasted_iota(jnp.int32, (sub, sub), 1)
    tri = idx.astype(BF16)
    kern = functools.partial(_sb_attn_kernel, sub=sub, n_sub=n_sub, scale=1.0 / math.sqrt(d))
    return pl.pallas_call(
        kern,
        grid=(b, h, s // tq),
        in_specs=[pl.BlockSpec((None, tq, d), lambda bi, hi, qi: (bi, qi, hi)),
                  pl.BlockSpec((None, s, d), lambda bi, hi, qi: (bi, 0, h + hi)),
                  pl.BlockSpec((None, s, d), lambda bi, hi, qi: (bi, 0, 2 * h + hi)),
                  pl.BlockSpec((sub, sub), lambda bi, hi, qi: (0, 0))],
        out_specs=pl.BlockSpec((None, tq, d), lambda bi, hi, qi: (bi, qi, hi)),
        out_shape=jax.ShapeDtypeStruct((b, s, h * d), BF16),
        scratch_shapes=[pltpu.VMEM((n_sub, sub, d), F32), pltpu.VMEM((n_sub, sub, LANES), F32)],
        compiler_params=_cparams("arbitrary", "arbitrary", "arbitrary"),
        name="sb_attn",
    )(qkv, qkv, qkv, tri)


def _round_robin(stagewise):
    out = [None] * len(stagewise)
    live = list(range(len(stagewise)))
    while live:
        for i in list(live):
            try:
                next(stagewise[i])
            except StopIteration as stop:
                out[i] = stop.value
                live.remove(i)
    return out


def _mlstm_head(head, q, k, vb, og, gcol, grow, nw, tril_ref, triu_ref, c_st, n_st, m_st, *, chunk):
    li_col = gcol[:, head:head + 1]
    lf_col = _log_sigmoid(gcol[:, ML_HEADS + head:ML_HEADS + head + 1])
    li_row = grow[head:head + 1, :]
    lf_row = _log_sigmoid(grow[ML_HEADS + head:ML_HEADS + head + 1, :])

    hi, lo = _split_bf16(jnp.broadcast_to(lf_col, (chunk, LANES)))
    b_col = (jnp.dot(tril_ref[...], hi, preferred_element_type=F32)
             + jnp.dot(tril_ref[...], lo, preferred_element_type=F32))[:, 0:1]
    hi, lo = _split_bf16(jnp.broadcast_to(lf_row, (2 * SUBLANES, chunk)))
    b_row = (jnp.dot(hi, triu_ref[...], preferred_element_type=F32)
             + jnp.dot(lo, triu_ref[...], preferred_element_type=F32))[0:1, :]
    qb = q.astype(BF16)
    kb = k.astype(BF16)
    qk = lax.dot_general(qb, kb, (((1,), (1,)), ((), ())), preferred_element_type=F32)
    q_mem = jnp.dot(qb, c_st[...].astype(BF16), preferred_element_type=F32)
    yield

    m_prev = m_st[...]
    a_col = b_col + m_prev
    row_t = lax.broadcasted_iota(jnp.int32, (chunk, chunk), 0)
    col_s = lax.broadcasted_iota(jnp.int32, (chunk, chunk), 1)
    d_mat = jnp.where(col_s <= row_t, b_col - b_row + li_row, -jnp.inf)
    m_col = jnp.maximum(a_col, jnp.max(d_mat, axis=1, keepdims=True))
    w_intra = jnp.exp(d_mat - m_col)
    w_inter = jnp.exp(a_col - m_col)
    sc = qk * w_intra
    sc_v = jnp.dot(sc.astype(BF16), vb, preferred_element_type=F32)

    b_last = b_col[chunk - 1:chunk, :]
    g_col = b_last - b_col + li_col
    m_new = jnp.maximum(b_last + m_prev, jnp.max(g_col, axis=0, keepdims=True))
    decay = jnp.exp(b_last + m_prev - m_new)
    kw = k * jnp.exp(g_col - m_new)
    kw_v = lax.dot_general(kw.astype(BF16), vb, (((0,), (0,)), ((), ())), preferred_element_type=F32)
    yield

    num = w_inter * q_mem + sc_v
    den = (w_inter * jnp.sum(q * n_st[...], axis=1, keepdims=True)
           + jnp.sum(sc, axis=1, keepdims=True))
    h_out = num / jnp.maximum(jnp.abs(den), jnp.exp(-m_col))
    c_st[...] = decay * c_st[...] + kw_v
    n_st[...] = decay * n_st[...] + jnp.sum(kw, axis=0, keepdims=True)
    m_st[...] = m_new

    hn = h_out * lax.rsqrt(jnp.mean(h_out * h_out, axis=1, keepdims=True) + NORM_EPS) * nw
    return jax.nn.sigmoid(og) * hn


def _mlstm_kernel(qp_ref, kp_ref, v_ref, og_ref, gcol_ref, grow_ref, cwq_ref, cwk_ref, nw_ref,
                  tril_ref, triu_ref, y_ref, c_st, n_st, m_st, qbuf, kbuf, *, chunk, heads, dim):
    halo = SUBLANES

    @pl.when(pl.program_id(1) == 0)
    def _():
        c_st[...] = jnp.zeros_like(c_st)
        n_st[...] = jnp.zeros_like(n_st)
        m_st[...] = jnp.zeros_like(m_st)
        qbuf[0:halo, :] = jnp.zeros((halo, heads * dim), F32)
        kbuf[0:halo, :] = jnp.zeros((halo, heads * dim), F32)

    qbuf[halo:halo + chunk, :] = qp_ref[...]
    kbuf[halo:halo + chunk, :] = kp_ref[...]

    def conv_silu(buf, cw_ref):
        acc = buf[halo:halo + chunk, :] * cw_ref[CONV_WIDTH - 1:CONV_WIDTH, :]
        for j in range(CONV_WIDTH - 1):
            off = halo - (CONV_WIDTH - 1) + j
            acc = acc + buf[off:off + chunk, :] * cw_ref[j:j + 1, :]
        return acc * jax.nn.sigmoid(acc)

    q = conv_silu(qbuf, cwq_ref)
    k = conv_silu(kbuf, cwk_ref) * (1.0 / math.sqrt(dim))
    qbuf[0:halo, :] = qbuf[chunk:chunk + halo, :]
    kbuf[0:halo, :] = kbuf[chunk:chunk + halo, :]

    gcol = gcol_ref[...]
    grow = grow_ref[...]
    cols = [slice(head * dim, (head + 1) * dim) for head in range(heads)]
    ys = _round_robin([
        _mlstm_head(head, q[:, c], k[:, c], v_ref[:, c], og_ref[:, c], gcol, grow, nw_ref[:, c],
                    tril_ref, triu_ref, c_st.at[head], n_st.at[head], m_st.at[head], chunk=chunk)
        for head, c in enumerate(cols)])
    for y, c in zip(ys, cols):
        y_ref[:, c] = y.astype(y_ref.dtype)


def _mlstm(ml_qk, ml_v, ml_o, gates, conv_w, ml_norm_w, chunk=ML_CHUNK):
    b, s, _ = ml_v.shape
    h, d = ML_HEADS, ML_HEAD_DIM
    w = h * d
    gates_t = jnp.swapaxes(gates, 1, 2)
    r = lax.broadcasted_iota(jnp.int32, (chunk, chunk), 0)
    c = lax.broadcasted_iota(jnp.int32, (chunk, chunk), 1)
    tril = (c <= r).astype(BF16)
    triu = (r <= c).astype(BF16)
    wide = lambda off: pl.BlockSpec((None, chunk, w), lambda bi, ci: (bi, ci, off))
    const = lambda shape, off=0: pl.BlockSpec(shape, lambda bi, ci: (0, off))
    kern = functools.partial(_mlstm_kernel, chunk=chunk, heads=h, dim=d)
    return pl.pallas_call(
        kern,
        grid=(b, s // chunk),
        in_specs=[wide(0), wide(1), wide(0), wide(0),
                  pl.BlockSpec((None, chunk, 2 * h), lambda bi, ci: (bi, ci, 0)),
                  pl.BlockSpec((None, 2 * h, chunk), lambda bi, ci: (bi, 0, ci)),
                  const((CONV_WIDTH, w)), const((CONV_WIDTH, w), 1), const((1, w)),
                  const((chunk, chunk)), const((chunk, chunk))],
        out_specs=wide(0),
        out_shape=jax.ShapeDtypeStruct((b, s, w), BF16),
        scratch_shapes=[pltpu.VMEM((h, d, d), F32), pltpu.VMEM((h, 1, d), F32), pltpu.VMEM((h, 1, 1), F32),
                        pltpu.VMEM((chunk + SUBLANES, w), F32), pltpu.VMEM((chunk + SUBLANES, w), F32)],
        compiler_params=_cparams("arbitrary", "arbitrary"),
        name="mlstm",
    )(ml_qk, ml_qk, ml_v, ml_o, gates, gates_t, conv_w, conv_w, ml_norm_w.reshape(1, w), tril, triu)


def _out_proj_kernel(ya_ref, yb_ref, wa_ref, wb_ref, x_ref, g_ref, o_ref):
    mix = (jnp.dot(ya_ref[...], wa_ref[...], preferred_element_type=F32)
           + jnp.dot(yb_ref[...], wb_ref[...], preferred_element_type=F32))
    o_ref[...] = x_ref[...] + g_ref[...] * mix


def _out_proj(y_sb, y_ml, w_out, x, gate, tm, tn=1024):
    b, s, d = x.shape
    ka, kb = y_sb.shape[-1], y_ml.shape[-1]
    assert ka == kb
    n = b * s
    rows_per_seq = s // tm
    f32_tile = tm * tn * 4
    vmem = min(2 * (tm + tn) * (ka + kb) * 2 + 4 * f32_tile + 3 * f32_tile, VMEM_CAPACITY - VMEM_RESERVE)
    out = pl.pallas_call(
        _out_proj_kernel,
        grid=(n // tm, d // tn),
        in_specs=[pl.BlockSpec((tm, ka), lambda i, j: (i, 0)),
                  pl.BlockSpec((tm, kb), lambda i, j: (i, 0)),
                  pl.BlockSpec((ka, tn), lambda i, j: (0, j)),
                  pl.BlockSpec((kb, tn), lambda i, j: (1, j)),
                  pl.BlockSpec((tm, tn), lambda i, j: (i, j)),
                  pl.BlockSpec((None, 1, tn), lambda i, j: (i // rows_per_seq, 0, j))],
        out_specs=pl.BlockSpec((tm, tn), lambda i, j: (i, j)),
        out_shape=jax.ShapeDtypeStruct((n, d), F32),
        compiler_params=_cparams("arbitrary", "arbitrary", vmem=vmem),
        name="out_proj",
    )(y_sb.reshape(n, ka), y_ml.reshape(n, kb), w_out, w_out, x.reshape(n, d), gate.reshape(b, 1, d))
    return out.reshape(b, s, d)


def _router_kernel(h_ref, nw_ref, sh_ref, sc_ref, wr_ref, br_ref, tri_ref, u_ref, ri_ref, rw_ref, cnt_ref, run_ref):
    first = jnp.logical_and(pl.program_id(0) == 0, pl.program_id(1) == 0)

    @pl.when(first)
    def _():
        run_ref[...] = jnp.zeros_like(run_ref)

    u = _rms_mod(h_ref[...], nw_ref[...], sh_ref[...], sc_ref[...])
    tm = u.shape[0]
    uh, ul = _split_bf16(u)
    u_ref[...] = _pack_bf16_pairs(uh)
    wh, wl = _split_bf16(wr_ref[...])
    logits = (jnp.dot(uh, wh, preferred_element_type=F32) + jnp.dot(ul, wh, preferred_element_type=F32)
              + jnp.dot(uh, wl, preferred_element_type=F32) + br_ref[...])

    lane = lax.broadcasted_iota(jnp.int32, (tm, LANES), 1)
    lane_f = lane.astype(F32)
    neg = -jnp.inf
    first_lane = lambda hit: jnp.min(jnp.where(hit, lane_f, float(LANES)), axis=1, keepdims=True)

    gl = jnp.where(lane < N_GROUPS, logits, neg)
    g_max = jnp.max(gl, axis=1, keepdims=True)
    g_p = 1.0 / jnp.sum(jnp.exp(gl - g_max), axis=1, keepdims=True)
    g_sel = first_lane(gl == g_max)

    lo_lane = N_GROUPS + EXPERTS_PER_GROUP * g_sel
    el = jnp.where(jnp.logical_and(lane_f >= lo_lane, lane_f < lo_lane + EXPERTS_PER_GROUP), logits, neg)
    e1_max = jnp.max(el, axis=1, keepdims=True)
    l1 = first_lane(el == e1_max)
    el2 = jnp.where(lane_f == l1, neg, el)
    e2_max = jnp.max(el2, axis=1, keepdims=True)
    l2 = first_lane(el2 == e2_max)
    z_sum = jnp.sum(jnp.exp(el - e1_max), axis=1, keepdims=True)
    p1 = 1.0 / z_sum
    p2 = jnp.exp(e2_max - e1_max) / z_sum
    w1 = p1 / (p1 + p2) * g_p
    w2 = p2 / (p1 + p2) * g_p
    e1 = l1 - N_GROUPS
    e2 = l2 - N_GROUPS

    hit1 = lane_f == e1
    hit2 = lane_f == e2
    onehot = jnp.where(jnp.logical_or(hit1, hit2), 1.0, 0.0)
    before = jnp.dot(tri_ref[...], onehot.astype(BF16), preferred_element_type=F32) + run_ref[...]
    rank1 = jnp.sum(jnp.where(hit1, before, 0.0), axis=1, keepdims=True)
    rank2 = jnp.sum(jnp.where(hit2, before, 0.0), axis=1, keepdims=True)
    run = run_ref[...] + jnp.sum(onehot, axis=0, keepdims=True)
    run_ref[...] = run

    ri_ref[...] = jnp.where(lane == 0, e1, jnp.where(lane == 1, e2, jnp.where(lane == 2, rank1,
                            jnp.where(lane == 3, rank2, 0.0)))).astype(jnp.int32)
    rw_ref[...] = jnp.where(lane == 0, w1, jnp.where(lane == 1, w2, 0.0))
    cnt_ref[...] = jnp.broadcast_to(run, cnt_ref.shape).astype(jnp.int32)


def _router(h, norm_w, shift, scale, w_router, b_router, tm=256):
    b, s, d = h.shape
    n = b * s
    tri = (lax.broadcasted_iota(jnp.int32, (tm, tm), 1) < lax.broadcasted_iota(jnp.int32, (tm, tm), 0)).astype(BF16)
    vec = pl.BlockSpec((None, 1, d), lambda bi, i: (bi, 0, 0))
    rows = lambda width: pl.BlockSpec((tm, width), lambda bi, i: (bi * (s // tm) + i, 0))
    return pl.pallas_call(
        _router_kernel,
        grid=(b, s // tm),
        in_specs=[pl.BlockSpec((None, tm, d), lambda bi, i: (bi, i, 0)),
                  pl.BlockSpec((1, d), lambda bi, i: (0, 0)), vec, vec,
                  pl.BlockSpec((d, LANES), lambda bi, i: (0, 0)),
                  pl.BlockSpec((1, LANES), lambda bi, i: (0, 0)),
                  pl.BlockSpec((tm, tm), lambda bi, i: (0, 0))],
        out_specs=[rows(d // 2), rows(LANES), rows(LANES),
                   pl.BlockSpec((SUBLANES, LANES), lambda bi, i: (0, 0))],
        out_shape=[jax.ShapeDtypeStruct((n, d // 2), jnp.int32), jax.ShapeDtypeStruct((n, LANES), jnp.int32),
                   jax.ShapeDtypeStruct((n, LANES), F32), jax.ShapeDtypeStruct((SUBLANES, LANES), jnp.int32)],
        scratch_shapes=[pltpu.VMEM((1, LANES), F32)],
        compiler_params=_cparams("arbitrary", "arbitrary"),
        name="router",
    )(h, norm_w.reshape(1, d), shift.reshape(b, 1, d), scale.reshape(b, 1, d), w_router, b_router, tri)


ROW_LOOP_UNROLL = 8
ZERO_ROWS = 128


def _dispatch_kernel(tail_ref, dest_ref, u_ref, xb_hbm, zbuf, zsem, sem, *, tokens, block):
    n_exp = tail_ref.shape[0]

    def clear(e, half):
        row = pl.multiple_of(tail_ref[e] + half * ZERO_ROWS, ZERO_ROWS)
        return pltpu.make_async_copy(zbuf, xb_hbm.at[pl.ds(row, ZERO_ROWS)], zsem)

    @pl.when(pl.program_id(0) == 0)
    def _():
        zbuf[...] = jnp.zeros_like(zbuf)
        for start in (True, False):
            def each(e, carry):
                @pl.when(tail_ref[e] >= 0)
                def _():
                    for half in range(block // ZERO_ROWS):
                        clear(e, half).start() if start else clear(e, half).wait()
                return carry
            lax.fori_loop(0, n_exp, each, 0)

    def row_copy(t, k):
        return pltpu.make_async_copy(u_ref.at[pl.ds(t, 1)], xb_hbm.at[pl.ds(dest_ref[0, TOP_K * t + k], 1)], sem)

    def issue(t, carry):
        for k in range(TOP_K):
            row_copy(t, k).start(priority=k % 2)
        return carry

    lax.fori_loop(0, tokens, issue, 0, unroll=ROW_LOOP_UNROLL)

    def drain(t, carry):
        for k in range(TOP_K):
            row_copy(t, k).wait()
        return carry

    lax.fori_loop(0, tokens, drain, 0, unroll=ROW_LOOP_UNROLL)


def _dispatch(u, dest, tail_block_row, cap, tokens=512, block=MOE_BLOCK):
    n, d = u.shape
    steps = n // tokens
    kern = functools.partial(_dispatch_kernel, tokens=tokens, block=block)
    return pl.pallas_call(
        kern,
        grid_spec=pltpu.PrefetchScalarGridSpec(
            num_scalar_prefetch=1,
            grid=(steps,),
            in_specs=[pl.BlockSpec((None, 1, TOP_K * tokens), lambda i, tail: (i, 0, 0), memory_space=pltpu.SMEM),
                      pl.BlockSpec((tokens, d), lambda i, tail: (i, 0))],
            out_specs=pl.BlockSpec(memory_space=pl.ANY),
            scratch_shapes=[pltpu.VMEM((ZERO_ROWS, d), u.dtype), pltpu.SemaphoreType.DMA(()),
                            pltpu.SemaphoreType.DMA(())]),
        out_shape=jax.ShapeDtypeStruct((cap, d), u.dtype),
        compiler_params=_cparams("arbitrary"),
        name="dispatch",
    )(tail_block_row, dest.reshape(steps, 1, TOP_K * tokens), u)


def _lookup(table, idx):
    hit = idx[:, None] == jnp.arange(table.shape[0], dtype=idx.dtype)[None, :]
    return jnp.sum(jnp.where(hit, table[None, :], 0), axis=1).astype(table.dtype)


def _run_plan(keys, ok):
    steps = ok.shape[0]
    idx = jnp.arange(steps, dtype=jnp.int32)
    changed = idx == 0
    for key in keys:
        changed = jnp.logical_or(changed, key != jnp.roll(key, 1))
    first = jnp.logical_and(changed, ok)
    slot = (jnp.cumsum(first.astype(jnp.int32)) - 1) % 2
    starts_from = lax.cummin(jnp.where(first, idx, steps), axis=0, reverse=True)
    nxt = jnp.concatenate([starts_from[1:], jnp.full((1,), steps, jnp.int32)])
    more = nxt < steps
    as_i32 = lambda a: a.astype(jnp.int32)
    return as_i32(first), as_i32(slot), as_i32(more), as_i32(jnp.minimum(nxt, steps - 1))


def _run_weights(step, first_ref, slot_ref, more_ref, fetch_cur, fetch_next):
    slot = slot_ref[step]

    @pl.when(step == 0)
    def _():
        for part, copy in enumerate(fetch_cur(slot)):
            copy.start(priority=part % 2)

    @pl.when(first_ref[step] == 1)
    def _():
        @pl.when(more_ref[step] == 1)
        def _():
            for part, copy in enumerate(fetch_next(1 - slot)):
                copy.start(priority=part % 2)

        for copy in fetch_cur(slot):
            copy.wait()

    return slot


def _e1_kernel(blk_ref, ex_ref, ft_ref, oblk_ref, oft_ref, ok_ref, first_ref, slot_ref, more_ref, nex_ref, nft_ref,
               x_ref, wg_hbm, wu_hbm, o_ref, wg_buf, wu_buf, sems, *, tf):
    del blk_ref, oblk_ref, oft_ref
    step = pl.program_id(0)

    def fetch(e, f, slot):
        cols = pl.ds(pl.multiple_of(f * tf, tf), tf)
        rows = wg_buf.shape[1] // WEIGHT_DMA_PARTS
        return [pltpu.make_async_copy(hbm.at[e, pl.ds(p * rows, rows), cols],
                                      buf.at[slot, pl.ds(p * rows, rows)], sems.at[slot, which])
                for p in range(WEIGHT_DMA_PARTS)
                for which, (hbm, buf) in enumerate(((wg_hbm, wg_buf), (wu_hbm, wu_buf)))]

    slot = _run_weights(step, first_ref, slot_ref, more_ref,
                        lambda sl: fetch(ex_ref[step], ft_ref[step], sl),
                        lambda sl: fetch(nex_ref[step], nft_ref[step], sl))
    ok = ok_ref[step] == 1

    @pl.when(ok)
    def _():
        x_lo, x_hi = _unpack_bf16_pairs(x_ref[...])
        half = x_lo.shape[1]
        mm = lambda w_buf: (jnp.dot(x_lo, w_buf[slot, :half, :], preferred_element_type=F32)
                            + jnp.dot(x_hi, w_buf[slot, half:, :], preferred_element_type=F32))
        g = mm(wg_buf)
        up = mm(wu_buf)
        o_ref[...] = ((g * jax.nn.sigmoid(g)) * up).astype(o_ref.dtype)

    @pl.when(jnp.logical_not(ok))
    def _():
        o_ref[...] = jnp.zeros_like(o_ref)


def _e2_kernel(blk_ref, ex_ref, oblk_ref, ok_ref, first_ref, slot_ref, more_ref, nex_ref,
               h_ref, wd_hbm, o_ref, wd_buf, sems):
    del blk_ref, oblk_ref
    step = pl.program_id(0)
    rows = wd_buf.shape[1] // WEIGHT_DMA_PARTS
    fetch = lambda e, slot: [pltpu.make_async_copy(wd_hbm.at[e, pl.ds(p * rows, rows)],
                                                   wd_buf.at[slot, pl.ds(p * rows, rows)], sems.at[slot])
                             for p in range(WEIGHT_DMA_PARTS)]
    slot = _run_weights(step, first_ref, slot_ref, more_ref,
                        lambda sl: fetch(ex_ref[step], sl), lambda sl: fetch(nex_ref[step], sl))
    ok = ok_ref[step] == 1

    @pl.when(ok)
    def _():
        o_ref[...] = jnp.dot(h_ref[...].astype(F32), wd_buf[slot], preferred_element_type=F32)

    @pl.when(jnp.logical_not(ok))
    def _():
        o_ref[...] = jnp.zeros_like(o_ref)


def _experts(xb, blocks_per_expert, w_gate, w_up, w_down, bm=MOE_BLOCK, tf=512):
    cap, dp = xb.shape
    n_exp, d, dff = w_gate.shape
    n_blocks = cap // bm
    n_ft = dff // tf
    nb = blocks_per_expert.astype(jnp.int32)
    bstart = jnp.cumsum(nb) - nb
    used = jnp.sum(nb)

    steps1 = n_blocks * n_ft
    step = jnp.arange(steps1, dtype=jnp.int32)
    ok1 = step < used * n_ft
    send = jnp.cumsum(nb * n_ft)
    s_idx = jnp.minimum(step, used * n_ft - 1)
    e_of = jnp.minimum(jnp.sum(s_idx[:, None] >= send[None, :], axis=1), n_exp - 1).astype(jnp.int32)
    nb_of = _lookup(nb, e_of)
    local = s_idx - (_lookup(send, e_of) - nb_of * n_ft)
    nbe = jnp.maximum(nb_of, 1)
    ft_of = (local // nbe).astype(jnp.int32)
    blk_of = (_lookup(bstart, e_of) + local % nbe).astype(jnp.int32)
    spare = step - used * n_ft
    oblk_of = jnp.where(ok1, blk_of, used + spare // n_ft).astype(jnp.int32)
    oft_of = jnp.where(ok1, ft_of, spare % n_ft).astype(jnp.int32)

    first1, slot1, more1, nxt1 = _run_plan([e_of, ft_of], ok1)
    hmid = pl.pallas_call(
        functools.partial(_e1_kernel, tf=tf),
        grid_spec=pltpu.PrefetchScalarGridSpec(
            num_scalar_prefetch=11,
            grid=(steps1,),
            in_specs=[pl.BlockSpec((bm, dp), lambda s, blk, *_: (blk[s], 0)),
                      pl.BlockSpec(memory_space=pl.ANY),
                      pl.BlockSpec(memory_space=pl.ANY)],
            out_specs=pl.BlockSpec((bm, tf), lambda s, blk, ex, ft, oblk, oft, *_: (oblk[s], oft[s])),
            scratch_shapes=[pltpu.VMEM((2, d, tf), F32), pltpu.VMEM((2, d, tf), F32),
                            pltpu.SemaphoreType.DMA((2, 2))]),
        out_shape=jax.ShapeDtypeStruct((cap, dff), BF16),
        compiler_params=_cparams("arbitrary"),
        name="e1",
    )(blk_of, e_of, ft_of, oblk_of, oft_of, ok1.astype(jnp.int32), first1, slot1, more1, _lookup(e_of, nxt1), _lookup(ft_of, nxt1),
      xb, w_gate, w_up)

    blk = jnp.arange(n_blocks, dtype=jnp.int32)
    b_idx = jnp.minimum(blk, used - 1)
    bend = jnp.cumsum(nb)
    e2_of = jnp.minimum(jnp.sum(b_idx[:, None] >= bend[None, :], axis=1), n_exp - 1).astype(jnp.int32)
    ok2 = blk < used
    first2, slot2, more2, nxt2 = _run_plan([e2_of], ok2)
    yb = pl.pallas_call(
        _e2_kernel,
        grid_spec=pltpu.PrefetchScalarGridSpec(
            num_scalar_prefetch=8,
            grid=(n_blocks,),
            in_specs=[pl.BlockSpec((bm, dff), lambda s, hb, *_: (hb[s], 0)),
                      pl.BlockSpec(memory_space=pl.ANY)],
            out_specs=pl.BlockSpec((bm, d), lambda s, hb, ex, ob, *_: (ob[s], 0)),
            scratch_shapes=[pltpu.VMEM((2, dff, d), F32), pltpu.SemaphoreType.DMA((2,))]),
        out_shape=jax.ShapeDtypeStruct((cap, d), F32),
        compiler_params=_cparams("arbitrary"),
        name="e2",
    )(b_idx, e2_of, blk, ok2.astype(jnp.int32), first2, slot2, more2, _lookup(e2_of, nxt2), hmid, w_down)
    return yb


def _combine_kernel(dest_ref, dest_next_ref, yb_hbm, h_ref, rw_ref, g_ref, nw_ref, sh_ref, sc_ref, o_ref,
                    rows, sems, *, tokens, steps):
    step = pl.program_id(0)
    slot = step % 2

    def row_copy(d_ref, buf, t, k):
        return pltpu.make_async_copy(yb_hbm.at[pl.ds(d_ref[0, TOP_K * t + k], 1)],
                                     rows.at[buf, k, pl.ds(t, 1)], sems.at[buf])

    def gather(d_ref, buf):
        def issue(t, carry):
            for k in range(TOP_K):
                row_copy(d_ref, buf, t, k).start(priority=k % 2)
            return carry

        lax.fori_loop(0, tokens, issue, 0, unroll=ROW_LOOP_UNROLL)

    @pl.when(step == 0)
    def _():
        gather(dest_ref, 0)

    @pl.when(step + 1 < steps)
    def _():
        gather(dest_next_ref, 1 - slot)

    def drain(t, carry):
        for k in range(TOP_K):
            row_copy(dest_ref, slot, t, k).wait()
        return carry

    lax.fori_loop(0, tokens, drain, 0, unroll=ROW_LOOP_UNROLL)

    rw = rw_ref[...]
    ffn = rows[slot, 0] * rw[:, 0:1]
    for k in range(1, TOP_K):
        ffn = ffn + rows[slot, k] * rw[:, k:k + 1]
    h2 = h_ref[...] + g_ref[...] * ffn
    o_ref[...] = _rms_mod(h2, nw_ref[...], sh_ref[...], sc_ref[...])


def _combine(yb, dest, h, route_w, gate, norm_w, shift, scale, tokens=256):
    b, s, d = h.shape
    n = b * s
    per_seq = s // tokens
    steps = n // tokens
    kern = functools.partial(_combine_kernel, tokens=tokens, steps=steps)
    vec = pl.BlockSpec((None, 1, d), lambda i: (i // per_seq, 0, 0))
    dest_blocks = dest.reshape(steps, 1, TOP_K * tokens)
    dest_spec = lambda ahead: pl.BlockSpec((None, 1, TOP_K * tokens),
                                           lambda i: (jnp.minimum(i + ahead, steps - 1), 0, 0),
                                           memory_space=pltpu.SMEM)
    out = pl.pallas_call(
        kern,
        grid=(steps,),
        in_specs=[dest_spec(0), dest_spec(1),
                  pl.BlockSpec(memory_space=pl.ANY),
                  pl.BlockSpec((tokens, d), lambda i: (i, 0)),
                  pl.BlockSpec((tokens, LANES), lambda i: (i, 0)),
                  vec,
                  pl.BlockSpec((1, d), lambda i: (0, 0)), vec, vec],
        out_specs=pl.BlockSpec((tokens, d), lambda i: (i, 0)),
        out_shape=jax.ShapeDtypeStruct((n, d), F32),
        scratch_shapes=[pltpu.VMEM((2, TOP_K, tokens, d), F32), pltpu.SemaphoreType.DMA((2,))],
        compiler_params=_cparams("arbitrary"),
        name="combine",
    )(dest_blocks, dest_blocks, yb, h.reshape(n, d), route_w,
      gate.reshape(b, 1, d), norm_w.reshape(1, d), shift.reshape(b, 1, d), scale.reshape(b, 1, d))
    return out.reshape(b, s, d)


def _layer(h, mod, norm1_w, w_in, conv_w, ml_gate_bias, ml_norm_w, w_out, norm2_w,
           w_router_group, b_router_group, w_router_expert, b_router_expert,
           w_exp_gate, w_exp_up, w_exp_down, final):
    b, s, d = h.shape
    n = b * s
    sh1, sc1, g1, sh2, sc2, g2 = jnp.split(mod, 6, axis=-1)
    sb_w = SB_HEADS * SB_HEAD_DIM
    ml_w = ML_HEADS * ML_HEAD_DIM

    u = _norm_mod(h, norm1_w, sh1, sc1, BF16).reshape(n, d)
    tm = 1024 if n % 1024 == 0 else 512
    w_in_t = w_in.T
    sb_qkv = _matmul(u, w_in_t, 0, 3 * sb_w, BF16, tm, PROJ_TN)
    ml_qk = _matmul(u, w_in_t, 3 * sb_w, 2 * ml_w, F32, tm, PROJ_TN)
    ml_v = _matmul(u, w_in_t, 3 * sb_w + 2 * ml_w, ml_w, BF16, tm, PROJ_TN)
    ml_o = _matmul(u, w_in_t, 3 * sb_w + 3 * ml_w, ml_w, F32, tm, PROJ_TN)
    n_gate = 2 * ML_HEADS
    w_gate = jnp.pad(w_in_t[3 * sb_w + 4 * ml_w:, :], ((0, LANES - n_gate), (0, 0)))
    b_gate = jnp.pad(ml_gate_bias, (0, LANES - n_gate)).reshape(1, LANES)
    gates = _matmul(u, w_gate, 0, LANES, F32, tm, LANES, bias=b_gate)[:, :n_gate]

    y_sb = _sb_attention(sb_qkv.reshape(b, s, 3 * sb_w))
    y_ml = _mlstm(ml_qk.reshape(b, s, 2 * ml_w), ml_v.reshape(b, s, ml_w), ml_o.reshape(b, s, ml_w),
                  gates.reshape(b, s, n_gate), conv_w, ml_norm_w)
    h = _out_proj(y_sb, y_ml, w_out.astype(BF16), h, g1, tm=min(1024, s))

    w_router = jnp.pad(jnp.concatenate([w_router_group, w_router_expert], axis=1),
                       ((0, 0), (0, LANES - N_GROUPS - N_EXPERTS)))
    b_router = jnp.pad(jnp.concatenate([b_router_group, b_router_expert]),
                       (0, LANES - N_GROUPS - N_EXPERTS)).reshape(1, LANES)
    u2, route_i, route_w, counts = _router(h, norm2_w, sh2, sc2, w_router, b_router)
    counts = counts[0, :N_EXPERTS]
    blocks_per_expert = (counts + MOE_BLOCK - 1) // MOE_BLOCK
    padded = blocks_per_expert * MOE_BLOCK
    pstarts = jnp.cumsum(padded) - padded
    dest = (_lookup(pstarts, route_i[:, 0:TOP_K].reshape(-1)) + route_i[:, TOP_K:2 * TOP_K].reshape(-1))
    cap = (-(-(n * TOP_K) // MOE_BLOCK) + N_EXPERTS) * MOE_BLOCK
    tail_block_row = jnp.where(counts % MOE_BLOCK != 0, pstarts + padded - MOE_BLOCK, -1)
    spare_row = jnp.sum(padded) + MOE_BLOCK * jnp.arange(N_EXPERTS, dtype=jnp.int32)
    spare_row = jnp.where(spare_row < cap, spare_row, -1)
    clear_rows = jnp.concatenate([tail_block_row, spare_row]).astype(jnp.int32)
    xb = _dispatch(u2, dest, clear_rows, cap)
    yb = _experts(xb, blocks_per_expert, w_exp_gate, w_exp_up, w_exp_down)
    norm_w, shift, scale = final
    return _combine(yb, dest, h, route_w, g2, norm_w, shift, scale)


def kernel(x, c, norm1_w, w_in, conv_w, ml_gate_bias, ml_norm_w, w_out, norm2_w, w_router_group, b_router_group, w_router_expert, b_router_expert, w_exp_gate, w_exp_up, w_exp_down, w_ada, b_ada, final_norm_w, w_ada_final, b_ada_final):
    b, s, d = x.shape
    depth = w_in.shape[0]
    assert depth == 1, "the final norm is fused into the (single) layer's combine step"
    c_rep = jnp.broadcast_to(c[:, :, None], (b, d, LANES))
    fmod = _ada(c_rep, w_ada_final, b_ada_final)
    sh_f, sc_f = jnp.split(fmod, 2, axis=-1)
    mod = _ada(c_rep, w_ada[0], b_ada[0])
    return _layer(x, mod, norm1_w[0], w_in[0], conv_w[0], ml_gate_bias[0], ml_norm_w[0], w_out[0], norm2_w[0],
                  w_router_group[0], b_router_group[0], w_router_expert[0], b_router_expert[0],
                  w_exp_gate[0], w_exp_up[0], w_exp_down[0], (final_norm_w, sh_f, sc_f))
```

```python
import functools
import math

import jax
import jax.numpy as jnp
from jax import lax
from jax.experimental import pallas as pl
from jax.experimental.pallas import tpu as pltpu

SB_HEADS = 16
SB_HEAD_DIM = 128
ML_HEADS = 4
ML_HEAD_DIM = 512
CONV_WIDTH = 4
N_GROUPS = 4
EXPERTS_PER_GROUP = 8
N_EXPERTS = N_GROUPS * EXPERTS_PER_GROUP
TOP_K = 2
NORM_EPS = 1e-6

LANES = 128
SUBLANES = 8
VMEM_CAPACITY = 64 * 1024 * 1024
VMEM_RESERVE = 4 * 1024 * 1024
VMEM_LIMIT = 56 * 1024 * 1024

ATTN_SUB = 128
ATTN_SUBS_PER_STEP = 16
ATTN_ALWAYS = 3
ML_CHUNK = 256
MOE_BLOCK = 256
PROJ_TN = 1024
WEIGHT_DMA_PARTS = 4
UNDERFLOW_LOG = -90.0

F32 = jnp.float32
BF16 = jnp.bfloat16


def _cparams(*sem, vmem=VMEM_LIMIT):
    return pltpu.CompilerParams(dimension_semantics=sem, vmem_limit_bytes=vmem)


def _split_bf16(x):
    hi = x.astype(BF16)
    lo = (x - hi.astype(F32)).astype(BF16)
    return hi, lo


_HIGH_HALF = -65536


def _pack_bf16_pairs(x):
    k = x.shape[1] // 2
    bits = lambda v: lax.bitcast_convert_type(v.astype(F32), jnp.int32)
    return lax.shift_right_logical(bits(x[:, :k]), 16) | (bits(x[:, k:]) & _HIGH_HALF)


def _unpack_bf16_pairs(w):
    lo = lax.bitcast_convert_type(lax.shift_left(w, 16), F32)
    hi = lax.bitcast_convert_type(w & _HIGH_HALF, F32)
    return lo, hi


def _log_sigmoid(x):
    return jnp.minimum(x, 0.0) - jnp.log1p(jnp.exp(-jnp.abs(x)))


def _ada_kernel(c_ref, w_ref, b_ref, o_ref):
    nb, tn = o_ref.shape
    rows = []
    for b in range(nb):
        cb = c_ref[b]
        cb = cb * jax.nn.sigmoid(cb)
        pieces = [jnp.sum(w_ref[:, j * LANES:(j + 1) * LANES] * cb, axis=0, keepdims=True)
                  for j in range(tn // LANES)]
        rows.append(jnp.concatenate(pieces, axis=1))
    o_ref[...] = jnp.concatenate(rows, axis=0) + b_ref[...]


def _ada(c_rep, w, bias, tn=1024):
    nb, k, _ = c_rep.shape
    n_out = w.shape[1]
    return pl.pallas_call(
        _ada_kernel,
        grid=(n_out // tn,),
        in_specs=[pl.BlockSpec((nb, k, LANES), lambda j: (0, 0, 0)),
                  pl.BlockSpec((k, tn), lambda j: (0, j)),
                  pl.BlockSpec((1, tn), lambda j: (0, j))],
        out_specs=pl.BlockSpec((nb, tn), lambda j: (0, j)),
        out_shape=jax.ShapeDtypeStruct((nb, n_out), F32),
        compiler_params=_cparams("arbitrary"),
        name="ada",
    )(c_rep, w, bias.reshape(1, n_out))


def _rms_mod(x, w, shift, scale):
    var = jnp.mean(x * x, axis=-1, keepdims=True)
    y = x * lax.rsqrt(var + NORM_EPS) * w
    return y * (1.0 + scale) + shift


def _norm_mod_kernel(x_ref, w_ref, sh_ref, sc_ref, o_ref):
    o_ref[...] = _rms_mod(x_ref[...], w_ref[...], sh_ref[...], sc_ref[...]).astype(o_ref.dtype)


def _norm_mod(x, w, shift, scale, out_dtype, tm=512):
    b, s, d = x.shape
    vec = pl.BlockSpec((None, 1, d), lambda bi, i: (bi, 0, 0))
    return pl.pallas_call(
        _norm_mod_kernel,
        grid=(b, s // tm),
        in_specs=[pl.BlockSpec((None, tm, d), lambda bi, i: (bi, i, 0)),
                  pl.BlockSpec((1, d), lambda bi, i: (0, 0)), vec, vec],
        out_specs=pl.BlockSpec((None, tm, d), lambda bi, i: (bi, i, 0)),
        out_shape=jax.ShapeDtypeStruct((b, s, d), out_dtype),
        compiler_params=_cparams("arbitrary", "arbitrary"),
        name="norm_mod",
    )(x, w.reshape(1, d), shift.reshape(b, 1, d), scale.reshape(b, 1, d))


def _matmul_kernel(a_ref, wt_hbm, *rest, has_bias, row0, n_col_tiles):
    b_ref, o_ref, stage, w16, sem = rest if has_bias else (None,) + rest
    j = pl.program_id(0)
    tn = stage.shape[0]
    rows = tn // WEIGHT_DMA_PARTS

    def fetch(col_tile):
        first = row0 + col_tile * tn
        return [pltpu.make_async_copy(wt_hbm.at[pl.ds(pl.multiple_of(first + p * rows, rows), rows), :],
                                      stage.at[pl.ds(p * rows, rows), :], sem)
                for p in range(WEIGHT_DMA_PARTS)]

    @pl.when(pl.program_id(1) == 0)
    def _():
        @pl.when(j == 0)
        def _():
            for part, copy in enumerate(fetch(0)):
                copy.start(priority=part % 2)

        for copy in fetch(j):
            copy.wait()
        chunk = min(tn, 2 * LANES)
        for c in range(0, tn, chunk):
            w16[:, c:c + chunk] = stage[c:c + chunk, :].T.astype(BF16)

        @pl.when(j + 1 < n_col_tiles)
        def _():
            for part, copy in enumerate(fetch(j + 1)):
                copy.start(priority=part % 2)

    acc = jnp.dot(a_ref[...], w16[...], preferred_element_type=F32)
    if has_bias:
        acc = acc + b_ref[...]
    o_ref[...] = acc.astype(o_ref.dtype)


def _matmul(a, w_t, col0, ncols, out_dtype, tm, tn, bias=None):
    m, k = a.shape
    f32_tile = tm * tn * 4
    vmem = min(tn * k * 4 + k * tn * 2 + 2 * tm * k * 2 + 2 * f32_tile + 3 * f32_tile, VMEM_CAPACITY - VMEM_RESERVE)
    in_specs = [pl.BlockSpec((tm, k), lambda j, i: (i, 0)),
                pl.BlockSpec(memory_space=pl.ANY)]
    args = [a, w_t]
    if bias is not None:
        in_specs.append(pl.BlockSpec((1, tn), lambda j, i: (0, j)))
        args.append(bias)
    return pl.pallas_call(
        functools.partial(_matmul_kernel, has_bias=bias is not None, row0=col0, n_col_tiles=ncols // tn),
        grid=(ncols // tn, m // tm),
        in_specs=in_specs,
        out_specs=pl.BlockSpec((tm, tn), lambda j, i: (i, j)),
        out_shape=jax.ShapeDtypeStruct((m, ncols), out_dtype),
        scratch_shapes=[pltpu.VMEM((tn, k), F32), pltpu.VMEM((k, tn), BF16), pltpu.SemaphoreType.DMA(())],
        compiler_params=_cparams("arbitrary", "arbitrary", vmem=vmem),
        name="proj",
    )(*args)


def _sb_attn_kernel(q_ref, k_ref, v_ref, tri_ref, o_ref, acc_ref, r_ref, *, sub, n_sub, scale):
    first = pl.program_id(2) * n_sub
    tri = tri_ref[...]
    row = lax.broadcasted_iota(jnp.int32, (sub, sub), 0)
    col = lax.broadcasted_iota(jnp.int32, (sub, sub), 1)
    causal = col < row

    def visit(q, blk, r, mask):
        start = pl.multiple_of(jnp.maximum(blk, 0) * sub, sub)
        kb = k_ref[pl.ds(start, sub), :]
        vb = v_ref[pl.ds(start, sub), :]
        z = lax.dot_general(q, kb, (((1,), (1,)), ((), ())), preferred_element_type=F32) * scale
        log_keep = -(jnp.maximum(z, 0.0) + jnp.log(1.0 + jnp.exp(-jnp.abs(z))))
        log_beta = log_keep + z
        if mask is not None:
            log_keep = jnp.where(mask, log_keep, 0.0)
        hi, lo = _split_bf16(log_keep)
        between = (jnp.dot(hi, tri, preferred_element_type=F32)
                   + jnp.dot(lo, tri, preferred_element_type=F32))
        w = jnp.exp(log_beta + between + r)
        if mask is not None:
            w = jnp.where(mask, w, 0.0)
        pv = jnp.dot(w.astype(BF16), vb, preferred_element_type=F32)
        return pv, r + jnp.sum(log_keep, axis=1, keepdims=True)

    n_win = ATTN_ALWAYS
    roww = lax.broadcasted_iota(jnp.int32, (sub, n_win * sub), 0)
    colw = lax.broadcasted_iota(jnp.int32, (sub, n_win * sub), 1)

    def win_start(s):
        blk = first + s
        clamp = s < n_win - 1
        first_blk = jnp.maximum(blk - (n_win - 1), 0) if clamp else blk - (n_win - 1)
        return pl.multiple_of(first_blk * sub, sub), clamp

    def masked(s, x):
        start, clamp = win_start(s)
        if clamp:
            return jnp.where(start + colw < (first + s) * sub + roww, x, 0.0)
        last = (n_win - 1) * sub
        return jnp.concatenate([x[:, :last], jnp.where(causal, x[:, last:], 0.0)], axis=1)

    scores = []
    for s in range(n_sub):
        kw = k_ref[pl.ds(win_start(s)[0], n_win * sub), :]
        q = q_ref[s * sub:(s + 1) * sub, :]
        scores.append(lax.dot_general(q, kw, (((1,), (1,)), ((), ())), preferred_element_type=F32) * scale)

    staged = []
    for s in range(n_sub):
        z = scores[s]
        log_keep = -(jnp.maximum(z, 0.0) + jnp.log(1.0 + jnp.exp(-jnp.abs(z))))
        log_beta = log_keep + z
        log_keep = masked(s, log_keep)
        chunks = [log_keep[:, c * sub:(c + 1) * sub] for c in range(n_win)]
        hi, lo = _split_bf16(jnp.concatenate(chunks, axis=0))
        inside = (jnp.dot(hi, tri, preferred_element_type=F32)
                  + jnp.dot(lo, tri, preferred_element_type=F32))
        staged.append((log_beta, chunks, inside))

    r_top = None
    for s in range(n_sub):
        log_beta, chunks, inside = staged[s]
        later = 0.0
        between = [None] * n_win
        for c in reversed(range(n_win)):
            between[c] = inside[c * sub:(c + 1) * sub, :] + later
            later = later + jnp.sum(chunks[c], axis=1, keepdims=True)
        w = masked(s, jnp.exp(log_beta + jnp.concatenate(between, axis=1)))
        vw = v_ref[pl.ds(win_start(s)[0], n_win * sub), :]
        acc_ref[s] = jnp.dot(w.astype(BF16), vw, preferred_element_type=F32)
        r = jnp.broadcast_to(later, (sub, LANES))
        r_ref[s] = r
        r_top = r if r_top is None else jnp.maximum(r_top, r)

    @pl.when(jnp.max(r_top) > UNDERFLOW_LOG)
    def _():
        for s in range(n_sub):
            q = q_ref[s * sub:(s + 1) * sub, :]

            def cond(carry):
                j, r_max = carry
                return jnp.logical_and(j >= 0, r_max > UNDERFLOW_LOG)

            def body(carry, s=s, q=q):
                j, _ = carry
                pv, r = visit(q, j, r_ref[s], None)
                acc_ref[s] += pv
                r_ref[s] = r
                return j - 1, jnp.max(r)

            lax.while_loop(cond, body, (first + s - ATTN_ALWAYS, jnp.max(r_ref[s])))

    for s in range(n_sub):
        o_ref[s * sub:(s + 1) * sub, :] = acc_ref[s].astype(o_ref.dtype)


def _sb_attention(qkv, sub=ATTN_SUB, n_sub=ATTN_SUBS_PER_STEP):
    b, s, _ = qkv.shape
    h, d = SB_HEADS, SB_HEAD_DIM
    assert sub == LANES, "the carried row sums are kept lane-replicated at the key sub-block width"
    assert s >= ATTN_ALWAYS * sub
    n_sub = min(n_sub, s // sub)
    tq = sub * n_sub
    idx = lax.broadcasted_iota(jnp.int32, (sub, sub), 0) > lax.broadcasted_iota(jnp.int32, (sub, sub), 1)
    tri = idx.astype(BF16)
    kern = functools.partial(_sb_attn_kernel, sub=sub, n_sub=n_sub, scale=1.0 / math.sqrt(d))
    return pl.pallas_call(
        kern,
        grid=(b, h, s // tq),
        in_specs=[pl.BlockSpec((None, tq, d), lambda bi, hi, qi: (bi, qi, hi)),
                  pl.BlockSpec((None, s, d), lambda bi, hi, qi: (bi, 0, h + hi)),
                  pl.BlockSpec((None, s, d), lambda bi, hi, qi: (bi, 0, 2 * h + hi)),
                  pl.BlockSpec((sub, sub), lambda bi, hi, qi: (0, 0))],
        out_specs=pl.BlockSpec((None, tq, d), lambda bi, hi, qi: (bi, qi, hi)),
        out_shape=jax.ShapeDtypeStruct((b, s, h * d), BF16),
        scratch_shapes=[pltpu.VMEM((n_sub, sub, d), F32), pltpu.VMEM((n_sub, sub, LANES), F32)],
        compiler_params=_cparams("arbitrary", "arbitrary", "arbitrary"),
        name="sb_attn",
    )(qkv, qkv, qkv, tri)


def _round_robin(stagewise):
    out = [None] * len(stagewise)
    live = list(range(len(stagewise)))
    while live:
        for i in list(live):
            try:
                next(stagewise[i])
            except StopIteration as stop:
                out[i] = stop.value
                live.remove(i)
    return out


def _mlstm_head(head, q, k, vb, og, gcol, grow, nw, tril_ref, triu_ref, c_st, n_st, m_st, *, chunk):
    li_col = gcol[:, head:head + 1]
    lf_col = _log_sigmoid(gcol[:, ML_HEADS + head:ML_HEADS + head + 1])
    li_row = grow[head:head + 1, :]
    lf_row = _log_sigmoid(grow[ML_HEADS + head:ML_HEADS + head + 1, :])

    hi, lo = _split_bf16(jnp.broadcast_to(lf_col, (chunk, LANES)))
    b_col = (jnp.dot(tril_ref[...], hi, preferred_element_type=F32)
             + jnp.dot(tril_ref[...], lo, preferred_element_type=F32))[:, 0:1]
    hi, lo = _split_bf16(jnp.broadcast_to(lf_row, (2 * SUBLANES, chunk)))
    b_row = (jnp.dot(hi, triu_ref[...], preferred_element_type=F32)
             + jnp.dot(lo, triu_ref[...], preferred_element_type=F32))[0:1, :]
    qb = q.astype(BF16)
    kb = k.astype(BF16)
    qk = lax.dot_general(qb, kb, (((1,), (1,)), ((), ())), preferred_element_type=F32)
    q_mem = jnp.dot(qb, c_st[...].astype(BF16), preferred_element_type=F32)
    yield

    m_prev = m_st[...]
    a_col = b_col + m_prev
    row_t = lax.broadcasted_iota(jnp.int32, (chunk, chunk), 0)
    col_s = lax.broadcasted_iota(jnp.int32, (chunk, chunk), 1)
    d_mat = jnp.where(col_s <= row_t, b_col - b_row + li_row, -jnp.inf)
    m_col = jnp.maximum(a_col, jnp.max(d_mat, axis=1, keepdims=True))
    w_intra = jnp.exp(d_mat - m_col)
    w_inter = jnp.exp(a_col - m_col)
    sc = qk * w_intra
    sc_v = jnp.dot(sc.astype(BF16), vb, preferred_element_type=F32)

    b_last = b_col[chunk - 1:chunk, :]
    g_col = b_last - b_col + li_col
    m_new = jnp.maximum(b_last + m_prev, jnp.max(g_col, axis=0, keepdims=True))
    decay = jnp.exp(b_last + m_prev - m_new)
    kw = k * jnp.exp(g_col - m_new)
    kw_v = lax.dot_general(kw.astype(BF16), vb, (((0,), (0,)), ((), ())), preferred_element_type=F32)
    yield

    num = w_inter * q_mem + sc_v
    den = (w_inter * jnp.sum(q * n_st[...], axis=1, keepdims=True)
           + jnp.sum(sc, axis=1, keepdims=True))
    h_out = num / jnp.maximum(jnp.abs(den), jnp.exp(-m_col))
    c_st[...] = decay * c_st[...] + kw_v
    n_st[...] = decay * n_st[...] + jnp.sum(kw, axis=0, keepdims=True)
    m_st[...] = m_new

    hn = h_out * lax.rsqrt(jnp.mean(h_out * h_out, axis=1, keepdims=True) + NORM_EPS) * nw
    return jax.nn.sigmoid(og) * hn


def _mlstm_kernel(qp_ref, kp_ref, v_ref, og_ref, gcol_ref, grow_ref, cwq_ref, cwk_ref, nw_ref,
                  tril_ref, triu_ref, y_ref, c_st, n_st, m_st, qbuf, kbuf, *, chunk, heads, dim):
    halo = SUBLANES

    @pl.when(pl.program_id(1) == 0)
    def _():
        c_st[...] = jnp.zeros_like(c_st)
        n_st[...] = jnp.zeros_like(n_st)
        m_st[...] = jnp.zeros_like(m_st)
        qbuf[0:halo, :] = jnp.zeros((halo, heads * dim), F32)
        kbuf[0:halo, :] = jnp.zeros((halo, heads * dim), F32)

    qbuf[halo:halo + chunk, :] = qp_ref[...]
    kbuf[halo:halo + chunk, :] = kp_ref[...]

    def conv_silu(buf, cw_ref):
        acc = buf[halo:halo + chunk, :] * cw_ref[CONV_WIDTH - 1:CONV_WIDTH, :]
        for j in range(CONV_WIDTH - 1):
            off = halo - (CONV_WIDTH - 1) + j
            acc = acc + buf[off:off + chunk, :] * cw_ref[j:j + 1, :]
        return acc * jax.nn.sigmoid(acc)

    q = conv_silu(qbuf, cwq_ref)
    k = conv_silu(kbuf, cwk_ref) * (1.0 / math.sqrt(dim))
    qbuf[0:halo, :] = qbuf[chunk:chunk + halo, :]
    kbuf[0:halo, :] = kbuf[chunk:chunk + halo, :]

    gcol = gcol_ref[...]
    grow = grow_ref[...]
    cols = [slice(head * dim, (head + 1) * dim) for head in range(heads)]
    ys = _round_robin([
        _mlstm_head(head, q[:, c], k[:, c], v_ref[:, c], og_ref[:, c], gcol, grow, nw_ref[:, c],
                    tril_ref, triu_ref, c_st.at[head], n_st.at[head], m_st.at[head], chunk=chunk)
        for head, c in enumerate(cols)])
    for y, c in zip(ys, cols):
        y_ref[:, c] = y.astype(y_ref.dtype)


def _mlstm(ml_qk, ml_v, ml_o, gates, conv_w, ml_norm_w, chunk=ML_CHUNK):
    b, s, _ = ml_v.shape
    h, d = ML_HEADS, ML_HEAD_DIM
    w = h * d
    gates_t = jnp.swapaxes(gates, 1, 2)
    r = lax.broadcasted_iota(jnp.int32, (chunk, chunk), 0)
    c = lax.broadcasted_iota(jnp.int32, (chunk, chunk), 1)
    tril = (c <= r).astype(BF16)
    triu = (r <= c).astype(BF16)
    wide = lambda off: pl.BlockSpec((None, chunk, w), lambda bi, ci: (bi, ci, off))
    const = lambda shape, off=0: pl.BlockSpec(shape, lambda bi, ci: (0, off))
    kern = functools.partial(_mlstm_kernel, chunk=chunk, heads=h, dim=d)
    return pl.pallas_call(
        kern,
        grid=(b, s // chunk),
        in_specs=[wide(0), wide(1), wide(0), wide(0),
                  pl.BlockSpec((None, chunk, 2 * h), lambda bi, ci: (bi, ci, 0)),
                  pl.BlockSpec((None, 2 * h, chunk), lambda bi, ci: (bi, 0, ci)),
                  const((CONV_WIDTH, w)), const((CONV_WIDTH, w), 1), const((1, w)),
                  const((chunk, chunk)), const((chunk, chunk))],
        out_specs=wide(0),
        out_shape=jax.ShapeDtypeStruct((b, s, w), BF16),
        scratch_shapes=[pltpu.VMEM((h, d, d), F32), pltpu.VMEM((h, 1, d), F32), pltpu.VMEM((h, 1, 1), F32),
                        pltpu.VMEM((chunk + SUBLANES, w), F32), pltpu.VMEM((chunk + SUBLANES, w), F32)],
        compiler_params=_cparams("arbitrary", "arbitrary"),
        name="mlstm",
    )(ml_qk, ml_qk, ml_v, ml_o, gates, gates_t, conv_w, conv_w, ml_norm_w.reshape(1, w), tril, triu)


def _out_proj_kernel(ya_ref, yb_ref, wa_ref, wb_ref, x_ref, g_ref, o_ref):
    mix = (jnp.dot(ya_ref[...], wa_ref[...], preferred_element_type=F32)
           + jnp.dot(yb_ref[...], wb_ref[...], preferred_element_type=F32))
    o_ref[...] = x_ref[...] + g_ref[...] * mix


def _out_proj(y_sb, y_ml, w_out, x, gate, tm, tn=1024):
    b, s, d = x.shape
    ka, kb = y_sb.shape[-1], y_ml.shape[-1]
    assert ka == kb
    n = b * s
    rows_per_seq = s // tm
    f32_tile = tm * tn * 4
    vmem = min(2 * (tm + tn) * (ka + kb) * 2 + 4 * f32_tile + 3 * f32_tile, VMEM_CAPACITY - VMEM_RESERVE)
    out = pl.pallas_call(
        _out_proj_kernel,
        grid=(n // tm, d // tn),
        in_specs=[pl.BlockSpec((tm, ka), lambda i, j: (i, 0)),
                  pl.BlockSpec((tm, kb), lambda i, j: (i, 0)),
                  pl.BlockSpec((ka, tn), lambda i, j: (0, j)),
                  pl.BlockSpec((kb, tn), lambda i, j: (1, j)),
                  pl.BlockSpec((tm, tn), lambda i, j: (i, j)),
                  pl.BlockSpec((None, 1, tn), lambda i, j: (i // rows_per_seq, 0, j))],
        out_specs=pl.BlockSpec((tm, tn), lambda i, j: (i, j)),
        out_shape=jax.ShapeDtypeStruct((n, d), F32),
        compiler_params=_cparams("arbitrary", "arbitrary", vmem=vmem),
        name="out_proj",
    )(y_sb.reshape(n, ka), y_ml.reshape(n, kb), w_out, w_out, x.reshape(n, d), gate.reshape(b, 1, d))
    return out.reshape(b, s, d)


def _router_kernel(h_ref, nw_ref, sh_ref, sc_ref, wr_ref, br_ref, tri_ref, u_ref, ri_ref, rw_ref, cnt_ref, run_ref):
    first = jnp.logical_and(pl.program_id(0) == 0, pl.program_id(1) == 0)

    @pl.when(first)
    def _():
        run_ref[...] = jnp.zeros_like(run_ref)

    u = _rms_mod(h_ref[...], nw_ref[...], sh_ref[...], sc_ref[...])
    tm = u.shape[0]
    uh, ul = _split_bf16(u)
    u_ref[...] = _pack_bf16_pairs(uh)
    wh, wl = _split_bf16(wr_ref[...])
    logits = (jnp.dot(uh, wh, preferred_element_type=F32) + jnp.dot(ul, wh, preferred_element_type=F32)
              + jnp.dot(uh, wl, preferred_element_type=F32) + br_ref[...])

    lane = lax.broadcasted_iota(jnp.int32, (tm, LANES), 1)
    lane_f = lane.astype(F32)
    neg = -jnp.inf
    first_lane = lambda hit: jnp.min(jnp.where(hit, lane_f, float(LANES)), axis=1, keepdims=True)

    gl = jnp.where(lane < N_GROUPS, logits, neg)
    g_max = jnp.max(gl, axis=1, keepdims=True)
    g_p = 1.0 / jnp.sum(jnp.exp(gl - g_max), axis=1, keepdims=True)
    g_sel = first_lane(gl == g_max)

    lo_lane = N_GROUPS + EXPERTS_PER_GROUP * g_sel
    el = jnp.where(jnp.logical_and(lane_f >= lo_lane, lane_f < lo_lane + EXPERTS_PER_GROUP), logits, neg)
    e1_max = jnp.max(el, axis=1, keepdims=True)
    l1 = first_lane(el == e1_max)
    el2 = jnp.where(lane_f == l1, neg, el)
    e2_max = jnp.max(el2, axis=1, keepdims=True)
    l2 = first_lane(el2 == e2_max)
    z_sum = jnp.sum(jnp.exp(el - e1_max), axis=1, keepdims=True)
    p1 = 1.0 / z_sum
    p2 = jnp.exp(e2_max - e1_max) / z_sum
    w1 = p1 / (p1 + p2) * g_p
    w2 = p2 / (p1 + p2) * g_p
    e1 = l1 - N_GROUPS
    e2 = l2 - N_GROUPS

    hit1 = lane_f == e1
    hit2 = lane_f == e2
    onehot = jnp.where(jnp.logical_or(hit1, hit2), 1.0, 0.0)
    before = jnp.dot(tri_ref[...], onehot.astype(BF16), preferred_element_type=F32) + run_ref[...]
    rank1 = jnp.sum(jnp.where(hit1, before, 0.0), axis=1, keepdims=True)
    rank2 = jnp.sum(jnp.where(hit2, before, 0.0), axis=1, keepdims=True)
    run = run_ref[...] + jnp.sum(onehot, axis=0, keepdims=True)
    run_ref[...] = run

    ri_ref[...] = jnp.where(lane == 0, e1, jnp.where(lane == 1, e2, jnp.where(lane == 2, rank1,
                            jnp.where(lane == 3, rank2, 0.0)))).astype(jnp.int32)
    rw_ref[...] = jnp.where(lane == 0, w1, jnp.where(lane == 1, w2, 0.0))
    cnt_ref[...] = jnp.broadcast_to(run, cnt_ref.shape).astype(jnp.int32)


def _router(h, norm_w, shift, scale, w_router, b_router, tm=512):
    b, s, d = h.shape
    n = b * s
    tri = (lax.broadcasted_iota(jnp.int32, (tm, tm), 1) < lax.broadcasted_iota(jnp.int32, (tm, tm), 0)).astype(BF16)
    vec = pl.BlockSpec((None, 1, d), lambda bi, i: (bi, 0, 0))
    rows = lambda width: pl.BlockSpec((tm, width), lambda bi, i: (bi * (s // tm) + i, 0))
    return pl.pallas_call(
        _router_kernel,
        grid=(b, s // tm),
        in_specs=[pl.BlockSpec((None, tm, d), lambda bi, i: (bi, i, 0)),
                  pl.BlockSpec((1, d), lambda bi, i: (0, 0)), vec, vec,
                  pl.BlockSpec((d, LANES), lambda bi, i: (0, 0)),
                  pl.BlockSpec((1, LANES), lambda bi, i: (0, 0)),
                  pl.BlockSpec((tm, tm), lambda bi, i: (0, 0))],
        out_specs=[rows(d // 2), rows(LANES), rows(LANES),
                   pl.BlockSpec((SUBLANES, LANES), lambda bi, i: (0, 0))],
        out_shape=[jax.ShapeDtypeStruct((n, d // 2), jnp.int32), jax.ShapeDtypeStruct((n, LANES), jnp.int32),
                   jax.ShapeDtypeStruct((n, LANES), F32), jax.ShapeDtypeStruct((SUBLANES, LANES), jnp.int32)],
        scratch_shapes=[pltpu.VMEM((1, LANES), F32)],
        compiler_params=_cparams("arbitrary", "arbitrary"),
        name="router",
    )(h, norm_w.reshape(1, d), shift.reshape(b, 1, d), scale.reshape(b, 1, d), w_router, b_router, tri)


ROW_LOOP_UNROLL = 8
ZERO_ROWS = 128


def _dispatch_kernel(tail_ref, dest_ref, u_ref, xb_hbm, zbuf, zsem, sem, *, tokens, block):
    n_exp = tail_ref.shape[0]

    def clear(e, half):
        row = pl.multiple_of(tail_ref[e] + half * ZERO_ROWS, ZERO_ROWS)
        return pltpu.make_async_copy(zbuf, xb_hbm.at[pl.ds(row, ZERO_ROWS)], zsem)

    @pl.when(pl.program_id(0) == 0)
    def _():
        zbuf[...] = jnp.zeros_like(zbuf)
        for start in (True, False):
            def each(e, carry):
                @pl.when(tail_ref[e] >= 0)
                def _():
                    for half in range(block // ZERO_ROWS):
                        clear(e, half).start() if start else clear(e, half).wait()
                return carry
            lax.fori_loop(0, n_exp, each, 0)

    def row_copy(t, k):
        return pltpu.make_async_copy(u_ref.at[pl.ds(t, 1)], xb_hbm.at[pl.ds(dest_ref[0, TOP_K * t + k], 1)], sem)

    def issue(t, carry):
        for k in range(TOP_K):
            row_copy(t, k).start(priority=k % 2)
        return carry

    lax.fori_loop(0, tokens, issue, 0, unroll=ROW_LOOP_UNROLL)

    def drain(t, carry):
        for k in range(TOP_K):
            row_copy(t, k).wait()
        return carry

    lax.fori_loop(0, tokens, drain, 0, unroll=ROW_LOOP_UNROLL)


def _dispatch(u, dest, tail_block_row, cap, tokens=512, block=MOE_BLOCK):
    n, d = u.shape
    steps = n // tokens
    kern = functools.partial(_dispatch_kernel, tokens=tokens, block=block)
    return pl.pallas_call(
        kern,
        grid_spec=pltpu.PrefetchScalarGridSpec(
            num_scalar_prefetch=1,
            grid=(steps,),
            in_specs=[pl.BlockSpec((None, 1, TOP_K * tokens), lambda i, tail: (i, 0, 0), memory_space=pltpu.SMEM),
                      pl.BlockSpec((tokens, d), lambda i, tail: (i, 0))],
            out_specs=pl.BlockSpec(memory_space=pl.ANY),
            scratch_shapes=[pltpu.VMEM((ZERO_ROWS, d), u.dtype), pltpu.SemaphoreType.DMA(()),
                            pltpu.SemaphoreType.DMA(())]),
        out_shape=jax.ShapeDtypeStruct((cap, d), u.dtype),
        compiler_params=_cparams("arbitrary"),
        name="dispatch",
    )(tail_block_row, dest.reshape(steps, 1, TOP_K * tokens), u)


def _lookup(table, idx):
    hit = idx[:, None] == jnp.arange(table.shape[0], dtype=idx.dtype)[None, :]
    return jnp.sum(jnp.where(hit, table[None, :], 0), axis=1).astype(table.dtype)


def _run_plan(keys, ok):
    steps = ok.shape[0]
    idx = jnp.arange(steps, dtype=jnp.int32)
    changed = idx == 0
    for key in keys:
        changed = jnp.logical_or(changed, key != jnp.roll(key, 1))
    first = jnp.logical_and(changed, ok)
    slot = (jnp.cumsum(first.astype(jnp.int32)) - 1) % 2
    starts_from = lax.cummin(jnp.where(first, idx, steps), axis=0, reverse=True)
    nxt = jnp.concatenate([starts_from[1:], jnp.full((1,), steps, jnp.int32)])
    more = nxt < steps
    as_i32 = lambda a: a.astype(jnp.int32)
    return as_i32(first), as_i32(slot), as_i32(more), as_i32(jnp.minimum(nxt, steps - 1))


def _run_weights(step, first_ref, slot_ref, more_ref, fetch_cur, fetch_next):
    slot = slot_ref[step]

    @pl.when(step == 0)
    def _():
        for part, copy in enumerate(fetch_cur(slot)):
            copy.start(priority=part % 2)

    @pl.when(first_ref[step] == 1)
    def _():
        @pl.when(more_ref[step] == 1)
        def _():
            for part, copy in enumerate(fetch_next(1 - slot)):
                copy.start(priority=part % 2)

        for copy in fetch_cur(slot):
            copy.wait()

    return slot


def _e1_kernel(blk_ref, ex_ref, ft_ref, oblk_ref, oft_ref, ok_ref, first_ref, slot_ref, more_ref, nex_ref, nft_ref,
               x_ref, wg_hbm, wu_hbm, o_ref, wg_buf, wu_buf, sems, *, tf):
    del blk_ref, oblk_ref, oft_ref
    step = pl.program_id(0)

    def fetch(e, f, slot):
        cols = pl.ds(pl.multiple_of(f * tf, tf), tf)
        rows = wg_buf.shape[1] // WEIGHT_DMA_PARTS
        return [pltpu.make_async_copy(hbm.at[e, pl.ds(p * rows, rows), cols],
                                      buf.at[slot, pl.ds(p * rows, rows)], sems.at[slot, which])
                for p in range(WEIGHT_DMA_PARTS)
                for which, (hbm, buf) in enumerate(((wg_hbm, wg_buf), (wu_hbm, wu_buf)))]

    slot = _run_weights(step, first_ref, slot_ref, more_ref,
                        lambda sl: fetch(ex_ref[step], ft_ref[step], sl),
                        lambda sl: fetch(nex_ref[step], nft_ref[step], sl))
    ok = ok_ref[step] == 1

    @pl.when(ok)
    def _():
        x_lo, x_hi = _unpack_bf16_pairs(x_ref[...])
        half = x_lo.shape[1]
        mm = lambda w_buf: (jnp.dot(x_lo, w_buf[slot, :half, :], preferred_element_type=F32)
                            + jnp.dot(x_hi, w_buf[slot, half:, :], preferred_element_type=F32))
        g = mm(wg_buf)
        up = mm(wu_buf)
        o_ref[...] = ((g * jax.nn.sigmoid(g)) * up).astype(o_ref.dtype)

    @pl.when(jnp.logical_not(ok))
    def _():
        o_ref[...] = jnp.zeros_like(o_ref)


def _e2_kernel(blk_ref, ex_ref, oblk_ref, ok_ref, first_ref, slot_ref, more_ref, nex_ref,
               h_ref, wd_hbm, o_ref, wd_buf, sems):
    del blk_ref, oblk_ref
    step = pl.program_id(0)
    rows = wd_buf.shape[1] // WEIGHT_DMA_PARTS
    fetch = lambda e, slot: [pltpu.make_async_copy(wd_hbm.at[e, pl.ds(p * rows, rows)],
                                                   wd_buf.at[slot, pl.ds(p * rows, rows)], sems.at[slot])
                             for p in range(WEIGHT_DMA_PARTS)]
    slot = _run_weights(step, first_ref, slot_ref, more_ref,
                        lambda sl: fetch(ex_ref[step], sl), lambda sl: fetch(nex_ref[step], sl))
    ok = ok_ref[step] == 1

    @pl.when(ok)
    def _():
        o_ref[...] = jnp.dot(h_ref[...].astype(F32), wd_buf[slot], preferred_element_type=F32)

    @pl.when(jnp.logical_not(ok))
    def _():
        o_ref[...] = jnp.zeros_like(o_ref)


def _experts(xb, blocks_per_expert, w_gate, w_up, w_down, bm=MOE_BLOCK, tf=512):
    cap, dp = xb.shape
    n_exp, d, dff = w_gate.shape
    n_blocks = cap // bm
    n_ft = dff // tf
    nb = blocks_per_expert.astype(jnp.int32)
    bstart = jnp.cumsum(nb) - nb
    used = jnp.sum(nb)

    steps1 = n_blocks * n_ft
    step = jnp.arange(steps1, dtype=jnp.int32)
    ok1 = step < used * n_ft
    send = jnp.cumsum(nb * n_ft)
    s_idx = jnp.minimum(step, used * n_ft - 1)
    e_of = jnp.minimum(jnp.sum(s_idx[:, None] >= send[None, :], axis=1), n_exp - 1).astype(jnp.int32)
    nb_of = _lookup(nb, e_of)
    local = s_idx - (_lookup(send, e_of) - nb_of * n_ft)
    nbe = jnp.maximum(nb_of, 1)
    ft_of = (local // nbe).astype(jnp.int32)
    blk_of = (_lookup(bstart, e_of) + local % nbe).astype(jnp.int32)
    spare = step - used * n_ft
    oblk_of = jnp.where(ok1, blk_of, used + spare // n_ft).astype(jnp.int32)
    oft_of = jnp.where(ok1, ft_of, spare % n_ft).astype(jnp.int32)

    first1, slot1, more1, nxt1 = _run_plan([e_of, ft_of], ok1)
    hmid = pl.pallas_call(
        functools.partial(_e1_kernel, tf=tf),
        grid_spec=pltpu.PrefetchScalarGridSpec(
            num_scalar_prefetch=11,
            grid=(steps1,),
            in_specs=[pl.BlockSpec((bm, dp), lambda s, blk, *_: (blk[s], 0)),
                      pl.BlockSpec(memory_space=pl.ANY),
                      pl.BlockSpec(memory_space=pl.ANY)],
            out_specs=pl.BlockSpec((bm, tf), lambda s, blk, ex, ft, oblk, oft, *_: (oblk[s], oft[s])),
            scratch_shapes=[pltpu.VMEM((2, d, tf), F32), pltpu.VMEM((2, d, tf), F32),
                            pltpu.SemaphoreType.DMA((2, 2))]),
        out_shape=jax.ShapeDtypeStruct((cap, dff), BF16),
        compiler_params=_cparams("arbitrary"),
        name="e1",
    )(blk_of, e_of, ft_of, oblk_of, oft_of, ok1.astype(jnp.int32), first1, slot1, more1, _lookup(e_of, nxt1), _lookup(ft_of, nxt1),
      xb, w_gate, w_up)

    blk = jnp.arange(n_blocks, dtype=jnp.int32)
    b_idx = jnp.minimum(blk, used - 1)
    bend = jnp.cumsum(nb)
    e2_of = jnp.minimum(jnp.sum(b_idx[:, None] >= bend[None, :], axis=1), n_exp - 1).astype(jnp.int32)
    ok2 = blk < used
    first2, slot2, more2, nxt2 = _run_plan([e2_of], ok2)
    yb = pl.pallas_call(
        _e2_kernel,
        grid_spec=pltpu.PrefetchScalarGridSpec(
            num_scalar_prefetch=8,
            grid=(n_blocks,),
            in_specs=[pl.BlockSpec((bm, dff), lambda s, hb, *_: (hb[s], 0)),
                      pl.BlockSpec(memory_space=pl.ANY)],
            out_specs=pl.BlockSpec((bm, d), lambda s, hb, ex, ob, *_: (ob[s], 0)),
            scratch_shapes=[pltpu.VMEM((2, dff, d), F32), pltpu.SemaphoreType.DMA((2,))]),
        out_shape=jax.ShapeDtypeStruct((cap, d), F32),
        compiler_params=_cparams("arbitrary"),
        name="e2",
    )(b_idx, e2_of, blk, ok2.astype(jnp.int32), first2, slot2, more2, _lookup(e2_of, nxt2), hmid, w_down)
    return yb


def _combine_kernel(dest_ref, dest_next_ref, yb_hbm, h_ref, rw_ref, g_ref, nw_ref, sh_ref, sc_ref, o_ref,
                    rows, sems, *, tokens, steps):
    step = pl.program_id(0)
    slot = step % 2

    def row_copy(d_ref, buf, t, k):
        return pltpu.make_async_copy(yb_hbm.at[pl.ds(d_ref[0, TOP_K * t + k], 1)],
                                     rows.at[buf, k, pl.ds(t, 1)], sems.at[buf])

    def gather(d_ref, buf):
        def issue(t, carry):
            for k in range(TOP_K):
                row_copy(d_ref, buf, t, k).start(priority=k % 2)
            return carry

        lax.fori_loop(0, tokens, issue, 0, unroll=ROW_LOOP_UNROLL)

    @pl.when(step == 0)
    def _():
        gather(dest_ref, 0)

    @pl.when(step + 1 < steps)
    def _():
        gather(dest_next_ref, 1 - slot)

    def drain(t, carry):
        for k in range(TOP_K):
            row_copy(dest_ref, slot, t, k).wait()
        return carry

    lax.fori_loop(0, tokens, drain, 0, unroll=ROW_LOOP_UNROLL)

    rw = rw_ref[...]
    ffn = rows[slot, 0] * rw[:, 0:1]
    for k in range(1, TOP_K):
        ffn = ffn + rows[slot, k] * rw[:, k:k + 1]
    h2 = h_ref[...] + g_ref[...] * ffn
    o_ref[...] = _rms_mod(h2, nw_ref[...], sh_ref[...], sc_ref[...])


def _combine(yb, dest, h, route_w, gate, norm_w, shift, scale, tokens=256):
    b, s, d = h.shape
    n = b * s
    per_seq = s // tokens
    steps = n // tokens
    kern = functools.partial(_combine_kernel, tokens=tokens, steps=steps)
    vec = pl.BlockSpec((None, 1, d), lambda i: (i // per_seq, 0, 0))
    dest_blocks = dest.reshape(steps, 1, TOP_K * tokens)
    dest_spec = lambda ahead: pl.BlockSpec((None, 1, TOP_K * tokens),
                                           lambda i: (jnp.minimum(i + ahead, steps - 1), 0, 0),
                                           memory_space=pltpu.SMEM)
    out = pl.pallas_call(
        kern,
        grid=(steps,),
        in_specs=[dest_spec(0), dest_spec(1),
                  pl.BlockSpec(memory_space=pl.ANY),
                  pl.BlockSpec((tokens, d), lambda i: (i, 0)),
                  pl.BlockSpec((tokens, LANES), lambda i: (i, 0)),
                  vec,
                  pl.BlockSpec((1, d), lambda i: (0, 0)), vec, vec],
        out_specs=pl.BlockSpec((tokens, d), lambda i: (i, 0)),
        out_shape=jax.ShapeDtypeStruct((n, d), F32),
        scratch_shapes=[pltpu.VMEM((2, TOP_K, tokens, d), F32), pltpu.SemaphoreType.DMA((2,))],
        compiler_params=_cparams("arbitrary"),
        name="combine",
    )(dest_blocks, dest_blocks, yb, h.reshape(n, d), route_w,
      gate.reshape(b, 1, d), norm_w.reshape(1, d), shift.reshape(b, 1, d), scale.reshape(b, 1, d))
    return out.reshape(b, s, d)


def _layer(h, mod, norm1_w, w_in, conv_w, ml_gate_bias, ml_norm_w, w_out, norm2_w,
           w_router_group, b_router_group, w_router_expert, b_router_expert,
           w_exp_gate, w_exp_up, w_exp_down, final):
    b, s, d = h.shape
    n = b * s
    sh1, sc1, g1, sh2, sc2, g2 = jnp.split(mod, 6, axis=-1)
    sb_w = SB_HEADS * SB_HEAD_DIM
    ml_w = ML_HEADS * ML_HEAD_DIM

    u = _norm_mod(h, norm1_w, sh1, sc1, BF16).reshape(n, d)
    tm = 1024 if n % 1024 == 0 else 512
    w_in_t = w_in.T
    sb_qkv = _matmul(u, w_in_t, 0, 3 * sb_w, BF16, tm, PROJ_TN)
    ml_qk = _matmul(u, w_in_t, 3 * sb_w, 2 * ml_w, F32, tm, PROJ_TN)
    ml_v = _matmul(u, w_in_t, 3 * sb_w + 2 * ml_w, ml_w, BF16, tm, PROJ_TN)
    ml_o = _matmul(u, w_in_t, 3 * sb_w + 3 * ml_w, ml_w, F32, tm, PROJ_TN)
    n_gate = 2 * ML_HEADS
    w_gate = jnp.pad(w_in_t[3 * sb_w + 4 * ml_w:, :], ((0, LANES - n_gate), (0, 0)))
    b_gate = jnp.pad(ml_gate_bias, (0, LANES - n_gate)).reshape(1, LANES)
    gates = _matmul(u, w_gate, 0, LANES, F32, tm, LANES, bias=b_gate)[:, :n_gate]

    y_sb = _sb_attention(sb_qkv.reshape(b, s, 3 * sb_w))
    y_ml = _mlstm(ml_qk.reshape(b, s, 2 * ml_w), ml_v.reshape(b, s, ml_w), ml_o.reshape(b, s, ml_w),
                  gates.reshape(b, s, n_gate), conv_w, ml_norm_w)
    h = _out_proj(y_sb, y_ml, w_out.astype(BF16), h, g1, tm=min(1024, s))

    w_router = jnp.pad(jnp.concatenate([w_router_group, w_router_expert], axis=1),
                       ((0, 0), (0, LANES - N_GROUPS - N_EXPERTS)))
    b_router = jnp.pad(jnp.concatenate([b_router_group, b_router_expert]),
                       (0, LANES - N_GROUPS - N_EXPERTS)).reshape(1, LANES)
    u2, route_i, route_w, counts = _router(h, norm2_w, sh2, sc2, w_router, b_router)
    counts = counts[0, :N_EXPERTS]
    blocks_per_expert = (counts + MOE_BLOCK - 1) // MOE_BLOCK
    padded = blocks_per_expert * MOE_BLOCK
    pstarts = jnp.cumsum(padded) - padded
    dest = (_lookup(pstarts, route_i[:, 0:TOP_K].reshape(-1)) + route_i[:, TOP_K:2 * TOP_K].reshape(-1))
    cap = (-(-(n * TOP_K) // MOE_BLOCK) + N_EXPERTS) * MOE_BLOCK
    tail_block_row = jnp.where(counts % MOE_BLOCK != 0, pstarts + padded - MOE_BLOCK, -1)
    spare_row = jnp.sum(padded) + MOE_BLOCK * jnp.arange(N_EXPERTS, dtype=jnp.int32)
    spare_row = jnp.where(spare_row < cap, spare_row, -1)
    clear_rows = jnp.concatenate([tail_block_row, spare_row]).astype(jnp.int32)
    xb = _dispatch(u2, dest, clear_rows, cap)
    yb = _experts(xb, blocks_per_expert, w_exp_gate, w_exp_up, w_exp_down)
    norm_w, shift, scale = final
    return _combine(yb, dest, h, route_w, g2, norm_w, shift, scale)


def kernel(x, c, norm1_w, w_in, conv_w, ml_gate_bias, ml_norm_w, w_out, norm2_w, w_router_group, b_router_group, w_router_expert, b_router_expert, w_exp_gate, w_exp_up, w_exp_down, w_ada, b_ada, final_norm_w, w_ada_final, b_ada_final):
    b, s, d = x.shape
    depth = w_in.shape[0]
    assert depth == 1, "the final norm is fused into the (single) layer's combine step"
    c_rep = jnp.broadcast_to(c[:, :, None], (b, d, LANES))
    fmod = _ada(c_rep, w_ada_final, b_ada_final)
    sh_f, sc_f = jnp.split(fmod, 2, axis=-1)
    mod = _ada(c_rep, w_ada[0], b_ada[0])
    return _layer(x, mod, norm1_w[0], w_in[0], conv_w[0], ml_gate_bias[0], ml_norm_w[0], w_out[0], norm2_w[0],
                  w_router_group[0], b_router_group[0], w_router_expert[0], b_router_expert[0],
                  w_exp_gate[0], w_exp_up[0], w_exp_down[0], (final_norm_w, sh_f, sc_f))
```

```python
import functools
import math

import jax
import jax.numpy as jnp
from jax import lax
from jax.experimental import pallas as pl
from jax.experimental.pallas import tpu as pltpu

SB_HEADS = 16
SB_HEAD_DIM = 128
ML_HEADS = 4
ML_HEAD_DIM = 512
CONV_WIDTH = 4
N_GROUPS = 4
EXPERTS_PER_GROUP = 8
N_EXPERTS = N_GROUPS * EXPERTS_PER_GROUP
TOP_K = 2
NORM_EPS = 1e-6

LANES = 128
SUBLANES = 8
VMEM_CAPACITY = 64 * 1024 * 1024
VMEM_RESERVE = 4 * 1024 * 1024
VMEM_LIMIT = 56 * 1024 * 1024

ATTN_SUB = 128
ATTN_SUBS_PER_STEP = 16
ATTN_ALWAYS = 3
ML_CHUNK = 256
MOE_BLOCK = 256
PROJ_TN = 1024
WEIGHT_DMA_PARTS = 4
COMBINE_CHUNK = 32
UNDERFLOW_LOG = -90.0

F32 = jnp.float32
BF16 = jnp.bfloat16


def _cparams(*sem, vmem=VMEM_LIMIT):
    return pltpu.CompilerParams(dimension_semantics=sem, vmem_limit_bytes=vmem)


def _split_bf16(x):
    hi = x.astype(BF16)
    lo = (x - hi.astype(F32)).astype(BF16)
    return hi, lo


_HIGH_HALF = -65536


def _pack_bf16_pairs(x):
    k = x.shape[1] // 2
    bits = lambda v: lax.bitcast_convert_type(v.astype(F32), jnp.int32)
    return lax.shift_right_logical(bits(x[:, :k]), 16) | (bits(x[:, k:]) & _HIGH_HALF)


def _unpack_bf16_pairs(w):
    lo = lax.bitcast_convert_type(lax.shift_left(w, 16), F32)
    hi = lax.bitcast_convert_type(w & _HIGH_HALF, F32)
    return lo, hi


def _log_sigmoid(x):
    return jnp.minimum(x, 0.0) - jnp.log1p(jnp.exp(-jnp.abs(x)))


def _ada_kernel(c_ref, w_ref, b_ref, o_ref):
    nb, tn = o_ref.shape
    rows = []
    for b in range(nb):
        cb = c_ref[b]
        cb = cb * jax.nn.sigmoid(cb)
        pieces = [jnp.sum(w_ref[:, j * LANES:(j + 1) * LANES] * cb, axis=0, keepdims=True)
                  for j in range(tn // LANES)]
        rows.append(jnp.concatenate(pieces, axis=1))
    o_ref[...] = jnp.concatenate(rows, axis=0) + b_ref[...]


def _ada(c_rep, w, bias, tn=1024):
    nb, k, _ = c_rep.shape
    n_out = w.shape[1]
    return pl.pallas_call(
        _ada_kernel,
        grid=(n_out // tn,),
        in_specs=[pl.BlockSpec((nb, k, LANES), lambda j: (0, 0, 0)),
                  pl.BlockSpec((k, tn), lambda j: (0, j)),
                  pl.BlockSpec((1, tn), lambda j: (0, j))],
        out_specs=pl.BlockSpec((nb, tn), lambda j: (0, j)),
        out_shape=jax.ShapeDtypeStruct((nb, n_out), F32),
        compiler_params=_cparams("arbitrary"),
        name="ada",
    )(c_rep, w, bias.reshape(1, n_out))


def _rms_mod(x, w, shift, scale):
    var = jnp.mean(x * x, axis=-1, keepdims=True)
    y = x * lax.rsqrt(var + NORM_EPS) * w
    return y * (1.0 + scale) + shift


def _norm_mod_kernel(x_ref, w_ref, sh_ref, sc_ref, o_ref):
    o_ref[...] = _rms_mod(x_ref[...], w_ref[...], sh_ref[...], sc_ref[...]).astype(o_ref.dtype)


def _norm_mod(x, w, shift, scale, out_dtype, tm=512):
    b, s, d = x.shape
    vec = pl.BlockSpec((None, 1, d), lambda bi, i: (bi, 0, 0))
    return pl.pallas_call(
        _norm_mod_kernel,
        grid=(b, s // tm),
        in_specs=[pl.BlockSpec((None, tm, d), lambda bi, i: (bi, i, 0)),
                  pl.BlockSpec((1, d), lambda bi, i: (0, 0)), vec, vec],
        out_specs=pl.BlockSpec((None, tm, d), lambda bi, i: (bi, i, 0)),
        out_shape=jax.ShapeDtypeStruct((b, s, d), out_dtype),
        compiler_params=_cparams("arbitrary", "arbitrary"),
        name="norm_mod",
    )(x, w.reshape(1, d), shift.reshape(b, 1, d), scale.reshape(b, 1, d))


def _matmul_kernel(a_ref, wt_hbm, *rest, has_bias, row0, n_col_tiles):
    b_ref, o_ref, stage, w16, sem = rest if has_bias else (None,) + rest
    j = pl.program_id(0)
    tn = stage.shape[0]
    rows = tn // WEIGHT_DMA_PARTS

    def fetch(col_tile):
        first = row0 + col_tile * tn
        return [pltpu.make_async_copy(wt_hbm.at[pl.ds(pl.multiple_of(first + p * rows, rows), rows), :],
                                      stage.at[pl.ds(p * rows, rows), :], sem)
                for p in range(WEIGHT_DMA_PARTS)]

    @pl.when(pl.program_id(1) == 0)
    def _():
        @pl.when(j == 0)
        def _():
            for part, copy in enumerate(fetch(0)):
                copy.start(priority=part % 2)

        for copy in fetch(j):
            copy.wait()
        chunk = min(tn, 2 * LANES)
        for c in range(0, tn, chunk):
            w16[:, c:c + chunk] = stage[c:c + chunk, :].T.astype(BF16)

        @pl.when(j + 1 < n_col_tiles)
        def _():
            for part, copy in enumerate(fetch(j + 1)):
                copy.start(priority=part % 2)

    acc = jnp.dot(a_ref[...], w16[...], preferred_element_type=F32)
    if has_bias:
        acc = acc + b_ref[...]
    o_ref[...] = acc.astype(o_ref.dtype)


def _matmul(a, w_t, col0, ncols, out_dtype, tm, tn, bias=None):
    m, k = a.shape
    f32_tile = tm * tn * 4
    vmem = min(tn * k * 4 + k * tn * 2 + 2 * tm * k * 2 + 2 * f32_tile + 3 * f32_tile, VMEM_CAPACITY - VMEM_RESERVE)
    in_specs = [pl.BlockSpec((tm, k), lambda j, i: (i, 0)),
                pl.BlockSpec(memory_space=pl.ANY)]
    args = [a, w_t]
    if bias is not None:
        in_specs.append(pl.BlockSpec((1, tn), lambda j, i: (0, j)))
        args.append(bias)
    return pl.pallas_call(
        functools.partial(_matmul_kernel, has_bias=bias is not None, row0=col0, n_col_tiles=ncols // tn),
        grid=(ncols // tn, m // tm),
        in_specs=in_specs,
        out_specs=pl.BlockSpec((tm, tn), lambda j, i: (i, j)),
        out_shape=jax.ShapeDtypeStruct((m, ncols), out_dtype),
        scratch_shapes=[pltpu.VMEM((tn, k), F32), pltpu.VMEM((k, tn), BF16), pltpu.SemaphoreType.DMA(())],
        compiler_params=_cparams("arbitrary", "arbitrary", vmem=vmem),
        name="proj",
    )(*args)


def _sb_attn_kernel(q_ref, k_ref, v_ref, tri_ref, o_ref, acc_ref, r_ref, *, sub, n_sub, scale):
    first = pl.program_id(2) * n_sub
    tri = tri_ref[...]
    row = lax.broadcasted_iota(jnp.int32, (sub, sub), 0)
    col = lax.broadcasted_iota(jnp.int32, (sub, sub), 1)
    causal = col < row

    def visit(q, blk, r):
        start = pl.multiple_of(blk * sub, sub)
        kb = k_ref[pl.ds(start, sub), :]
        vb = v_ref[pl.ds(start, sub), :]
        z = lax.dot_general(q, kb, (((1,), (1,)), ((), ())), preferred_element_type=F32) * scale
        log_keep = -(jnp.maximum(z, 0.0) + jnp.log(1.0 + jnp.exp(-jnp.abs(z))))
        log_beta = log_keep + z
        hi, lo = _split_bf16(log_keep)
        between = (jnp.dot(hi, tri, preferred_element_type=F32)
                   + jnp.dot(lo, tri, preferred_element_type=F32))
        w = jnp.exp(log_beta + between + r)
        pv = jnp.dot(w.astype(BF16), vb, preferred_element_type=F32)
        return pv, r + jnp.sum(log_keep, axis=1, keepdims=True)

    n_win = ATTN_ALWAYS
    roww = lax.broadcasted_iota(jnp.int32, (sub, n_win * sub), 0)
    colw = lax.broadcasted_iota(jnp.int32, (sub, n_win * sub), 1)

    def win_start(s):
        blk = first + s
        clamp = s < n_win - 1
        first_blk = jnp.maximum(blk - (n_win - 1), 0) if clamp else blk - (n_win - 1)
        return pl.multiple_of(first_blk * sub, sub), clamp

    def masked(s, x):
        start, clamp = win_start(s)
        if clamp:
            return jnp.where(start + colw < (first + s) * sub + roww, x, 0.0)
        last = (n_win - 1) * sub
        return jnp.concatenate([x[:, :last], jnp.where(causal, x[:, last:], 0.0)], axis=1)

    scores = []
    for s in range(n_sub):
        kw = k_ref[pl.ds(win_start(s)[0], n_win * sub), :]
        q = q_ref[s * sub:(s + 1) * sub, :]
        scores.append(lax.dot_general(q, kw, (((1,), (1,)), ((), ())), preferred_element_type=F32) * scale)

    staged = []
    for s in range(n_sub):
        z = scores[s]
        log_keep = -(jnp.maximum(z, 0.0) + jnp.log(1.0 + jnp.exp(-jnp.abs(z))))
        log_beta = log_keep + z
        log_keep = masked(s, log_keep)
        chunks = [log_keep[:, c * sub:(c + 1) * sub] for c in range(n_win)]
        hi, lo = _split_bf16(jnp.concatenate(chunks, axis=0))
        inside = (jnp.dot(hi, tri, preferred_element_type=F32)
                  + jnp.dot(lo, tri, preferred_element_type=F32))
        staged.append((log_beta, chunks, inside))

    r_top = None
    for s in range(n_sub):
        log_beta, chunks, inside = staged[s]
        later = 0.0
        between = [None] * n_win
        for c in reversed(range(n_win)):
            between[c] = inside[c * sub:(c + 1) * sub, :] + later
            later = later + jnp.sum(chunks[c], axis=1, keepdims=True)
        w = masked(s, jnp.exp(log_beta + jnp.concatenate(between, axis=1)))
        vw = v_ref[pl.ds(win_start(s)[0], n_win * sub), :]
        acc_ref[s] = jnp.dot(w.astype(BF16), vw, preferred_element_type=F32)
        r = jnp.broadcast_to(later, (sub, LANES))
        r_ref[s] = r
        r_top = r if r_top is None else jnp.maximum(r_top, r)

    @pl.when(jnp.max(r_top) > UNDERFLOW_LOG)
    def _():
        for s in range(n_sub):
            q = q_ref[s * sub:(s + 1) * sub, :]

            def cond(carry):
                j, r_max = carry
                return jnp.logical_and(j >= 0, r_max > UNDERFLOW_LOG)

            def body(carry, s=s, q=q):
                j, _ = carry
                pv, r = visit(q, j, r_ref[s])
                acc_ref[s] += pv
                r_ref[s] = r
                return j - 1, jnp.max(r)

            lax.while_loop(cond, body, (first + s - ATTN_ALWAYS, jnp.max(r_ref[s])))

    for s in range(n_sub):
        o_ref[s * sub:(s + 1) * sub, :] = acc_ref[s].astype(o_ref.dtype)


def _sb_attention(qkv, sub=ATTN_SUB, n_sub=ATTN_SUBS_PER_STEP):
    b, s, _ = qkv.shape
    h, d = SB_HEADS, SB_HEAD_DIM
    assert sub == LANES, "the carried row sums are kept lane-replicated at the key sub-block width"
    assert s >= ATTN_ALWAYS * sub
    n_sub = min(n_sub, s // sub)
    tq = sub * n_sub
    idx = lax.broadcasted_iota(jnp.int32, (sub, sub), 0) > lax.broadcasted_iota(jnp.int32, (sub, sub), 1)
    tri = idx.astype(BF16)
    kern = functools.partial(_sb_attn_kernel, sub=sub, n_sub=n_sub, scale=1.0 / math.sqrt(d))
    return pl.pallas_call(
        kern,
        grid=(b, h, s // tq),
        in_specs=[pl.BlockSpec((None, tq, d), lambda bi, hi, qi: (bi, qi, hi)),
                  pl.BlockSpec((None, s, d), lambda bi, hi, qi: (bi, 0, h + hi)),
                  pl.BlockSpec((None, s, d), lambda bi, hi, qi: (bi, 0, 2 * h + hi)),
                  pl.BlockSpec((sub, sub), lambda bi, hi, qi: (0, 0))],
        out_specs=pl.BlockSpec((None, tq, d), lambda bi, hi, qi: (bi, qi, hi)),
        out_shape=jax.ShapeDtypeStruct((b, s, h * d), BF16),
        scratch_shapes=[pltpu.VMEM((n_sub, sub, d), F32), pltpu.VMEM((n_sub, sub, LANES), F32)],
        compiler_params=_cparams("arbitrary", "arbitrary", "arbitrary"),
        name="sb_attn",
    )(qkv, qkv, qkv, tri)


def _round_robin(stagewise):
    out = [None] * len(stagewise)
    live = list(range(len(stagewise)))
    while live:
        for i in list(live):
            try:
                next(stagewise[i])
            except StopIteration as stop:
                out[i] = stop.value
                live.remove(i)
    return out


def _mlstm_head(head, q, k, vb, og, gcol, grow, nw, tril_ref, triu_ref, c_st, n_st, m_st, *, chunk):
    li_col = gcol[:, head:head + 1]
    lf_col = _log_sigmoid(gcol[:, ML_HEADS + head:ML_HEADS + head + 1])
    li_row = grow[head:head + 1, :]
    lf_row = _log_sigmoid(grow[ML_HEADS + head:ML_HEADS + head + 1, :])

    hi, lo = _split_bf16(jnp.broadcast_to(lf_col, (chunk, LANES)))
    b_col = (jnp.dot(tril_ref[...], hi, preferred_element_type=F32)
             + jnp.dot(tril_ref[...], lo, preferred_element_type=F32))[:, 0:1]
    hi, lo = _split_bf16(jnp.broadcast_to(lf_row, (2 * SUBLANES, chunk)))
    b_row = (jnp.dot(hi, triu_ref[...], preferred_element_type=F32)
             + jnp.dot(lo, triu_ref[...], preferred_element_type=F32))[0:1, :]
    qb = q.astype(BF16)
    kb = k.astype(BF16)
    qk = lax.dot_general(qb, kb, (((1,), (1,)), ((), ())), preferred_element_type=F32)
    q_mem = jnp.dot(qb, c_st[...].astype(BF16), preferred_element_type=F32)
    yield

    m_prev = m_st[...]
    a_col = b_col + m_prev
    row_t = lax.broadcasted_iota(jnp.int32, (chunk, chunk), 0)
    col_s = lax.broadcasted_iota(jnp.int32, (chunk, chunk), 1)
    d_mat = jnp.where(col_s <= row_t, b_col - b_row + li_row, -jnp.inf)
    m_col = jnp.maximum(a_col, jnp.max(d_mat, axis=1, keepdims=True))
    w_intra = jnp.exp(d_mat - m_col)
    w_inter = jnp.exp(a_col - m_col)
    sc = qk * w_intra
    sc_v = jnp.dot(sc.astype(BF16), vb, preferred_element_type=F32)

    b_last = b_col[chunk - 1:chunk, :]
    g_col = b_last - b_col + li_col
    m_new = jnp.maximum(b_last + m_prev, jnp.max(g_col, axis=0, keepdims=True))
    decay = jnp.exp(b_last + m_prev - m_new)
    kw = k * jnp.exp(g_col - m_new)
    kw_v = lax.dot_general(kw.astype(BF16), vb, (((0,), (0,)), ((), ())), preferred_element_type=F32)
    yield

    num = w_inter * q_mem + sc_v
    den = (w_inter * jnp.sum(q * n_st[...], axis=1, keepdims=True)
           + jnp.sum(sc, axis=1, keepdims=True))
    h_out = num / jnp.maximum(jnp.abs(den), jnp.exp(-m_col))
    c_st[...] = decay * c_st[...] + kw_v
    n_st[...] = decay * n_st[...] + jnp.sum(kw, axis=0, keepdims=True)
    m_st[...] = m_new

    hn = h_out * lax.rsqrt(jnp.mean(h_out * h_out, axis=1, keepdims=True) + NORM_EPS) * nw
    return jax.nn.sigmoid(og) * hn


def _mlstm_kernel(qp_ref, kp_ref, v_ref, og_ref, gcol_ref, grow_ref, cwq_ref, cwk_ref, nw_ref,
                  tril_ref, triu_ref, y_ref, c_st, n_st, m_st, qbuf, kbuf, *, chunk, heads, dim):
    halo = SUBLANES

    @pl.when(pl.program_id(1) == 0)
    def _():
        c_st[...] = jnp.zeros_like(c_st)
        n_st[...] = jnp.zeros_like(n_st)
        m_st[...] = jnp.zeros_like(m_st)
        qbuf[0:halo, :] = jnp.zeros((halo, heads * dim), F32)
        kbuf[0:halo, :] = jnp.zeros((halo, heads * dim), F32)

    qbuf[halo:halo + chunk, :] = qp_ref[...]
    kbuf[halo:halo + chunk, :] = kp_ref[...]

    def conv_silu(buf, cw_ref):
        acc = buf[halo:halo + chunk, :] * cw_ref[CONV_WIDTH - 1:CONV_WIDTH, :]
        for j in range(CONV_WIDTH - 1):
            off = halo - (CONV_WIDTH - 1) + j
            acc = acc + buf[off:off + chunk, :] * cw_ref[j:j + 1, :]
        return acc * jax.nn.sigmoid(acc)

    q = conv_silu(qbuf, cwq_ref)
    k = conv_silu(kbuf, cwk_ref) * (1.0 / math.sqrt(dim))
    qbuf[0:halo, :] = qbuf[chunk:chunk + halo, :]
    kbuf[0:halo, :] = kbuf[chunk:chunk + halo, :]

    gcol = gcol_ref[...]
    grow = grow_ref[...]
    cols = [slice(head * dim, (head + 1) * dim) for head in range(heads)]
    ys = _round_robin([
        _mlstm_head(head, q[:, c], k[:, c], v_ref[:, c], og_ref[:, c], gcol, grow, nw_ref[:, c],
                    tril_ref, triu_ref, c_st.at[head], n_st.at[head], m_st.at[head], chunk=chunk)
        for head, c in enumerate(cols)])
    for y, c in zip(ys, cols):
        y_ref[:, c] = y.astype(y_ref.dtype)


def _mlstm(ml_qk, ml_v, ml_o, gates, conv_w, ml_norm_w, chunk=ML_CHUNK):
    b, s, _ = ml_v.shape
    h, d = ML_HEADS, ML_HEAD_DIM
    w = h * d
    gates_t = jnp.swapaxes(gates, 1, 2)
    r = lax.broadcasted_iota(jnp.int32, (chunk, chunk), 0)
    c = lax.broadcasted_iota(jnp.int32, (chunk, chunk), 1)
    tril = (c <= r).astype(BF16)
    triu = (r <= c).astype(BF16)
    wide = lambda off: pl.BlockSpec((None, chunk, w), lambda bi, ci: (bi, ci, off))
    const = lambda shape, off=0: pl.BlockSpec(shape, lambda bi, ci: (0, off))
    kern = functools.partial(_mlstm_kernel, chunk=chunk, heads=h, dim=d)
    return pl.pallas_call(
        kern,
        grid=(b, s // chunk),
        in_specs=[wide(0), wide(1), wide(0), wide(0),
                  pl.BlockSpec((None, chunk, 2 * h), lambda bi, ci: (bi, ci, 0)),
                  pl.BlockSpec((None, 2 * h, chunk), lambda bi, ci: (bi, 0, ci)),
                  const((CONV_WIDTH, w)), const((CONV_WIDTH, w), 1), const((1, w)),
                  const((chunk, chunk)), const((chunk, chunk))],
        out_specs=wide(0),
        out_shape=jax.ShapeDtypeStruct((b, s, w), BF16),
        scratch_shapes=[pltpu.VMEM((h, d, d), F32), pltpu.VMEM((h, 1, d), F32), pltpu.VMEM((h, 1, 1), F32),
                        pltpu.VMEM((chunk + SUBLANES, w), F32), pltpu.VMEM((chunk + SUBLANES, w), F32)],
        compiler_params=_cparams("arbitrary", "arbitrary"),
        name="mlstm",
    )(ml_qk, ml_qk, ml_v, ml_o, gates, gates_t, conv_w, conv_w, ml_norm_w.reshape(1, w), tril, triu)


def _out_proj_kernel(ya_ref, yb_ref, wa_ref, wb_ref, x_ref, g_ref, o_ref):
    mix = (jnp.dot(ya_ref[...], wa_ref[...], preferred_element_type=F32)
           + jnp.dot(yb_ref[...], wb_ref[...], preferred_element_type=F32))
    o_ref[...] = x_ref[...] + g_ref[...] * mix


def _out_proj(y_sb, y_ml, w_out, x, gate, tm, tn=1024):
    b, s, d = x.shape
    ka, kb = y_sb.shape[-1], y_ml.shape[-1]
    assert ka == kb
    n = b * s
    rows_per_seq = s // tm
    f32_tile = tm * tn * 4
    vmem = min(2 * (tm + tn) * (ka + kb) * 2 + 4 * f32_tile + 3 * f32_tile, VMEM_CAPACITY - VMEM_RESERVE)
    out = pl.pallas_call(
        _out_proj_kernel,
        grid=(n // tm, d // tn),
        in_specs=[pl.BlockSpec((tm, ka), lambda i, j: (i, 0)),
                  pl.BlockSpec((tm, kb), lambda i, j: (i, 0)),
                  pl.BlockSpec((ka, tn), lambda i, j: (0, j)),
                  pl.BlockSpec((kb, tn), lambda i, j: (1, j)),
                  pl.BlockSpec((tm, tn), lambda i, j: (i, j)),
                  pl.BlockSpec((None, 1, tn), lambda i, j: (i // rows_per_seq, 0, j))],
        out_specs=pl.BlockSpec((tm, tn), lambda i, j: (i, j)),
        out_shape=jax.ShapeDtypeStruct((n, d), F32),
        compiler_params=_cparams("arbitrary", "arbitrary", vmem=vmem),
        name="out_proj",
    )(y_sb.reshape(n, ka), y_ml.reshape(n, kb), w_out, w_out, x.reshape(n, d), gate.reshape(b, 1, d))
    return out.reshape(b, s, d)


def _router_kernel(h_ref, nw_ref, sh_ref, sc_ref, wr_ref, br_ref, tri_ref, u_ref, ri_ref, rw_ref, cnt_ref, run_ref):
    first = jnp.logical_and(pl.program_id(0) == 0, pl.program_id(1) == 0)

    @pl.when(first)
    def _():
        run_ref[...] = jnp.zeros_like(run_ref)

    u = _rms_mod(h_ref[...], nw_ref[...], sh_ref[...], sc_ref[...])
    tm = u.shape[0]
    uh, ul = _split_bf16(u)
    u_ref[...] = _pack_bf16_pairs(uh)
    wh, wl = _split_bf16(wr_ref[...])
    logits = (jnp.dot(uh, wh, preferred_element_type=F32) + jnp.dot(ul, wh, preferred_element_type=F32)
              + jnp.dot(uh, wl, preferred_element_type=F32) + br_ref[...])

    lane = lax.broadcasted_iota(jnp.int32, (tm, LANES), 1)
    lane_f = lane.astype(F32)
    neg = -jnp.inf
    first_lane = lambda hit: jnp.min(jnp.where(hit, lane_f, float(LANES)), axis=1, keepdims=True)

    gl = jnp.where(lane < N_GROUPS, logits, neg)
    g_max = jnp.max(gl, axis=1, keepdims=True)
    g_p = 1.0 / jnp.sum(jnp.exp(gl - g_max), axis=1, keepdims=True)
    g_sel = first_lane(gl == g_max)

    lo_lane = N_GROUPS + EXPERTS_PER_GROUP * g_sel
    el = jnp.where(jnp.logical_and(lane_f >= lo_lane, lane_f < lo_lane + EXPERTS_PER_GROUP), logits, neg)
    e1_max = jnp.max(el, axis=1, keepdims=True)
    l1 = first_lane(el == e1_max)
    el2 = jnp.where(lane_f == l1, neg, el)
    e2_max = jnp.max(el2, axis=1, keepdims=True)
    l2 = first_lane(el2 == e2_max)
    z_sum = jnp.sum(jnp.exp(el - e1_max), axis=1, keepdims=True)
    p1 = 1.0 / z_sum
    p2 = jnp.exp(e2_max - e1_max) / z_sum
    w1 = p1 / (p1 + p2) * g_p
    w2 = p2 / (p1 + p2) * g_p
    e1 = l1 - N_GROUPS
    e2 = l2 - N_GROUPS

    hit1 = lane_f == e1
    hit2 = lane_f == e2
    onehot = jnp.where(jnp.logical_or(hit1, hit2), 1.0, 0.0)
    before = jnp.dot(tri_ref[...], onehot.astype(BF16), preferred_element_type=F32) + run_ref[...]
    rank1 = jnp.sum(jnp.where(hit1, before, 0.0), axis=1, keepdims=True)
    rank2 = jnp.sum(jnp.where(hit2, before, 0.0), axis=1, keepdims=True)
    run = run_ref[...] + jnp.sum(onehot, axis=0, keepdims=True)
    run_ref[...] = run

    ri_ref[...] = jnp.where(lane == 0, e1, jnp.where(lane == 1, e2, jnp.where(lane == 2, rank1,
                            jnp.where(lane == 3, rank2, 0.0)))).astype(jnp.int32)
    rw_ref[...] = jnp.where(lane == 0, w1, jnp.where(lane == 1, w2, 0.0))
    cnt_ref[...] = jnp.broadcast_to(run, cnt_ref.shape).astype(jnp.int32)


def _router(h, norm_w, shift, scale, w_router, b_router, tm=512):
    b, s, d = h.shape
    n = b * s
    tri = (lax.broadcasted_iota(jnp.int32, (tm, tm), 1) < lax.broadcasted_iota(jnp.int32, (tm, tm), 0)).astype(BF16)
    vec = pl.BlockSpec((None, 1, d), lambda bi, i: (bi, 0, 0))
    rows = lambda width: pl.BlockSpec((tm, width), lambda bi, i: (bi * (s // tm) + i, 0))
    return pl.pallas_call(
        _router_kernel,
        grid=(b, s // tm),
        in_specs=[pl.BlockSpec((None, tm, d), lambda bi, i: (bi, i, 0)),
                  pl.BlockSpec((1, d), lambda bi, i: (0, 0)), vec, vec,
                  pl.BlockSpec((d, LANES), lambda bi, i: (0, 0)),
                  pl.BlockSpec((1, LANES), lambda bi, i: (0, 0)),
                  pl.BlockSpec((tm, tm), lambda bi, i: (0, 0))],
        out_specs=[rows(d // 2), rows(LANES), rows(LANES),
                   pl.BlockSpec((SUBLANES, LANES), lambda bi, i: (0, 0))],
        out_shape=[jax.ShapeDtypeStruct((n, d // 2), jnp.int32), jax.ShapeDtypeStruct((n, LANES), jnp.int32),
                   jax.ShapeDtypeStruct((n, LANES), F32), jax.ShapeDtypeStruct((SUBLANES, LANES), jnp.int32)],
        scratch_shapes=[pltpu.VMEM((1, LANES), F32)],
        compiler_params=_cparams("arbitrary", "arbitrary"),
        name="router",
    )(h, norm_w.reshape(1, d), shift.reshape(b, 1, d), scale.reshape(b, 1, d), w_router, b_router, tri)


ROW_LOOP_UNROLL = 8
ZERO_ROWS = 128


def _dispatch_kernel(tail_ref, dest_ref, u_ref, xb_hbm, zbuf, zsem, sem, *, tokens, block):
    n_exp = tail_ref.shape[0]

    def clear(e, half):
        row = pl.multiple_of(tail_ref[e] + half * ZERO_ROWS, ZERO_ROWS)
        return pltpu.make_async_copy(zbuf, xb_hbm.at[pl.ds(row, ZERO_ROWS)], zsem)

    @pl.when(pl.program_id(0) == 0)
    def _():
        zbuf[...] = jnp.zeros_like(zbuf)
        for start in (True, False):
            def each(e, carry):
                @pl.when(tail_ref[e] >= 0)
                def _():
                    for half in range(block // ZERO_ROWS):
                        clear(e, half).start() if start else clear(e, half).wait()
                return carry
            lax.fori_loop(0, n_exp, each, 0)

    def row_copy(t, k):
        return pltpu.make_async_copy(u_ref.at[pl.ds(t, 1)], xb_hbm.at[pl.ds(dest_ref[0, TOP_K * t + k], 1)], sem)

    def issue(t, carry):
        for k in range(TOP_K):
            row_copy(t, k).start(priority=k % 2)
        return carry

    lax.fori_loop(0, tokens, issue, 0, unroll=ROW_LOOP_UNROLL)

    def drain(t, carry):
        for k in range(TOP_K):
            row_copy(t, k).wait()
        return carry

    lax.fori_loop(0, tokens, drain, 0, unroll=ROW_LOOP_UNROLL)


def _dispatch(u, dest, tail_block_row, cap, tokens=512, block=MOE_BLOCK):
    n, d = u.shape
    steps = n // tokens
    kern = functools.partial(_dispatch_kernel, tokens=tokens, block=block)
    return pl.pallas_call(
        kern,
        grid_spec=pltpu.PrefetchScalarGridSpec(
            num_scalar_prefetch=1,
            grid=(steps,),
            in_specs=[pl.BlockSpec((None, 1, TOP_K * tokens), lambda i, tail: (i, 0, 0), memory_space=pltpu.SMEM),
                      pl.BlockSpec((tokens, d), lambda i, tail: (i, 0))],
            out_specs=pl.BlockSpec(memory_space=pl.ANY),
            scratch_shapes=[pltpu.VMEM((ZERO_ROWS, d), u.dtype), pltpu.SemaphoreType.DMA(()),
                            pltpu.SemaphoreType.DMA(())]),
        out_shape=jax.ShapeDtypeStruct((cap, d), u.dtype),
        compiler_params=_cparams("arbitrary"),
        name="dispatch",
    )(tail_block_row, dest.reshape(steps, 1, TOP_K * tokens), u)


def _lookup(table, idx):
    hit = idx[:, None] == jnp.arange(table.shape[0], dtype=idx.dtype)[None, :]
    return jnp.sum(jnp.where(hit, table[None, :], 0), axis=1).astype(table.dtype)


def _run_plan(keys, ok):
    steps = ok.shape[0]
    idx = jnp.arange(steps, dtype=jnp.int32)
    changed = idx == 0
    for key in keys:
        changed = jnp.logical_or(changed, key != jnp.roll(key, 1))
    first = jnp.logical_and(changed, ok)
    slot = (jnp.cumsum(first.astype(jnp.int32)) - 1) % 2
    starts_from = lax.cummin(jnp.where(first, idx, steps), axis=0, reverse=True)
    nxt = jnp.concatenate([starts_from[1:], jnp.full((1,), steps, jnp.int32)])
    more = nxt < steps
    as_i32 = lambda a: a.astype(jnp.int32)
    return as_i32(first), as_i32(slot), as_i32(more), as_i32(jnp.minimum(nxt, steps - 1))


def _run_weights(step, first_ref, slot_ref, more_ref, fetch_cur, fetch_next):
    slot = slot_ref[step]

    @pl.when(step == 0)
    def _():
        for part, copy in enumerate(fetch_cur(slot)):
            copy.start(priority=part % 2)

    @pl.when(first_ref[step] == 1)
    def _():
        @pl.when(more_ref[step] == 1)
        def _():
            for part, copy in enumerate(fetch_next(1 - slot)):
                copy.start(priority=part % 2)

        for copy in fetch_cur(slot):
            copy.wait()

    return slot


def _e1_kernel(blk_ref, ex_ref, ft_ref, oblk_ref, oft_ref, ok_ref, first_ref, slot_ref, more_ref, nex_ref, nft_ref,
               x_ref, wg_hbm, wu_hbm, o_ref, wg_buf, wu_buf, sems, *, tf):
    del blk_ref, oblk_ref, oft_ref
    step = pl.program_id(0)

    def fetch(e, f, slot):
        cols = pl.ds(pl.multiple_of(f * tf, tf), tf)
        rows = wg_buf.shape[1] // WEIGHT_DMA_PARTS
        return [pltpu.make_async_copy(hbm.at[e, pl.ds(p * rows, rows), cols],
                                      buf.at[slot, pl.ds(p * rows, rows)], sems.at[slot, which])
                for p in range(WEIGHT_DMA_PARTS)
                for which, (hbm, buf) in enumerate(((wg_hbm, wg_buf), (wu_hbm, wu_buf)))]

    slot = _run_weights(step, first_ref, slot_ref, more_ref,
                        lambda sl: fetch(ex_ref[step], ft_ref[step], sl),
                        lambda sl: fetch(nex_ref[step], nft_ref[step], sl))
    ok = ok_ref[step] == 1

    @pl.when(ok)
    def _():
        x_lo, x_hi = _unpack_bf16_pairs(x_ref[...])
        half = x_lo.shape[1]
        mm = lambda w_buf: (jnp.dot(x_lo, w_buf[slot, :half, :], preferred_element_type=F32)
                            + jnp.dot(x_hi, w_buf[slot, half:, :], preferred_element_type=F32))
        g = mm(wg_buf)
        up = mm(wu_buf)
        o_ref[...] = ((g * jax.nn.sigmoid(g)) * up).astype(o_ref.dtype)

    @pl.when(jnp.logical_not(ok))
    def _():
        o_ref[...] = jnp.zeros_like(o_ref)


def _e2_kernel(blk_ref, ex_ref, oblk_ref, ok_ref, first_ref, slot_ref, more_ref, nex_ref,
               h_ref, wd_hbm, o_ref, wd_buf, sems):
    del blk_ref, oblk_ref
    step = pl.program_id(0)
    rows = wd_buf.shape[1] // WEIGHT_DMA_PARTS
    fetch = lambda e, slot: [pltpu.make_async_copy(wd_hbm.at[e, pl.ds(p * rows, rows)],
                                                   wd_buf.at[slot, pl.ds(p * rows, rows)], sems.at[slot])
                             for p in range(WEIGHT_DMA_PARTS)]
    slot = _run_weights(step, first_ref, slot_ref, more_ref,
                        lambda sl: fetch(ex_ref[step], sl), lambda sl: fetch(nex_ref[step], sl))
    ok = ok_ref[step] == 1

    @pl.when(ok)
    def _():
        o_ref[...] = jnp.dot(h_ref[...].astype(F32), wd_buf[slot], preferred_element_type=F32)

    @pl.when(jnp.logical_not(ok))
    def _():
        o_ref[...] = jnp.zeros_like(o_ref)


def _experts(xb, blocks_per_expert, w_gate, w_up, w_down, bm=MOE_BLOCK, tf=512):
    cap, dp = xb.shape
    n_exp, d, dff = w_gate.shape
    n_blocks = cap // bm
    n_ft = dff // tf
    nb = blocks_per_expert.astype(jnp.int32)
    bstart = jnp.cumsum(nb) - nb
    used = jnp.sum(nb)

    steps1 = n_blocks * n_ft
    step = jnp.arange(steps1, dtype=jnp.int32)
    ok1 = step < used * n_ft
    send = jnp.cumsum(nb * n_ft)
    s_idx = jnp.minimum(step, used * n_ft - 1)
    e_of = jnp.minimum(jnp.sum(s_idx[:, None] >= send[None, :], axis=1), n_exp - 1).astype(jnp.int32)
    nb_of = _lookup(nb, e_of)
    local = s_idx - (_lookup(send, e_of) - nb_of * n_ft)
    nbe = jnp.maximum(nb_of, 1)
    ft_of = (local // nbe).astype(jnp.int32)
    blk_of = (_lookup(bstart, e_of) + local % nbe).astype(jnp.int32)
    spare = step - used * n_ft
    oblk_of = jnp.where(ok1, blk_of, used + spare // n_ft).astype(jnp.int32)
    oft_of = jnp.where(ok1, ft_of, spare % n_ft).astype(jnp.int32)

    first1, slot1, more1, nxt1 = _run_plan([e_of, ft_of], ok1)
    hmid = pl.pallas_call(
        functools.partial(_e1_kernel, tf=tf),
        grid_spec=pltpu.PrefetchScalarGridSpec(
            num_scalar_prefetch=11,
            grid=(steps1,),
            in_specs=[pl.BlockSpec((bm, dp), lambda s, blk, *_: (blk[s], 0)),
                      pl.BlockSpec(memory_space=pl.ANY),
                      pl.BlockSpec(memory_space=pl.ANY)],
            out_specs=pl.BlockSpec((bm, tf), lambda s, blk, ex, ft, oblk, oft, *_: (oblk[s], oft[s])),
            scratch_shapes=[pltpu.VMEM((2, d, tf), F32), pltpu.VMEM((2, d, tf), F32),
                            pltpu.SemaphoreType.DMA((2, 2))]),
        out_shape=jax.ShapeDtypeStruct((cap, dff), BF16),
        compiler_params=_cparams("arbitrary"),
        name="e1",
    )(blk_of, e_of, ft_of, oblk_of, oft_of, ok1.astype(jnp.int32), first1, slot1, more1, _lookup(e_of, nxt1), _lookup(ft_of, nxt1),
      xb, w_gate, w_up)

    blk = jnp.arange(n_blocks, dtype=jnp.int32)
    b_idx = jnp.minimum(blk, used - 1)
    bend = jnp.cumsum(nb)
    e2_of = jnp.minimum(jnp.sum(b_idx[:, None] >= bend[None, :], axis=1), n_exp - 1).astype(jnp.int32)
    ok2 = blk < used
    first2, slot2, more2, nxt2 = _run_plan([e2_of], ok2)
    yb = pl.pallas_call(
        _e2_kernel,
        grid_spec=pltpu.PrefetchScalarGridSpec(
            num_scalar_prefetch=8,
            grid=(n_blocks,),
            in_specs=[pl.BlockSpec((bm, dff), lambda s, hb, *_: (hb[s], 0)),
                      pl.BlockSpec(memory_space=pl.ANY)],
            out_specs=pl.BlockSpec((bm, d), lambda s, hb, ex, ob, *_: (ob[s], 0)),
            scratch_shapes=[pltpu.VMEM((2, dff, d), F32), pltpu.SemaphoreType.DMA((2,))]),
        out_shape=jax.ShapeDtypeStruct((cap, d), F32),
        compiler_params=_cparams("arbitrary"),
        name="e2",
    )(b_idx, e2_of, blk, ok2.astype(jnp.int32), first2, slot2, more2, _lookup(e2_of, nxt2), hmid, w_down)
    return yb


def _combine_kernel(dest_ref, dest_next_ref, yb_hbm, h_ref, rw_ref, g_ref, nw_ref, sh_ref, sc_ref, o_ref,
                    rows_even, rows_odd, sems, *, tokens, steps, chunk):
    step = pl.program_id(0)

    def row_copy(d_ref, rows, sem, t, k):
        return pltpu.make_async_copy(yb_hbm.at[pl.ds(d_ref[0, TOP_K * t + k], 1)], rows.at[k, pl.ds(t, 1)], sem)

    def each_row(fn):
        def body(t, carry):
            for k in range(TOP_K):
                fn(t, k)
            return carry

        lax.fori_loop(0, tokens, body, 0, unroll=ROW_LOOP_UNROLL)

    def run(cur, cur_sem, nxt, nxt_sem):
        @pl.when(step == 0)
        def _():
            each_row(lambda t, k: row_copy(dest_ref, cur, cur_sem, t, k).start(priority=k % 2))

        each_row(lambda t, k: row_copy(dest_ref, cur, cur_sem, t, k).wait())

        for c in range(0, tokens, chunk):
            for t in range(c, c + chunk):
                for k in range(TOP_K):
                    row_copy(dest_next_ref, nxt, nxt_sem, t, k).start(priority=k % 2)
            rw = rw_ref[c:c + chunk, :]
            ffn = cur[0, c:c + chunk, :] * rw[:, 0:1]
            for k in range(1, TOP_K):
                ffn = ffn + cur[k, c:c + chunk, :] * rw[:, k:k + 1]
            h2 = h_ref[c:c + chunk, :] + g_ref[...] * ffn
            o_ref[c:c + chunk, :] = _rms_mod(h2, nw_ref[...], sh_ref[...], sc_ref[...])

        @pl.when(step == steps - 1)
        def _():
            each_row(lambda t, k: row_copy(dest_next_ref, nxt, nxt_sem, t, k).wait())

    @pl.when(step % 2 == 0)
    def _():
        run(rows_even, sems.at[0], rows_odd, sems.at[1])

    @pl.when(step % 2 == 1)
    def _():
        run(rows_odd, sems.at[1], rows_even, sems.at[0])


def _combine(yb, dest, h, route_w, gate, norm_w, shift, scale, tokens=256):
    b, s, d = h.shape
    n = b * s
    per_seq = s // tokens
    steps = n // tokens
    kern = functools.partial(_combine_kernel, tokens=tokens, steps=steps, chunk=COMBINE_CHUNK)
    vec = pl.BlockSpec((None, 1, d), lambda i: (i // per_seq, 0, 0))
    dest_blocks = dest.reshape(steps, 1, TOP_K * tokens)
    dest_spec = lambda ahead: pl.BlockSpec((None, 1, TOP_K * tokens),
                                           lambda i: (jnp.minimum(i + ahead, steps - 1), 0, 0),
                                           memory_space=pltpu.SMEM)
    out = pl.pallas_call(
        kern,
        grid=(steps,),
        in_specs=[dest_spec(0), dest_spec(1),
                  pl.BlockSpec(memory_space=pl.ANY),
                  pl.BlockSpec((tokens, d), lambda i: (i, 0)),
                  pl.BlockSpec((tokens, LANES), lambda i: (i, 0)),
                  vec,
                  pl.BlockSpec((1, d), lambda i: (0, 0)), vec, vec],
        out_specs=pl.BlockSpec((tokens, d), lambda i: (i, 0)),
        out_shape=jax.ShapeDtypeStruct((n, d), F32),
        scratch_shapes=[pltpu.VMEM((TOP_K, tokens, d), F32), pltpu.VMEM((TOP_K, tokens, d), F32),
                        pltpu.SemaphoreType.DMA((2,))],
        compiler_params=_cparams("arbitrary"),
        name="combine",
    )(dest_blocks, dest_blocks, yb, h.reshape(n, d), route_w,
      gate.reshape(b, 1, d), norm_w.reshape(1, d), shift.reshape(b, 1, d), scale.reshape(b, 1, d))
    return out.reshape(b, s, d)


def _layer(h, mod, norm1_w, w_in, conv_w, ml_gate_bias, ml_norm_w, w_out, norm2_w,
           w_router_group, b_router_group, w_router_expert, b_router_expert,
           w_exp_gate, w_exp_up, w_exp_down, final):
    b, s, d = h.shape
    n = b * s
    sh1, sc1, g1, sh2, sc2, g2 = jnp.split(mod, 6, axis=-1)
    sb_w = SB_HEADS * SB_HEAD_DIM
    ml_w = ML_HEADS * ML_HEAD_DIM

    u = _norm_mod(h, norm1_w, sh1, sc1, BF16).reshape(n, d)
    tm = 1024 if n % 1024 == 0 else 512
    w_in_t = w_in.T
    sb_qkv = _matmul(u, w_in_t, 0, 3 * sb_w, BF16, tm, PROJ_TN)
    ml_qk = _matmul(u, w_in_t, 3 * sb_w, 2 * ml_w, F32, tm, PROJ_TN)
    ml_v = _matmul(u, w_in_t, 3 * sb_w + 2 * ml_w, ml_w, BF16, tm, PROJ_TN)
    ml_o = _matmul(u, w_in_t, 3 * sb_w + 3 * ml_w, ml_w, F32, tm, PROJ_TN)
    n_gate = 2 * ML_HEADS
    w_gate = jnp.pad(w_in_t[3 * sb_w + 4 * ml_w:, :], ((0, LANES - n_gate), (0, 0)))
    b_gate = jnp.pad(ml_gate_bias, (0, LANES - n_gate)).reshape(1, LANES)
    gates = _matmul(u, w_gate, 0, LANES, F32, tm, LANES, bias=b_gate)[:, :n_gate]

    y_sb = _sb_attention(sb_qkv.reshape(b, s, 3 * sb_w))
    y_ml = _mlstm(ml_qk.reshape(b, s, 2 * ml_w), ml_v.reshape(b, s, ml_w), ml_o.reshape(b, s, ml_w),
                  gates.reshape(b, s, n_gate), conv_w, ml_norm_w)
    h = _out_proj(y_sb, y_ml, w_out.astype(BF16), h, g1, tm=min(1024, s))

    w_router = jnp.pad(jnp.concatenate([w_router_group, w_router_expert], axis=1),
                       ((0, 0), (0, LANES - N_GROUPS - N_EXPERTS)))
    b_router = jnp.pad(jnp.concatenate([b_router_group, b_router_expert]),
                       (0, LANES - N_GROUPS - N_EXPERTS)).reshape(1, LANES)
    u2, route_i, route_w, counts = _router(h, norm2_w, sh2, sc2, w_router, b_router)
    counts = counts[0, :N_EXPERTS]
    blocks_per_expert = (counts + MOE_BLOCK - 1) // MOE_BLOCK
    padded = blocks_per_expert * MOE_BLOCK
    pstarts = jnp.cumsum(padded) - padded
    dest = (_lookup(pstarts, route_i[:, 0:TOP_K].reshape(-1)) + route_i[:, TOP_K:2 * TOP_K].reshape(-1))
    cap = (-(-(n * TOP_K) // MOE_BLOCK) + N_EXPERTS) * MOE_BLOCK
    tail_block_row = jnp.where(counts % MOE_BLOCK != 0, pstarts + padded - MOE_BLOCK, -1)
    spare_row = jnp.sum(padded) + MOE_BLOCK * jnp.arange(N_EXPERTS, dtype=jnp.int32)
    spare_row = jnp.where(spare_row < cap, spare_row, -1)
    clear_rows = jnp.concatenate([tail_block_row, spare_row]).astype(jnp.int32)
    xb = _dispatch(u2, dest, clear_rows, cap)
    yb = _experts(xb, blocks_per_expert, w_exp_gate, w_exp_up, w_exp_down)
    norm_w, shift, scale = final
    return _combine(yb, dest, h, route_w, g2, norm_w, shift, scale)


def kernel(x, c, norm1_w, w_in, conv_w, ml_gate_bias, ml_norm_w, w_out, norm2_w, w_router_group, b_router_group, w_router_expert, b_router_expert, w_exp_gate, w_exp_up, w_exp_down, w_ada, b_ada, final_norm_w, w_ada_final, b_ada_final):
    b, s, d = x.shape
    depth = w_in.shape[0]
    assert depth == 1, "the final norm is fused into the (single) layer's combine step"
    c_rep = jnp.broadcast_to(c[:, :, None], (b, d, LANES))
    fmod = _ada(c_rep, w_ada_final, b_ada_final)
    sh_f, sc_f = jnp.split(fmod, 2, axis=-1)
    mod = _ada(c_rep, w_ada[0], b_ada[0])
    return _layer(x, mod, norm1_w[0], w_in[0], conv_w[0], ml_gate_bias[0], ml_norm_w[0], w_out[0], norm2_w[0],
                  w_router_group[0], b_router_group[0], w_router_expert[0], b_router_expert[0],
                  w_exp_gate[0], w_exp_up[0], w_exp_down[0], (final_norm_w, sh_f, sc_f))
```

```python
import functools
import math

import jax
import jax.numpy as jnp
from jax import lax
from jax.experimental import pallas as pl
from jax.experimental.pallas import tpu as pltpu

SB_HEADS = 16
SB_HEAD_DIM = 128
ML_HEADS = 4
ML_HEAD_DIM = 512
CONV_WIDTH = 4
N_GROUPS = 4
EXPERTS_PER_GROUP = 8
N_EXPERTS = N_GROUPS * EXPERTS_PER_GROUP
TOP_K = 2
NORM_EPS = 1e-6

LANES = 128
SUBLANES = 8
VMEM_CAPACITY = 64 * 1024 * 1024
VMEM_RESERVE = 4 * 1024 * 1024
VMEM_LIMIT = 56 * 1024 * 1024

ATTN_SUB = 128
ATTN_SUBS_PER_STEP = 16
ATTN_ALWAYS = 3
ML_CHUNK = 256
MOE_BLOCK = 256
PROJ_TN = 1024
WEIGHT_DMA_PARTS = 4
UNDERFLOW_LOG = -90.0

F32 = jnp.float32
BF16 = jnp.bfloat16


def _cparams(*sem, vmem=VMEM_LIMIT):
    return pltpu.CompilerParams(dimension_semantics=sem, vmem_limit_bytes=vmem)


def _split_bf16(x):
    hi = x.astype(BF16)
    lo = (x - hi.astype(F32)).astype(BF16)
    return hi, lo


_HIGH_HALF = -65536


def _pack_bf16_pairs(x):
    k = x.shape[1] // 2
    bits = lambda v: lax.bitcast_convert_type(v.astype(F32), jnp.int32)
    return lax.shift_right_logical(bits(x[:, :k]), 16) | (bits(x[:, k:]) & _HIGH_HALF)


def _unpack_bf16_pairs(w):
    lo = lax.bitcast_convert_type(lax.shift_left(w, 16), F32)
    hi = lax.bitcast_convert_type(w & _HIGH_HALF, F32)
    return lo, hi


def _log_sigmoid(x):
    return jnp.minimum(x, 0.0) - jnp.log1p(jnp.exp(-jnp.abs(x)))


def _ada_kernel(c_ref, w_ref, b_ref, o_ref):
    nb, tn = o_ref.shape
    rows = []
    for b in range(nb):
        cb = c_ref[b]
        cb = cb * jax.nn.sigmoid(cb)
        pieces = [jnp.sum(w_ref[:, j * LANES:(j + 1) * LANES] * cb, axis=0, keepdims=True)
                  for j in range(tn // LANES)]
        rows.append(jnp.concatenate(pieces, axis=1))
    o_ref[...] = jnp.concatenate(rows, axis=0) + b_ref[...]


def _ada(c_rep, w, bias, tn=1024):
    nb, k, _ = c_rep.shape
    n_out = w.shape[1]
    return pl.pallas_call(
        _ada_kernel,
        grid=(n_out // tn,),
        in_specs=[pl.BlockSpec((nb, k, LANES), lambda j: (0, 0, 0)),
                  pl.BlockSpec((k, tn), lambda j: (0, j)),
                  pl.BlockSpec((1, tn), lambda j: (0, j))],
        out_specs=pl.BlockSpec((nb, tn), lambda j: (0, j)),
        out_shape=jax.ShapeDtypeStruct((nb, n_out), F32),
        compiler_params=_cparams("arbitrary"),
        name="ada",
    )(c_rep, w, bias.reshape(1, n_out))


def _rms_mod(x, w, shift, scale):
    var = jnp.mean(x * x, axis=-1, keepdims=True)
    y = x * lax.rsqrt(var + NORM_EPS) * w
    return y * (1.0 + scale) + shift


def _norm_mod_kernel(x_ref, w_ref, sh_ref, sc_ref, o_ref):
    o_ref[...] = _rms_mod(x_ref[...], w_ref[...], sh_ref[...], sc_ref[...]).astype(o_ref.dtype)


def _norm_mod(x, w, shift, scale, out_dtype, tm=512):
    b, s, d = x.shape
    vec = pl.BlockSpec((None, 1, d), lambda bi, i: (bi, 0, 0))
    return pl.pallas_call(
        _norm_mod_kernel,
        grid=(b, s // tm),
        in_specs=[pl.BlockSpec((None, tm, d), lambda bi, i: (bi, i, 0)),
                  pl.BlockSpec((1, d), lambda bi, i: (0, 0)), vec, vec],
        out_specs=pl.BlockSpec((None, tm, d), lambda bi, i: (bi, i, 0)),
        out_shape=jax.ShapeDtypeStruct((b, s, d), out_dtype),
        compiler_params=_cparams("arbitrary", "arbitrary"),
        name="norm_mod",
    )(x, w.reshape(1, d), shift.reshape(b, 1, d), scale.reshape(b, 1, d))


def _matmul_kernel(a_ref, wt_hbm, *rest, has_bias, row0, n_col_tiles):
    b_ref, o_ref, stage, w16, sem = rest if has_bias else (None,) + rest
    j = pl.program_id(0)
    tn = stage.shape[0]
    rows = tn // WEIGHT_DMA_PARTS

    def fetch(col_tile):
        first = row0 + col_tile * tn
        return [pltpu.make_async_copy(wt_hbm.at[pl.ds(pl.multiple_of(first + p * rows, rows), rows), :],
                                      stage.at[pl.ds(p * rows, rows), :], sem)
                for p in range(WEIGHT_DMA_PARTS)]

    @pl.when(pl.program_id(1) == 0)
    def _():
        @pl.when(j == 0)
        def _():
            for part, copy in enumerate(fetch(0)):
                copy.start(priority=part % 2)

        for copy in fetch(j):
            copy.wait()
        chunk = min(tn, 2 * LANES)
        for c in range(0, tn, chunk):
            w16[:, c:c + chunk] = stage[c:c + chunk, :].T.astype(BF16)

        @pl.when(j + 1 < n_col_tiles)
        def _():
            for part, copy in enumerate(fetch(j + 1)):
                copy.start(priority=part % 2)

    acc = jnp.dot(a_ref[...], w16[...], preferred_element_type=F32)
    if has_bias:
        acc = acc + b_ref[...]
    o_ref[...] = acc.astype(o_ref.dtype)


def _matmul(a, w_t, col0, ncols, out_dtype, tm, tn, bias=None):
    m, k = a.shape
    f32_tile = tm * tn * 4
    vmem = min(tn * k * 4 + k * tn * 2 + 2 * tm * k * 2 + 2 * f32_tile + 3 * f32_tile, VMEM_CAPACITY - VMEM_RESERVE)
    in_specs = [pl.BlockSpec((tm, k), lambda j, i: (i, 0)),
                pl.BlockSpec(memory_space=pl.ANY)]
    args = [a, w_t]
    if bias is not None:
        in_specs.append(pl.BlockSpec((1, tn), lambda j, i: (0, j)))
        args.append(bias)
    return pl.pallas_call(
        functools.partial(_matmul_kernel, has_bias=bias is not None, row0=col0, n_col_tiles=ncols // tn),
        grid=(ncols // tn, m // tm),
        in_specs=in_specs,
        out_specs=pl.BlockSpec((tm, tn), lambda j, i: (i, j)),
        out_shape=jax.ShapeDtypeStruct((m, ncols), out_dtype),
        scratch_shapes=[pltpu.VMEM((tn, k), F32), pltpu.VMEM((k, tn), BF16), pltpu.SemaphoreType.DMA(())],
        compiler_params=_cparams("arbitrary", "arbitrary", vmem=vmem),
        name="proj",
    )(*args)


def _sb_attn_kernel(q_ref, k_ref, v_ref, tri_ref, o_ref, acc_ref, r_ref, *, sub, n_sub, scale):
    first = pl.program_id(2) * n_sub
    tri = tri_ref[...]
    row = lax.broadcasted_iota(jnp.int32, (sub, sub), 0)
    col = lax.broadcasted_iota(jnp.int32, (sub, sub), 1)
    causal = col < row

    def visit(q, blk, r):
        start = pl.multiple_of(blk * sub, sub)
        kb = k_ref[pl.ds(start, sub), :]
        vb = v_ref[pl.ds(start, sub), :]
        z = lax.dot_general(q, kb, (((1,), (1,)), ((), ())), preferred_element_type=F32) * scale
        log_keep = -(jnp.maximum(z, 0.0) + jnp.log(1.0 + jnp.exp(-jnp.abs(z))))
        log_beta = log_keep + z
        hi, lo = _split_bf16(log_keep)
        between = (jnp.dot(hi, tri, preferred_element_type=F32)
                   + jnp.dot(lo, tri, preferred_element_type=F32))
        w = jnp.exp(log_beta + between + r)
        pv = jnp.dot(w.astype(BF16), vb, preferred_element_type=F32)
        return pv, r + jnp.sum(log_keep, axis=1, keepdims=True)

    n_win = ATTN_ALWAYS
    roww = lax.broadcasted_iota(jnp.int32, (sub, n_win * sub), 0)
    colw = lax.broadcasted_iota(jnp.int32, (sub, n_win * sub), 1)

    def win_start(s):
        blk = first + s
        clamp = s < n_win - 1
        first_blk = jnp.maximum(blk - (n_win - 1), 0) if clamp else blk - (n_win - 1)
        return pl.multiple_of(first_blk * sub, sub), clamp

    def masked(s, x):
        start, clamp = win_start(s)
        if clamp:
            return jnp.where(start + colw < (first + s) * sub + roww, x, 0.0)
        last = (n_win - 1) * sub
        return jnp.concatenate([x[:, :last], jnp.where(causal, x[:, last:], 0.0)], axis=1)

    scores = []
    for s in range(n_sub):
        kw = k_ref[pl.ds(win_start(s)[0], n_win * sub), :]
        q = q_ref[s * sub:(s + 1) * sub, :]
        scores.append(lax.dot_general(q, kw, (((1,), (1,)), ((), ())), preferred_element_type=F32) * scale)

    staged = []
    for s in range(n_sub):
        z = scores[s]
        log_keep = -(jnp.maximum(z, 0.0) + jnp.log(1.0 + jnp.exp(-jnp.abs(z))))
        log_beta = log_keep + z
        log_keep = masked(s, log_keep)
        chunks = [log_keep[:, c * sub:(c + 1) * sub] for c in range(n_win)]
        hi, lo = _split_bf16(jnp.concatenate(chunks, axis=0))
        inside = (jnp.dot(hi, tri, preferred_element_type=F32)
                  + jnp.dot(lo, tri, preferred_element_type=F32))
        staged.append((log_beta, chunks, inside))

    r_top = None
    for s in range(n_sub):
        log_beta, chunks, inside = staged[s]
        later = 0.0
        between = [None] * n_win
        for c in reversed(range(n_win)):
            between[c] = inside[c * sub:(c + 1) * sub, :] + later
            later = later + jnp.sum(chunks[c], axis=1, keepdims=True)
        w = masked(s, jnp.exp(log_beta + jnp.concatenate(between, axis=1)))
        vw = v_ref[pl.ds(win_start(s)[0], n_win * sub), :]
        acc_ref[s] = jnp.dot(w.astype(BF16), vw, preferred_element_type=F32)
        r = jnp.broadcast_to(later, (sub, LANES))
        r_ref[s] = r
        r_top = r if r_top is None else jnp.maximum(r_top, r)

    @pl.when(jnp.max(r_top) > UNDERFLOW_LOG)
    def _():
        for s in range(n_sub):
            q = q_ref[s * sub:(s + 1) * sub, :]

            def cond(carry):
                j, r_max = carry
                return jnp.logical_and(j >= 0, r_max > UNDERFLOW_LOG)

            def body(carry, s=s, q=q):
                j, _ = carry
                pv, r = visit(q, j, r_ref[s])
                acc_ref[s] += pv
                r_ref[s] = r
                return j - 1, jnp.max(r)

            lax.while_loop(cond, body, (first + s - ATTN_ALWAYS, jnp.max(r_ref[s])))

    for s in range(n_sub):
        o_ref[s * sub:(s + 1) * sub, :] = acc_ref[s].astype(o_ref.dtype)


def _sb_attention(qkv, sub=ATTN_SUB, n_sub=ATTN_SUBS_PER_STEP):
    b, s, _ = qkv.shape
    h, d = SB_HEADS, SB_HEAD_DIM
    assert sub == LANES, "the carried row sums are kept lane-replicated at the key sub-block width"
    assert s >= ATTN_ALWAYS * sub
    n_sub = min(n_sub, s // sub)
    tq = sub * n_sub
    idx = lax.broadcasted_iota(jnp.int32, (sub, sub), 0) > lax.broadcasted_iota(jnp.int32, (sub, sub), 1)
    tri = idx.astype(BF16)
    kern = functools.partial(_sb_attn_kernel, sub=sub, n_sub=n_sub, scale=1.0 / math.sqrt(d))
    return pl.pallas_call(
        kern,
        grid=(b, h, s // tq),
        in_specs=[pl.BlockSpec((None, tq, d), lambda bi, hi, qi: (bi, qi, hi)),
                  pl.BlockSpec((None, s, d), lambda bi, hi, qi: (bi, 0, h + hi)),
                  pl.BlockSpec((None, s, d), lambda bi, hi, qi: (bi, 0, 2 * h + hi)),
                  pl.BlockSpec((sub, sub), lambda bi, hi, qi: (0, 0))],
        out_specs=pl.BlockSpec((None, tq, d), lambda bi, hi, qi: (bi, qi, hi)),
        out_shape=jax.ShapeDtypeStruct((b, s, h * d), BF16),
        scratch_shapes=[pltpu.VMEM((n_sub, sub, d), F32), pltpu.VMEM((n_sub, sub, LANES), F32)],
        compiler_params=_cparams("arbitrary", "arbitrary", "arbitrary"),
        name="sb_attn",
    )(qkv, qkv, qkv, tri)


def _round_robin(stagewise):
    out = [None] * len(stagewise)
    live = list(range(len(stagewise)))
    while live:
        for i in list(live):
            try:
                next(stagewise[i])
            except StopIteration as stop:
                out[i] = stop.value
                live.remove(i)
    return out


def _mlstm_head(head, q, k, vb, og, gcol, grow, nw, tril_ref, triu_ref, c_st, n_st, m_st, *, chunk):
    li_col = gcol[:, head:head + 1]
    lf_col = _log_sigmoid(gcol[:, ML_HEADS + head:ML_HEADS + head + 1])
    li_row = grow[head:head + 1, :]
    lf_row = _log_sigmoid(grow[ML_HEADS + head:ML_HEADS + head + 1, :])

    hi, lo = _split_bf16(jnp.broadcast_to(lf_col, (chunk, LANES)))
    b_col = (jnp.dot(tril_ref[...], hi, preferred_element_type=F32)
             + jnp.dot(tril_ref[...], lo, preferred_element_type=F32))[:, 0:1]
    hi, lo = _split_bf16(jnp.broadcast_to(lf_row, (2 * SUBLANES, chunk)))
    b_row = (jnp.dot(hi, triu_ref[...], preferred_element_type=F32)
             + jnp.dot(lo, triu_ref[...], preferred_element_type=F32))[0:1, :]
    qb = q.astype(BF16)
    kb = k.astype(BF16)
    qk = lax.dot_general(qb, kb, (((1,), (1,)), ((), ())), preferred_element_type=F32)
    q_mem = jnp.dot(qb, c_st[...].astype(BF16), preferred_element_type=F32)
    yield

    m_prev = m_st[...]
    a_col = b_col + m_prev
    row_t = lax.broadcasted_iota(jnp.int32, (chunk, chunk), 0)
    col_s = lax.broadcasted_iota(jnp.int32, (chunk, chunk), 1)
    d_mat = jnp.where(col_s <= row_t, b_col - b_row + li_row, -jnp.inf)
    m_col = jnp.maximum(a_col, jnp.max(d_mat, axis=1, keepdims=True))
    w_intra = jnp.exp(d_mat - m_col)
    w_inter = jnp.exp(a_col - m_col)
    sc = qk * w_intra
    sc_v = jnp.dot(sc.astype(BF16), vb, preferred_element_type=F32)

    b_last = b_col[chunk - 1:chunk, :]
    g_col = b_last - b_col + li_col
    m_new = jnp.maximum(b_last + m_prev, jnp.max(g_col, axis=0, keepdims=True))
    decay = jnp.exp(b_last + m_prev - m_new)
    kw = k * jnp.exp(g_col - m_new)
    kw_v = lax.dot_general(kw.astype(BF16), vb, (((0,), (0,)), ((), ())), preferred_element_type=F32)
    yield

    num = w_inter * q_mem + sc_v
    den = (w_inter * jnp.sum(q * n_st[...], axis=1, keepdims=True)
           + jnp.sum(sc, axis=1, keepdims=True))
    h_out = num / jnp.maximum(jnp.abs(den), jnp.exp(-m_col))
    c_st[...] = decay * c_st[...] + kw_v
    n_st[...] = decay * n_st[...] + jnp.sum(kw, axis=0, keepdims=True)
    m_st[...] = m_new

    hn = h_out * lax.rsqrt(jnp.mean(h_out * h_out, axis=1, keepdims=True) + NORM_EPS) * nw
    return jax.nn.sigmoid(og) * hn


def _mlstm_kernel(qp_ref, kp_ref, v_ref, og_ref, gcol_ref, grow_ref, cwq_ref, cwk_ref, nw_ref,
                  tril_ref, triu_ref, y_ref, c_st, n_st, m_st, qbuf, kbuf, *, chunk, heads, dim):
    halo = SUBLANES

    @pl.when(pl.program_id(1) == 0)
    def _():
        c_st[...] = jnp.zeros_like(c_st)
        n_st[...] = jnp.zeros_like(n_st)
        m_st[...] = jnp.zeros_like(m_st)
        qbuf[0:halo, :] = jnp.zeros((halo, heads * dim), F32)
        kbuf[0:halo, :] = jnp.zeros((halo, heads * dim), F32)

    qbuf[halo:halo + chunk, :] = qp_ref[...]
    kbuf[halo:halo + chunk, :] = kp_ref[...]

    def conv_silu(buf, cw_ref):
        acc = buf[halo:halo + chunk, :] * cw_ref[CONV_WIDTH - 1:CONV_WIDTH, :]
        for j in range(CONV_WIDTH - 1):
            off = halo - (CONV_WIDTH - 1) + j
            acc = acc + buf[off:off + chunk, :] * cw_ref[j:j + 1, :]
        return acc * jax.nn.sigmoid(acc)

    q = conv_silu(qbuf, cwq_ref)
    k = conv_silu(kbuf, cwk_ref) * (1.0 / math.sqrt(dim))
    qbuf[0:halo, :] = qbuf[chunk:chunk + halo, :]
    kbuf[0:halo, :] = kbuf[chunk:chunk + halo, :]

    gcol = gcol_ref[...]
    grow = grow_ref[...]
    cols = [slice(head * dim, (head + 1) * dim) for head in range(heads)]
    ys = _round_robin([
        _mlstm_head(head, q[:, c], k[:, c], v_ref[:, c], og_ref[:, c], gcol, grow, nw_ref[:, c],
                    tril_ref, triu_ref, c_st.at[head], n_st.at[head], m_st.at[head], chunk=chunk)
        for head, c in enumerate(cols)])
    for y, c in zip(ys, cols):
        y_ref[:, c] = y.astype(y_ref.dtype)


def _mlstm(ml_qk, ml_v, ml_o, gates, conv_w, ml_norm_w, chunk=ML_CHUNK):
    b, s, _ = ml_v.shape
    h, d = ML_HEADS, ML_HEAD_DIM
    w = h * d
    gates_t = jnp.swapaxes(gates, 1, 2)
    r = lax.broadcasted_iota(jnp.int32, (chunk, chunk), 0)
    c = lax.broadcasted_iota(jnp.int32, (chunk, chunk), 1)
    tril = (c <= r).astype(BF16)
    triu = (r <= c).astype(BF16)
    wide = lambda off: pl.BlockSpec((None, chunk, w), lambda bi, ci: (bi, ci, off))
    const = lambda shape, off=0: pl.BlockSpec(shape, lambda bi, ci: (0, off))
    kern = functools.partial(_mlstm_kernel, chunk=chunk, heads=h, dim=d)
    return pl.pallas_call(
        kern,
        grid=(b, s // chunk),
        in_specs=[wide(0), wide(1), wide(0), wide(0),
                  pl.BlockSpec((None, chunk, 2 * h), lambda bi, ci: (bi, ci, 0)),
                  pl.BlockSpec((None, 2 * h, chunk), lambda bi, ci: (bi, 0, ci)),
                  const((CONV_WIDTH, w)), const((CONV_WIDTH, w), 1), const((1, w)),
                  const((chunk, chunk)), const((chunk, chunk))],
        out_specs=wide(0),
        out_shape=jax.ShapeDtypeStruct((b, s, w), BF16),
        scratch_shapes=[pltpu.VMEM((h, d, d), F32), pltpu.VMEM((h, 1, d), F32), pltpu.VMEM((h, 1, 1), F32),
                        pltpu.VMEM((chunk + SUBLANES, w), F32), pltpu.VMEM((chunk + SUBLANES, w), F32)],
        compiler_params=_cparams("arbitrary", "arbitrary"),
        name="mlstm",
    )(ml_qk, ml_qk, ml_v, ml_o, gates, gates_t, conv_w, conv_w, ml_norm_w.reshape(1, w), tril, triu)


def _out_proj_kernel(ya_ref, yb_ref, wa_ref, wb_ref, x_ref, g_ref, o_ref):
    mix = (jnp.dot(ya_ref[...], wa_ref[...], preferred_element_type=F32)
           + jnp.dot(yb_ref[...], wb_ref[...], preferred_element_type=F32))
    o_ref[...] = x_ref[...] + g_ref[...] * mix


def _out_proj(y_sb, y_ml, w_out, x, gate, tm, tn=1024):
    b, s, d = x.shape
    ka, kb = y_sb.shape[-1], y_ml.shape[-1]
    assert ka == kb
    n = b * s
    rows_per_seq = s // tm
    f32_tile = tm * tn * 4
    vmem = min(2 * (tm + tn) * (ka + kb) * 2 + 4 * f32_tile + 3 * f32_tile, VMEM_CAPACITY - VMEM_RESERVE)
    out = pl.pallas_call(
        _out_proj_kernel,
        grid=(n // tm, d // tn),
        in_specs=[pl.BlockSpec((tm, ka), lambda i, j: (i, 0)),
                  pl.BlockSpec((tm, kb), lambda i, j: (i, 0)),
                  pl.BlockSpec((ka, tn), lambda i, j: (0, j)),
                  pl.BlockSpec((kb, tn), lambda i, j: (1, j)),
                  pl.BlockSpec((tm, tn), lambda i, j: (i, j)),
                  pl.BlockSpec((None, 1, tn), lambda i, j: (i // rows_per_seq, 0, j))],
        out_specs=pl.BlockSpec((tm, tn), lambda i, j: (i, j)),
        out_shape=jax.ShapeDtypeStruct((n, d), F32),
        compiler_params=_cparams("arbitrary", "arbitrary", vmem=vmem),
        name="out_proj",
    )(y_sb.reshape(n, ka), y_ml.reshape(n, kb), w_out, w_out, x.reshape(n, d), gate.reshape(b, 1, d))
    return out.reshape(b, s, d)


def _router_kernel(h_ref, nw_ref, sh_ref, sc_ref, wr_ref, br_ref, tri_ref, u_ref, ri_ref, rw_ref, cnt_ref, run_ref):
    first = jnp.logical_and(pl.program_id(0) == 0, pl.program_id(1) == 0)

    @pl.when(first)
    def _():
        run_ref[...] = jnp.zeros_like(run_ref)

    u = _rms_mod(h_ref[...], nw_ref[...], sh_ref[...], sc_ref[...])
    tm = u.shape[0]
    uh, ul = _split_bf16(u)
    u_ref[...] = _pack_bf16_pairs(uh)
    wh, wl = _split_bf16(wr_ref[...])
    logits = (jnp.dot(uh, wh, preferred_element_type=F32) + jnp.dot(ul, wh, preferred_element_type=F32)
              + jnp.dot(uh, wl, preferred_element_type=F32) + br_ref[...])

    lane = lax.broadcasted_iota(jnp.int32, (tm, LANES), 1)
    lane_f = lane.astype(F32)
    neg = -jnp.inf
    first_lane = lambda hit: jnp.min(jnp.where(hit, lane_f, float(LANES)), axis=1, keepdims=True)

    gl = jnp.where(lane < N_GROUPS, logits, neg)
    g_max = jnp.max(gl, axis=1, keepdims=True)
    g_p = 1.0 / jnp.sum(jnp.exp(gl - g_max), axis=1, keepdims=True)
    g_sel = first_lane(gl == g_max)

    lo_lane = N_GROUPS + EXPERTS_PER_GROUP * g_sel
    el = jnp.where(jnp.logical_and(lane_f >= lo_lane, lane_f < lo_lane + EXPERTS_PER_GROUP), logits, neg)
    e1_max = jnp.max(el, axis=1, keepdims=True)
    l1 = first_lane(el == e1_max)
    el2 = jnp.where(lane_f == l1, neg, el)
    e2_max = jnp.max(el2, axis=1, keepdims=True)
    l2 = first_lane(el2 == e2_max)
    z_sum = jnp.sum(jnp.exp(el - e1_max), axis=1, keepdims=True)
    p1 = 1.0 / z_sum
    p2 = jnp.exp(e2_max - e1_max) / z_sum
    w1 = p1 / (p1 + p2) * g_p
    w2 = p2 / (p1 + p2) * g_p
    e1 = l1 - N_GROUPS
    e2 = l2 - N_GROUPS

    hit1 = lane_f == e1
    hit2 = lane_f == e2
    onehot = jnp.where(jnp.logical_or(hit1, hit2), 1.0, 0.0)
    before = jnp.dot(tri_ref[...], onehot.astype(BF16), preferred_element_type=F32) + run_ref[...]
    rank1 = jnp.sum(jnp.where(hit1, before, 0.0), axis=1, keepdims=True)
    rank2 = jnp.sum(jnp.where(hit2, before, 0.0), axis=1, keepdims=True)
    run = run_ref[...] + jnp.sum(onehot, axis=0, keepdims=True)
    run_ref[...] = run

    ri_ref[...] = jnp.where(lane == 0, e1, jnp.where(lane == 1, e2, jnp.where(lane == 2, rank1,
                            jnp.where(lane == 3, rank2, 0.0)))).astype(jnp.int32)
    rw_ref[...] = jnp.where(lane == 0, w1, jnp.where(lane == 1, w2, 0.0))
    cnt_ref[...] = jnp.broadcast_to(run, cnt_ref.shape).astype(jnp.int32)


def _router(h, norm_w, shift, scale, w_router, b_router, tm=512):
    b, s, d = h.shape
    n = b * s
    tri = (lax.broadcasted_iota(jnp.int32, (tm, tm), 1) < lax.broadcasted_iota(jnp.int32, (tm, tm), 0)).astype(BF16)
    vec = pl.BlockSpec((None, 1, d), lambda bi, i: (bi, 0, 0))
    rows = lambda width: pl.BlockSpec((tm, width), lambda bi, i: (bi * (s // tm) + i, 0))
    return pl.pallas_call(
        _router_kernel,
        grid=(b, s // tm),
        in_specs=[pl.BlockSpec((None, tm, d), lambda bi, i: (bi, i, 0)),
                  pl.BlockSpec((1, d), lambda bi, i: (0, 0)), vec, vec,
                  pl.BlockSpec((d, LANES), lambda bi, i: (0, 0)),
                  pl.BlockSpec((1, LANES), lambda bi, i: (0, 0)),
                  pl.BlockSpec((tm, tm), lambda bi, i: (0, 0))],
        out_specs=[rows(d // 2), rows(LANES), rows(LANES),
                   pl.BlockSpec((SUBLANES, LANES), lambda bi, i: (0, 0))],
        out_shape=[jax.ShapeDtypeStruct((n, d // 2), jnp.int32), jax.ShapeDtypeStruct((n, LANES), jnp.int32),
                   jax.ShapeDtypeStruct((n, LANES), F32), jax.ShapeDtypeStruct((SUBLANES, LANES), jnp.int32)],
        scratch_shapes=[pltpu.VMEM((1, LANES), F32)],
        compiler_params=_cparams("arbitrary", "arbitrary"),
        name="router",
    )(h, norm_w.reshape(1, d), shift.reshape(b, 1, d), scale.reshape(b, 1, d), w_router, b_router, tri)


ROW_LOOP_UNROLL = 8
ZERO_ROWS = 128


def _dispatch_kernel(tail_ref, dest_ref, u_ref, xb_hbm, zbuf, zsem, sem, *, tokens, block):
    n_exp = tail_ref.shape[0]

    def clear(e, half):
        row = pl.multiple_of(tail_ref[e] + half * ZERO_ROWS, ZERO_ROWS)
        return pltpu.make_async_copy(zbuf, xb_hbm.at[pl.ds(row, ZERO_ROWS)], zsem)

    @pl.when(pl.program_id(0) == 0)
    def _():
        zbuf[...] = jnp.zeros_like(zbuf)
        for start in (True, False):
            def each(e, carry):
                @pl.when(tail_ref[e] >= 0)
                def _():
                    for half in range(block // ZERO_ROWS):
                        clear(e, half).start() if start else clear(e, half).wait()
                return carry
            lax.fori_loop(0, n_exp, each, 0)

    def row_copy(t, k):
        return pltpu.make_async_copy(u_ref.at[pl.ds(t, 1)], xb_hbm.at[pl.ds(dest_ref[0, TOP_K * t + k], 1)], sem)

    def issue(t, carry):
        for k in range(TOP_K):
            row_copy(t, k).start(priority=k % 2)
        return carry

    lax.fori_loop(0, tokens, issue, 0, unroll=ROW_LOOP_UNROLL)

    def drain(t, carry):
        for k in range(TOP_K):
            row_copy(t, k).wait()
        return carry

    lax.fori_loop(0, tokens, drain, 0, unroll=ROW_LOOP_UNROLL)


def _dispatch(u, dest, tail_block_row, cap, tokens=512, block=MOE_BLOCK):
    n, d = u.shape
    steps = n // tokens
    kern = functools.partial(_dispatch_kernel, tokens=tokens, block=block)
    return pl.pallas_call(
        kern,
        grid_spec=pltpu.PrefetchScalarGridSpec(
            num_scalar_prefetch=1,
            grid=(steps,),
            in_specs=[pl.BlockSpec((None, 1, TOP_K * tokens), lambda i, tail: (i, 0, 0), memory_space=pltpu.SMEM),
                      pl.BlockSpec((tokens, d), lambda i, tail: (i, 0))],
            out_specs=pl.BlockSpec(memory_space=pl.ANY),
            scratch_shapes=[pltpu.VMEM((ZERO_ROWS, d), u.dtype), pltpu.SemaphoreType.DMA(()),
                            pltpu.SemaphoreType.DMA(())]),
        out_shape=jax.ShapeDtypeStruct((cap, d), u.dtype),
        compiler_params=_cparams("arbitrary"),
        name="dispatch",
    )(tail_block_row, dest.reshape(steps, 1, TOP_K * tokens), u)


def _lookup(table, idx):
    hit = idx[:, None] == jnp.arange(table.shape[0], dtype=idx.dtype)[None, :]
    return jnp.sum(jnp.where(hit, table[None, :], 0), axis=1).astype(table.dtype)


def _run_plan(keys, ok):
    steps = ok.shape[0]
    idx = jnp.arange(steps, dtype=jnp.int32)
    changed = idx == 0
    for key in keys:
        changed = jnp.logical_or(changed, key != jnp.roll(key, 1))
    first = jnp.logical_and(changed, ok)
    slot = (jnp.cumsum(first.astype(jnp.int32)) - 1) % 2
    starts_from = lax.cummin(jnp.where(first, idx, steps), axis=0, reverse=True)
    nxt = jnp.concatenate([starts_from[1:], jnp.full((1,), steps, jnp.int32)])
    more = nxt < steps
    as_i32 = lambda a: a.astype(jnp.int32)
    return as_i32(first), as_i32(slot), as_i32(more), as_i32(jnp.minimum(nxt, steps - 1))


def _run_weights(step, first_ref, slot_ref, more_ref, fetch_cur, fetch_next):
    slot = slot_ref[step]

    @pl.when(step == 0)
    def _():
        for part, copy in enumerate(fetch_cur(slot)):
            copy.start(priority=part % 2)

    @pl.when(first_ref[step] == 1)
    def _():
        @pl.when(more_ref[step] == 1)
        def _():
            for part, copy in enumerate(fetch_next(1 - slot)):
                copy.start(priority=part % 2)

        for copy in fetch_cur(slot):
            copy.wait()

    return slot


def _e1_kernel(blk_ref, ex_ref, ft_ref, oblk_ref, oft_ref, ok_ref, first_ref, slot_ref, more_ref, nex_ref, nft_ref,
               x_ref, wg_hbm, wu_hbm, o_ref, wg_buf, wu_buf, sems, *, tf):
    del blk_ref, oblk_ref, oft_ref
    step = pl.program_id(0)

    def fetch(e, f, slot):
        cols = pl.ds(pl.multiple_of(f * tf, tf), tf)
        rows = wg_buf.shape[1] // WEIGHT_DMA_PARTS
        return [pltpu.make_async_copy(hbm.at[e, pl.ds(p * rows, rows), cols],
                                      buf.at[slot, pl.ds(p * rows, rows)], sems.at[slot, which])
                for p in range(WEIGHT_DMA_PARTS)
                for which, (hbm, buf) in enumerate(((wg_hbm, wg_buf), (wu_hbm, wu_buf)))]

    slot = _run_weights(step, first_ref, slot_ref, more_ref,
                        lambda sl: fetch(ex_ref[step], ft_ref[step], sl),
                        lambda sl: fetch(nex_ref[step], nft_ref[step], sl))
    ok = ok_ref[step] == 1

    @pl.when(ok)
    def _():
        x_lo, x_hi = _unpack_bf16_pairs(x_ref[...])
        half = x_lo.shape[1]
        mm = lambda w_buf: (jnp.dot(x_lo, w_buf[slot, :half, :], preferred_element_type=F32)
                            + jnp.dot(x_hi, w_buf[slot, half:, :], preferred_element_type=F32))
        g = mm(wg_buf)
        up = mm(wu_buf)
        o_ref[...] = ((g * jax.nn.sigmoid(g)) * up).astype(o_ref.dtype)

    @pl.when(jnp.logical_not(ok))
    def _():
        o_ref[...] = jnp.zeros_like(o_ref)


def _e2_kernel(blk_ref, ex_ref, oblk_ref, ok_ref, first_ref, slot_ref, more_ref, nex_ref,
               h_ref, wd_hbm, o_ref, wd_buf, sems):
    del blk_ref, oblk_ref
    step = pl.program_id(0)
    rows = wd_buf.shape[1] // WEIGHT_DMA_PARTS
    fetch = lambda e, slot: [pltpu.make_async_copy(wd_hbm.at[e, pl.ds(p * rows, rows)],
                                                   wd_buf.at[slot, pl.ds(p * rows, rows)], sems.at[slot])
                             for p in range(WEIGHT_DMA_PARTS)]
    slot = _run_weights(step, first_ref, slot_ref, more_ref,
                        lambda sl: fetch(ex_ref[step], sl), lambda sl: fetch(nex_ref[step], sl))
    ok = ok_ref[step] == 1

    @pl.when(ok)
    def _():
        o_ref[...] = jnp.dot(h_ref[...].astype(F32), wd_buf[slot], preferred_element_type=F32)

    @pl.when(jnp.logical_not(ok))
    def _():
        o_ref[...] = jnp.zeros_like(o_ref)


def _experts(xb, blocks_per_expert, w_gate, w_up, w_down, bm=MOE_BLOCK, tf=512):
    cap, dp = xb.shape
    n_exp, d, dff = w_gate.shape
    n_blocks = cap // bm
    n_ft = dff // tf
    nb = blocks_per_expert.astype(jnp.int32)
    bstart = jnp.cumsum(nb) - nb
    used = jnp.sum(nb)

    steps1 = n_blocks * n_ft
    step = jnp.arange(steps1, dtype=jnp.int32)
    ok1 = step < used * n_ft
    send = jnp.cumsum(nb * n_ft)
    s_idx = jnp.minimum(step, used * n_ft - 1)
    e_of = jnp.minimum(jnp.sum(s_idx[:, None] >= send[None, :], axis=1), n_exp - 1).astype(jnp.int32)
    nb_of = _lookup(nb, e_of)
    local = s_idx - (_lookup(send, e_of) - nb_of * n_ft)
    nbe = jnp.maximum(nb_of, 1)
    ft_of = (local // nbe).astype(jnp.int32)
    blk_of = (_lookup(bstart, e_of) + local % nbe).astype(jnp.int32)
    spare = step - used * n_ft
    oblk_of = jnp.where(ok1, blk_of, used + spare // n_ft).astype(jnp.int32)
    oft_of = jnp.where(ok1, ft_of, spare % n_ft).astype(jnp.int32)

    first1, slot1, more1, nxt1 = _run_plan([e_of, ft_of], ok1)
    hmid = pl.pallas_call(
        functools.partial(_e1_kernel, tf=tf),
        grid_spec=pltpu.PrefetchScalarGridSpec(
            num_scalar_prefetch=11,
            grid=(steps1,),
            in_specs=[pl.BlockSpec((bm, dp), lambda s, blk, *_: (blk[s], 0)),
                      pl.BlockSpec(memory_space=pl.ANY),
                      pl.BlockSpec(memory_space=pl.ANY)],
            out_specs=pl.BlockSpec((bm, tf), lambda s, blk, ex, ft, oblk, oft, *_: (oblk[s], oft[s])),
            scratch_shapes=[pltpu.VMEM((2, d, tf), F32), pltpu.VMEM((2, d, tf), F32),
                            pltpu.SemaphoreType.DMA((2, 2))]),
        out_shape=jax.ShapeDtypeStruct((cap, dff), BF16),
        compiler_params=_cparams("arbitrary"),
        name="e1",
    )(blk_of, e_of, ft_of, oblk_of, oft_of, ok1.astype(jnp.int32), first1, slot1, more1, _lookup(e_of, nxt1), _lookup(ft_of, nxt1),
      xb, w_gate, w_up)

    blk = jnp.arange(n_blocks, dtype=jnp.int32)
    b_idx = jnp.minimum(blk, used - 1)
    bend = jnp.cumsum(nb)
    e2_of = jnp.minimum(jnp.sum(b_idx[:, None] >= bend[None, :], axis=1), n_exp - 1).astype(jnp.int32)
    ok2 = blk < used
    first2, slot2, more2, nxt2 = _run_plan([e2_of], ok2)
    yb = pl.pallas_call(
        _e2_kernel,
        grid_spec=pltpu.PrefetchScalarGridSpec(
            num_scalar_prefetch=8,
            grid=(n_blocks,),
            in_specs=[pl.BlockSpec((bm, dff), lambda s, hb, *_: (hb[s], 0)),
                      pl.BlockSpec(memory_space=pl.ANY)],
            out_specs=pl.BlockSpec((bm, d), lambda s, hb, ex, ob, *_: (ob[s], 0)),
            scratch_shapes=[pltpu.VMEM((2, dff, d), F32), pltpu.SemaphoreType.DMA((2,))]),
        out_shape=jax.ShapeDtypeStruct((cap, d), F32),
        compiler_params=_cparams("arbitrary"),
        name="e2",
    )(b_idx, e2_of, blk, ok2.astype(jnp.int32), first2, slot2, more2, _lookup(e2_of, nxt2), hmid, w_down)
    return yb


def _combine_kernel(dest_ref, dest_next_ref, yb_hbm, h_ref, rw_ref, g_ref, nw_ref, sh_ref, sc_ref, o_ref,
                    rows, sems, *, tokens, steps):
    step = pl.program_id(0)
    slot = step % 2

    def row_copy(d_ref, buf, t, k):
        return pltpu.make_async_copy(yb_hbm.at[pl.ds(d_ref[0, TOP_K * t + k], 1)],
                                     rows.at[buf, k, pl.ds(t, 1)], sems.at[buf])

    def gather(d_ref, buf):
        def issue(t, carry):
            for k in range(TOP_K):
                row_copy(d_ref, buf, t, k).start(priority=k % 2)
            return carry

        lax.fori_loop(0, tokens, issue, 0, unroll=ROW_LOOP_UNROLL)

    @pl.when(step == 0)
    def _():
        gather(dest_ref, 0)

    @pl.when(step + 1 < steps)
    def _():
        gather(dest_next_ref, 1 - slot)

    def drain(t, carry):
        for k in range(TOP_K):
            row_copy(dest_ref, slot, t, k).wait()
        return carry

    lax.fori_loop(0, tokens, drain, 0, unroll=ROW_LOOP_UNROLL)

    rw = rw_ref[...]
    ffn = rows[slot, 0] * rw[:, 0:1]
    for k in range(1, TOP_K):
        ffn = ffn + rows[slot, k] * rw[:, k:k + 1]
    h2 = h_ref[...] + g_ref[...] * ffn
    o_ref[...] = _rms_mod(h2, nw_ref[...], sh_ref[...], sc_ref[...])


def _combine(yb, dest, h, route_w, gate, norm_w, shift, scale, tokens=256):
    b, s, d = h.shape
    n = b * s
    per_seq = s // tokens
    steps = n // tokens
    kern = functools.partial(_combine_kernel, tokens=tokens, steps=steps)
    vec = pl.BlockSpec((None, 1, d), lambda i: (i // per_seq, 0, 0))
    dest_blocks = dest.reshape(steps, 1, TOP_K * tokens)
    dest_spec = lambda ahead: pl.BlockSpec((None, 1, TOP_K * tokens),
                                           lambda i: (jnp.minimum(i + ahead, steps - 1), 0, 0),
                                           memory_space=pltpu.SMEM)
    out = pl.pallas_call(
        kern,
        grid=(steps,),
        in_specs=[dest_spec(0), dest_spec(1),
                  pl.BlockSpec(memory_space=pl.ANY),
                  pl.BlockSpec((tokens, d), lambda i: (i, 0)),
                  pl.BlockSpec((tokens, LANES), lambda i: (i, 0)),
                  vec,
                  pl.BlockSpec((1, d), lambda i: (0, 0)), vec, vec],
        out_specs=pl.BlockSpec((tokens, d), lambda i: (i, 0)),
        out_shape=jax.ShapeDtypeStruct((n, d), F32),
        scratch_shapes=[pltpu.VMEM((2, TOP_K, tokens, d), F32), pltpu.SemaphoreType.DMA((2,))],
        compiler_params=_cparams("arbitrary"),
        name="combine",
    )(dest_blocks, dest_blocks, yb, h.reshape(n, d), route_w,
      gate.reshape(b, 1, d), norm_w.reshape(1, d), shift.reshape(b, 1, d), scale.reshape(b, 1, d))
    return out.reshape(b, s, d)


def _layer(h, mod, norm1_w, w_in, conv_w, ml_gate_bias, ml_norm_w, w_out, norm2_w,
           w_router_group, b_router_group, w_router_expert, b_router_expert,
           w_exp_gate, w_exp_up, w_exp_down, final):
    b, s, d = h.shape
    n = b * s
    sh1, sc1, g1, sh2, sc2, g2 = jnp.split(mod, 6, axis=-1)
    sb_w = SB_HEADS * SB_HEAD_DIM
    ml_w = ML_HEADS * ML_HEAD_DIM

    u = _norm_mod(h, norm1_w, sh1, sc1, BF16).reshape(n, d)
    tm = 1024 if n % 1024 == 0 else 512
    w_in_t = w_in.T
    sb_qkv = _matmul(u, w_in_t, 0, 3 * sb_w, BF16, tm, PROJ_TN)
    ml_qk = _matmul(u, w_in_t, 3 * sb_w, 2 * ml_w, F32, tm, PROJ_TN)
    ml_v = _matmul(u, w_in_t, 3 * sb_w + 2 * ml_w, ml_w, BF16, tm, PROJ_TN)
    ml_o = _matmul(u, w_in_t, 3 * sb_w + 3 * ml_w, ml_w, F32, tm, PROJ_TN)
    n_gate = 2 * ML_HEADS
    w_gate = jnp.pad(w_in_t[3 * sb_w + 4 * ml_w:, :], ((0, LANES - n_gate), (0, 0)))
    b_gate = jnp.pad(ml_gate_bias, (0, LANES - n_gate)).reshape(1, LANES)
    gates = _matmul(u, w_gate, 0, LANES, F32, tm, LANES, bias=b_gate)[:, :n_gate]

    y_sb = _sb_attention(sb_qkv.reshape(b, s, 3 * sb_w))
    y_ml = _mlstm(ml_qk.reshape(b, s, 2 * ml_w), ml_v.reshape(b, s, ml_w), ml_o.reshape(b, s, ml_w),
                  gates.reshape(b, s, n_gate), conv_w, ml_norm_w)
    h = _out_proj(y_sb, y_ml, w_out.astype(BF16), h, g1, tm=min(1024, s))

    w_router = jnp.pad(jnp.concatenate([w_router_group, w_router_expert], axis=1),
                       ((0, 0), (0, LANES - N_GROUPS - N_EXPERTS)))
    b_router = jnp.pad(jnp.concatenate([b_router_group, b_router_expert]),
                       (0, LANES - N_GROUPS - N_EXPERTS)).reshape(1, LANES)
    u2, route_i, route_w, counts = _router(h, norm2_w, sh2, sc2, w_router, b_router)
    counts = counts[0, :N_EXPERTS]
    blocks_per_expert = (counts + MOE_BLOCK - 1) // MOE_BLOCK
    padded = blocks_per_expert * MOE_BLOCK
    pstarts = jnp.cumsum(padded) - padded
    dest = (_lookup(pstarts, route_i[:, 0:TOP_K].reshape(-1)) + route_i[:, TOP_K:2 * TOP_K].reshape(-1))
    cap = (-(-(n * TOP_K) // MOE_BLOCK) + N_EXPERTS) * MOE_BLOCK
    tail_block_row = jnp.where(counts % MOE_BLOCK != 0, pstarts + padded - MOE_BLOCK, -1)
    spare_row = jnp.sum(padded) + MOE_BLOCK * jnp.arange(N_EXPERTS, dtype=jnp.int32)
    spare_row = jnp.where(spare_row < cap, spare_row, -1)
    clear_rows = jnp.concatenate([tail_block_row, spare_row]).astype(jnp.int32)
    xb = _dispatch(u2, dest, clear_rows, cap)
    yb = _experts(xb, blocks_per_expert, w_exp_gate, w_exp_up, w_exp_down)
    norm_w, shift, scale = final
    return _combine(yb, dest, h, route_w, g2, norm_w, shift, scale)


def kernel(x, c, norm1_w, w_in, conv_w, ml_gate_bias, ml_norm_w, w_out, norm2_w, w_router_group, b_router_group, w_router_expert, b_router_expert, w_exp_gate, w_exp_up, w_exp_down, w_ada, b_ada, final_norm_w, w_ada_final, b_ada_final):
    b, s, d = x.shape
    depth = w_in.shape[0]
    assert depth == 1, "the final norm is fused into the (single) layer's combine step"
    c_rep = jnp.broadcast_to(c[:, :, None], (b, d, LANES))
    fmod = _ada(c_rep, w_ada_final, b_ada_final)
    sh_f, sc_f = jnp.split(fmod, 2, axis=-1)
    mod = _ada(c_rep, w_ada[0], b_ada[0])
    return _layer(x, mod, norm1_w[0], w_in[0], conv_w[0], ml_gate_bias[0], ml_norm_w[0], w_out[0], norm2_w[0],
                  w_router_group[0], b_router_group[0], w_router_expert[0], b_router_expert[0],
                  w_exp_gate[0], w_exp_up[0], w_exp_down[0], (final_norm_w, sh_f, sc_f))
```

```python
import functools
import math

import jax
import jax.numpy as jnp
from jax import lax
from jax.experimental import pallas as pl
from jax.experimental.pallas import tpu as pltpu

SB_HEADS = 16
SB_HEAD_DIM = 128
ML_HEADS = 4
ML_HEAD_DIM = 512
CONV_WIDTH = 4
N_GROUPS = 4
EXPERTS_PER_GROUP = 8
N_EXPERTS = N_GROUPS * EXPERTS_PER_GROUP
TOP_K = 2
NORM_EPS = 1e-6

LANES = 128
SUBLANES = 8
VMEM_CAPACITY = 64 * 1024 * 1024
VMEM_RESERVE = 4 * 1024 * 1024
VMEM_LIMIT = 56 * 1024 * 1024

ATTN_SUB = 128
ATTN_SUBS_PER_STEP = 32
ATTN_ALWAYS = 3
ML_CHUNK = 256
MOE_BLOCK = 256
PROJ_TN = 1024
WEIGHT_DMA_PARTS = 4
UNDERFLOW_LOG = -90.0

F32 = jnp.float32
BF16 = jnp.bfloat16


def _cparams(*sem, vmem=VMEM_LIMIT):
    return pltpu.CompilerParams(dimension_semantics=sem, vmem_limit_bytes=vmem)


def _split_bf16(x):
    hi = x.astype(BF16)
    lo = (x - hi.astype(F32)).astype(BF16)
    return hi, lo


_HIGH_HALF = -65536


def _pack_bf16_pairs(x):
    k = x.shape[1] // 2
    bits = lambda v: lax.bitcast_convert_type(v.astype(F32), jnp.int32)
    return lax.shift_right_logical(bits(x[:, :k]), 16) | (bits(x[:, k:]) & _HIGH_HALF)


def _unpack_bf16_pairs(w):
    lo = lax.bitcast_convert_type(lax.shift_left(w, 16), F32)
    hi = lax.bitcast_convert_type(w & _HIGH_HALF, F32)
    return lo, hi


def _log_sigmoid(x):
    return jnp.minimum(x, 0.0) - jnp.log1p(jnp.exp(-jnp.abs(x)))


def _ada_kernel(c_ref, w_ref, b_ref, o_ref):
    nb, tn = o_ref.shape
    rows = []
    for b in range(nb):
        cb = c_ref[b]
        cb = cb * jax.nn.sigmoid(cb)
        pieces = [jnp.sum(w_ref[:, j * LANES:(j + 1) * LANES] * cb, axis=0, keepdims=True)
                  for j in range(tn // LANES)]
        rows.append(jnp.concatenate(pieces, axis=1))
    o_ref[...] = jnp.concatenate(rows, axis=0) + b_ref[...]


def _ada(c_rep, w, bias, tn=1024):
    nb, k, _ = c_rep.shape
    n_out = w.shape[1]
    return pl.pallas_call(
        _ada_kernel,
        grid=(n_out // tn,),
        in_specs=[pl.BlockSpec((nb, k, LANES), lambda j: (0, 0, 0)),
                  pl.BlockSpec((k, tn), lambda j: (0, j)),
                  pl.BlockSpec((1, tn), lambda j: (0, j))],
        out_specs=pl.BlockSpec((nb, tn), lambda j: (0, j)),
        out_shape=jax.ShapeDtypeStruct((nb, n_out), F32),
        compiler_params=_cparams("arbitrary"),
        name="ada",
    )(c_rep, w, bias.reshape(1, n_out))


def _rms_mod(x, w, shift, scale):
    var = jnp.mean(x * x, axis=-1, keepdims=True)
    y = x * lax.rsqrt(var + NORM_EPS) * w
    return y * (1.0 + scale) + shift


def _norm_mod_kernel(x_ref, w_ref, sh_ref, sc_ref, o_ref):
    o_ref[...] = _rms_mod(x_ref[...], w_ref[...], sh_ref[...], sc_ref[...]).astype(o_ref.dtype)


def _norm_mod(x, w, shift, scale, out_dtype, tm=512):
    b, s, d = x.shape
    vec = pl.BlockSpec((None, 1, d), lambda bi, i: (bi, 0, 0))
    return pl.pallas_call(
        _norm_mod_kernel,
        grid=(b, s // tm),
        in_specs=[pl.BlockSpec((None, tm, d), lambda bi, i: (bi, i, 0)),
                  pl.BlockSpec((1, d), lambda bi, i: (0, 0)), vec, vec],
        out_specs=pl.BlockSpec((None, tm, d), lambda bi, i: (bi, i, 0)),
        out_shape=jax.ShapeDtypeStruct((b, s, d), out_dtype),
        compiler_params=_cparams("arbitrary", "arbitrary"),
        name="norm_mod",
    )(x, w.reshape(1, d), shift.reshape(b, 1, d), scale.reshape(b, 1, d))


def _matmul_kernel(a_ref, wt_hbm, *rest, has_bias, row0, n_col_tiles):
    b_ref, o_ref, stage, w16, sem = rest if has_bias else (None,) + rest
    j = pl.program_id(0)
    tn = stage.shape[0]
    rows = tn // WEIGHT_DMA_PARTS

    def fetch(col_tile):
        first = row0 + col_tile * tn
        return [pltpu.make_async_copy(wt_hbm.at[pl.ds(pl.multiple_of(first + p * rows, rows), rows), :],
                                      stage.at[pl.ds(p * rows, rows), :], sem)
                for p in range(WEIGHT_DMA_PARTS)]

    @pl.when(pl.program_id(1) == 0)
    def _():
        @pl.when(j == 0)
        def _():
            for part, copy in enumerate(fetch(0)):
                copy.start(priority=part % 2)

        for copy in fetch(j):
            copy.wait()
        chunk = min(tn, 2 * LANES)
        for c in range(0, tn, chunk):
            w16[:, c:c + chunk] = stage[c:c + chunk, :].T.astype(BF16)

        @pl.when(j + 1 < n_col_tiles)
        def _():
            for part, copy in enumerate(fetch(j + 1)):
                copy.start(priority=part % 2)

    acc = jnp.dot(a_ref[...], w16[...], preferred_element_type=F32)
    if has_bias:
        acc = acc + b_ref[...]
    o_ref[...] = acc.astype(o_ref.dtype)


def _matmul(a, w_t, col0, ncols, out_dtype, tm, tn, bias=None):
    m, k = a.shape
    f32_tile = tm * tn * 4
    vmem = min(tn * k * 4 + k * tn * 2 + 2 * tm * k * 2 + 2 * f32_tile + 3 * f32_tile, VMEM_CAPACITY - VMEM_RESERVE)
    in_specs = [pl.BlockSpec((tm, k), lambda j, i: (i, 0)),
                pl.BlockSpec(memory_space=pl.ANY)]
    args = [a, w_t]
    if bias is not None:
        in_specs.append(pl.BlockSpec((1, tn), lambda j, i: (0, j)))
        args.append(bias)
    return pl.pallas_call(
        functools.partial(_matmul_kernel, has_bias=bias is not None, row0=col0, n_col_tiles=ncols // tn),
        grid=(ncols // tn, m // tm),
        in_specs=in_specs,
        out_specs=pl.BlockSpec((tm, tn), lambda j, i: (i, j)),
        out_shape=jax.ShapeDtypeStruct((m, ncols), out_dtype),
        scratch_shapes=[pltpu.VMEM((tn, k), F32), pltpu.VMEM((k, tn), BF16), pltpu.SemaphoreType.DMA(())],
        compiler_params=_cparams("arbitrary", "arbitrary", vmem=vmem),
        name="proj",
    )(*args)


def _sb_attn_kernel(q_ref, k_ref, v_ref, tri_ref, o_ref, acc_ref, r_ref, *, sub, n_sub, scale):
    first = pl.program_id(2) * n_sub
    tri = tri_ref[...]
    row = lax.broadcasted_iota(jnp.int32, (sub, sub), 0)
    col = lax.broadcasted_iota(jnp.int32, (sub, sub), 1)
    causal = col < row

    def visit(q, blk, r):
        start = pl.multiple_of(blk * sub, sub)
        kb = k_ref[pl.ds(start, sub), :]
        vb = v_ref[pl.ds(start, sub), :]
        z = lax.dot_general(q, kb, (((1,), (1,)), ((), ())), preferred_element_type=F32) * scale
        log_keep = -(jnp.maximum(z, 0.0) + jnp.log(1.0 + jnp.exp(-jnp.abs(z))))
        log_beta = log_keep + z
        hi, lo = _split_bf16(log_keep)
        between = (jnp.dot(hi, tri, preferred_element_type=F32)
                   + jnp.dot(lo, tri, preferred_element_type=F32))
        w = jnp.exp(log_beta + between + r)
        pv = jnp.dot(w.astype(BF16), vb, preferred_element_type=F32)
        return pv, r + jnp.sum(log_keep, axis=1, keepdims=True)

    n_win = ATTN_ALWAYS
    roww = lax.broadcasted_iota(jnp.int32, (sub, n_win * sub), 0)
    colw = lax.broadcasted_iota(jnp.int32, (sub, n_win * sub), 1)

    def win_start(s):
        blk = first + s
        clamp = s < n_win - 1
        first_blk = jnp.maximum(blk - (n_win - 1), 0) if clamp else blk - (n_win - 1)
        return pl.multiple_of(first_blk * sub, sub), clamp

    def masked(s, x):
        start, clamp = win_start(s)
        if clamp:
            return jnp.where(start + colw < (first + s) * sub + roww, x, 0.0)
        last = (n_win - 1) * sub
        return jnp.concatenate([x[:, :last], jnp.where(causal, x[:, last:], 0.0)], axis=1)

    scores = []
    for s in range(n_sub):
        kw = k_ref[pl.ds(win_start(s)[0], n_win * sub), :]
        q = q_ref[s * sub:(s + 1) * sub, :]
        scores.append(lax.dot_general(q, kw, (((1,), (1,)), ((), ())), preferred_element_type=F32) * scale)

    staged = []
    for s in range(n_sub):
        z = scores[s]
        log_keep = -(jnp.maximum(z, 0.0) + jnp.log(1.0 + jnp.exp(-jnp.abs(z))))
        log_beta = log_keep + z
        log_keep = masked(s, log_keep)
        chunks = [log_keep[:, c * sub:(c + 1) * sub] for c in range(n_win)]
        hi, lo = _split_bf16(jnp.concatenate(chunks, axis=0))
        inside = (jnp.dot(hi, tri, preferred_element_type=F32)
                  + jnp.dot(lo, tri, preferred_element_type=F32))
        staged.append((log_beta, chunks, inside))

    r_top = None
    for s in range(n_sub):
        log_beta, chunks, inside = staged[s]
        later = 0.0
        between = [None] * n_win
        for c in reversed(range(n_win)):
            between[c] = inside[c * sub:(c + 1) * sub, :] + later
            later = later + jnp.sum(chunks[c], axis=1, keepdims=True)
        w = masked(s, jnp.exp(log_beta + jnp.concatenate(between, axis=1)))
        vw = v_ref[pl.ds(win_start(s)[0], n_win * sub), :]
        acc_ref[s] = jnp.dot(w.astype(BF16), vw, preferred_element_type=F32)
        r = jnp.broadcast_to(later, (sub, LANES))
        r_ref[s] = r
        r_top = r if r_top is None else jnp.maximum(r_top, r)

    @pl.when(jnp.max(r_top) > UNDERFLOW_LOG)
    def _():
        for s in range(n_sub):
            q = q_ref[s * sub:(s + 1) * sub, :]

            def cond(carry):
                j, r_max = carry
                return jnp.logical_and(j >= 0, r_max > UNDERFLOW_LOG)

            def body(carry, s=s, q=q):
                j, _ = carry
                pv, r = visit(q, j, r_ref[s])
                acc_ref[s] += pv
                r_ref[s] = r
                return j - 1, jnp.max(r)

            lax.while_loop(cond, body, (first + s - ATTN_ALWAYS, jnp.max(r_ref[s])))

    for s in range(n_sub):
        o_ref[s * sub:(s + 1) * sub, :] = acc_ref[s].astype(o_ref.dtype)


def _sb_attention(qkv, sub=ATTN_SUB, n_sub=ATTN_SUBS_PER_STEP):
    b, s, _ = qkv.shape
    h, d = SB_HEADS, SB_HEAD_DIM
    assert sub == LANES, "the carried row sums are kept lane-replicated at the key sub-block width"
    assert s >= ATTN_ALWAYS * sub
    n_sub = min(n_sub, s // sub)
    tq = sub * n_sub
    idx = lax.broadcasted_iota(jnp.int32, (sub, sub), 0) > lax.broadcasted_iota(jnp.int32, (sub, sub), 1)
    tri = idx.astype(BF16)
    kern = functools.partial(_sb_attn_kernel, sub=sub, n_sub=n_sub, scale=1.0 / math.sqrt(d))
    return pl.pallas_call(
        kern,
        grid=(b, h, s // tq),
        in_specs=[pl.BlockSpec((None, tq, d), lambda bi, hi, qi: (bi, qi, hi)),
                  pl.BlockSpec((None, s, d), lambda bi, hi, qi: (bi, 0, h + hi)),
                  pl.BlockSpec((None, s, d), lambda bi, hi, qi: (bi, 0, 2 * h + hi)),
                  pl.BlockSpec((sub, sub), lambda bi, hi, qi: (0, 0))],
        out_specs=pl.BlockSpec((None, tq, d), lambda bi, hi, qi: (bi, qi, hi)),
        out_shape=jax.ShapeDtypeStruct((b, s, h * d), BF16),
        scratch_shapes=[pltpu.VMEM((n_sub, sub, d), F32), pltpu.VMEM((n_sub, sub, LANES), F32)],
        compiler_params=_cparams("arbitrary", "arbitrary", "arbitrary"),
        name="sb_attn",
    )(qkv, qkv, qkv, tri)


def _round_robin(stagewise):
    out = [None] * len(stagewise)
    live = list(range(len(stagewise)))
    while live:
        for i in list(live):
            try:
                next(stagewise[i])
            except StopIteration as stop:
                out[i] = stop.value
                live.remove(i)
    return out


def _mlstm_head(head, q, k, vb, og, gcol, grow, nw, tril_ref, triu_ref, c_st, n_st, m_st, *, chunk):
    li_col = gcol[:, head:head + 1]
    lf_col = _log_sigmoid(gcol[:, ML_HEADS + head:ML_HEADS + head + 1])
    li_row = grow[head:head + 1, :]
    lf_row = _log_sigmoid(grow[ML_HEADS + head:ML_HEADS + head + 1, :])

    hi, lo = _split_bf16(jnp.broadcast_to(lf_col, (chunk, LANES)))
    b_col = (jnp.dot(tril_ref[...], hi, preferred_element_type=F32)
             + jnp.dot(tril_ref[...], lo, preferred_element_type=F32))[:, 0:1]
    hi, lo = _split_bf16(jnp.broadcast_to(lf_row, (2 * SUBLANES, chunk)))
    b_row = (jnp.dot(hi, triu_ref[...], preferred_element_type=F32)
             + jnp.dot(lo, triu_ref[...], preferred_element_type=F32))[0:1, :]
    qb = q.astype(BF16)
    kb = k.astype(BF16)
    qk = lax.dot_general(qb, kb, (((1,), (1,)), ((), ())), preferred_element_type=F32)
    q_mem = jnp.dot(qb, c_st[...].astype(BF16), preferred_element_type=F32)
    yield

    m_prev = m_st[...]
    a_col = b_col + m_prev
    row_t = lax.broadcasted_iota(jnp.int32, (chunk, chunk), 0)
    col_s = lax.broadcasted_iota(jnp.int32, (chunk, chunk), 1)
    d_mat = jnp.where(col_s <= row_t, b_col - b_row + li_row, -jnp.inf)
    m_col = jnp.maximum(a_col, jnp.max(d_mat, axis=1, keepdims=True))
    w_intra = jnp.exp(d_mat - m_col)
    w_inter = jnp.exp(a_col - m_col)
    sc = qk * w_intra
    sc_v = jnp.dot(sc.astype(BF16), vb, preferred_element_type=F32)

    b_last = b_col[chunk - 1:chunk, :]
    g_col = b_last - b_col + li_col
    m_new = jnp.maximum(b_last + m_prev, jnp.max(g_col, axis=0, keepdims=True))
    decay = jnp.exp(b_last + m_prev - m_new)
    kw = k * jnp.exp(g_col - m_new)
    kw_v = lax.dot_general(kw.astype(BF16), vb, (((0,), (0,)), ((), ())), preferred_element_type=F32)
    yield

    num = w_inter * q_mem + sc_v
    den = (w_inter * jnp.sum(q * n_st[...], axis=1, keepdims=True)
           + jnp.sum(sc, axis=1, keepdims=True))
    h_out = num / jnp.maximum(jnp.abs(den), jnp.exp(-m_col))
    c_st[...] = decay * c_st[...] + kw_v
    n_st[...] = decay * n_st[...] + jnp.sum(kw, axis=0, keepdims=True)
    m_st[...] = m_new

    hn = h_out * lax.rsqrt(jnp.mean(h_out * h_out, axis=1, keepdims=True) + NORM_EPS) * nw
    return jax.nn.sigmoid(og) * hn


def _mlstm_kernel(qp_ref, kp_ref, v_ref, og_ref, gcol_ref, grow_ref, cwq_ref, cwk_ref, nw_ref,
                  tril_ref, triu_ref, y_ref, c_st, n_st, m_st, qbuf, kbuf, *, chunk, heads, dim):
    halo = SUBLANES

    @pl.when(pl.program_id(1) == 0)
    def _():
        c_st[...] = jnp.zeros_like(c_st)
        n_st[...] = jnp.zeros_like(n_st)
        m_st[...] = jnp.zeros_like(m_st)
        qbuf[0:halo, :] = jnp.zeros((halo, heads * dim), F32)
        kbuf[0:halo, :] = jnp.zeros((halo, heads * dim), F32)

    qbuf[halo:halo + chunk, :] = qp_ref[...]
    kbuf[halo:halo + chunk, :] = kp_ref[...]

    def conv_silu(buf, cw_ref):
        acc = buf[halo:halo + chunk, :] * cw_ref[CONV_WIDTH - 1:CONV_WIDTH, :]
        for j in range(CONV_WIDTH - 1):
            off = halo - (CONV_WIDTH - 1) + j
            acc = acc + buf[off:off + chunk, :] * cw_ref[j:j + 1, :]
        return acc * jax.nn.sigmoid(acc)

    q = conv_silu(qbuf, cwq_ref)
    k = conv_silu(kbuf, cwk_ref) * (1.0 / math.sqrt(dim))
    qbuf[0:halo, :] = qbuf[chunk:chunk + halo, :]
    kbuf[0:halo, :] = kbuf[chunk:chunk + halo, :]

    gcol = gcol_ref[...]
    grow = grow_ref[...]
    cols = [slice(head * dim, (head + 1) * dim) for head in range(heads)]
    ys = _round_robin([
        _mlstm_head(head, q[:, c], k[:, c], v_ref[:, c], og_ref[:, c], gcol, grow, nw_ref[:, c],
                    tril_ref, triu_ref, c_st.at[head], n_st.at[head], m_st.at[head], chunk=chunk)
        for head, c in enumerate(cols)])
    for y, c in zip(ys, cols):
        y_ref[:, c] = y.astype(y_ref.dtype)


def _mlstm(ml_qk, ml_v, ml_o, gates, conv_w, ml_norm_w, chunk=ML_CHUNK):
    b, s, _ = ml_v.shape
    h, d = ML_HEADS, ML_HEAD_DIM
    w = h * d
    gates_t = jnp.swapaxes(gates, 1, 2)
    r = lax.broadcasted_iota(jnp.int32, (chunk, chunk), 0)
    c = lax.broadcasted_iota(jnp.int32, (chunk, chunk), 1)
    tril = (c <= r).astype(BF16)
    triu = (r <= c).astype(BF16)
    wide = lambda off: pl.BlockSpec((None, chunk, w), lambda bi, ci: (bi, ci, off))
    const = lambda shape, off=0: pl.BlockSpec(shape, lambda bi, ci: (0, off))
    kern = functools.partial(_mlstm_kernel, chunk=chunk, heads=h, dim=d)
    return pl.pallas_call(
        kern,
        grid=(b, s // chunk),
        in_specs=[wide(0), wide(1), wide(0), wide(0),
                  pl.BlockSpec((None, chunk, 2 * h), lambda bi, ci: (bi, ci, 0)),
                  pl.BlockSpec((None, 2 * h, chunk), lambda bi, ci: (bi, 0, ci)),
                  const((CONV_WIDTH, w)), const((CONV_WIDTH, w), 1), const((1, w)),
                  const((chunk, chunk)), const((chunk, chunk))],
        out_specs=wide(0),
        out_shape=jax.ShapeDtypeStruct((b, s, w), BF16),
        scratch_shapes=[pltpu.VMEM((h, d, d), F32), pltpu.VMEM((h, 1, d), F32), pltpu.VMEM((h, 1, 1), F32),
                        pltpu.VMEM((chunk + SUBLANES, w), F32), pltpu.VMEM((chunk + SUBLANES, w), F32)],
        compiler_params=_cparams("arbitrary", "arbitrary"),
        name="mlstm",
    )(ml_qk, ml_qk, ml_v, ml_o, gates, gates_t, conv_w, conv_w, ml_norm_w.reshape(1, w), tril, triu)


def _out_proj_kernel(ya_ref, yb_ref, wa_ref, wb_ref, x_ref, g_ref, o_ref):
    mix = (jnp.dot(ya_ref[...], wa_ref[...], preferred_element_type=F32)
           + jnp.dot(yb_ref[...], wb_ref[...], preferred_element_type=F32))
    o_ref[...] = x_ref[...] + g_ref[...] * mix


def _out_proj(y_sb, y_ml, w_out, x, gate, tm, tn=1024):
    b, s, d = x.shape
    ka, kb = y_sb.shape[-1], y_ml.shape[-1]
    assert ka == kb
    n = b * s
    rows_per_seq = s // tm
    f32_tile = tm * tn * 4
    vmem = min(2 * (tm + tn) * (ka + kb) * 2 + 4 * f32_tile + 3 * f32_tile, VMEM_CAPACITY - VMEM_RESERVE)
    out = pl.pallas_call(
        _out_proj_kernel,
        grid=(n // tm, d // tn),
        in_specs=[pl.BlockSpec((tm, ka), lambda i, j: (i, 0)),
                  pl.BlockSpec((tm, kb), lambda i, j: (i, 0)),
                  pl.BlockSpec((ka, tn), lambda i, j: (0, j)),
                  pl.BlockSpec((kb, tn), lambda i, j: (1, j)),
                  pl.BlockSpec((tm, tn), lambda i, j: (i, j)),
                  pl.BlockSpec((None, 1, tn), lambda i, j: (i // rows_per_seq, 0, j))],
        out_specs=pl.BlockSpec((tm, tn), lambda i, j: (i, j)),
        out_shape=jax.ShapeDtypeStruct((n, d), F32),
        compiler_params=_cparams("arbitrary", "arbitrary", vmem=vmem),
        name="out_proj",
    )(y_sb.reshape(n, ka), y_ml.reshape(n, kb), w_out, w_out, x.reshape(n, d), gate.reshape(b, 1, d))
    return out.reshape(b, s, d)


def _router_kernel(h_ref, nw_ref, sh_ref, sc_ref, wr_ref, br_ref, tri_ref, u_ref, ri_ref, rw_ref, cnt_ref, run_ref):
    first = jnp.logical_and(pl.program_id(0) == 0, pl.program_id(1) == 0)

    @pl.when(first)
    def _():
        run_ref[...] = jnp.zeros_like(run_ref)

    u = _rms_mod(h_ref[...], nw_ref[...], sh_ref[...], sc_ref[...])
    tm = u.shape[0]
    uh, ul = _split_bf16(u)
    u_ref[...] = _pack_bf16_pairs(uh)
    wh, wl = _split_bf16(wr_ref[...])
    logits = (jnp.dot(uh, wh, preferred_element_type=F32) + jnp.dot(ul, wh, preferred_element_type=F32)
              + jnp.dot(uh, wl, preferred_element_type=F32) + br_ref[...])

    lane = lax.broadcasted_iota(jnp.int32, (tm, LANES), 1)
    lane_f = lane.astype(F32)
    neg = -jnp.inf
    first_lane = lambda hit: jnp.min(jnp.where(hit, lane_f, float(LANES)), axis=1, keepdims=True)

    gl = jnp.where(lane < N_GROUPS, logits, neg)
    g_max = jnp.max(gl, axis=1, keepdims=True)
    g_p = 1.0 / jnp.sum(jnp.exp(gl - g_max), axis=1, keepdims=True)
    g_sel = first_lane(gl == g_max)

    lo_lane = N_GROUPS + EXPERTS_PER_GROUP * g_sel
    el = jnp.where(jnp.logical_and(lane_f >= lo_lane, lane_f < lo_lane + EXPERTS_PER_GROUP), logits, neg)
    e1_max = jnp.max(el, axis=1, keepdims=True)
    l1 = first_lane(el == e1_max)
    el2 = jnp.where(lane_f == l1, neg, el)
    e2_max = jnp.max(el2, axis=1, keepdims=True)
    l2 = first_lane(el2 == e2_max)
    z_sum = jnp.sum(jnp.exp(el - e1_max), axis=1, keepdims=True)
    p1 = 1.0 / z_sum
    p2 = jnp.exp(e2_max - e1_max) / z_sum
    w1 = p1 / (p1 + p2) * g_p
    w2 = p2 / (p1 + p2) * g_p
    e1 = l1 - N_GROUPS
    e2 = l2 - N_GROUPS

    hit1 = lane_f == e1
    hit2 = lane_f == e2
    onehot = jnp.where(jnp.logical_or(hit1, hit2), 1.0, 0.0)
    before = jnp.dot(tri_ref[...], onehot.astype(BF16), preferred_element_type=F32) + run_ref[...]
    rank1 = jnp.sum(jnp.where(hit1, before, 0.0), axis=1, keepdims=True)
    rank2 = jnp.sum(jnp.where(hit2, before, 0.0), axis=1, keepdims=True)
    run = run_ref[...] + jnp.sum(onehot, axis=0, keepdims=True)
    run_ref[...] = run

    ri_ref[...] = jnp.where(lane == 0, e1, jnp.where(lane == 1, e2, jnp.where(lane == 2, rank1,
                            jnp.where(lane == 3, rank2, 0.0)))).astype(jnp.int32)
    rw_ref[...] = jnp.where(lane == 0, w1, jnp.where(lane == 1, w2, 0.0))
    cnt_ref[...] = jnp.broadcast_to(run, cnt_ref.shape).astype(jnp.int32)


def _router(h, norm_w, shift, scale, w_router, b_router, tm=512):
    b, s, d = h.shape
    n = b * s
    tri = (lax.broadcasted_iota(jnp.int32, (tm, tm), 1) < lax.broadcasted_iota(jnp.int32, (tm, tm), 0)).astype(BF16)
    vec = pl.BlockSpec((None, 1, d), lambda bi, i: (bi, 0, 0))
    rows = lambda width: pl.BlockSpec((tm, width), lambda bi, i: (bi * (s // tm) + i, 0))
    return pl.pallas_call(
        _router_kernel,
        grid=(b, s // tm),
        in_specs=[pl.BlockSpec((None, tm, d), lambda bi, i: (bi, i, 0)),
                  pl.BlockSpec((1, d), lambda bi, i: (0, 0)), vec, vec,
                  pl.BlockSpec((d, LANES), lambda bi, i: (0, 0)),
                  pl.BlockSpec((1, LANES), lambda bi, i: (0, 0)),
                  pl.BlockSpec((tm, tm), lambda bi, i: (0, 0))],
        out_specs=[rows(d // 2), rows(LANES), rows(LANES),
                   pl.BlockSpec((SUBLANES, LANES), lambda bi, i: (0, 0))],
        out_shape=[jax.ShapeDtypeStruct((n, d // 2), jnp.int32), jax.ShapeDtypeStruct((n, LANES), jnp.int32),
                   jax.ShapeDtypeStruct((n, LANES), F32), jax.ShapeDtypeStruct((SUBLANES, LANES), jnp.int32)],
        scratch_shapes=[pltpu.VMEM((1, LANES), F32)],
        compiler_params=_cparams("arbitrary", "arbitrary"),
        name="router",
    )(h, norm_w.reshape(1, d), shift.reshape(b, 1, d), scale.reshape(b, 1, d), w_router, b_router, tri)


ROW_LOOP_UNROLL = 8
ZERO_ROWS = 128


def _dispatch_kernel(tail_ref, dest_ref, u_ref, xb_hbm, zbuf, zsem, sem, *, tokens, block):
    n_exp = tail_ref.shape[0]

    def clear(e, half):
        row = pl.multiple_of(tail_ref[e] + half * ZERO_ROWS, ZERO_ROWS)
        return pltpu.make_async_copy(zbuf, xb_hbm.at[pl.ds(row, ZERO_ROWS)], zsem)

    @pl.when(pl.program_id(0) == 0)
    def _():
        zbuf[...] = jnp.zeros_like(zbuf)
        for start in (True, False):
            def each(e, carry):
                @pl.when(tail_ref[e] >= 0)
                def _():
                    for half in range(block // ZERO_ROWS):
                        clear(e, half).start() if start else clear(e, half).wait()
                return carry
            lax.fori_loop(0, n_exp, each, 0)

    def row_copy(t, k):
        return pltpu.make_async_copy(u_ref.at[pl.ds(t, 1)], xb_hbm.at[pl.ds(dest_ref[0, TOP_K * t + k], 1)], sem)

    def issue(t, carry):
        for k in range(TOP_K):
            row_copy(t, k).start(priority=k % 2)
        return carry

    lax.fori_loop(0, tokens, issue, 0, unroll=ROW_LOOP_UNROLL)

    def drain(t, carry):
        for k in range(TOP_K):
            row_copy(t, k).wait()
        return carry

    lax.fori_loop(0, tokens, drain, 0, unroll=ROW_LOOP_UNROLL)


def _dispatch(u, dest, tail_block_row, cap, tokens=512, block=MOE_BLOCK):
    n, d = u.shape
    steps = n // tokens
    kern = functools.partial(_dispatch_kernel, tokens=tokens, block=block)
    return pl.pallas_call(
        kern,
        grid_spec=pltpu.PrefetchScalarGridSpec(
            num_scalar_prefetch=1,
            grid=(steps,),
            in_specs=[pl.BlockSpec((None, 1, TOP_K * tokens), lambda i, tail: (i, 0, 0), memory_space=pltpu.SMEM),
                      pl.BlockSpec((tokens, d), lambda i, tail: (i, 0))],
            out_specs=pl.BlockSpec(memory_space=pl.ANY),
            scratch_shapes=[pltpu.VMEM((ZERO_ROWS, d), u.dtype), pltpu.SemaphoreType.DMA(()),
                            pltpu.SemaphoreType.DMA(())]),
        out_shape=jax.ShapeDtypeStruct((cap, d), u.dtype),
        compiler_params=_cparams("arbitrary"),
        name="dispatch",
    )(tail_block_row, dest.reshape(steps, 1, TOP_K * tokens), u)


def _lookup(table, idx):
    hit = idx[:, None] == jnp.arange(table.shape[0], dtype=idx.dtype)[None, :]
    return jnp.sum(jnp.where(hit, table[None, :], 0), axis=1).astype(table.dtype)


def _run_plan(keys, ok):
    steps = ok.shape[0]
    idx = jnp.arange(steps, dtype=jnp.int32)
    changed = idx == 0
    for key in keys:
        changed = jnp.logical_or(changed, key != jnp.roll(key, 1))
    first = jnp.logical_and(changed, ok)
    slot = (jnp.cumsum(first.astype(jnp.int32)) - 1) % 2
    starts_from = lax.cummin(jnp.where(first, idx, steps), axis=0, reverse=True)
    nxt = jnp.concatenate([starts_from[1:], jnp.full((1,), steps, jnp.int32)])
    more = nxt < steps
    as_i32 = lambda a: a.astype(jnp.int32)
    return as_i32(first), as_i32(slot), as_i32(more), as_i32(jnp.minimum(nxt, steps - 1))


def _run_weights(step, first_ref, slot_ref, more_ref, fetch_cur, fetch_next):
    slot = slot_ref[step]

    @pl.when(step == 0)
    def _():
        for part, copy in enumerate(fetch_cur(slot)):
            copy.start(priority=part % 2)

    @pl.when(first_ref[step] == 1)
    def _():
        @pl.when(more_ref[step] == 1)
        def _():
            for part, copy in enumerate(fetch_next(1 - slot)):
                copy.start(priority=part % 2)

        for copy in fetch_cur(slot):
            copy.wait()

    return slot


def _e1_kernel(blk_ref, ex_ref, ft_ref, oblk_ref, oft_ref, ok_ref, first_ref, slot_ref, more_ref, nex_ref, nft_ref,
               x_ref, wg_hbm, wu_hbm, o_ref, wg_buf, wu_buf, sems, *, tf):
    del blk_ref, oblk_ref, oft_ref
    step = pl.program_id(0)

    def fetch(e, f, slot):
        cols = pl.ds(pl.multiple_of(f * tf, tf), tf)
        rows = wg_buf.shape[1] // WEIGHT_DMA_PARTS
        return [pltpu.make_async_copy(hbm.at[e, pl.ds(p * rows, rows), cols],
                                      buf.at[slot, pl.ds(p * rows, rows)], sems.at[slot, which])
                for p in range(WEIGHT_DMA_PARTS)
                for which, (hbm, buf) in enumerate(((wg_hbm, wg_buf), (wu_hbm, wu_buf)))]

    slot = _run_weights(step, first_ref, slot_ref, more_ref,
                        lambda sl: fetch(ex_ref[step], ft_ref[step], sl),
                        lambda sl: fetch(nex_ref[step], nft_ref[step], sl))
    ok = ok_ref[step] == 1

    @pl.when(ok)
    def _():
        x_lo, x_hi = _unpack_bf16_pairs(x_ref[...])
        half = x_lo.shape[1]
        mm = lambda w_buf: (jnp.dot(x_lo, w_buf[slot, :half, :], preferred_element_type=F32)
                            + jnp.dot(x_hi, w_buf[slot, half:, :], preferred_element_type=F32))
        g = mm(wg_buf)
        up = mm(wu_buf)
        o_ref[...] = ((g * jax.nn.sigmoid(g)) * up).astype(o_ref.dtype)

    @pl.when(jnp.logical_not(ok))
    def _():
        o_ref[...] = jnp.zeros_like(o_ref)


def _e2_kernel(blk_ref, ex_ref, oblk_ref, ok_ref, first_ref, slot_ref, more_ref, nex_ref,
               h_ref, wd_hbm, o_ref, wd_buf, sems):
    del blk_ref, oblk_ref
    step = pl.program_id(0)
    rows = wd_buf.shape[1] // WEIGHT_DMA_PARTS
    fetch = lambda e, slot: [pltpu.make_async_copy(wd_hbm.at[e, pl.ds(p * rows, rows)],
                                                   wd_buf.at[slot, pl.ds(p * rows, rows)], sems.at[slot])
                             for p in range(WEIGHT_DMA_PARTS)]
    slot = _run_weights(step, first_ref, slot_ref, more_ref,
                        lambda sl: fetch(ex_ref[step], sl), lambda sl: fetch(nex_ref[step], sl))
    ok = ok_ref[step] == 1

    @pl.when(ok)
    def _():
        o_ref[...] = jnp.dot(h_ref[...].astype(F32), wd_buf[slot], preferred_element_type=F32)

    @pl.when(jnp.logical_not(ok))
    def _():
        o_ref[...] = jnp.zeros_like(o_ref)


def _experts(xb, blocks_per_expert, w_gate, w_up, w_down, bm=MOE_BLOCK, tf=512):
    cap, dp = xb.shape
    n_exp, d, dff = w_gate.shape
    n_blocks = cap // bm
    n_ft = dff // tf
    nb = blocks_per_expert.astype(jnp.int32)
    bstart = jnp.cumsum(nb) - nb
    used = jnp.sum(nb)

    steps1 = n_blocks * n_ft
    step = jnp.arange(steps1, dtype=jnp.int32)
    ok1 = step < used * n_ft
    send = jnp.cumsum(nb * n_ft)
    s_idx = jnp.minimum(step, used * n_ft - 1)
    e_of = jnp.minimum(jnp.sum(s_idx[:, None] >= send[None, :], axis=1), n_exp - 1).astype(jnp.int32)
    nb_of = _lookup(nb, e_of)
    local = s_idx - (_lookup(send, e_of) - nb_of * n_ft)
    nbe = jnp.maximum(nb_of, 1)
    ft_of = (local // nbe).astype(jnp.int32)
    blk_of = (_lookup(bstart, e_of) + local % nbe).astype(jnp.int32)
    spare = step - used * n_ft
    oblk_of = jnp.where(ok1, blk_of, used + spare // n_ft).astype(jnp.int32)
    oft_of = jnp.where(ok1, ft_of, spare % n_ft).astype(jnp.int32)

    first1, slot1, more1, nxt1 = _run_plan([e_of, ft_of], ok1)
    hmid = pl.pallas_call(
        functools.partial(_e1_kernel, tf=tf),
        grid_spec=pltpu.PrefetchScalarGridSpec(
            num_scalar_prefetch=11,
            grid=(steps1,),
            in_specs=[pl.BlockSpec((bm, dp), lambda s, blk, *_: (blk[s], 0)),
                      pl.BlockSpec(memory_space=pl.ANY),
                      pl.BlockSpec(memory_space=pl.ANY)],
            out_specs=pl.BlockSpec((bm, tf), lambda s, blk, ex, ft, oblk, oft, *_: (oblk[s], oft[s])),
            scratch_shapes=[pltpu.VMEM((2, d, tf), F32), pltpu.VMEM((2, d, tf), F32),
                            pltpu.SemaphoreType.DMA((2, 2))]),
        out_shape=jax.ShapeDtypeStruct((cap, dff), BF16),
        compiler_params=_cparams("arbitrary"),
        name="e1",
    )(blk_of, e_of, ft_of, oblk_of, oft_of, ok1.astype(jnp.int32), first1, slot1, more1, _lookup(e_of, nxt1), _lookup(ft_of, nxt1),
      xb, w_gate, w_up)

    blk = jnp.arange(n_blocks, dtype=jnp.int32)
    b_idx = jnp.minimum(blk, used - 1)
    bend = jnp.cumsum(nb)
    e2_of = jnp.minimum(jnp.sum(b_idx[:, None] >= bend[None, :], axis=1), n_exp - 1).astype(jnp.int32)
    ok2 = blk < used
    first2, slot2, more2, nxt2 = _run_plan([e2_of], ok2)
    yb = pl.pallas_call(
        _e2_kernel,
        grid_spec=pltpu.PrefetchScalarGridSpec(
            num_scalar_prefetch=8,
            grid=(n_blocks,),
            in_specs=[pl.BlockSpec((bm, dff), lambda s, hb, *_: (hb[s], 0)),
                      pl.BlockSpec(memory_space=pl.ANY)],
            out_specs=pl.BlockSpec((bm, d), lambda s, hb, ex, ob, *_: (ob[s], 0)),
            scratch_shapes=[pltpu.VMEM((2, dff, d), F32), pltpu.SemaphoreType.DMA((2,))]),
        out_shape=jax.ShapeDtypeStruct((cap, d), F32),
        compiler_params=_cparams("arbitrary"),
        name="e2",
    )(b_idx, e2_of, blk, ok2.astype(jnp.int32), first2, slot2, more2, _lookup(e2_of, nxt2), hmid, w_down)
    return yb


def _combine_kernel(dest_ref, dest_next_ref, yb_hbm, h_ref, rw_ref, g_ref, nw_ref, sh_ref, sc_ref, o_ref,
                    rows, sems, *, tokens, steps):
    step = pl.program_id(0)
    slot = step % 2

    def row_copy(d_ref, buf, t, k):
        return pltpu.make_async_copy(yb_hbm.at[pl.ds(d_ref[0, TOP_K * t + k], 1)],
                                     rows.at[buf, k, pl.ds(t, 1)], sems.at[buf])

    def gather(d_ref, buf):
        def issue(t, carry):
            for k in range(TOP_K):
                row_copy(d_ref, buf, t, k).start(priority=k % 2)
            return carry

        lax.fori_loop(0, tokens, issue, 0, unroll=ROW_LOOP_UNROLL)

    @pl.when(step == 0)
    def _():
        gather(dest_ref, 0)

    @pl.when(step + 1 < steps)
    def _():
        gather(dest_next_ref, 1 - slot)

    def drain(t, carry):
        for k in range(TOP_K):
            row_copy(dest_ref, slot, t, k).wait()
        return carry

    lax.fori_loop(0, tokens, drain, 0, unroll=ROW_LOOP_UNROLL)

    rw = rw_ref[...]
    ffn = rows[slot, 0] * rw[:, 0:1]
    for k in range(1, TOP_K):
        ffn = ffn + rows[slot, k] * rw[:, k:k + 1]
    h2 = h_ref[...] + g_ref[...] * ffn
    o_ref[...] = _rms_mod(h2, nw_ref[...], sh_ref[...], sc_ref[...])


def _combine(yb, dest, h, route_w, gate, norm_w, shift, scale, tokens=256):
    b, s, d = h.shape
    n = b * s
    per_seq = s // tokens
    steps = n // tokens
    kern = functools.partial(_combine_kernel, tokens=tokens, steps=steps)
    vec = pl.BlockSpec((None, 1, d), lambda i: (i // per_seq, 0, 0))
    dest_blocks = dest.reshape(steps, 1, TOP_K * tokens)
    dest_spec = lambda ahead: pl.BlockSpec((None, 1, TOP_K * tokens),
                                           lambda i: (jnp.minimum(i + ahead, steps - 1), 0, 0),
                                           memory_space=pltpu.SMEM)
    out = pl.pallas_call(
        kern,
        grid=(steps,),
        in_specs=[dest_spec(0), dest_spec(1),
                  pl.BlockSpec(memory_space=pl.ANY),
                  pl.BlockSpec((tokens, d), lambda i: (i, 0)),
                  pl.BlockSpec((tokens, LANES), lambda i: (i, 0)),
                  vec,
                  pl.BlockSpec((1, d), lambda i: (0, 0)), vec, vec],
        out_specs=pl.BlockSpec((tokens, d), lambda i: (i, 0)),
        out_shape=jax.ShapeDtypeStruct((n, d), F32),
        scratch_shapes=[pltpu.VMEM((2, TOP_K, tokens, d), F32), pltpu.SemaphoreType.DMA((2,))],
        compiler_params=_cparams("arbitrary"),
        name="combine",
    )(dest_blocks, dest_blocks, yb, h.reshape(n, d), route_w,
      gate.reshape(b, 1, d), norm_w.reshape(1, d), shift.reshape(b, 1, d), scale.reshape(b, 1, d))
    return out.reshape(b, s, d)


def _layer(h, mod, norm1_w, w_in, conv_w, ml_gate_bias, ml_norm_w, w_out, norm2_w,
           w_router_group, b_router_group, w_router_expert, b_router_expert,
           w_exp_gate, w_exp_up, w_exp_down, final):
    b, s, d = h.shape
    n = b * s
    sh1, sc1, g1, sh2, sc2, g2 = jnp.split(mod, 6, axis=-1)
    sb_w = SB_HEADS * SB_HEAD_DIM
    ml_w = ML_HEADS * ML_HEAD_DIM

    u = _norm_mod(h, norm1_w, sh1, sc1, BF16).reshape(n, d)
    tm = 1024 if n % 1024 == 0 else 512
    w_in_t = w_in.T
    sb_qkv = _matmul(u, w_in_t, 0, 3 * sb_w, BF16, tm, PROJ_TN)
    ml_qk = _matmul(u, w_in_t, 3 * sb_w, 2 * ml_w, F32, tm, PROJ_TN)
    ml_v = _matmul(u, w_in_t, 3 * sb_w + 2 * ml_w, ml_w, BF16, tm, PROJ_TN)
    ml_o = _matmul(u, w_in_t, 3 * sb_w + 3 * ml_w, ml_w, F32, tm, PROJ_TN)
    n_gate = 2 * ML_HEADS
    w_gate = jnp.pad(w_in_t[3 * sb_w + 4 * ml_w:, :], ((0, LANES - n_gate), (0, 0)))
    b_gate = jnp.pad(ml_gate_bias, (0, LANES - n_gate)).reshape(1, LANES)
    gates = _matmul(u, w_gate, 0, LANES, F32, tm, LANES, bias=b_gate)[:, :n_gate]

    y_sb = _sb_attention(sb_qkv.reshape(b, s, 3 * sb_w))
    y_ml = _mlstm(ml_qk.reshape(b, s, 2 * ml_w), ml_v.reshape(b, s, ml_w), ml_o.reshape(b, s, ml_w),
                  gates.reshape(b, s, n_gate), conv_w, ml_norm_w)
    h = _out_proj(y_sb, y_ml, w_out.astype(BF16), h, g1, tm=min(1024, s))

    w_router = jnp.pad(jnp.concatenate([w_router_group, w_router_expert], axis=1),
                       ((0, 0), (0, LANES - N_GROUPS - N_EXPERTS)))
    b_router = jnp.pad(jnp.concatenate([b_router_group, b_router_expert]),
                       (0, LANES - N_GROUPS - N_EXPERTS)).reshape(1, LANES)
    u2, route_i, route_w, counts = _router(h, norm2_w, sh2, sc2, w_router, b_router)
    counts = counts[0, :N_EXPERTS]
    blocks_per_expert = (counts + MOE_BLOCK - 1) // MOE_BLOCK
    padded = blocks_per_expert * MOE_BLOCK
    pstarts = jnp.cumsum(padded) - padded
    dest = (_lookup(pstarts, route_i[:, 0:TOP_K].reshape(-1)) + route_i[:, TOP_K:2 * TOP_K].reshape(-1))
    cap = (-(-(n * TOP_K) // MOE_BLOCK) + N_EXPERTS) * MOE_BLOCK
    tail_block_row = jnp.where(counts % MOE_BLOCK != 0, pstarts + padded - MOE_BLOCK, -1)
    spare_row = jnp.sum(padded) + MOE_BLOCK * jnp.arange(N_EXPERTS, dtype=jnp.int32)
    spare_row = jnp.where(spare_row < cap, spare_row, -1)
    clear_rows = jnp.concatenate([tail_block_row, spare_row]).astype(jnp.int32)
    xb = _dispatch(u2, dest, clear_rows, cap)
    yb = _experts(xb, blocks_per_expert, w_exp_gate, w_exp_up, w_exp_down)
    norm_w, shift, scale = final
    return _combine(yb, dest, h, route_w, g2, norm_w, shift, scale)


def kernel(x, c, norm1_w, w_in, conv_w, ml_gate_bias, ml_norm_w, w_out, norm2_w, w_router_group, b_router_group, w_router_expert, b_router_expert, w_exp_gate, w_exp_up, w_exp_down, w_ada, b_ada, final_norm_w, w_ada_final, b_ada_final):
    b, s, d = x.shape
    depth = w_in.shape[0]
    assert depth == 1, "the final norm is fused into the (single) layer's combine step"
    c_rep = jnp.broadcast_to(c[:, :, None], (b, d, LANES))
    fmod = _ada(c_rep, w_ada_final, b_ada_final)
    sh_f, sc_f = jnp.split(fmod, 2, axis=-1)
    mod = _ada(c_rep, w_ada[0], b_ada[0])
    return _layer(x, mod, norm1_w[0], w_in[0], conv_w[0], ml_gate_bias[0], ml_norm_w[0], w_out[0], norm2_w[0],
                  w_router_group[0], b_router_group[0], w_router_expert[0], b_router_expert[0],
                  w_exp_gate[0], w_exp_up[0], w_exp_down[0], (final_norm_w, sh_f, sc_f))
```
